```python
import math
import jax, jax.numpy as jnp
from jax import lax
import numpy as np

D_MODEL = 2048
BATCH = 2
SEQ = 8192
DEPTH = 1

N_META = 16
BLOCK_Q = 128
LEAD_PAD = BLOCK_Q - N_META
RMS_EPS = 1e-6
NEG_INF = -1e30

DA_HEADS = 8
DA_HEAD_DIM = 64
DA_V_DIM = 2 * DA_HEAD_DIM
DA_LAMBDA_DIM = DA_HEAD_DIM

MLA_HEADS = 8
MLA_Q_RANK = 768
MLA_KV_RANK = 512
MLA_NOPE_DIM = 128
MLA_ROPE_DIM = 64
MLA_V_DIM = 128
ROPE_THETA = 10000.0

REL_BUCKETS = 32
REL_MAX_DIST = 128

D_FF = 5632

N_BRANCH = 2
DA_QK_W = DA_HEADS * 2 * DA_HEAD_DIM
DA_V_W = DA_HEADS * DA_V_DIM
MLA_OUT_W = MLA_HEADS * MLA_V_DIM
COL_SPLITS = (DA_QK_W, DA_QK_W, DA_V_W, MLA_Q_RANK, MLA_KV_RANK, MLA_ROPE_DIM, N_BRANCH * D_MODEL)
IN_PROJ_W = sum(COL_SPLITS)

kernel_name = "hybrid_diffattn_mla_gated_macaron"


def rmsnorm(x, g):
    xf = x.astype(jnp.float32)
    y = xf * lax.rsqrt(jnp.mean(xf * xf, axis=-1, keepdims=True) + RMS_EPS)
    return (y * g.astype(jnp.float32)).astype(x.dtype)


def swiglu(x, w_gate, w_up, w_down):
    return (jax.nn.silu(x @ w_gate) * (x @ w_up)) @ w_down


def rope(x, pos):
    half = x.shape[-1] // 2
    inv = ROPE_THETA ** (-jnp.arange(half, dtype=jnp.float32) * 2.0 / x.shape[-1])
    ang = pos.astype(jnp.float32)[:, None] * inv[None, :]
    ang = ang.reshape((1, pos.shape[0]) + (1,) * (x.ndim - 3) + (half,))
    cos, sin = jnp.cos(ang), jnp.sin(ang)
    xf = x.astype(jnp.float32)
    x1, x2 = xf[..., :half], xf[..., half:]
    return jnp.concatenate([x1 * cos - x2 * sin, x1 * sin + x2 * cos], axis=-1).astype(x.dtype)


def t5_causal_bucket(rel):
    n = jnp.maximum(rel, 0)
    max_exact = REL_BUCKETS // 2
    n_f = jnp.maximum(n, 1).astype(jnp.float32)
    large = max_exact + (jnp.log(n_f / max_exact) / math.log(REL_MAX_DIST / max_exact)
                         * (REL_BUCKETS - max_exact)).astype(jnp.int32)
    large = jnp.minimum(large, REL_BUCKETS - 1)
    return jnp.where(n < max_exact, n, large)


def causal_mask(q_pos, k_pos):
    kp = k_pos[None, :]
    qp = q_pos[:, None]
    return (kp <= qp) & ((kp >= LEAD_PAD) | (kp == qp))


def diff_attention(q, k, v, lam, lambda_init, sub_g, rel_bias_table):
    B, Lp, _ = q.shape
    n_blk = Lp // BLOCK_Q
    q = q.reshape(B, n_blk, BLOCK_Q, DA_HEADS, 2, DA_HEAD_DIM).transpose(1, 0, 3, 4, 2, 5)
    k = k.reshape(B, Lp, DA_HEADS, 2, DA_HEAD_DIM).transpose(0, 2, 3, 1, 4)
    v = v.reshape(B, Lp, DA_HEADS, DA_V_DIM).transpose(0, 2, 1, 3)
    k_pos = jnp.arange(Lp)
    scale = DA_HEAD_DIM ** -0.5
    table = rel_bias_table.astype(jnp.float32)

    def block(args):
        q_blk, b = args
        q_pos = b * BLOCK_Q + jnp.arange(BLOCK_Q)
        s = jnp.einsum("bhcqd,bhckd->bhcqk", q_blk, k, preferred_element_type=jnp.float32) * scale
        bias = table[t5_causal_bucket(q_pos[:, None] - k_pos[None, :])]
        s = s + jnp.transpose(bias, (2, 0, 1))[None, :, None]
        s = jnp.where(causal_mask(q_pos, k_pos), s, NEG_INF)
        p = jax.nn.softmax(s, axis=-1)
        a = p[:, :, 0] - lam * p[:, :, 1]
        return jnp.einsum("bhqk,bhkv->bhqv", a.astype(v.dtype), v)

    o = lax.map(block, (q, jnp.arange(n_blk)))
    o = o.transpose(1, 0, 3, 2, 4).reshape(B, Lp, DA_HEADS, DA_V_DIM)
    o = rmsnorm(o, sub_g) * (1.0 - lambda_init)
    return o.reshape(B, Lp, DA_V_W)


def mla_attention(c_q, c_kv, k_rope_in, q_norm_g, kv_norm_g, w_uq, w_ukv, pos):
    B, Lp, _ = c_q.shape
    n_blk = Lp // BLOCK_Q
    q = (rmsnorm(c_q, q_norm_g) @ w_uq).reshape(B, Lp, MLA_HEADS, MLA_NOPE_DIM + MLA_ROPE_DIM)
    q_nope = q[..., :MLA_NOPE_DIM]
    q_rope = rope(q[..., MLA_NOPE_DIM:], pos)
    kv = (rmsnorm(c_kv, kv_norm_g) @ w_ukv).reshape(B, Lp, MLA_HEADS, MLA_NOPE_DIM + MLA_V_DIM)
    k_nope, v = kv[..., :MLA_NOPE_DIM], kv[..., MLA_NOPE_DIM:]
    k_rope = rope(k_rope_in, pos)
    k_pos = jnp.arange(Lp)
    scale = (MLA_NOPE_DIM + MLA_ROPE_DIM) ** -0.5

    def to_blocks(t):
        return t.reshape((B, n_blk, BLOCK_Q) + t.shape[2:]).swapaxes(0, 1)

    def block(args):
        qn, qr, b = args
        q_pos = b * BLOCK_Q + jnp.arange(BLOCK_Q)
        s = (jnp.einsum("bqhd,bkhd->bhqk", qn, k_nope, preferred_element_type=jnp.float32)
             + jnp.einsum("bqhr,bkr->bhqk", qr, k_rope, preferred_element_type=jnp.float32)) * scale
        s = jnp.where(causal_mask(q_pos, k_pos), s, NEG_INF)
        p = jax.nn.softmax(s, axis=-1)
        return jnp.einsum("bhqk,bkhv->bqhv", p.astype(v.dtype), v)

    o = lax.map(block, (to_blocks(q_nope), to_blocks(q_rope), jnp.arange(n_blk)))
    return o.swapaxes(0, 1).reshape(B, Lp, MLA_OUT_W)


def setup_inputs(seed: int = 0) -> dict:
    key = jax.random.key(seed)
    ks = iter(jax.random.split(key, 40))
    f32 = jnp.float32

    def w(shape, fan_in):
        return jax.random.normal(next(ks), shape, f32) * (fan_in ** -0.5)

    def gain(shape):
        return 1.0 + 0.1 * jax.random.normal(next(ks), shape, f32)

    L = DEPTH
    return {
        "x": jax.random.normal(next(ks), (BATCH, SEQ, D_MODEL), f32),
        "meta_tokens": jax.random.normal(next(ks), (N_META, D_MODEL), f32),
        "rel_bias_table": 0.3 * jax.random.normal(next(ks), (REL_BUCKETS, DA_HEADS), f32),
        "ffn1_pre_g": gain((L, D_MODEL)),
        "ffn1_post_g": gain((L, D_MODEL)),
        "ffn1_w_gate": w((L, D_MODEL, D_FF), D_MODEL),
        "ffn1_w_up": w((L, D_MODEL, D_FF), D_MODEL),
        "ffn1_w_down": w((L, D_FF, D_MODEL), D_FF),
        "mix_pre_g": gain((L, D_MODEL)),
        "mix_post_g": gain((L, D_MODEL)),
        "w_in": w((L, D_MODEL, IN_PROJ_W), D_MODEL),
        "b_gate": 0.1 * jax.random.normal(next(ks), (L, N_BRANCH * D_MODEL), f32),
        "da_lambda_q1": 0.1 * jax.random.normal(next(ks), (L, DA_LAMBDA_DIM), f32),
        "da_lambda_k1": 0.1 * jax.random.normal(next(ks), (L, DA_LAMBDA_DIM), f32),
        "da_lambda_q2": 0.1 * jax.random.normal(next(ks), (L, DA_LAMBDA_DIM), f32),
        "da_lambda_k2": 0.1 * jax.random.normal(next(ks), (L, DA_LAMBDA_DIM), f32),
        "da_sub_g": gain((L, DA_V_DIM)),
        "mla_q_norm_g": gain((L, MLA_Q_RANK)),
        "mla_kv_norm_g": gain((L, MLA_KV_RANK)),
        "mla_w_uq": w((L, MLA_Q_RANK, MLA_HEADS * (MLA_NOPE_DIM + MLA_ROPE_DIM)), MLA_Q_RANK),
        "mla_w_ukv": w((L, MLA_KV_RANK, MLA_HEADS * (MLA_NOPE_DIM + MLA_V_DIM)), MLA_KV_RANK),
        "w_branch_da": w((L, DA_V_W, D_MODEL), DA_V_W),
        "w_branch_mla": w((L, MLA_OUT_W, D_MODEL), MLA_OUT_W),
        "w_out": w((L, D_MODEL, D_MODEL), D_MODEL),
        "ffn2_pre_g": gain((L, D_MODEL)),
        "ffn2_post_g": gain((L, D_MODEL)),
        "ffn2_w_gate": w((L, D_MODEL, D_FF), D_MODEL),
        "ffn2_w_up": w((L, D_MODEL, D_FF), D_MODEL),
        "ffn2_w_down": w((L, D_FF, D_MODEL), D_FF),
    }


def reference(x, meta_tokens, rel_bias_table,
              ffn1_pre_g, ffn1_post_g, ffn1_w_gate, ffn1_w_up, ffn1_w_down,
              mix_pre_g, mix_post_g, w_in, b_gate,
              da_lambda_q1, da_lambda_k1, da_lambda_q2, da_lambda_k2, da_sub_g,
              mla_q_norm_g, mla_kv_norm_g, mla_w_uq, mla_w_ukv,
              w_branch_da, w_branch_mla, w_out,
              ffn2_pre_g, ffn2_post_g, ffn2_w_gate, ffn2_w_up, ffn2_w_down):
    B = x.shape[0]
    pad = jnp.zeros((B, LEAD_PAD, D_MODEL), x.dtype)
    meta = jnp.broadcast_to(meta_tokens.astype(x.dtype)[None], (B, N_META, D_MODEL))
    h = jnp.concatenate([pad, meta, x], axis=1)
    Lp = h.shape[1]
    pos = jnp.arange(Lp) - LEAD_PAD
    offs = np.cumsum(np.array(COL_SPLITS))[:-1].tolist()

    for l in range(DEPTH):
        f = swiglu(rmsnorm(h, ffn1_pre_g[l]), ffn1_w_gate[l], ffn1_w_up[l], ffn1_w_down[l])
        h = h + 0.5 * rmsnorm(f, ffn1_post_g[l])

        hn = rmsnorm(h, mix_pre_g[l])
        proj = hn @ w_in[l]
        da_q, da_k, da_v, mla_cq, mla_ckv, mla_kr, gate_logits = jnp.split(proj, offs, axis=-1)

        lambda_init = 0.8 - 0.6 * math.exp(-0.3 * l)
        lam = (jnp.exp(jnp.sum(da_lambda_q1[l].astype(jnp.float32) * da_lambda_k1[l].astype(jnp.float32)))
               - jnp.exp(jnp.sum(da_lambda_q2[l].astype(jnp.float32) * da_lambda_k2[l].astype(jnp.float32)))
               + lambda_init)
        y_da = diff_attention(da_q, da_k, da_v, lam, lambda_init, da_sub_g[l], rel_bias_table)
        y_mla = mla_attention(mla_cq, mla_ckv, mla_kr, mla_q_norm_g[l], mla_kv_norm_g[l],
                              mla_w_uq[l], mla_w_ukv[l], pos)

        gates = jax.nn.sigmoid((gate_logits + b_gate[l]).astype(jnp.float32)).reshape(B, Lp, N_BRANCH, D_MODEL)
        merged = (gates[:, :, 0] * (y_da @ w_branch_da[l]).astype(jnp.float32)
                  + gates[:, :, 1] * (y_mla @ w_branch_mla[l]).astype(jnp.float32)).astype(h.dtype)
        m = merged @ w_out[l]
        h = h + rmsnorm(m, mix_post_g[l])

        f = swiglu(rmsnorm(h, ffn2_pre_g[l]), ffn2_w_gate[l], ffn2_w_up[l], ffn2_w_down[l])
        h = h + 0.5 * rmsnorm(f, ffn2_post_g[l])

    return h[:, LEAD_PAD + N_META:]
```

```python
import functools
import math

import numpy as np
import jax
import jax.numpy as jnp
from jax import lax
from jax.experimental import pallas as pl
from jax.experimental.pallas import tpu as pltpu

F32 = jnp.float32
BF16 = jnp.bfloat16

N_META = 16
RMS_EPS = 1e-6
NEG_INF = -1e30
HEADS = 8
HEAD_W = 128
DA_HEAD_DIM = 64
MLA_Q_RANK = 768
MLA_KV_RANK = 512
MLA_NOPE = 128
MLA_ROPE = 64
MLA_QK_W = 256
ROPE_THETA = 10000.0
REL_BUCKETS = 32
REL_MAX_DIST = 128
LAMBDA_INIT = 0.8 - 0.6 * math.exp(-0.3 * 0)

V7X_LANES = 128
V7X_BF16_SUBLANES = 16
V7X_VMEM_BYTES = 64 * 1024 * 1024
VMEM_CAP_BYTES = V7X_VMEM_BYTES - 6 * 1024 * 1024

META_BLK = V7X_LANES


def _row_tile(rows, max_tile):
    best = None
    for t in range(V7X_BF16_SUBLANES, max_tile + 1, V7X_BF16_SUBLANES):
        if rows % t == 0:
            best = t
    assert best is not None, (rows, max_tile)
    return best


def _col_tile(cols, max_tile):
    best = None
    for t in range(V7X_LANES, max_tile + 1, V7X_LANES):
        if cols % t == 0:
            best = t
    assert best is not None, (cols, max_tile)
    return best


def _params(semantics, vmem_bytes):
    return pltpu.CompilerParams(
        dimension_semantics=semantics,
        vmem_limit_bytes=int(min(max(vmem_bytes, 16 * 1024 * 1024), VMEM_CAP_BYTES)))


def _resident(block_shape, index_map):
    return pl.BlockSpec(block_shape, index_map, pipeline_mode=pl.Buffered(1))


def _rms(x, g):
    ms = jnp.mean(x * x, axis=-1, keepdims=True)
    return x * lax.rsqrt(ms + RMS_EPS) * g


def _ffn_kernel(x_ref, pre_ref, post_ref, wg_ref, wu_ref, wd_ref, o_ref, xn_ref):
    k = pl.program_id(1)

    @pl.when(k == 0)
    def _():
        xn_ref[...] = _rms(x_ref[...], pre_ref[...]).astype(BF16)

    xn = xn_ref[...]
    g = jnp.dot(xn, wg_ref[...], preferred_element_type=F32)
    u = jnp.dot(xn, wu_ref[...], preferred_element_type=F32)
    a = (g * jax.nn.sigmoid(g) * u).astype(BF16)
    c = jnp.dot(a, wd_ref[...], preferred_element_type=F32)

    @pl.when(k == 0)
    def _():
        o_ref[...] = c

    @pl.when(k > 0)
    def _():
        o_ref[...] += c

    @pl.when(k == pl.num_programs(1) - 1)
    def _():
        o_ref[...] = x_ref[...] + 0.5 * _rms(o_ref[...], post_ref[...])


def _ffn(x, pre_g, post_g, wg, wu, wd, *, name):
    rows, d = x.shape
    f = wg.shape[1]
    tm = _row_tile(rows, 704)
    tf = _col_tile(f, 512)
    vmem = (2 * 2 * tm * d * 4
            + tm * d * 2
            + 2 * 3 * d * tf * 2
            + 4 * tm * tf * 4
            + tm * d * 4)
    return pl.pallas_call(
        _ffn_kernel,
        out_shape=jax.ShapeDtypeStruct((rows, d), F32),
        grid=(rows // tm, f // tf),
        in_specs=[
            pl.BlockSpec((tm, d), lambda i, k: (i, 0)),
            _resident((1, d), lambda i, k: (0, 0)),
            _resident((1, d), lambda i, k: (0, 0)),
            pl.BlockSpec((d, tf), lambda i, k: (0, k)),
            pl.BlockSpec((d, tf), lambda i, k: (0, k)),
            pl.BlockSpec((tf, d), lambda i, k: (k, 0)),
        ],
        out_specs=pl.BlockSpec((tm, d), lambda i, k: (i, 0)),
        scratch_shapes=[pltpu.VMEM((tm, d), BF16)],
        compiler_params=_params(("parallel", "arbitrary"), vmem),
        name=name,
    )(x, pre_g, post_g, wg, wu, wd)


_A_Q, _A_K, _A_V = 0, 1024, 2048
_A_CQ = 3072
_A_CKV = _A_CQ + MLA_Q_RANK
_A_KR1 = _A_CKV + MLA_KV_RANK
_A_KR2 = _A_KR1 + HEAD_W
_A_END = _A_KR2 + HEAD_W
_UQ_HEAD_W = 3 * HEAD_W


def _proj_kernel(h_ref, g_ref, cos_ref, sin_ref, wa_ref, gq_ref, gkv_ref, wuq_ref, wukv_ref,
                 qda_ref, kda_ref, vda_ref, qm_ref, km_ref, vm_ref):
    hn = _rms(h_ref[...], g_ref[...]).astype(BF16)
    cos = cos_ref[...]
    sin = sin_ref[...]

    def proj(lo, hi):
        return jnp.dot(hn, wa_ref[:, lo:hi], preferred_element_type=F32)

    q = proj(_A_Q, _A_K) * (DA_HEAD_DIM ** -0.5)
    k = proj(_A_K, _A_V)
    v = proj(_A_V, _A_CQ)
    for h in range(HEADS):
        sl = slice(h * HEAD_W, (h + 1) * HEAD_W)
        qda_ref[0, h] = q[:, sl].astype(BF16)
        kda_ref[0, h] = k[:, sl].astype(BF16)
        vda_ref[0, h] = v[:, sl].astype(BF16)

    cq = _rms(proj(_A_CQ, _A_CKV), gq_ref[...]).astype(BF16)
    qm = jnp.dot(cq, wuq_ref[...], preferred_element_type=F32)
    mla_scale = (MLA_NOPE + MLA_ROPE) ** -0.5
    for h in range(HEADS):
        base = h * _UQ_HEAD_W
        nope = qm[:, base:base + HEAD_W]
        r1 = qm[:, base + HEAD_W:base + 2 * HEAD_W]
        r2 = qm[:, base + 2 * HEAD_W:base + 3 * HEAD_W]
        qm_ref[0, h, :, 0:HEAD_W] = (nope * mla_scale).astype(BF16)
        qm_ref[0, h, :, HEAD_W:MLA_QK_W] = ((r1 * cos + r2 * sin) * mla_scale).astype(BF16)

    ckv = _rms(proj(_A_CKV, _A_KR1), gkv_ref[...]).astype(BF16)
    kv = jnp.dot(ckv, wukv_ref[...], preferred_element_type=F32)
    k_rope = (proj(_A_KR1, _A_KR2) * cos + proj(_A_KR2, _A_END) * sin).astype(BF16)
    for h in range(HEADS):
        base = h * 2 * HEAD_W
        km_ref[0, h, :, 0:HEAD_W] = kv[:, base:base + HEAD_W].astype(BF16)
        km_ref[0, h, :, HEAD_W:MLA_QK_W] = k_rope
        vm_ref[0, h] = kv[:, base + HEAD_W:base + 2 * HEAD_W].astype(BF16)


def _proj(h1, g, cos, sin, w_a, gq, gkv, w_uq, w_ukv):
    b, lp, d = h1.shape
    tm = _row_tile(lp, 320)
    nt = lp // tm
    head_out = lambda w: pl.BlockSpec((1, HEADS, tm, w), lambda i: (i // nt, 0, i % nt, 0))
    head_shape = lambda w: jax.ShapeDtypeStruct((b, HEADS, lp, w), BF16)
    vmem = (2 * tm * d * 4 + tm * d * 2
            + (d * _A_END + MLA_Q_RANK * HEADS * _UQ_HEAD_W + MLA_KV_RANK * HEADS * 2 * HEAD_W) * 2
            + 2 * tm * HEADS * (4 * HEAD_W + 2 * MLA_QK_W) * 2
            + 3 * tm * HEADS * _UQ_HEAD_W * 4)
    return pl.pallas_call(
        _proj_kernel,
        out_shape=[head_shape(HEAD_W), head_shape(HEAD_W), head_shape(HEAD_W),
                   head_shape(MLA_QK_W), head_shape(MLA_QK_W), head_shape(HEAD_W)],
        grid=(b * nt,),
        in_specs=[
            pl.BlockSpec((None, tm, d), lambda i: (i // nt, i % nt, 0)),
            _resident((1, d), lambda i: (0, 0)),
            pl.BlockSpec((tm, HEAD_W), lambda i: (i % nt, 0)),
            pl.BlockSpec((tm, HEAD_W), lambda i: (i % nt, 0)),
            _resident(w_a.shape, lambda i: (0, 0)),
            _resident((1, MLA_Q_RANK), lambda i: (0, 0)),
            _resident((1, MLA_KV_RANK), lambda i: (0, 0)),
            _resident(w_uq.shape, lambda i: (0, 0)),
            _resident(w_ukv.shape, lambda i: (0, 0)),
        ],
        out_specs=[head_out(HEAD_W), head_out(HEAD_W), head_out(HEAD_W),
                   head_out(MLA_QK_W), head_out(MLA_QK_W), head_out(HEAD_W)],
        compiler_params=_params(("parallel",), vmem),
        name="mixer_in_proj",
    )(h1, g, cos, sin, w_a, gq, gkv, w_uq, w_ukv)


def _t5_bucket(rel):
    n = np.maximum(rel, 0)
    max_exact = REL_BUCKETS // 2
    n_f = np.maximum(n, 1).astype(np.float64)
    large = max_exact + (np.log(n_f / max_exact) / math.log(REL_MAX_DIST / max_exact)
                         * (REL_BUCKETS - max_exact)).astype(np.int32)
    large = np.minimum(large, REL_BUCKETS - 1)
    return np.where(n < max_exact, n, large).astype(np.int32)


def _bucket_tiles(t):
    i = np.arange(t)[:, None]
    j = np.arange(t)[None, :]
    diag = np.where(j <= i, _t5_bucket(i - j), -1)
    sub = _t5_bucket(t + i - j)
    assert _t5_bucket(np.array([t + 1]))[0] == REL_BUCKETS - 1
    jm = np.arange(META_BLK)[None, :]
    meta_first = np.where(jm < N_META, _t5_bucket(N_META + i - jm), -1)
    meta_rest = np.where(jm < N_META, REL_BUCKETS - 1, -1) + 0 * i
    assert _t5_bucket(np.array([t + 1]))[0] == REL_BUCKETS - 1
    return (diag.astype(np.int32), sub.astype(np.int32),
            np.stack([meta_first, meta_rest]).astype(np.int32))


def _bias_kernel(table_ref, bd_ref, bs_ref, bm_ref, od_ref, os_ref, om_ref):
    h = pl.program_id(0)
    far = table_ref[REL_BUCKETS - 1, h]

    def lookup(bucket):
        acc = jnp.zeros(bucket.shape, F32)
        for b in range(REL_BUCKETS - 1):
            acc = jnp.where(bucket == b, table_ref[b, h] - far, acc)
        return jnp.where(bucket < 0, NEG_INF, acc)

    od_ref[0] = lookup(bd_ref[...])
    os_ref[0] = lookup(bs_ref[...])
    om_ref[0, 0] = lookup(bm_ref[0])
    om_ref[0, 1] = lookup(bm_ref[1])


def _bias_tiles(table, t):
    bd, bs, bm = _bucket_tiles(t)
    return pl.pallas_call(
        _bias_kernel,
        out_shape=[jax.ShapeDtypeStruct((HEADS, t, t), F32),
                   jax.ShapeDtypeStruct((HEADS, t, t), F32),
                   jax.ShapeDtypeStruct((HEADS, 2, t, META_BLK), F32)],
        grid=(HEADS,),
        in_specs=[
            pl.BlockSpec(memory_space=pltpu.SMEM),
            pl.BlockSpec((t, t), lambda h: (0, 0)),
            pl.BlockSpec((t, t), lambda h: (0, 0)),
            pl.BlockSpec((2, t, META_BLK), lambda h: (0, 0, 0)),
        ],
        out_specs=[pl.BlockSpec((1, t, t), lambda h: (h, 0, 0)),
                   pl.BlockSpec((1, t, t), lambda h: (h, 0, 0)),
                   pl.BlockSpec((1, 2, t, META_BLK), lambda h: (h, 0, 0, 0))],
        compiler_params=_params(("parallel",), 12 * t * t * 4),
        name="rel_bias_tiles",
    )(table, jnp.asarray(bd), jnp.asarray(bs), jnp.asarray(bm))


def _qk(q, k):
    return lax.dot_general(q, k, (((1,), (1,)), ((), ())), preferred_element_type=F32)


def _softmax_init(s, v, m_ref, l_ref, acc_ref):
    m = jnp.max(s, axis=-1, keepdims=True)
    p = jnp.exp(s - m)
    m_ref[...] = m
    l_ref[...] = jnp.sum(p, axis=-1, keepdims=True)
    acc_ref[...] = jnp.dot(p.astype(BF16), v, preferred_element_type=F32)


def _softmax_update(s, v, m_ref, l_ref, acc_ref):
    m_prev = m_ref[...]
    m_new = jnp.maximum(m_prev, jnp.max(s, axis=-1, keepdims=True))
    alpha = jnp.exp(m_prev - m_new)
    p = jnp.exp(s - m_new)
    l_ref[...] = alpha * l_ref[...] + jnp.sum(p, axis=-1, keepdims=True)
    acc_ref[...] = alpha * acc_ref[...] + jnp.dot(p.astype(BF16), v, preferred_element_type=F32)
    m_ref[...] = m_new


def _da_kernel(q_ref, k_ref, v_ref, bd_ref, bs_ref, bm_ref, lam_ref, subg_ref, o_ref,
               m_ref, l_ref, acc_ref, *, t, s_len):
    qi = pl.program_id(2)
    q = q_ref[0, 0]
    lane = lax.broadcasted_iota(jnp.int32, (1, HEAD_W), 1)
    zero = jnp.zeros_like(q)
    qq = jnp.concatenate([jnp.where(lane < DA_HEAD_DIM, q, zero),
                          jnp.where(lane >= DA_HEAD_DIM, q, zero)], axis=0)

    def scores(k_blk, bias):
        s = _qk(qq, k_blk)
        if bias is not None:
            s = (s.reshape(2, t, -1) + bias[None]).reshape(2 * t, -1)
        return s

    def kv_block(start, size):
        return k_ref[0, 0, pl.ds(start, size), :], v_ref[0, 0, pl.ds(start, size), :]

    k_blk, v_blk = kv_block(s_len, META_BLK)
    _softmax_init(scores(k_blk, bm_ref[0, jnp.minimum(qi, 1)]), v_blk, m_ref, l_ref, acc_ref)

    def far(kb, carry):
        k_blk, v_blk = kv_block(pl.multiple_of(kb * t, t), t)
        _softmax_update(scores(k_blk, None), v_blk, m_ref, l_ref, acc_ref)
        return carry

    lax.fori_loop(0, qi - 1, far, 0)

    @pl.when(qi >= 1)
    def _():
        k_blk, v_blk = kv_block(pl.multiple_of((qi - 1) * t, t), t)
        _softmax_update(scores(k_blk, bs_ref[0]), v_blk, m_ref, l_ref, acc_ref)

    k_blk, v_blk = kv_block(pl.multiple_of(qi * t, t), t)
    _softmax_update(scores(k_blk, bd_ref[0]), v_blk, m_ref, l_ref, acc_ref)

    o = acc_ref[...] / l_ref[...]
    lp = lam_ref[...]
    lam = (jnp.exp(jnp.sum(lp[0:1] * lp[1:2], axis=-1, keepdims=True))
           - jnp.exp(jnp.sum(lp[2:3] * lp[3:4], axis=-1, keepdims=True)) + LAMBDA_INIT)
    d = o[:t] - lam * o[t:]
    o_ref[0] = (_rms(d, subg_ref[...]) * (1.0 - LAMBDA_INIT)).astype(BF16)


def _da_attention(q, k, v, bias_d, bias_s, bias_m, lam_p, sub_g, *, t, s_len):
    b, _, lp, _ = q.shape
    nq = s_len // t
    vmem = (2 * 2 * lp * HEAD_W * 2 + 2 * 2 * t * t * 4 + 2 * 2 * t * META_BLK * 4
            + 2 * 2 * t * HEAD_W * 2 + 3 * 2 * t * HEAD_W * 4
            + 4 * 2 * t * t * 4)
    return pl.pallas_call(
        functools.partial(_da_kernel, t=t, s_len=s_len),
        out_shape=jax.ShapeDtypeStruct((b, s_len, HEADS * HEAD_W), BF16),
        grid=(b, HEADS, nq),
        in_specs=[
            pl.BlockSpec((1, 1, t, HEAD_W), lambda bi, h, qi: (bi, h, qi, 0)),
            pl.BlockSpec((1, 1, lp, HEAD_W), lambda bi, h, qi: (bi, h, 0, 0)),
            pl.BlockSpec((1, 1, lp, HEAD_W), lambda bi, h, qi: (bi, h, 0, 0)),
            pl.BlockSpec((1, t, t), lambda bi, h, qi: (h, 0, 0)),
            pl.BlockSpec((1, t, t), lambda bi, h, qi: (h, 0, 0)),
            pl.BlockSpec((1, 2, t, META_BLK), lambda bi, h, qi: (h, 0, 0, 0)),
            _resident((4, DA_HEAD_DIM), lambda bi, h, qi: (0, 0)),
            _resident((1, HEAD_W), lambda bi, h, qi: (0, 0)),
        ],
        out_specs=pl.BlockSpec((1, t, HEAD_W), lambda bi, h, qi: (bi, qi, h)),
        scratch_shapes=[pltpu.VMEM((2 * t, 1), F32), pltpu.VMEM((2 * t, 1), F32),
                        pltpu.VMEM((2 * t, HEAD_W), F32)],
        compiler_params=_params(("parallel", "parallel", "arbitrary"), vmem),
        name="diff_attention",
    )(q, k, v, bias_d, bias_s, bias_m, lam_p, sub_g)


def _mla_kernel(q_ref, k_ref, v_ref, o_ref, m_ref, l_ref, acc_ref, *, t, s_len):
    qi = pl.program_id(2)
    q = q_ref[0, 0]

    def kv_block(start, size):
        return k_ref[0, 0, pl.ds(start, size), :], v_ref[0, 0, pl.ds(start, size), :]

    k_blk, v_blk = kv_block(s_len, META_BLK)
    key = lax.broadcasted_iota(jnp.int32, (t, META_BLK), 1)
    _softmax_init(jnp.where(key < N_META, _qk(q, k_blk), NEG_INF), v_blk, m_ref, l_ref, acc_ref)

    def far(kb, carry):
        k_blk, v_blk = kv_block(pl.multiple_of(kb * t, t), t)
        _softmax_update(_qk(q, k_blk), v_blk, m_ref, l_ref, acc_ref)
        return carry

    lax.fori_loop(0, qi, far, 0)

    k_blk, v_blk = kv_block(pl.multiple_of(qi * t, t), t)
    row = lax.broadcasted_iota(jnp.int32, (t, t), 0)
    col = lax.broadcasted_iota(jnp.int32, (t, t), 1)
    _softmax_update(jnp.where(col <= row, _qk(q, k_blk), NEG_INF), v_blk, m_ref, l_ref, acc_ref)

    o_ref[0] = (acc_ref[...] / l_ref[...]).astype(BF16)


def _mla_attention(q, k, v, *, t, s_len):
    b, _, lp, _ = q.shape
    nq = s_len // t
    vmem = (2 * lp * (MLA_QK_W + HEAD_W) * 2 + 2 * t * MLA_QK_W * 2
            + 2 * t * HEAD_W * 2 + 3 * t * HEAD_W * 4 + 4 * t * t * 4)
    return pl.pallas_call(
        functools.partial(_mla_kernel, t=t, s_len=s_len),
        out_shape=jax.ShapeDtypeStruct((b, s_len, HEADS * HEAD_W), BF16),
        grid=(b, HEADS, nq),
        in_specs=[
            pl.BlockSpec((1, 1, t, MLA_QK_W), lambda bi, h, qi: (bi, h, qi, 0)),
            pl.BlockSpec((1, 1, lp, MLA_QK_W), lambda bi, h, qi: (bi, h, 0, 0)),
            pl.BlockSpec((1, 1, lp, HEAD_W), lambda bi, h, qi: (bi, h, 0, 0)),
        ],
        out_specs=pl.BlockSpec((1, t, HEAD_W), lambda bi, h, qi: (bi, qi, h)),
        scratch_shapes=[pltpu.VMEM((t, 1), F32), pltpu.VMEM((t, 1), F32),
                        pltpu.VMEM((t, HEAD_W), F32)],
        compiler_params=_params(("parallel", "parallel", "arbitrary"), vmem),
        name="mla_attention",
    )(q, k, v)


def _merge_kernel(h_ref, pre_ref, post_ref, yda_ref, ymla_ref, wg0_ref, wg1_ref, b0_ref, b1_ref,
                  wbd_ref, wbm_ref, wo_ref, o_ref, hn_ref):
    j = pl.program_id(1)

    @pl.when(j == 0)
    def _():
        hn_ref[...] = _rms(h_ref[0], pre_ref[...]).astype(BF16)

    hn = hn_ref[...]
    g0 = jax.nn.sigmoid(jnp.dot(hn, wg0_ref[...], preferred_element_type=F32) + b0_ref[...])
    g1 = jax.nn.sigmoid(jnp.dot(hn, wg1_ref[...], preferred_element_type=F32) + b1_ref[...])
    merged = (g0 * jnp.dot(yda_ref[0], wbd_ref[...], preferred_element_type=F32)
              + g1 * jnp.dot(ymla_ref[0], wbm_ref[...], preferred_element_type=F32))
    c = jnp.dot(merged.astype(BF16), wo_ref[...], preferred_element_type=F32)

    @pl.when(j == 0)
    def _():
        o_ref[0] = c

    @pl.when(j > 0)
    def _():
        o_ref[0] += c

    @pl.when(j == pl.num_programs(1) - 1)
    def _():
        o_ref[0] = h_ref[0] + _rms(o_ref[0], post_ref[...])


def _merge(h1, pre_g, post_g, y_da, y_mla, w_gate, b_gate, wb_da, wb_mla, w_out, *, s_len):
    b, _, d = h1.shape
    yw = y_da.shape[-1]
    tm = _row_tile(s_len, 512)
    tn = _col_tile(d, 512)
    nt = s_len // tm
    nj = d // tn
    vmem = (2 * 2 * tm * d * 4 + tm * d * 2 + 2 * 2 * tm * yw * 2
            + 2 * (3 * d * tn + 2 * yw * tn) * 2 + 6 * tm * tn * 4 + tm * d * 4)
    return pl.pallas_call(
        _merge_kernel,
        out_shape=jax.ShapeDtypeStruct((b, s_len, d), F32),
        grid=(b * nt, nj),
        in_specs=[
            pl.BlockSpec((1, tm, d), lambda i, j: (i // nt, i % nt, 0)),
            _resident((1, d), lambda i, j: (0, 0)),
            _resident((1, d), lambda i, j: (0, 0)),
            pl.BlockSpec((1, tm, yw), lambda i, j: (i // nt, i % nt, 0)),
            pl.BlockSpec((1, tm, yw), lambda i, j: (i // nt, i % nt, 0)),
            pl.BlockSpec((d, tn), lambda i, j: (0, j)),
            pl.BlockSpec((d, tn), lambda i, j: (0, nj + j)),
            pl.BlockSpec((1, tn), lambda i, j: (0, j)),
            pl.BlockSpec((1, tn), lambda i, j: (0, nj + j)),
            pl.BlockSpec((yw, tn), lambda i, j: (0, j)),
            pl.BlockSpec((yw, tn), lambda i, j: (0, j)),
            pl.BlockSpec((tn, d), lambda i, j: (j, 0)),
        ],
        out_specs=pl.BlockSpec((1, tm, d), lambda i, j: (i // nt, i % nt, 0)),
        scratch_shapes=[pltpu.VMEM((tm, d), BF16)],
        compiler_params=_params(("parallel", "arbitrary"), vmem),
        name="gated_merge_out_proj",
    )(h1, pre_g, post_g, y_da, y_mla, w_gate, w_gate, b_gate, b_gate, wb_da, wb_mla, w_out)


def _rope_tables(s_len):
    half = MLA_ROPE // 2
    pos = jnp.concatenate([N_META + jnp.arange(s_len), jnp.arange(N_META),
                           jnp.zeros((META_BLK - N_META,), jnp.int32)]).astype(F32)
    inv = ROPE_THETA ** (-jnp.arange(half, dtype=F32) * 2.0 / MLA_ROPE)
    ang = pos[:, None] * inv[None, :]
    cos, sin = jnp.cos(ang), jnp.sin(ang)
    pad = jnp.zeros((pos.shape[0], HEAD_W - MLA_ROPE), F32)
    return (jnp.concatenate([cos, cos, pad], axis=-1),
            jnp.concatenate([-sin, sin, pad], axis=-1))


def _swap_halves(w):
    half = w.shape[-1] // 2
    return jnp.concatenate([w[..., half:], w[..., :half]], axis=-1)


def _pad_lanes(w, width):
    return jnp.concatenate([w, jnp.zeros(w.shape[:-1] + (width - w.shape[-1],), w.dtype)], axis=-1)


def kernel(x, meta_tokens, rel_bias_table, ffn1_pre_g, ffn1_post_g, ffn1_w_gate, ffn1_w_up, ffn1_w_down, mix_pre_g, mix_post_g, w_in, b_gate, da_lambda_q1, da_lambda_k1, da_lambda_q2, da_lambda_k2, da_sub_g, mla_q_norm_g, mla_kv_norm_g, mla_w_uq, mla_w_ukv, w_branch_da, w_branch_mla, w_out, ffn2_pre_g, ffn2_post_g, ffn2_w_gate, ffn2_w_up, ffn2_w_down):
    b, s_len, d = x.shape
    assert ffn1_pre_g.shape[0] == 1, "single-layer trunk"
    lp = s_len + META_BLK
    t = _row_tile(s_len, 512)
    assert t >= V7X_LANES
    row = lambda g: g.reshape(1, -1).astype(F32)

    meta = jnp.broadcast_to(meta_tokens.astype(x.dtype)[None], (b, N_META, d))
    h0 = jnp.concatenate([x, meta, jnp.zeros((b, META_BLK - N_META, d), x.dtype)], axis=1)

    h1 = _ffn(h0.reshape(b * lp, d), row(ffn1_pre_g), row(ffn1_post_g),
              ffn1_w_gate[0].astype(BF16), ffn1_w_up[0].astype(BF16), ffn1_w_down[0].astype(BF16),
              name="ffn1").reshape(b, lp, d)

    w = w_in[0]
    kr = w[:, _A_KR1:_A_KR1 + MLA_ROPE]
    gate_off = _A_KR1 + MLA_ROPE
    w_a = jnp.concatenate([w[:, :_A_KR1], _pad_lanes(kr, HEAD_W),
                           _pad_lanes(_swap_halves(kr), HEAD_W)], axis=-1).astype(BF16)
    uq = mla_w_uq[0].reshape(MLA_Q_RANK, HEADS, MLA_NOPE + MLA_ROPE)
    uq_rope = uq[..., MLA_NOPE:]
    w_uq = jnp.concatenate([uq[..., :MLA_NOPE], _pad_lanes(uq_rope, HEAD_W),
                            _pad_lanes(_swap_halves(uq_rope), HEAD_W)],
                           axis=-1).reshape(MLA_Q_RANK, HEADS * _UQ_HEAD_W).astype(BF16)
    cos, sin = _rope_tables(s_len)

    q_da, k_da, v_da, q_mla, k_mla, v_mla = _proj(
        h1, row(mix_pre_g), cos, sin, w_a, row(mla_q_norm_g), row(mla_kv_norm_g),
        w_uq, mla_w_ukv[0].astype(BF16))

    bias_d, bias_s, bias_m = _bias_tiles(rel_bias_table.astype(F32), t)
    lam_p = jnp.concatenate([da_lambda_q1, da_lambda_k1, da_lambda_q2, da_lambda_k2], axis=0).astype(F32)
    y_da = _da_attention(q_da, k_da, v_da, bias_d, bias_s, bias_m, lam_p, row(da_sub_g),
                         t=t, s_len=s_len)
    y_mla = _mla_attention(q_mla, k_mla, v_mla, t=t, s_len=s_len)

    h2 = _merge(h1, row(mix_pre_g), row(mix_post_g), y_da, y_mla,
                w[:, gate_off:].astype(BF16), row(b_gate),
                w_branch_da[0].astype(BF16), w_branch_mla[0].astype(BF16), w_out[0].astype(BF16),
                s_len=s_len)

    out = _ffn(h2.reshape(b * s_len, d), row(ffn2_pre_g), row(ffn2_post_g),
               ffn2_w_gate[0].astype(BF16), ffn2_w_up[0].astype(BF16), ffn2_w_down[0].astype(BF16),
               name="ffn2")
    return out.reshape(b, s_len, d)
```

```python
import functools
import math

import numpy as np
import jax
import jax.numpy as jnp
from jax import lax
from jax.experimental import pallas as pl
from jax.experimental.pallas import tpu as pltpu

F32 = jnp.float32
BF16 = jnp.bfloat16

N_META = 16
RMS_EPS = 1e-6
NEG_INF = -1e30
HEADS = 8
HEAD_W = 128
DA_HEAD_DIM = 64
MLA_Q_RANK = 768
MLA_KV_RANK = 512
MLA_NOPE = 128
MLA_ROPE = 64
MLA_QK_W = 256
ROPE_THETA = 10000.0
REL_BUCKETS = 32
REL_MAX_DIST = 128
LAMBDA_INIT = 0.8 - 0.6 * math.exp(-0.3 * 0)

V7X_LANES = 128
V7X_BF16_SUBLANES = 16
V7X_VMEM_BYTES = 64 * 1024 * 1024
VMEM_CAP_BYTES = V7X_VMEM_BYTES - 6 * 1024 * 1024

META_BLK = V7X_LANES


def _row_tile(rows, max_tile):
    best = None
    for t in range(V7X_BF16_SUBLANES, max_tile + 1, V7X_BF16_SUBLANES):
        if rows % t == 0:
            best = t
    assert best is not None, (rows, max_tile)
    return best


def _col_tile(cols, max_tile):
    best = None
    for t in range(V7X_LANES, max_tile + 1, V7X_LANES):
        if cols % t == 0:
            best = t
    assert best is not None, (cols, max_tile)
    return best


def _params(semantics, vmem_bytes):
    return pltpu.CompilerParams(
        dimension_semantics=semantics,
        vmem_limit_bytes=int(min(max(vmem_bytes, 16 * 1024 * 1024), VMEM_CAP_BYTES)))


def _resident(block_shape, index_map):
    return pl.BlockSpec(block_shape, index_map, pipeline_mode=pl.Buffered(1))


def _rms(x, g):
    ms = jnp.mean(x * x, axis=-1, keepdims=True)
    return x * lax.rsqrt(ms + RMS_EPS) * g


def _ffn_kernel(x_ref, pre_ref, post_ref, wg_ref, wu_ref, wd_ref, o_ref, xn_ref):
    k = pl.program_id(1)

    @pl.when(k == 0)
    def _():
        xn_ref[...] = _rms(x_ref[...], pre_ref[...]).astype(BF16)

    xn = xn_ref[...]
    g = jnp.dot(xn, wg_ref[...], preferred_element_type=F32)
    u = jnp.dot(xn, wu_ref[...], preferred_element_type=F32)
    a = (g * jax.nn.sigmoid(g) * u).astype(BF16)
    c = jnp.dot(a, wd_ref[...], preferred_element_type=F32)

    @pl.when(k == 0)
    def _():
        o_ref[...] = c

    @pl.when(k > 0)
    def _():
        o_ref[...] += c

    @pl.when(k == pl.num_programs(1) - 1)
    def _():
        o_ref[...] = x_ref[...] + 0.5 * _rms(o_ref[...], post_ref[...])


def _ffn(x, pre_g, post_g, wg, wu, wd, *, name):
    rows, d = x.shape
    f = wg.shape[1]
    tm = _row_tile(rows, 704)
    tf = _col_tile(f, 512)
    vmem = (2 * 2 * tm * d * 4
            + tm * d * 2
            + 2 * 3 * d * tf * 2
            + 4 * tm * tf * 4
            + tm * d * 4)
    return pl.pallas_call(
        _ffn_kernel,
        out_shape=jax.ShapeDtypeStruct((rows, d), F32),
        grid=(rows // tm, f // tf),
        in_specs=[
            pl.BlockSpec((tm, d), lambda i, k: (i, 0)),
            _resident((1, d), lambda i, k: (0, 0)),
            _resident((1, d), lambda i, k: (0, 0)),
            pl.BlockSpec((d, tf), lambda i, k: (0, k)),
            pl.BlockSpec((d, tf), lambda i, k: (0, k)),
            pl.BlockSpec((tf, d), lambda i, k: (k, 0)),
        ],
        out_specs=pl.BlockSpec((tm, d), lambda i, k: (i, 0)),
        scratch_shapes=[pltpu.VMEM((tm, d), BF16)],
        compiler_params=_params(("parallel", "arbitrary"), vmem),
        name=name,
    )(x, pre_g, post_g, wg, wu, wd)


_A_Q, _A_K, _A_V = 0, 1024, 2048
_A_CQ = 3072
_A_CKV = _A_CQ + MLA_Q_RANK
_A_KR1 = _A_CKV + MLA_KV_RANK
_A_KR2 = _A_KR1 + HEAD_W
_A_END = _A_KR2 + HEAD_W
_UQ_HEAD_W = 3 * HEAD_W


def _proj_kernel(h_ref, g_ref, cos_ref, sin_ref, wa_ref, gq_ref, gkv_ref, wuq_ref, wukv_ref,
                 qda_ref, kda_ref, vda_ref, qm_ref, km_ref, vm_ref):
    hn = _rms(h_ref[...], g_ref[...]).astype(BF16)
    cos = cos_ref[...]
    sin = sin_ref[...]

    def proj(lo, hi):
        return jnp.dot(hn, wa_ref[:, lo:hi], preferred_element_type=F32)

    q = proj(_A_Q, _A_K) * (DA_HEAD_DIM ** -0.5)
    k = proj(_A_K, _A_V)
    v = proj(_A_V, _A_CQ)
    for h in range(HEADS):
        sl = slice(h * HEAD_W, (h + 1) * HEAD_W)
        qda_ref[0, h] = q[:, sl].astype(BF16)
        kda_ref[0, h] = k[:, sl].astype(BF16)
        vda_ref[0, h] = v[:, sl].astype(BF16)

    cq = _rms(proj(_A_CQ, _A_CKV), gq_ref[...]).astype(BF16)
    qm = jnp.dot(cq, wuq_ref[...], preferred_element_type=F32)
    mla_scale = (MLA_NOPE + MLA_ROPE) ** -0.5
    for h in range(HEADS):
        base = h * _UQ_HEAD_W
        nope = qm[:, base:base + HEAD_W]
        r1 = qm[:, base + HEAD_W:base + 2 * HEAD_W]
        r2 = qm[:, base + 2 * HEAD_W:base + 3 * HEAD_W]
        qm_ref[0, h, :, 0:HEAD_W] = (nope * mla_scale).astype(BF16)
        qm_ref[0, h, :, HEAD_W:MLA_QK_W] = ((r1 * cos + r2 * sin) * mla_scale).astype(BF16)

    ckv = _rms(proj(_A_CKV, _A_KR1), gkv_ref[...]).astype(BF16)
    kv = jnp.dot(ckv, wukv_ref[...], preferred_element_type=F32)
    k_rope = (proj(_A_KR1, _A_KR2) * cos + proj(_A_KR2, _A_END) * sin).astype(BF16)
    for h in range(HEADS):
        base = h * 2 * HEAD_W
        km_ref[0, h, :, 0:HEAD_W] = kv[:, base:base + HEAD_W].astype(BF16)
        km_ref[0, h, :, HEAD_W:MLA_QK_W] = k_rope
        vm_ref[0, h] = kv[:, base + HEAD_W:base + 2 * HEAD_W].astype(BF16)


def _proj(h1, g, cos, sin, w_a, gq, gkv, w_uq, w_ukv):
    b, lp, d = h1.shape
    tm = _row_tile(lp, 320)
    nt = lp // tm
    head_out = lambda w: pl.BlockSpec((1, HEADS, tm, w), lambda i: (i // nt, 0, i % nt, 0))
    head_shape = lambda w: jax.ShapeDtypeStruct((b, HEADS, lp, w), BF16)
    vmem = (2 * tm * d * 4 + tm * d * 2
            + (d * _A_END + MLA_Q_RANK * HEADS * _UQ_HEAD_W + MLA_KV_RANK * HEADS * 2 * HEAD_W) * 2
            + 2 * tm * HEADS * (4 * HEAD_W + 2 * MLA_QK_W) * 2
            + 3 * tm * HEADS * _UQ_HEAD_W * 4)
    return pl.pallas_call(
        _proj_kernel,
        out_shape=[head_shape(HEAD_W), head_shape(HEAD_W), head_shape(HEAD_W),
                   head_shape(MLA_QK_W), head_shape(MLA_QK_W), head_shape(HEAD_W)],
        grid=(b * nt,),
        in_specs=[
            pl.BlockSpec((None, tm, d), lambda i: (i // nt, i % nt, 0)),
            _resident((1, d), lambda i: (0, 0)),
            pl.BlockSpec((tm, HEAD_W), lambda i: (i % nt, 0)),
            pl.BlockSpec((tm, HEAD_W), lambda i: (i % nt, 0)),
            _resident(w_a.shape, lambda i: (0, 0)),
            _resident((1, MLA_Q_RANK), lambda i: (0, 0)),
            _resident((1, MLA_KV_RANK), lambda i: (0, 0)),
            _resident(w_uq.shape, lambda i: (0, 0)),
            _resident(w_ukv.shape, lambda i: (0, 0)),
        ],
        out_specs=[head_out(HEAD_W), head_out(HEAD_W), head_out(HEAD_W),
                   head_out(MLA_QK_W), head_out(MLA_QK_W), head_out(HEAD_W)],
        compiler_params=_params(("parallel",), vmem),
        name="mixer_in_proj",
    )(h1, g, cos, sin, w_a, gq, gkv, w_uq, w_ukv)


def _t5_bucket(rel):
    n = np.maximum(rel, 0)
    max_exact = REL_BUCKETS // 2
    n_f = np.maximum(n, 1).astype(np.float64)
    large = max_exact + (np.log(n_f / max_exact) / math.log(REL_MAX_DIST / max_exact)
                         * (REL_BUCKETS - max_exact)).astype(np.int32)
    large = np.minimum(large, REL_BUCKETS - 1)
    return np.where(n < max_exact, n, large).astype(np.int32)


def _bucket_tiles(t):
    i = np.arange(t)[:, None]
    j = np.arange(t)[None, :]
    diag = np.where(j <= i, _t5_bucket(i - j), -1)
    sub = _t5_bucket(t + i - j)
    assert _t5_bucket(np.array([t + 1]))[0] == REL_BUCKETS - 1
    jm = np.arange(META_BLK)[None, :]
    meta_first = np.where(jm < N_META, _t5_bucket(N_META + i - jm), -1)
    meta_rest = np.where(jm < N_META, REL_BUCKETS - 1, -1) + 0 * i
    assert _t5_bucket(np.array([t + 1]))[0] == REL_BUCKETS - 1
    return (diag.astype(np.int32), sub.astype(np.int32),
            np.stack([meta_first, meta_rest]).astype(np.int32))


def _bias_kernel(table_ref, bd_ref, bs_ref, bm_ref, od_ref, os_ref, om_ref):
    h = pl.program_id(0)
    far = table_ref[REL_BUCKETS - 1, h]

    def lookup(bucket):
        acc = jnp.zeros(bucket.shape, F32)
        for b in range(REL_BUCKETS - 1):
            acc = jnp.where(bucket == b, table_ref[b, h] - far, acc)
        return jnp.where(bucket < 0, NEG_INF, acc)

    od_ref[0] = lookup(bd_ref[...])
    os_ref[0] = lookup(bs_ref[...])
    om_ref[0, 0] = lookup(bm_ref[0])
    om_ref[0, 1] = lookup(bm_ref[1])


def _bias_tiles(table, t):
    bd, bs, bm = _bucket_tiles(t)
    return pl.pallas_call(
        _bias_kernel,
        out_shape=[jax.ShapeDtypeStruct((HEADS, t, t), F32),
                   jax.ShapeDtypeStruct((HEADS, t, t), F32),
                   jax.ShapeDtypeStruct((HEADS, 2, t, META_BLK), F32)],
        grid=(HEADS,),
        in_specs=[
            pl.BlockSpec(memory_space=pltpu.SMEM),
            pl.BlockSpec((t, t), lambda h: (0, 0)),
            pl.BlockSpec((t, t), lambda h: (0, 0)),
            pl.BlockSpec((2, t, META_BLK), lambda h: (0, 0, 0)),
        ],
        out_specs=[pl.BlockSpec((1, t, t), lambda h: (h, 0, 0)),
                   pl.BlockSpec((1, t, t), lambda h: (h, 0, 0)),
                   pl.BlockSpec((1, 2, t, META_BLK), lambda h: (h, 0, 0, 0))],
        compiler_params=_params(("parallel",), 12 * t * t * 4),
        name="rel_bias_tiles",
    )(table, jnp.asarray(bd), jnp.asarray(bs), jnp.asarray(bm))


def _qk(q, k):
    return lax.dot_general(q, k, (((1,), (1,)), ((), ())), preferred_element_type=F32)


ATTN_ROW_CHUNK = 256


def _attend_block(q_ref, k_blk, v_blk, bias_fn, m_ref, l_ref, acc_ref, *, first):
    rows = q_ref.shape[0]
    rc = min(ATTN_ROW_CHUNK, rows)
    for c in range(rows // rc):
        sl = pl.ds(c * rc, rc)
        s = _qk(q_ref[sl, :], k_blk)
        bias = bias_fn(c) if bias_fn is not None else None
        if bias is not None:
            s = s + bias
        s_max = jnp.max(s, axis=-1, keepdims=True)
        if first:
            p = jnp.exp(s - s_max)
            m_ref[sl, :] = s_max
            l_ref[sl, :] = jnp.sum(p, axis=-1, keepdims=True)
            acc_ref[sl, :] = jnp.dot(p.astype(BF16), v_blk, preferred_element_type=F32)
        else:
            m_prev = m_ref[sl, :]
            m_new = jnp.maximum(m_prev, s_max)
            alpha = jnp.exp(m_prev - m_new)
            p = jnp.exp(s - m_new)
            l_ref[sl, :] = alpha * l_ref[sl, :] + jnp.sum(p, axis=-1, keepdims=True)
            acc_ref[sl, :] = alpha * acc_ref[sl, :] + jnp.dot(p.astype(BF16), v_blk,
                                                              preferred_element_type=F32)
            m_ref[sl, :] = m_new


def _da_kernel(q_ref, k_ref, v_ref, bd_ref, bs_ref, bm_ref, lam_ref, subg_ref, o_ref,
               qq_ref, m_ref, l_ref, acc_ref, *, t, s_len):
    qi = pl.program_id(2)
    q = q_ref[0, 0]
    lane = lax.broadcasted_iota(jnp.int32, (1, HEAD_W), 1)
    zero = jnp.zeros_like(q)
    qq_ref[0:t, :] = jnp.where(lane < DA_HEAD_DIM, q, zero)
    qq_ref[t:2 * t, :] = jnp.where(lane >= DA_HEAD_DIM, q, zero)
    rc = min(ATTN_ROW_CHUNK, t)
    chunks_per_map = t // rc

    def bias_rows(read):
        return lambda c: read(pl.ds((c % chunks_per_map) * rc, rc))

    def kv_block(start, size):
        return k_ref[0, 0, pl.ds(start, size), :], v_ref[0, 0, pl.ds(start, size), :]

    state = (m_ref, l_ref, acc_ref)
    meta_sel = jnp.minimum(qi, 1)
    k_blk, v_blk = kv_block(s_len, META_BLK)
    _attend_block(qq_ref, k_blk, v_blk, bias_rows(lambda r: bm_ref[0, meta_sel, r, :]), *state, first=True)

    def far(kb, carry):
        k_blk, v_blk = kv_block(pl.multiple_of(kb * t, t), t)
        _attend_block(qq_ref, k_blk, v_blk, None, *state, first=False)
        return carry

    lax.fori_loop(0, qi - 1, far, 0)

    @pl.when(qi >= 1)
    def _():
        k_blk, v_blk = kv_block(pl.multiple_of((qi - 1) * t, t), t)
        _attend_block(qq_ref, k_blk, v_blk, bias_rows(lambda r: bs_ref[0, r, :]), *state, first=False)

    k_blk, v_blk = kv_block(pl.multiple_of(qi * t, t), t)
    _attend_block(qq_ref, k_blk, v_blk, bias_rows(lambda r: bd_ref[0, r, :]), *state, first=False)

    o = acc_ref[...] / l_ref[...]
    lp = lam_ref[...]
    lam = (jnp.exp(jnp.sum(lp[0:1] * lp[1:2], axis=-1, keepdims=True))
           - jnp.exp(jnp.sum(lp[2:3] * lp[3:4], axis=-1, keepdims=True)) + LAMBDA_INIT)
    d = o[:t] - lam * o[t:]
    o_ref[0] = (_rms(d, subg_ref[...]) * (1.0 - LAMBDA_INIT)).astype(BF16)


def _da_attention(q, k, v, bias_d, bias_s, bias_m, lam_p, sub_g, *, t, s_len):
    b, _, lp, _ = q.shape
    nq = s_len // t
    vmem = (2 * 2 * lp * HEAD_W * 2 + 2 * 2 * t * t * 4 + 2 * 2 * t * META_BLK * 4
            + 2 * 2 * t * HEAD_W * 2 + 3 * 2 * t * HEAD_W * 4
            + 4 * 2 * t * t * 4)
    return pl.pallas_call(
        functools.partial(_da_kernel, t=t, s_len=s_len),
        out_shape=jax.ShapeDtypeStruct((b, s_len, HEADS * HEAD_W), BF16),
        grid=(b, HEADS, nq),
        in_specs=[
            pl.BlockSpec((1, 1, t, HEAD_W), lambda bi, h, qi: (bi, h, qi, 0)),
            pl.BlockSpec((1, 1, lp, HEAD_W), lambda bi, h, qi: (bi, h, 0, 0)),
            pl.BlockSpec((1, 1, lp, HEAD_W), lambda bi, h, qi: (bi, h, 0, 0)),
            pl.BlockSpec((1, t, t), lambda bi, h, qi: (h, 0, 0)),
            pl.BlockSpec((1, t, t), lambda bi, h, qi: (h, 0, 0)),
            pl.BlockSpec((1, 2, t, META_BLK), lambda bi, h, qi: (h, 0, 0, 0)),
            _resident((4, DA_HEAD_DIM), lambda bi, h, qi: (0, 0)),
            _resident((1, HEAD_W), lambda bi, h, qi: (0, 0)),
        ],
        out_specs=pl.BlockSpec((1, t, HEAD_W), lambda bi, h, qi: (bi, qi, h)),
        scratch_shapes=[pltpu.VMEM((2 * t, HEAD_W), BF16),
                        pltpu.VMEM((2 * t, 1), F32), pltpu.VMEM((2 * t, 1), F32),
                        pltpu.VMEM((2 * t, HEAD_W), F32)],
        compiler_params=_params(("parallel", "parallel", "arbitrary"), vmem),
        name="diff_attention",
    )(q, k, v, bias_d, bias_s, bias_m, lam_p, sub_g)


def _mla_kernel(q_ref, k_ref, v_ref, o_ref, m_ref, l_ref, acc_ref, *, t, s_len):
    qi = pl.program_id(2)
    qv_ref = q_ref.at[0, 0]
    rc = min(ATTN_ROW_CHUNK, t)

    def kv_block(start, size):
        return k_ref[0, 0, pl.ds(start, size), :], v_ref[0, 0, pl.ds(start, size), :]

    state = (m_ref, l_ref, acc_ref)
    k_blk, v_blk = kv_block(s_len, META_BLK)
    key = lax.broadcasted_iota(jnp.int32, (rc, META_BLK), 1)
    meta_mask = jnp.where(key < N_META, 0.0, NEG_INF)
    _attend_block(qv_ref, k_blk, v_blk, lambda c: meta_mask, *state, first=True)

    def far(kb, carry):
        k_blk, v_blk = kv_block(pl.multiple_of(kb * t, t), t)
        _attend_block(qv_ref, k_blk, v_blk, None, *state, first=False)
        return carry

    lax.fori_loop(0, qi, far, 0)

    k_blk, v_blk = kv_block(pl.multiple_of(qi * t, t), t)
    row = lax.broadcasted_iota(jnp.int32, (rc, t), 0)
    col = lax.broadcasted_iota(jnp.int32, (rc, t), 1)
    _attend_block(qv_ref, k_blk, v_blk,
                  lambda c: jnp.where(col <= row + c * rc, 0.0, NEG_INF), *state, first=False)

    o_ref[0] = (acc_ref[...] / l_ref[...]).astype(BF16)


def _mla_attention(q, k, v, *, t, s_len):
    b, _, lp, _ = q.shape
    nq = s_len // t
    vmem = (2 * lp * (MLA_QK_W + HEAD_W) * 2 + 2 * t * MLA_QK_W * 2
            + 2 * t * HEAD_W * 2 + 3 * t * HEAD_W * 4 + 4 * t * t * 4)
    return pl.pallas_call(
        functools.partial(_mla_kernel, t=t, s_len=s_len),
        out_shape=jax.ShapeDtypeStruct((b, s_len, HEADS * HEAD_W), BF16),
        grid=(b, HEADS, nq),
        in_specs=[
            pl.BlockSpec((1, 1, t, MLA_QK_W), lambda bi, h, qi: (bi, h, qi, 0)),
            pl.BlockSpec((1, 1, lp, MLA_QK_W), lambda bi, h, qi: (bi, h, 0, 0)),
            pl.BlockSpec((1, 1, lp, HEAD_W), lambda bi, h, qi: (bi, h, 0, 0)),
        ],
        out_specs=pl.BlockSpec((1, t, HEAD_W), lambda bi, h, qi: (bi, qi, h)),
        scratch_shapes=[pltpu.VMEM((t, 1), F32), pltpu.VMEM((t, 1), F32),
                        pltpu.VMEM((t, HEAD_W), F32)],
        compiler_params=_params(("parallel", "parallel", "arbitrary"), vmem),
        name="mla_attention",
    )(q, k, v)


def _merge_kernel(h_ref, pre_ref, post_ref, yda_ref, ymla_ref, wg0_ref, wg1_ref, b0_ref, b1_ref,
                  wbd_ref, wbm_ref, wo_ref, o_ref, hn_ref):
    j = pl.program_id(1)

    @pl.when(j == 0)
    def _():
        hn_ref[...] = _rms(h_ref[0], pre_ref[...]).astype(BF16)

    hn = hn_ref[...]
    g0 = jax.nn.sigmoid(jnp.dot(hn, wg0_ref[...], preferred_element_type=F32) + b0_ref[...])
    g1 = jax.nn.sigmoid(jnp.dot(hn, wg1_ref[...], preferred_element_type=F32) + b1_ref[...])
    merged = (g0 * jnp.dot(yda_ref[0], wbd_ref[...], preferred_element_type=F32)
              + g1 * jnp.dot(ymla_ref[0], wbm_ref[...], preferred_element_type=F32))
    c = jnp.dot(merged.astype(BF16), wo_ref[...], preferred_element_type=F32)

    @pl.when(j == 0)
    def _():
        o_ref[0] = c

    @pl.when(j > 0)
    def _():
        o_ref[0] += c

    @pl.when(j == pl.num_programs(1) - 1)
    def _():
        o_ref[0] = h_ref[0] + _rms(o_ref[0], post_ref[...])


def _merge(h1, pre_g, post_g, y_da, y_mla, w_gate, b_gate, wb_da, wb_mla, w_out, *, s_len):
    b, _, d = h1.shape
    yw = y_da.shape[-1]
    tm = _row_tile(s_len, 512)
    tn = _col_tile(d, 512)
    nt = s_len // tm
    nj = d // tn
    vmem = (2 * 2 * tm * d * 4 + tm * d * 2 + 2 * 2 * tm * yw * 2
            + 2 * (3 * d * tn + 2 * yw * tn) * 2 + 6 * tm * tn * 4 + tm * d * 4)
    return pl.pallas_call(
        _merge_kernel,
        out_shape=jax.ShapeDtypeStruct((b, s_len, d), F32),
        grid=(b * nt, nj),
        in_specs=[
            pl.BlockSpec((1, tm, d), lambda i, j: (i // nt, i % nt, 0)),
            _resident((1, d), lambda i, j: (0, 0)),
            _resident((1, d), lambda i, j: (0, 0)),
            pl.BlockSpec((1, tm, yw), lambda i, j: (i // nt, i % nt, 0)),
            pl.BlockSpec((1, tm, yw), lambda i, j: (i // nt, i % nt, 0)),
            pl.BlockSpec((d, tn), lambda i, j: (0, j)),
            pl.BlockSpec((d, tn), lambda i, j: (0, nj + j)),
            pl.BlockSpec((1, tn), lambda i, j: (0, j)),
            pl.BlockSpec((1, tn), lambda i, j: (0, nj + j)),
            pl.BlockSpec((yw, tn), lambda i, j: (0, j)),
            pl.BlockSpec((yw, tn), lambda i, j: (0, j)),
            pl.BlockSpec((tn, d), lambda i, j: (j, 0)),
        ],
        out_specs=pl.BlockSpec((1, tm, d), lambda i, j: (i // nt, i % nt, 0)),
        scratch_shapes=[pltpu.VMEM((tm, d), BF16)],
        compiler_params=_params(("parallel", "arbitrary"), vmem),
        name="gated_merge_out_proj",
    )(h1, pre_g, post_g, y_da, y_mla, w_gate, w_gate, b_gate, b_gate, wb_da, wb_mla, w_out)


def _rope_tables(s_len):
    half = MLA_ROPE // 2
    pos = jnp.concatenate([N_META + jnp.arange(s_len), jnp.arange(N_META),
                           jnp.zeros((META_BLK - N_META,), jnp.int32)]).astype(F32)
    inv = ROPE_THETA ** (-jnp.arange(half, dtype=F32) * 2.0 / MLA_ROPE)
    ang = pos[:, None] * inv[None, :]
    cos, sin = jnp.cos(ang), jnp.sin(ang)
    pad = jnp.zeros((pos.shape[0], HEAD_W - MLA_ROPE), F32)
    return (jnp.concatenate([cos, cos, pad], axis=-1),
            jnp.concatenate([-sin, sin, pad], axis=-1))


def _swap_halves(w):
    half = w.shape[-1] // 2
    return jnp.concatenate([w[..., half:], w[..., :half]], axis=-1)


def _pad_lanes(w, width):
    return jnp.concatenate([w, jnp.zeros(w.shape[:-1] + (width - w.shape[-1],), w.dtype)], axis=-1)


def kernel(x, meta_tokens, rel_bias_table, ffn1_pre_g, ffn1_post_g, ffn1_w_gate, ffn1_w_up, ffn1_w_down, mix_pre_g, mix_post_g, w_in, b_gate, da_lambda_q1, da_lambda_k1, da_lambda_q2, da_lambda_k2, da_sub_g, mla_q_norm_g, mla_kv_norm_g, mla_w_uq, mla_w_ukv, w_branch_da, w_branch_mla, w_out, ffn2_pre_g, ffn2_post_g, ffn2_w_gate, ffn2_w_up, ffn2_w_down):
    b, s_len, d = x.shape
    assert ffn1_pre_g.shape[0] == 1, "single-layer trunk"
    lp = s_len + META_BLK
    t = _row_tile(s_len, 512)
    assert t >= V7X_LANES
    row = lambda g: g.reshape(1, -1).astype(F32)

    meta = jnp.broadcast_to(meta_tokens.astype(x.dtype)[None], (b, N_META, d))
    h0 = jnp.concatenate([x, meta, jnp.zeros((b, META_BLK - N_META, d), x.dtype)], axis=1)

    h1 = _ffn(h0.reshape(b * lp, d), row(ffn1_pre_g), row(ffn1_post_g),
              ffn1_w_gate[0].astype(BF16), ffn1_w_up[0].astype(BF16), ffn1_w_down[0].astype(BF16),
              name="ffn1").reshape(b, lp, d)

    w = w_in[0]
    kr = w[:, _A_KR1:_A_KR1 + MLA_ROPE]
    gate_off = _A_KR1 + MLA_ROPE
    w_a = jnp.concatenate([w[:, :_A_KR1], _pad_lanes(kr, HEAD_W),
                           _pad_lanes(_swap_halves(kr), HEAD_W)], axis=-1).astype(BF16)
    uq = mla_w_uq[0].reshape(MLA_Q_RANK, HEADS, MLA_NOPE + MLA_ROPE)
    uq_rope = uq[..., MLA_NOPE:]
    w_uq = jnp.concatenate([uq[..., :MLA_NOPE], _pad_lanes(uq_rope, HEAD_W),
                            _pad_lanes(_swap_halves(uq_rope), HEAD_W)],
                           axis=-1).reshape(MLA_Q_RANK, HEADS * _UQ_HEAD_W).astype(BF16)
    cos, sin = _rope_tables(s_len)

    q_da, k_da, v_da, q_mla, k_mla, v_mla = _proj(
        h1, row(mix_pre_g), cos, sin, w_a, row(mla_q_norm_g), row(mla_kv_norm_g),
        w_uq, mla_w_ukv[0].astype(BF16))

    bias_d, bias_s, bias_m = _bias_tiles(rel_bias_table.astype(F32), t)
    lam_p = jnp.concatenate([da_lambda_q1, da_lambda_k1, da_lambda_q2, da_lambda_k2], axis=0).astype(F32)
    y_da = _da_attention(q_da, k_da, v_da, bias_d, bias_s, bias_m, lam_p, row(da_sub_g),
                         t=t, s_len=s_len)
    y_mla = _mla_attention(q_mla, k_mla, v_mla, t=t, s_len=s_len)

    h2 = _merge(h1, row(mix_pre_g), row(mix_post_g), y_da, y_mla,
                w[:, gate_off:].astype(BF16), row(b_gate),
                w_branch_da[0].astype(BF16), w_branch_mla[0].astype(BF16), w_out[0].astype(BF16),
                s_len=s_len)

    out = _ffn(h2.reshape(b * s_len, d), row(ffn2_pre_g), row(ffn2_post_g),
               ffn2_w_gate[0].astype(BF16), ffn2_w_up[0].astype(BF16), ffn2_w_down[0].astype(BF16),
               name="ffn2")
    return out.reshape(b, s_len, d)
```

```python
import functools
import math

import numpy as np
import jax
import jax.numpy as jnp
from jax import lax
from jax.experimental import pallas as pl
from jax.experimental.pallas import tpu as pltpu

F32 = jnp.float32
BF16 = jnp.bfloat16

N_META = 16
RMS_EPS = 1e-6
NEG_INF = -1e30
HEADS = 8
HEAD_W = 128
DA_HEAD_DIM = 64
DA_W = HEADS * HEAD_W
MLA_Q_RANK = 768
MLA_KV_RANK = 512
MLA_NOPE = 128
MLA_ROPE = 64
MLA_QK_W = 256
ROPE_THETA = 10000.0
REL_BUCKETS = 32
REL_MAX_DIST = 128
LAMBDA_INIT = 0.8 - 0.6 * math.exp(-0.3 * 0)

V7X_LANES = 128
V7X_BF16_SUBLANES = 16
V7X_VMEM_BYTES = 64 * 1024 * 1024
VMEM_CAP_BYTES = V7X_VMEM_BYTES - 6 * 1024 * 1024

META_BLK = V7X_LANES


def _row_tile(rows, max_tile, multiple=V7X_BF16_SUBLANES):
    best = None
    for t in range(multiple, max_tile + 1, multiple):
        if rows % t == 0:
            best = t
    assert best is not None, (rows, max_tile)
    return best


def _col_tile(cols, max_tile):
    return _row_tile(cols, max_tile, V7X_LANES)


def _params(semantics, vmem_bytes):
    return pltpu.CompilerParams(
        dimension_semantics=semantics,
        vmem_limit_bytes=int(min(max(vmem_bytes, 16 * 1024 * 1024), VMEM_CAP_BYTES)))


def _resident(block_shape, index_map):
    return pl.BlockSpec(block_shape, index_map, pipeline_mode=pl.Buffered(1))


def _rms(x, g):
    ms = jnp.mean(x * x, axis=-1, keepdims=True)
    return x * lax.rsqrt(ms + RMS_EPS) * g


def _dot(a, b):
    return jnp.dot(a, b, preferred_element_type=F32)


def _dot_nt(a, b):
    return lax.dot_general(a, b, (((1,), (1,)), ((), ())), preferred_element_type=F32)


def _ffn_kernel(x_ref, pre_ref, post_ref, wg_ref, wu_ref, wd_ref, o_ref, xn_ref):
    k = pl.program_id(1)

    @pl.when(k == 0)
    def _():
        xn_ref[...] = _rms(x_ref[...], pre_ref[...]).astype(BF16)

    xn = xn_ref[...]
    g = _dot(xn, wg_ref[...])
    u = _dot(xn, wu_ref[...])
    a = (g * jax.nn.sigmoid(g) * u).astype(BF16)
    c = _dot(a, wd_ref[...])

    @pl.when(k == 0)
    def _():
        o_ref[...] = c

    @pl.when(k > 0)
    def _():
        o_ref[...] += c

    @pl.when(k == pl.num_programs(1) - 1)
    def _():
        o_ref[...] = x_ref[...] + 0.5 * _rms(o_ref[...], post_ref[...])


def _ffn(x, pre_g, post_g, wg, wu, wd, *, name):
    rows, d = x.shape
    f = wg.shape[1]
    tm = _row_tile(rows, 704)
    tf = _col_tile(f, 512)
    vmem = (2 * 2 * tm * d * 4
            + tm * d * 2
            + 2 * 3 * d * tf * 2
            + 4 * tm * tf * 4
            + tm * d * 4)
    return pl.pallas_call(
        _ffn_kernel,
        out_shape=jax.ShapeDtypeStruct((rows, d), F32),
        grid=(rows // tm, f // tf),
        in_specs=[
            pl.BlockSpec((tm, d), lambda i, k: (i, 0)),
            _resident((1, d), lambda i, k: (0, 0)),
            _resident((1, d), lambda i, k: (0, 0)),
            pl.BlockSpec((d, tf), lambda i, k: (0, k)),
            pl.BlockSpec((d, tf), lambda i, k: (0, k)),
            pl.BlockSpec((tf, d), lambda i, k: (k, 0)),
        ],
        out_specs=pl.BlockSpec((tm, d), lambda i, k: (i, 0)),
        scratch_shapes=[pltpu.VMEM((tm, d), BF16)],
        compiler_params=_params(("parallel", "arbitrary"), vmem),
        name=name,
    )(x, pre_g, post_g, wg, wu, wd)


def _da_proj_kernel(h_ref, g_ref, wqt_ref, wk_ref, wvt_ref, qt_ref, k_ref, vt_ref):
    hn = _rms(h_ref[...], g_ref[...]).astype(BF16)
    tm = hn.shape[0]
    qt = _dot_nt(wqt_ref[...], hn) * (DA_HEAD_DIM ** -0.5)
    qt_ref[0] = qt.reshape(HEADS, HEAD_W, tm).astype(BF16)
    k = _dot(hn, wk_ref[...])
    for h in range(HEADS):
        k_ref[0, h] = k[:, h * HEAD_W:(h + 1) * HEAD_W].astype(BF16)
    vt_ref[0] = _dot_nt(wvt_ref[...], hn).reshape(HEADS, HEAD_W, tm).astype(BF16)


_C_CKV = MLA_Q_RANK
_C_KR1 = _C_CKV + MLA_KV_RANK
_C_KR2 = _C_KR1 + HEAD_W
_C_END = _C_KR2 + HEAD_W
_UQ_HEAD_W = 3 * HEAD_W


def _mla_proj_kernel(h_ref, g_ref, cos_ref, sin_ref, cost_ref, sint_ref, wc_ref, gq_ref, gkv_ref,
                     wuqt_ref, wuk_ref, wuvt_ref, qt_ref, k_ref, vt_ref):
    hn = _rms(h_ref[...], g_ref[...]).astype(BF16)
    tm = hn.shape[0]
    scale = (MLA_NOPE + MLA_ROPE) ** -0.5

    def proj(lo, hi):
        return _dot(hn, wc_ref[:, lo:hi])

    cq = _rms(proj(0, _C_CKV), gq_ref[...]).astype(BF16)
    qt = _dot_nt(wuqt_ref[...], cq)
    cos_t = cost_ref[...]
    sin_t = sint_ref[...]
    for h in range(HEADS):
        base = h * _UQ_HEAD_W
        nope = qt[base:base + HEAD_W]
        r1 = qt[base + HEAD_W:base + 2 * HEAD_W]
        r2 = qt[base + 2 * HEAD_W:base + 3 * HEAD_W]
        qt_ref[0, h, 0:HEAD_W, :] = (nope * scale).astype(BF16)
        qt_ref[0, h, HEAD_W:MLA_QK_W, :] = ((r1 * cos_t + r2 * sin_t) * scale).astype(BF16)

    ckv = _rms(proj(_C_CKV, _C_KR1), gkv_ref[...]).astype(BF16)
    k_nope = _dot(ckv, wuk_ref[...])
    k_rope = (proj(_C_KR1, _C_KR2) * cos_ref[...] + proj(_C_KR2, _C_END) * sin_ref[...]).astype(BF16)
    for h in range(HEADS):
        k_ref[0, h, :, 0:HEAD_W] = k_nope[:, h * HEAD_W:(h + 1) * HEAD_W].astype(BF16)
        k_ref[0, h, :, HEAD_W:MLA_QK_W] = k_rope
    vt_ref[0] = _dot_nt(wuvt_ref[...], ckv).reshape(HEADS, HEAD_W, tm).astype(BF16)


def _proj_specs(b, lp, tm, q_w):
    nt = lp // tm
    rows = lambda w: pl.BlockSpec((1, HEADS, tm, w), lambda i: (i // nt, 0, i % nt, 0))
    cols = lambda w: pl.BlockSpec((1, HEADS, w, tm), lambda i: (i // nt, 0, 0, i % nt))
    shapes = [jax.ShapeDtypeStruct((b, HEADS, q_w, lp), BF16),
              jax.ShapeDtypeStruct((b, HEADS, lp, q_w), BF16),
              jax.ShapeDtypeStruct((b, HEADS, HEAD_W, lp), BF16)]
    return shapes, [cols(q_w), rows(q_w), cols(HEAD_W)]


def _da_proj(h1, g, wqt, wk, wvt):
    b, lp, d = h1.shape
    tm = _col_tile(lp, 704)
    nt = lp // tm
    shapes, out_specs = _proj_specs(b, lp, tm, HEAD_W)
    vmem = (2 * tm * d * 4 + tm * d * 2 + 3 * DA_W * d * 2
            + 2 * 3 * tm * DA_W * 2 + 3 * tm * DA_W * 4)
    return pl.pallas_call(
        _da_proj_kernel,
        out_shape=shapes,
        grid=(b * nt,),
        in_specs=[
            pl.BlockSpec((None, tm, d), lambda i: (i // nt, i % nt, 0)),
            _resident((1, d), lambda i: (0, 0)),
            _resident(wqt.shape, lambda i: (0, 0)),
            _resident(wk.shape, lambda i: (0, 0)),
            _resident(wvt.shape, lambda i: (0, 0)),
        ],
        out_specs=out_specs,
        compiler_params=_params(("parallel",), vmem),
        name="da_in_proj",
    )(h1, g, wqt, wk, wvt)


def _mla_proj(h1, g, cos, sin, cos_t, sin_t, wc, gq, gkv, wuqt, wuk, wuvt):
    b, lp, d = h1.shape
    tm = _col_tile(lp, 704)
    nt = lp // tm
    shapes, out_specs = _proj_specs(b, lp, tm, MLA_QK_W)
    vmem = (2 * tm * d * 4 + tm * d * 2
            + (d * _C_END + HEADS * _UQ_HEAD_W * MLA_Q_RANK + 2 * DA_W * MLA_KV_RANK) * 2
            + 2 * tm * HEADS * (2 * MLA_QK_W + HEAD_W) * 2
            + 2 * tm * HEADS * _UQ_HEAD_W * 4)
    return pl.pallas_call(
        _mla_proj_kernel,
        out_shape=shapes,
        grid=(b * nt,),
        in_specs=[
            pl.BlockSpec((None, tm, d), lambda i: (i // nt, i % nt, 0)),
            _resident((1, d), lambda i: (0, 0)),
            pl.BlockSpec((tm, HEAD_W), lambda i: (i % nt, 0)),
            pl.BlockSpec((tm, HEAD_W), lambda i: (i % nt, 0)),
            pl.BlockSpec((HEAD_W, tm), lambda i: (0, i % nt)),
            pl.BlockSpec((HEAD_W, tm), lambda i: (0, i % nt)),
            _resident(wc.shape, lambda i: (0, 0)),
            _resident((1, MLA_Q_RANK), lambda i: (0, 0)),
            _resident((1, MLA_KV_RANK), lambda i: (0, 0)),
            _resident(wuqt.shape, lambda i: (0, 0)),
            _resident(wuk.shape, lambda i: (0, 0)),
            _resident(wuvt.shape, lambda i: (0, 0)),
        ],
        out_specs=out_specs,
        compiler_params=_params(("parallel",), vmem),
        name="mla_in_proj",
    )(h1, g, cos, sin, cos_t, sin_t, wc, gq, gkv, wuqt, wuk, wuvt)


def _t5_bucket(rel):
    n = np.maximum(rel, 0)
    max_exact = REL_BUCKETS // 2
    n_f = np.maximum(n, 1).astype(np.float64)
    large = max_exact + (np.log(n_f / max_exact) / math.log(REL_MAX_DIST / max_exact)
                         * (REL_BUCKETS - max_exact)).astype(np.int32)
    large = np.minimum(large, REL_BUCKETS - 1)
    return np.where(n < max_exact, n, large).astype(np.int32)


def _bucket_tiles(t):
    j = np.arange(t)[:, None]
    i = np.arange(t)[None, :]
    diag = np.where(j <= i, _t5_bucket(i - j), -1)
    sub = _t5_bucket(t + i - j)
    assert _t5_bucket(np.array([t + 1]))[0] == REL_BUCKETS - 1
    jm = np.arange(META_BLK)[:, None]
    meta_first = np.where(jm < N_META, _t5_bucket(N_META + i - jm), -1)
    meta_rest = np.where(jm < N_META, REL_BUCKETS - 1, -1) + 0 * i
    return (diag.astype(np.int32), sub.astype(np.int32),
            np.stack([meta_first, meta_rest]).astype(np.int32))


def _bias_kernel(table_ref, bd_ref, bs_ref, bm_ref, od_ref, os_ref, om_ref):
    h = pl.program_id(0)
    far = table_ref[REL_BUCKETS - 1, h]

    def lookup(bucket):
        acc = jnp.zeros(bucket.shape, F32)
        for b in range(REL_BUCKETS - 1):
            acc = jnp.where(bucket == b, table_ref[b, h] - far, acc)
        return jnp.where(bucket < 0, NEG_INF, acc)

    od_ref[0] = lookup(bd_ref[...])
    os_ref[0] = lookup(bs_ref[...])
    om_ref[0, 0] = lookup(bm_ref[0])
    om_ref[0, 1] = lookup(bm_ref[1])


def _bias_tiles(table, t):
    bd, bs, bm = _bucket_tiles(t)
    return pl.pallas_call(
        _bias_kernel,
        out_shape=[jax.ShapeDtypeStruct((HEADS, t, t), F32),
                   jax.ShapeDtypeStruct((HEADS, t, t), F32),
                   jax.ShapeDtypeStruct((HEADS, 2, META_BLK, t), F32)],
        grid=(HEADS,),
        in_specs=[
            pl.BlockSpec(memory_space=pltpu.SMEM),
            pl.BlockSpec((t, t), lambda h: (0, 0)),
            pl.BlockSpec((t, t), lambda h: (0, 0)),
            pl.BlockSpec((2, META_BLK, t), lambda h: (0, 0, 0)),
        ],
        out_specs=[pl.BlockSpec((1, t, t), lambda h: (h, 0, 0)),
                   pl.BlockSpec((1, t, t), lambda h: (h, 0, 0)),
                   pl.BlockSpec((1, 2, META_BLK, t), lambda h: (h, 0, 0, 0))],
        compiler_params=_params(("parallel",), 12 * t * t * 4),
        name="rel_bias_tiles",
    )(table, jnp.asarray(bd), jnp.asarray(bs), jnp.asarray(bm))


ATTN_QUERY_CHUNK = 256


ATTN_SCORE_BUFFERS = 4


def _chunking(qt_ref, s_ref):
    n_chunks = qt_ref.shape[1] // s_ref.shape[2]
    return n_chunks, min(s_ref.shape[0] - 1, n_chunks)


def _slot(s_ref, ordinal, n_chunks, c):
    n_buf = s_ref.shape[0]
    if n_chunks % n_buf == 0:
        return c % n_buf
    return (ordinal * n_chunks + c) & (n_buf - 1)


def _issue_scores(qt_ref, s_ref, k_blk, c, slot):
    qc = s_ref.shape[2]
    s_ref[slot, 0:k_blk.shape[0], :] = _dot(k_blk, qt_ref[:, pl.ds(c * qc, qc)])


def _attend_prologue(qt_ref, s_ref, k_blk):
    n_chunks, ahead = _chunking(qt_ref, s_ref)
    for c in range(ahead):
        _issue_scores(qt_ref, s_ref, k_blk, c, _slot(s_ref, 0, n_chunks, c))


def _attend_block(qt_ref, s_ref, ordinal, k_blk, vt_blk, bias_fn, next_k, m_ref, l_ref, acc_ref,
                  *, first):
    qc = s_ref.shape[2]
    keys = k_blk.shape[0]
    n_chunks, ahead = _chunking(qt_ref, s_ref)
    for c in range(n_chunks):
        sl = pl.ds(c * qc, qc)
        nxt = c + ahead
        if nxt < n_chunks:
            _issue_scores(qt_ref, s_ref, k_blk, nxt, _slot(s_ref, ordinal, n_chunks, nxt))
        elif next_k is not None:
            _issue_scores(qt_ref, s_ref, next_k, nxt - n_chunks, _slot(s_ref, ordinal, n_chunks, nxt))
        s = s_ref[_slot(s_ref, ordinal, n_chunks, c), 0:keys, :]
        bias = bias_fn(c) if bias_fn is not None else None
        if bias is not None:
            s = s + bias
        s_max = jnp.max(s, axis=0, keepdims=True)
        if first:
            p = jnp.exp(s - s_max)
            m_ref[:, sl] = s_max
            l_ref[:, sl] = jnp.sum(p, axis=0, keepdims=True)
            acc_ref[:, sl] = _dot(vt_blk, p.astype(BF16))
        else:
            m_prev = m_ref[:, sl]
            m_new = jnp.maximum(m_prev, s_max)
            alpha = jnp.exp(m_prev - m_new)
            p = jnp.exp(s - m_new)
            l_ref[:, sl] = alpha * l_ref[:, sl] + jnp.sum(p, axis=0, keepdims=True)
            acc_ref[:, sl] = alpha * acc_ref[:, sl] + _dot(vt_blk, p.astype(BF16))
            m_ref[:, sl] = m_new


def _da_kernel(qt_ref, k_ref, vt_ref, bd_ref, bs_ref, bm_ref, lam_ref, subg_ref, o_ref,
               qq_ref, s_ref, m_ref, l_ref, acc_ref, *, t, s_len):
    qi = pl.program_id(2)
    qt = qt_ref[0, 0]
    feat = lax.broadcasted_iota(jnp.int32, (HEAD_W, 1), 0)
    zero = jnp.zeros_like(qt)
    qq_ref[:, 0:t] = jnp.where(feat < DA_HEAD_DIM, qt, zero)
    qq_ref[:, t:2 * t] = jnp.where(feat >= DA_HEAD_DIM, qt, zero)
    qc = min(ATTN_QUERY_CHUNK, t)
    chunks_per_map = t // qc

    def bias_cols(read):
        return lambda c: read(pl.ds((c % chunks_per_map) * qc, qc))

    def k_block(kb):
        return k_ref[0, 0, pl.ds(pl.multiple_of(kb * t, t), t), :]

    def vt_block(kb):
        return vt_ref[0, 0, :, pl.ds(pl.multiple_of(kb * t, t), t)]

    state = (m_ref, l_ref, acc_ref)
    meta_sel = jnp.minimum(qi, 1)
    k_meta = k_ref[0, 0, pl.ds(s_len, META_BLK), :]
    _attend_prologue(qq_ref, s_ref, k_meta)
    _attend_block(qq_ref, s_ref, 0, k_meta, vt_ref[0, 0, :, pl.ds(s_len, META_BLK)],
                  bias_cols(lambda cs: bm_ref[0, meta_sel, :, cs]), k_block(0), *state, first=True)

    def far(kb, carry):
        _attend_block(qq_ref, s_ref, kb + 1, k_block(kb), vt_block(kb), None, k_block(kb + 1),
                      *state, first=False)
        return carry

    lax.fori_loop(0, qi - 1, far, 0)

    @pl.when(qi >= 1)
    def _():
        _attend_block(qq_ref, s_ref, qi, k_block(qi - 1), vt_block(qi - 1),
                      bias_cols(lambda cs: bs_ref[0, :, cs]), k_block(qi), *state, first=False)

    _attend_block(qq_ref, s_ref, qi + 1, k_block(qi), vt_block(qi),
                  bias_cols(lambda cs: bd_ref[0, :, cs]), None, *state, first=False)

    ot = acc_ref[...] / l_ref[...]
    lp = lam_ref[...]
    lam = (jnp.exp(jnp.sum(lp[0:1] * lp[1:2], axis=-1, keepdims=True))
           - jnp.exp(jnp.sum(lp[2:3] * lp[3:4], axis=-1, keepdims=True)) + LAMBDA_INIT)
    d = (ot[:, :t] - lam * ot[:, t:]).T
    o_ref[0] = (_rms(d, subg_ref[...]) * (1.0 - LAMBDA_INIT)).astype(BF16)


def _attn_specs(t, lp, q_w):
    return [
        pl.BlockSpec((1, 1, q_w, t), lambda bi, h, qi: (bi, h, 0, qi)),
        pl.BlockSpec((1, 1, lp, q_w), lambda bi, h, qi: (bi, h, 0, 0)),
        pl.BlockSpec((1, 1, HEAD_W, lp), lambda bi, h, qi: (bi, h, 0, 0)),
    ]


def _da_attention(qt, k, vt, bias_d, bias_s, bias_m, lam_p, sub_g, *, t, s_len):
    b, _, lp, _ = k.shape
    nq = s_len // t
    vmem = (2 * 2 * lp * HEAD_W * 2 + 2 * 2 * t * t * 4 + 2 * 2 * t * META_BLK * 4
            + 2 * 2 * t * HEAD_W * 2 + 3 * 2 * t * HEAD_W * 4
            + 6 * t * ATTN_QUERY_CHUNK * 4)
    return pl.pallas_call(
        functools.partial(_da_kernel, t=t, s_len=s_len),
        out_shape=jax.ShapeDtypeStruct((b, s_len, DA_W), BF16),
        grid=(b, HEADS, nq),
        in_specs=_attn_specs(t, lp, HEAD_W) + [
            pl.BlockSpec((1, t, t), lambda bi, h, qi: (h, 0, 0)),
            pl.BlockSpec((1, t, t), lambda bi, h, qi: (h, 0, 0)),
            pl.BlockSpec((1, 2, META_BLK, t), lambda bi, h, qi: (h, 0, 0, 0)),
            _resident((4, DA_HEAD_DIM), lambda bi, h, qi: (0, 0)),
            _resident((1, HEAD_W), lambda bi, h, qi: (0, 0)),
        ],
        out_specs=pl.BlockSpec((1, t, HEAD_W), lambda bi, h, qi: (bi, qi, h)),
        scratch_shapes=[pltpu.VMEM((HEAD_W, 2 * t), BF16),
                        pltpu.VMEM((ATTN_SCORE_BUFFERS, t, min(ATTN_QUERY_CHUNK, t)), F32),
                        pltpu.VMEM((1, 2 * t), F32), pltpu.VMEM((1, 2 * t), F32),
                        pltpu.VMEM((HEAD_W, 2 * t), F32)],
        compiler_params=_params(("parallel", "parallel", "arbitrary"), vmem),
        name="diff_attention",
    )(qt, k, vt, bias_d, bias_s, bias_m, lam_p, sub_g)


def _mla_kernel(qt_ref, k_ref, vt_ref, o_ref, s_ref, m_ref, l_ref, acc_ref, *, tq, tk, s_len):
    qi = pl.program_id(2)
    qv_ref = qt_ref.at[0, 0]
    qc = s_ref.shape[2]
    diag_blocks = tq // tk

    def k_block(kb):
        return k_ref[0, 0, pl.ds(pl.multiple_of(kb * tk, tk), tk), :]

    def vt_block(kb):
        return vt_ref[0, 0, :, pl.ds(pl.multiple_of(kb * tk, tk), tk)]

    state = (m_ref, l_ref, acc_ref)
    k_meta = k_ref[0, 0, pl.ds(s_len, META_BLK), :]
    key = lax.broadcasted_iota(jnp.int32, (META_BLK, qc), 0)
    meta_mask = jnp.where(key < N_META, 0.0, NEG_INF)
    _attend_prologue(qv_ref, s_ref, k_meta)
    _attend_block(qv_ref, s_ref, 0, k_meta, vt_ref[0, 0, :, pl.ds(s_len, META_BLK)],
                  lambda c: meta_mask, k_block(0), *state, first=True)

    def far(kb, carry):
        _attend_block(qv_ref, s_ref, kb + 1, k_block(kb), vt_block(kb), None, k_block(kb + 1),
                      *state, first=False)
        return carry

    first_diag = qi * diag_blocks
    lax.fori_loop(0, first_diag, far, 0)

    key = lax.broadcasted_iota(jnp.int32, (tk, qc), 0)
    query = lax.broadcasted_iota(jnp.int32, (tk, qc), 1)
    for d in range(diag_blocks):
        kb = first_diag + d
        causal = lambda c, d=d: jnp.where(key + d * tk <= query + c * qc, 0.0, NEG_INF)
        _attend_block(qv_ref, s_ref, kb + 1, k_block(kb), vt_block(kb), causal,
                      k_block(kb + 1) if d + 1 < diag_blocks else None, *state, first=False)

    o_ref[0] = (acc_ref[...] / l_ref[...]).T.astype(BF16)


def _mla_attention(qt, k, vt, *, tq, tk, s_len):
    b, _, lp, _ = k.shape
    nq = s_len // tq
    qc = min(ATTN_QUERY_CHUNK, tq)
    vmem = (2 * lp * (MLA_QK_W + HEAD_W) * 2 + 2 * tq * MLA_QK_W * 2
            + 2 * tq * HEAD_W * 2 + 3 * tq * HEAD_W * 4 + (ATTN_SCORE_BUFFERS + 4) * tk * qc * 4)
    return pl.pallas_call(
        functools.partial(_mla_kernel, tq=tq, tk=tk, s_len=s_len),
        out_shape=jax.ShapeDtypeStruct((b, s_len, DA_W), BF16),
        grid=(b, HEADS, nq),
        in_specs=_attn_specs(tq, lp, MLA_QK_W),
        out_specs=pl.BlockSpec((1, tq, HEAD_W), lambda bi, h, qi: (bi, qi, h)),
        scratch_shapes=[pltpu.VMEM((ATTN_SCORE_BUFFERS, tk, qc), F32),
                        pltpu.VMEM((1, tq), F32), pltpu.VMEM((1, tq), F32),
                        pltpu.VMEM((HEAD_W, tq), F32)],
        compiler_params=_params(("parallel", "parallel", "arbitrary"), vmem),
        name="mla_attention",
    )(qt, k, vt)


def _merge_kernel(h_ref, pre_ref, post_ref, yda_ref, ymla_ref, wg0_ref, wg1_ref, b0_ref, b1_ref,
                  wbd_ref, wbm_ref, wo_ref, o_ref, hn_ref):
    j = pl.program_id(1)

    @pl.when(j == 0)
    def _():
        hn_ref[...] = _rms(h_ref[0], pre_ref[...]).astype(BF16)

    hn = hn_ref[...]
    g0 = jax.nn.sigmoid(_dot(hn, wg0_ref[...]) + b0_ref[...])
    g1 = jax.nn.sigmoid(_dot(hn, wg1_ref[...]) + b1_ref[...])
    merged = g0 * _dot(yda_ref[0], wbd_ref[...]) + g1 * _dot(ymla_ref[0], wbm_ref[...])
    c = _dot(merged.astype(BF16), wo_ref[...])

    @pl.when(j == 0)
    def _():
        o_ref[0] = c

    @pl.when(j > 0)
    def _():
        o_ref[0] += c

    @pl.when(j == pl.num_programs(1) - 1)
    def _():
        o_ref[0] = h_ref[0] + _rms(o_ref[0], post_ref[...])


def _merge(h1, pre_g, post_g, y_da, y_mla, w_gate, b_gate, wb_da, wb_mla, w_out, *, s_len):
    b, _, d = h1.shape
    yw = y_da.shape[-1]
    tm = _row_tile(s_len, 512)
    tn = _col_tile(d, 512)
    nt = s_len // tm
    nj = d // tn
    vmem = (2 * 2 * tm * d * 4 + tm * d * 2 + 2 * 2 * tm * yw * 2
            + 2 * (3 * d * tn + 2 * yw * tn) * 2 + 6 * tm * tn * 4 + tm * d * 4)
    return pl.pallas_call(
        _merge_kernel,
        out_shape=jax.ShapeDtypeStruct((b, s_len, d), F32),
        grid=(b * nt, nj),
        in_specs=[
            pl.BlockSpec((1, tm, d), lambda i, j: (i // nt, i % nt, 0)),
            _resident((1, d), lambda i, j: (0, 0)),
            _resident((1, d), lambda i, j: (0, 0)),
            pl.BlockSpec((1, tm, yw), lambda i, j: (i // nt, i % nt, 0)),
            pl.BlockSpec((1, tm, yw), lambda i, j: (i // nt, i % nt, 0)),
            pl.BlockSpec((d, tn), lambda i, j: (0, j)),
            pl.BlockSpec((d, tn), lambda i, j: (0, nj + j)),
            pl.BlockSpec((1, tn), lambda i, j: (0, j)),
            pl.BlockSpec((1, tn), lambda i, j: (0, nj + j)),
            pl.BlockSpec((yw, tn), lambda i, j: (0, j)),
            pl.BlockSpec((yw, tn), lambda i, j: (0, j)),
            pl.BlockSpec((tn, d), lambda i, j: (j, 0)),
        ],
        out_specs=pl.BlockSpec((1, tm, d), lambda i, j: (i // nt, i % nt, 0)),
        scratch_shapes=[pltpu.VMEM((tm, d), BF16)],
        compiler_params=_params(("parallel", "arbitrary"), vmem),
        name="gated_merge_out_proj",
    )(h1, pre_g, post_g, y_da, y_mla, w_gate, w_gate, b_gate, b_gate, wb_da, wb_mla, w_out)


def _rope_tables(s_len):
    half = MLA_ROPE // 2
    pos = jnp.concatenate([N_META + jnp.arange(s_len), jnp.arange(N_META),
                           jnp.zeros((META_BLK - N_META,), jnp.int32)]).astype(F32)
    inv = ROPE_THETA ** (-jnp.arange(half, dtype=F32) * 2.0 / MLA_ROPE)
    ang = pos[:, None] * inv[None, :]
    cos, sin = jnp.cos(ang), jnp.sin(ang)
    pad = jnp.zeros((pos.shape[0], HEAD_W - MLA_ROPE), F32)
    return (jnp.concatenate([cos, cos, pad], axis=-1),
            jnp.concatenate([-sin, sin, pad], axis=-1))


def _swap_halves(w):
    half = w.shape[-1] // 2
    return jnp.concatenate([w[..., half:], w[..., :half]], axis=-1)


def _pad_lanes(w, width):
    return jnp.concatenate([w, jnp.zeros(w.shape[:-1] + (width - w.shape[-1],), w.dtype)], axis=-1)


def kernel(x, meta_tokens, rel_bias_table, ffn1_pre_g, ffn1_post_g, ffn1_w_gate, ffn1_w_up, ffn1_w_down, mix_pre_g, mix_post_g, w_in, b_gate, da_lambda_q1, da_lambda_k1, da_lambda_q2, da_lambda_k2, da_sub_g, mla_q_norm_g, mla_kv_norm_g, mla_w_uq, mla_w_ukv, w_branch_da, w_branch_mla, w_out, ffn2_pre_g, ffn2_post_g, ffn2_w_gate, ffn2_w_up, ffn2_w_down):
    b, s_len, d = x.shape
    assert ffn1_pre_g.shape[0] == 1, "single-layer trunk"
    lp = s_len + META_BLK
    t = _col_tile(s_len, 512)
    row = lambda g: g.reshape(1, -1).astype(F32)

    meta = jnp.broadcast_to(meta_tokens.astype(x.dtype)[None], (b, N_META, d))
    h0 = jnp.concatenate([x, meta, jnp.zeros((b, META_BLK - N_META, d), x.dtype)], axis=1)

    h1 = _ffn(h0.reshape(b * lp, d), row(ffn1_pre_g), row(ffn1_post_g),
              ffn1_w_gate[0].astype(BF16), ffn1_w_up[0].astype(BF16), ffn1_w_down[0].astype(BF16),
              name="ffn1").reshape(b, lp, d)

    w = w_in[0]
    o_k, o_v, o_cq = DA_W, 2 * DA_W, 3 * DA_W
    o_kr = o_cq + MLA_Q_RANK + MLA_KV_RANK
    o_gate = o_kr + MLA_ROPE
    qt_da, k_da, vt_da = _da_proj(
        h1, row(mix_pre_g), w[:, :o_k].T.astype(BF16), w[:, o_k:o_v].astype(BF16),
        w[:, o_v:o_cq].T.astype(BF16))

    kr = w[:, o_kr:o_gate]
    w_c = jnp.concatenate([w[:, o_cq:o_kr], _pad_lanes(kr, HEAD_W),
                           _pad_lanes(_swap_halves(kr), HEAD_W)], axis=-1).astype(BF16)
    uq = mla_w_uq[0].reshape(MLA_Q_RANK, HEADS, MLA_NOPE + MLA_ROPE)
    uq_rope = uq[..., MLA_NOPE:]
    w_uqt = jnp.concatenate([uq[..., :MLA_NOPE], _pad_lanes(uq_rope, HEAD_W),
                             _pad_lanes(_swap_halves(uq_rope), HEAD_W)],
                            axis=-1).reshape(MLA_Q_RANK, HEADS * _UQ_HEAD_W).T.astype(BF16)
    ukv = mla_w_ukv[0].reshape(MLA_KV_RANK, HEADS, 2 * HEAD_W)
    w_uk = ukv[..., :HEAD_W].reshape(MLA_KV_RANK, DA_W).astype(BF16)
    w_uvt = ukv[..., HEAD_W:].reshape(MLA_KV_RANK, DA_W).T.astype(BF16)
    cos, sin = _rope_tables(s_len)
    qt_mla, k_mla, vt_mla = _mla_proj(
        h1, row(mix_pre_g), cos, sin, cos.T, sin.T, w_c, row(mla_q_norm_g), row(mla_kv_norm_g),
        w_uqt, w_uk, w_uvt)

    bias_d, bias_s, bias_m = _bias_tiles(rel_bias_table.astype(F32), t)
    lam_p = jnp.concatenate([da_lambda_q1, da_lambda_k1, da_lambda_q2, da_lambda_k2], axis=0).astype(F32)
    y_da = _da_attention(qt_da, k_da, vt_da, bias_d, bias_s, bias_m, lam_p, row(da_sub_g),
                         t=t, s_len=s_len)
    y_mla = _mla_attention(qt_mla, k_mla, vt_mla, tq=_col_tile(s_len, 2 * t), tk=t, s_len=s_len)

    h2 = _merge(h1, row(mix_pre_g), row(mix_post_g), y_da, y_mla,
                w[:, o_gate:].astype(BF16), row(b_gate),
                w_branch_da[0].astype(BF16), w_branch_mla[0].astype(BF16), w_out[0].astype(BF16),
                s_len=s_len)

    out = _ffn(h2.reshape(b * s_len, d), row(ffn2_pre_g), row(ffn2_post_g),
               ffn2_w_gate[0].astype(BF16), ffn2_w_up[0].astype(BF16), ffn2_w_down[0].astype(BF16),
               name="ffn2")
    return out.reshape(b, s_len, d)
```

```python
import functools
import math

import numpy as np
import jax
import jax.numpy as jnp
from jax import lax
from jax.experimental import pallas as pl
from jax.experimental.pallas import tpu as pltpu

F32 = jnp.float32
BF16 = jnp.bfloat16

N_META = 16
RMS_EPS = 1e-6
NEG_INF = -1e30
HEADS = 8
HEAD_W = 128
DA_HEAD_DIM = 64
DA_W = HEADS * HEAD_W
MLA_Q_RANK = 768
MLA_KV_RANK = 512
MLA_NOPE = 128
MLA_ROPE = 64
MLA_QK_W = 256
ROPE_THETA = 10000.0
REL_BUCKETS = 32
REL_MAX_DIST = 128
LAMBDA_INIT = 0.8 - 0.6 * math.exp(-0.3 * 0)
LOG2_E = math.log2(math.e)

V7X_LANES = 128
V7X_BF16_SUBLANES = 16
V7X_VMEM_BYTES = 64 * 1024 * 1024
VMEM_CAP_BYTES = V7X_VMEM_BYTES - 6 * 1024 * 1024

META_BLK = V7X_LANES


def _row_tile(rows, max_tile, multiple=V7X_BF16_SUBLANES):
    best = None
    for t in range(multiple, max_tile + 1, multiple):
        if rows % t == 0:
            best = t
    assert best is not None, (rows, max_tile)
    return best


def _col_tile(cols, max_tile):
    return _row_tile(cols, max_tile, V7X_LANES)


def _params(semantics, vmem_bytes):
    return pltpu.CompilerParams(
        dimension_semantics=semantics,
        vmem_limit_bytes=int(min(max(vmem_bytes, 16 * 1024 * 1024), VMEM_CAP_BYTES)))


def _resident(block_shape, index_map):
    return pl.BlockSpec(block_shape, index_map, pipeline_mode=pl.Buffered(1))


def _rms(x, g):
    ms = jnp.mean(x * x, axis=-1, keepdims=True)
    return x * lax.rsqrt(ms + RMS_EPS) * g


def _dot(a, b):
    return jnp.dot(a, b, preferred_element_type=F32)


def _dot_nt(a, b):
    return lax.dot_general(a, b, (((1,), (1,)), ((), ())), preferred_element_type=F32)


def _ffn_kernel(x_ref, pre_ref, post_ref, wg_ref, wu_ref, wd_ref, o_ref, xn_ref):
    k = pl.program_id(1)

    @pl.when(k == 0)
    def _():
        xn_ref[...] = _rms(x_ref[...], pre_ref[...]).astype(BF16)
        o_ref[...] = jnp.zeros_like(o_ref)

    xn = xn_ref[...]
    g = _dot(xn, wg_ref[...])
    u = _dot(xn, wu_ref[...])
    a = (g * jax.nn.sigmoid(g) * u).astype(BF16)
    o_ref[...] += _dot(a, wd_ref[...])

    @pl.when(k == pl.num_programs(1) - 1)
    def _():
        o_ref[...] = x_ref[...] + 0.5 * _rms(o_ref[...], post_ref[...])


def _ffn(x, pre_g, post_g, wg, wu, wd, *, name):
    rows, d = x.shape
    f = wg.shape[1]
    tm = _row_tile(rows, 704)
    tf = _col_tile(f, 512)
    vmem = (2 * 2 * tm * d * 4
            + tm * d * 2
            + 2 * 3 * d * tf * 2
            + 4 * tm * tf * 4
            + tm * d * 4)
    return pl.pallas_call(
        _ffn_kernel,
        out_shape=jax.ShapeDtypeStruct((rows, d), F32),
        grid=(rows // tm, f // tf),
        in_specs=[
            pl.BlockSpec((tm, d), lambda i, k: (i, 0)),
            _resident((1, d), lambda i, k: (0, 0)),
            _resident((1, d), lambda i, k: (0, 0)),
            pl.BlockSpec((d, tf), lambda i, k: (0, k)),
            pl.BlockSpec((d, tf), lambda i, k: (0, k)),
            pl.BlockSpec((tf, d), lambda i, k: (k, 0)),
        ],
        out_specs=pl.BlockSpec((tm, d), lambda i, k: (i, 0)),
        scratch_shapes=[pltpu.VMEM((tm, d), BF16)],
        compiler_params=_params(("parallel", "arbitrary"), vmem),
        name=name,
    )(x, pre_g, post_g, wg, wu, wd)


VT_ROWS = HEAD_W + V7X_BF16_SUBLANES


def _store_vt(vt_ref, vt):
    tm = vt.shape[1]
    vt_ref[0, :, 0:HEAD_W, :] = vt.reshape(HEADS, HEAD_W, tm).astype(BF16)
    extra = lax.broadcasted_iota(jnp.int32, (HEADS, VT_ROWS - HEAD_W, tm), 1)
    vt_ref[0, :, HEAD_W:VT_ROWS, :] = jnp.where(extra == 0, 1.0, 0.0).astype(BF16)


def _da_proj_kernel(h_ref, g_ref, wqt_ref, wk_ref, wvt_ref, qt_ref, k_ref, vt_ref):
    hn = _rms(h_ref[...], g_ref[...]).astype(BF16)
    tm = hn.shape[0]
    qt = _dot_nt(wqt_ref[...], hn) * (DA_HEAD_DIM ** -0.5 * LOG2_E)
    qt_ref[0] = qt.reshape(HEADS, HEAD_W, tm).astype(BF16)
    k = _dot(hn, wk_ref[...])
    for h in range(HEADS):
        k_ref[0, h] = k[:, h * HEAD_W:(h + 1) * HEAD_W].astype(BF16)
    _store_vt(vt_ref, _dot_nt(wvt_ref[...], hn))


_C_CKV = MLA_Q_RANK
_C_KR1 = _C_CKV + MLA_KV_RANK
_C_KR2 = _C_KR1 + HEAD_W
_C_END = _C_KR2 + HEAD_W
_UQ_HEAD_W = 3 * HEAD_W


def _mla_proj_kernel(h_ref, g_ref, cos_ref, sin_ref, cost_ref, sint_ref, wc_ref, gq_ref, gkv_ref,
                     wuqt_ref, wuk_ref, wuvt_ref, qt_ref, k_ref, vt_ref):
    hn = _rms(h_ref[...], g_ref[...]).astype(BF16)
    tm = hn.shape[0]
    scale = (MLA_NOPE + MLA_ROPE) ** -0.5 * LOG2_E

    def proj(lo, hi):
        return _dot(hn, wc_ref[:, lo:hi])

    cq = _rms(proj(0, _C_CKV), gq_ref[...]).astype(BF16)
    qt = _dot_nt(wuqt_ref[...], cq)
    cos_t = cost_ref[...]
    sin_t = sint_ref[...]
    for h in range(HEADS):
        base = h * _UQ_HEAD_W
        nope = qt[base:base + HEAD_W]
        r1 = qt[base + HEAD_W:base + 2 * HEAD_W]
        r2 = qt[base + 2 * HEAD_W:base + 3 * HEAD_W]
        qt_ref[0, h, 0:HEAD_W, :] = (nope * scale).astype(BF16)
        qt_ref[0, h, HEAD_W:MLA_QK_W, :] = ((r1 * cos_t + r2 * sin_t) * scale).astype(BF16)

    ckv = _rms(proj(_C_CKV, _C_KR1), gkv_ref[...]).astype(BF16)
    k_nope = _dot(ckv, wuk_ref[...])
    k_rope = (proj(_C_KR1, _C_KR2) * cos_ref[...] + proj(_C_KR2, _C_END) * sin_ref[...]).astype(BF16)
    for h in range(HEADS):
        k_ref[0, h, :, 0:HEAD_W] = k_nope[:, h * HEAD_W:(h + 1) * HEAD_W].astype(BF16)
        k_ref[0, h, :, HEAD_W:MLA_QK_W] = k_rope
    _store_vt(vt_ref, _dot_nt(wuvt_ref[...], ckv))


def _proj_specs(b, lp, tm, q_w):
    nt = lp // tm
    rows = lambda w: pl.BlockSpec((1, HEADS, tm, w), lambda i: (i // nt, 0, i % nt, 0))
    cols = lambda w: pl.BlockSpec((1, HEADS, w, tm), lambda i: (i // nt, 0, 0, i % nt))
    shapes = [jax.ShapeDtypeStruct((b, HEADS, q_w, lp), BF16),
              jax.ShapeDtypeStruct((b, HEADS, lp, q_w), BF16),
              jax.ShapeDtypeStruct((b, HEADS, VT_ROWS, lp), BF16)]
    return shapes, [cols(q_w), rows(q_w), cols(VT_ROWS)]


def _da_proj(h1, g, wqt, wk, wvt):
    b, lp, d = h1.shape
    tm = _col_tile(lp, 704)
    nt = lp // tm
    shapes, out_specs = _proj_specs(b, lp, tm, HEAD_W)
    vmem = (2 * tm * d * 4 + tm * d * 2 + 3 * DA_W * d * 2
            + 2 * 3 * tm * DA_W * 2 + 3 * tm * DA_W * 4)
    return pl.pallas_call(
        _da_proj_kernel,
        out_shape=shapes,
        grid=(b * nt,),
        in_specs=[
            pl.BlockSpec((None, tm, d), lambda i: (i // nt, i % nt, 0)),
            _resident((1, d), lambda i: (0, 0)),
            _resident(wqt.shape, lambda i: (0, 0)),
            _resident(wk.shape, lambda i: (0, 0)),
            _resident(wvt.shape, lambda i: (0, 0)),
        ],
        out_specs=out_specs,
        compiler_params=_params(("parallel",), vmem),
        name="da_in_proj",
    )(h1, g, wqt, wk, wvt)


def _mla_proj(h1, g, cos, sin, cos_t, sin_t, wc, gq, gkv, wuqt, wuk, wuvt):
    b, lp, d = h1.shape
    tm = _col_tile(lp, 704)
    nt = lp // tm
    shapes, out_specs = _proj_specs(b, lp, tm, MLA_QK_W)
    vmem = (2 * tm * d * 4 + tm * d * 2
            + (d * _C_END + HEADS * _UQ_HEAD_W * MLA_Q_RANK + 2 * DA_W * MLA_KV_RANK) * 2
            + 2 * tm * HEADS * (2 * MLA_QK_W + HEAD_W) * 2
            + 2 * tm * HEADS * _UQ_HEAD_W * 4)
    return pl.pallas_call(
        _mla_proj_kernel,
        out_shape=shapes,
        grid=(b * nt,),
        in_specs=[
            pl.BlockSpec((None, tm, d), lambda i: (i // nt, i % nt, 0)),
            _resident((1, d), lambda i: (0, 0)),
            pl.BlockSpec((tm, HEAD_W), lambda i: (i % nt, 0)),
            pl.BlockSpec((tm, HEAD_W), lambda i: (i % nt, 0)),
            pl.BlockSpec((HEAD_W, tm), lambda i: (0, i % nt)),
            pl.BlockSpec((HEAD_W, tm), lambda i: (0, i % nt)),
            _resident(wc.shape, lambda i: (0, 0)),
            _resident((1, MLA_Q_RANK), lambda i: (0, 0)),
            _resident((1, MLA_KV_RANK), lambda i: (0, 0)),
            _resident(wuqt.shape, lambda i: (0, 0)),
            _resident(wuk.shape, lambda i: (0, 0)),
            _resident(wuvt.shape, lambda i: (0, 0)),
        ],
        out_specs=out_specs,
        compiler_params=_params(("parallel",), vmem),
        name="mla_in_proj",
    )(h1, g, cos, sin, cos_t, sin_t, wc, gq, gkv, wuqt, wuk, wuvt)


def _t5_bucket(rel):
    n = np.maximum(rel, 0)
    max_exact = REL_BUCKETS // 2
    n_f = np.maximum(n, 1).astype(np.float64)
    large = max_exact + (np.log(n_f / max_exact) / math.log(REL_MAX_DIST / max_exact)
                         * (REL_BUCKETS - max_exact)).astype(np.int32)
    large = np.minimum(large, REL_BUCKETS - 1)
    return np.where(n < max_exact, n, large).astype(np.int32)


def _bucket_tiles(t):
    j = np.arange(t)[:, None]
    i = np.arange(t)[None, :]
    diag = np.where(j <= i, _t5_bucket(i - j), -1)
    sub = _t5_bucket(t + i - j)
    assert _t5_bucket(np.array([t + 1]))[0] == REL_BUCKETS - 1
    jm = np.arange(META_BLK)[:, None]
    meta_first = np.where(jm < N_META, _t5_bucket(N_META + i - jm), -1)
    meta_rest = np.where(jm < N_META, REL_BUCKETS - 1, -1) + 0 * i
    return (diag.astype(np.int32), sub.astype(np.int32),
            np.stack([meta_first, meta_rest]).astype(np.int32))


def _bias_kernel(table_ref, bd_ref, bs_ref, bm_ref, od_ref, os_ref, om_ref):
    h = pl.program_id(0)
    far = table_ref[REL_BUCKETS - 1, h]

    def lookup(bucket):
        acc = jnp.zeros(bucket.shape, F32)
        for b in range(REL_BUCKETS - 1):
            acc = jnp.where(bucket == b, (table_ref[b, h] - far) * LOG2_E, acc)
        return jnp.where(bucket < 0, NEG_INF, acc)

    od_ref[0] = lookup(bd_ref[...])
    os_ref[0] = lookup(bs_ref[...])
    om_ref[0, 0] = lookup(bm_ref[0])
    om_ref[0, 1] = lookup(bm_ref[1])


def _bias_tiles(table, t):
    bd, bs, bm = _bucket_tiles(t)
    return pl.pallas_call(
        _bias_kernel,
        out_shape=[jax.ShapeDtypeStruct((HEADS, t, t), F32),
                   jax.ShapeDtypeStruct((HEADS, t, t), F32),
                   jax.ShapeDtypeStruct((HEADS, 2, META_BLK, t), F32)],
        grid=(HEADS,),
        in_specs=[
            pl.BlockSpec(memory_space=pltpu.SMEM),
            pl.BlockSpec((t, t), lambda h: (0, 0)),
            pl.BlockSpec((t, t), lambda h: (0, 0)),
            pl.BlockSpec((2, META_BLK, t), lambda h: (0, 0, 0)),
        ],
        out_specs=[pl.BlockSpec((1, t, t), lambda h: (h, 0, 0)),
                   pl.BlockSpec((1, t, t), lambda h: (h, 0, 0)),
                   pl.BlockSpec((1, 2, META_BLK, t), lambda h: (h, 0, 0, 0))],
        compiler_params=_params(("parallel",), 12 * t * t * 4),
        name="rel_bias_tiles",
    )(table, jnp.asarray(bd), jnp.asarray(bs), jnp.asarray(bm))


ATTN_QUERY_CHUNK = 256


ATTN_SCORE_BUFFERS = 4


def _chunking(qt_ref, s_ref):
    n_chunks = qt_ref.shape[1] // s_ref.shape[2]
    return n_chunks, min(s_ref.shape[0] - 1, n_chunks)


def _slot(s_ref, ordinal, n_chunks, c):
    n_buf = s_ref.shape[0]
    if n_chunks % n_buf == 0:
        return c % n_buf
    return (ordinal * n_chunks + c) & (n_buf - 1)


def _issue_scores(qt_ref, s_ref, k_blk, c, slot):
    qc = s_ref.shape[2]
    s_ref[slot, 0:k_blk.shape[0], :] = _dot(k_blk, qt_ref[:, pl.ds(c * qc, qc)])


def _attend_prologue(qt_ref, s_ref, k_blk):
    n_chunks, ahead = _chunking(qt_ref, s_ref)
    for c in range(ahead):
        _issue_scores(qt_ref, s_ref, k_blk, c, _slot(s_ref, 0, n_chunks, c))


def _attend_block(qt_ref, s_ref, ordinal, k_blk, vt_blk, bias_fn, next_k, m_ref, acc_ref,
                  *, first):
    qc = s_ref.shape[2]
    keys = k_blk.shape[0]
    n_chunks, ahead = _chunking(qt_ref, s_ref)
    for c in range(n_chunks):
        sl = pl.ds(c * qc, qc)
        nxt = c + ahead
        if nxt < n_chunks:
            _issue_scores(qt_ref, s_ref, k_blk, nxt, _slot(s_ref, ordinal, n_chunks, nxt))
        elif next_k is not None:
            _issue_scores(qt_ref, s_ref, next_k, nxt - n_chunks, _slot(s_ref, ordinal, n_chunks, nxt))
        s = s_ref[_slot(s_ref, ordinal, n_chunks, c), 0:keys, :]
        bias = bias_fn(c) if bias_fn is not None else None
        if bias is not None:
            s = s + bias
        s_max = jnp.max(s, axis=0, keepdims=True)
        if first:
            p = jnp.exp2(s - s_max)
            m_ref[:, sl] = s_max
            acc_ref[:, sl] = _dot(vt_blk, p.astype(BF16))
        else:
            m_prev = m_ref[:, sl]
            m_new = jnp.maximum(m_prev, s_max)
            alpha = jnp.exp2(m_prev - m_new)
            p = jnp.exp2(s - m_new)
            acc_ref[:, sl] = alpha * acc_ref[:, sl] + _dot(vt_blk, p.astype(BF16))
            m_ref[:, sl] = m_new


FAR_UNROLL = 4


def _walk_far_blocks(n_far, attend_one):
    n_far = jnp.maximum(n_far, 0)

    def body(first, count):
        for u in range(count):
            attend_one(first + u)

    def loop_body(i, carry):
        body(i * FAR_UNROLL, FAR_UNROLL)
        return carry

    lax.fori_loop(0, n_far >> (FAR_UNROLL.bit_length() - 1), loop_body, 0)
    count = FAR_UNROLL // 2
    while count >= 1:
        first = n_far - (n_far & (2 * count - 1))

        @pl.when((n_far & count) != 0)
        def _(first=first, count=count):
            body(first, count)

        count //= 2


def _normalised(acc_ref):
    acc = acc_ref[...]
    return acc[0:HEAD_W] / acc[HEAD_W:HEAD_W + 1]


def _da_kernel(qt_ref, k_ref, vt_ref, bd_ref, bs_ref, bm_ref, lam_ref, subg_ref, o_ref,
               qq_ref, s_ref, m_ref, acc_ref, *, t, s_len):
    qi = pl.program_id(2)
    qt = qt_ref[0, 0]
    feat = lax.broadcasted_iota(jnp.int32, (HEAD_W, 1), 0)
    zero = jnp.zeros_like(qt)
    qq_ref[:, 0:t] = jnp.where(feat < DA_HEAD_DIM, qt, zero)
    qq_ref[:, t:2 * t] = jnp.where(feat >= DA_HEAD_DIM, qt, zero)
    qc = min(ATTN_QUERY_CHUNK, t)
    chunks_per_map = t // qc

    def bias_cols(read):
        return lambda c: read(pl.ds((c % chunks_per_map) * qc, qc))

    def k_block(kb):
        return k_ref[0, 0, pl.ds(pl.multiple_of(kb * t, t), t), :]

    def vt_block(kb):
        return vt_ref[0, 0, :, pl.ds(pl.multiple_of(kb * t, t), t)]

    state = (m_ref, acc_ref)
    meta_sel = jnp.minimum(qi, 1)
    k_meta = k_ref[0, 0, pl.ds(s_len, META_BLK), :]
    _attend_prologue(qq_ref, s_ref, k_meta)
    _attend_block(qq_ref, s_ref, 0, k_meta, vt_ref[0, 0, :, pl.ds(s_len, META_BLK)],
                  bias_cols(lambda cs: bm_ref[0, meta_sel, :, cs]), k_block(0), *state, first=True)

    def far(kb):
        _attend_block(qq_ref, s_ref, kb + 1, k_block(kb), vt_block(kb), None, k_block(kb + 1),
                      *state, first=False)

    _walk_far_blocks(qi - 1, far)

    @pl.when(qi >= 1)
    def _():
        _attend_block(qq_ref, s_ref, qi, k_block(qi - 1), vt_block(qi - 1),
                      bias_cols(lambda cs: bs_ref[0, :, cs]), k_block(qi), *state, first=False)

    _attend_block(qq_ref, s_ref, qi + 1, k_block(qi), vt_block(qi),
                  bias_cols(lambda cs: bd_ref[0, :, cs]), None, *state, first=False)

    ot = _normalised(acc_ref)
    lp = lam_ref[...]
    lam = (jnp.exp(jnp.sum(lp[0:1] * lp[1:2], axis=-1, keepdims=True))
           - jnp.exp(jnp.sum(lp[2:3] * lp[3:4], axis=-1, keepdims=True)) + LAMBDA_INIT)
    d = (ot[:, :t] - lam * ot[:, t:]).T
    o_ref[0] = (_rms(d, subg_ref[...]) * (1.0 - LAMBDA_INIT)).astype(BF16)


def _attn_specs(t, lp, q_w):
    return [
        pl.BlockSpec((1, 1, q_w, t), lambda bi, h, qi: (bi, h, 0, qi)),
        pl.BlockSpec((1, 1, lp, q_w), lambda bi, h, qi: (bi, h, 0, 0)),
        pl.BlockSpec((1, 1, VT_ROWS, lp), lambda bi, h, qi: (bi, h, 0, 0)),
    ]


def _da_attention(qt, k, vt, bias_d, bias_s, bias_m, lam_p, sub_g, *, t, s_len):
    b, _, lp, _ = k.shape
    nq = s_len // t
    vmem = (2 * 2 * lp * HEAD_W * 2 + 2 * 2 * t * t * 4 + 2 * 2 * t * META_BLK * 4
            + 2 * 2 * t * HEAD_W * 2 + 3 * 2 * t * HEAD_W * 4
            + 6 * t * ATTN_QUERY_CHUNK * 4)
    return pl.pallas_call(
        functools.partial(_da_kernel, t=t, s_len=s_len),
        out_shape=jax.ShapeDtypeStruct((b, s_len, DA_W), BF16),
        grid=(b, HEADS, nq),
        in_specs=_attn_specs(t, lp, HEAD_W) + [
            pl.BlockSpec((1, t, t), lambda bi, h, qi: (h, 0, 0)),
            pl.BlockSpec((1, t, t), lambda bi, h, qi: (h, 0, 0)),
            pl.BlockSpec((1, 2, META_BLK, t), lambda bi, h, qi: (h, 0, 0, 0)),
            _resident((4, DA_HEAD_DIM), lambda bi, h, qi: (0, 0)),
            _resident((1, HEAD_W), lambda bi, h, qi: (0, 0)),
        ],
        out_specs=pl.BlockSpec((1, t, HEAD_W), lambda bi, h, qi: (bi, qi, h)),
        scratch_shapes=[pltpu.VMEM((HEAD_W, 2 * t), BF16),
                        pltpu.VMEM((ATTN_SCORE_BUFFERS, t, min(ATTN_QUERY_CHUNK, t)), F32),
                        pltpu.VMEM((1, 2 * t), F32), pltpu.VMEM((VT_ROWS, 2 * t), F32)],
        compiler_params=_params(("parallel", "parallel", "arbitrary"), vmem),
        name="diff_attention",
    )(qt, k, vt, bias_d, bias_s, bias_m, lam_p, sub_g)


def _mla_kernel(qt_ref, k_ref, vt_ref, o_ref, s_ref, m_ref, acc_ref, *, tq, tk, s_len):
    qi = pl.program_id(2)
    qv_ref = qt_ref.at[0, 0]
    qc = s_ref.shape[2]
    diag_blocks = tq // tk

    def k_block(kb):
        return k_ref[0, 0, pl.ds(pl.multiple_of(kb * tk, tk), tk), :]

    def vt_block(kb):
        return vt_ref[0, 0, :, pl.ds(pl.multiple_of(kb * tk, tk), tk)]

    state = (m_ref, acc_ref)
    k_meta = k_ref[0, 0, pl.ds(s_len, META_BLK), :]
    key = lax.broadcasted_iota(jnp.int32, (META_BLK, qc), 0)
    meta_mask = jnp.where(key < N_META, 0.0, NEG_INF)
    _attend_prologue(qv_ref, s_ref, k_meta)
    _attend_block(qv_ref, s_ref, 0, k_meta, vt_ref[0, 0, :, pl.ds(s_len, META_BLK)],
                  lambda c: meta_mask, k_block(0), *state, first=True)

    def far(kb):
        _attend_block(qv_ref, s_ref, kb + 1, k_block(kb), vt_block(kb), None, k_block(kb + 1),
                      *state, first=False)

    first_diag = qi * diag_blocks
    _walk_far_blocks(first_diag, far)

    key = lax.broadcasted_iota(jnp.int32, (tk, qc), 0)
    query = lax.broadcasted_iota(jnp.int32, (tk, qc), 1)
    for d in range(diag_blocks):
        kb = first_diag + d
        causal = lambda c, d=d: jnp.where(key + d * tk <= query + c * qc, 0.0, NEG_INF)
        _attend_block(qv_ref, s_ref, kb + 1, k_block(kb), vt_block(kb), causal,
                      k_block(kb + 1) if d + 1 < diag_blocks else None, *state, first=False)

    o_ref[0] = _normalised(acc_ref).T.astype(BF16)


def _mla_attention(qt, k, vt, *, tq, tk, s_len):
    b, _, lp, _ = k.shape
    nq = s_len // tq
    qc = min(ATTN_QUERY_CHUNK, tq)
    vmem = (2 * lp * (MLA_QK_W + HEAD_W) * 2 + 2 * tq * MLA_QK_W * 2
            + 2 * tq * HEAD_W * 2 + 3 * tq * HEAD_W * 4 + (ATTN_SCORE_BUFFERS + 4) * tk * qc * 4)
    return pl.pallas_call(
        functools.partial(_mla_kernel, tq=tq, tk=tk, s_len=s_len),
        out_shape=jax.ShapeDtypeStruct((b, s_len, DA_W), BF16),
        grid=(b, HEADS, nq),
        in_specs=_attn_specs(tq, lp, MLA_QK_W),
        out_specs=pl.BlockSpec((1, tq, HEAD_W), lambda bi, h, qi: (bi, qi, h)),
        scratch_shapes=[pltpu.VMEM((ATTN_SCORE_BUFFERS, tk, qc), F32),
                        pltpu.VMEM((1, tq), F32), pltpu.VMEM((VT_ROWS, tq), F32)],
        compiler_params=_params(("parallel", "parallel", "arbitrary"), vmem),
        name="mla_attention",
    )(qt, k, vt)


def _merge_kernel(h_ref, pre_ref, post_ref, yda_ref, ymla_ref, wg0_ref, wg1_ref, b0_ref, b1_ref,
                  wbd_ref, wbm_ref, wo_ref, o_ref, hn_ref):
    j = pl.program_id(1)

    @pl.when(j == 0)
    def _():
        hn_ref[...] = _rms(h_ref[0], pre_ref[...]).astype(BF16)
        o_ref[...] = jnp.zeros_like(o_ref)

    hn = hn_ref[...]
    g0 = jax.nn.sigmoid(_dot(hn, wg0_ref[...]) + b0_ref[...])
    g1 = jax.nn.sigmoid(_dot(hn, wg1_ref[...]) + b1_ref[...])
    merged = g0 * _dot(yda_ref[0], wbd_ref[...]) + g1 * _dot(ymla_ref[0], wbm_ref[...])
    o_ref[0] += _dot(merged.astype(BF16), wo_ref[...])

    @pl.when(j == pl.num_programs(1) - 1)
    def _():
        o_ref[0] = h_ref[0] + _rms(o_ref[0], post_ref[...])


def _merge(h1, pre_g, post_g, y_da, y_mla, w_gate, b_gate, wb_da, wb_mla, w_out, *, s_len):
    b, _, d = h1.shape
    yw = y_da.shape[-1]
    tm = _row_tile(s_len, 512)
    tn = _col_tile(d, 512)
    nt = s_len // tm
    nj = d // tn
    vmem = (2 * 2 * tm * d * 4 + tm * d * 2 + 2 * 2 * tm * yw * 2
            + 2 * (3 * d * tn + 2 * yw * tn) * 2 + 6 * tm * tn * 4 + tm * d * 4)
    return pl.pallas_call(
        _merge_kernel,
        out_shape=jax.ShapeDtypeStruct((b, s_len, d), F32),
        grid=(b * nt, nj),
        in_specs=[
            pl.BlockSpec((1, tm, d), lambda i, j: (i // nt, i % nt, 0)),
            _resident((1, d), lambda i, j: (0, 0)),
            _resident((1, d), lambda i, j: (0, 0)),
            pl.BlockSpec((1, tm, yw), lambda i, j: (i // nt, i % nt, 0)),
            pl.BlockSpec((1, tm, yw), lambda i, j: (i // nt, i % nt, 0)),
            pl.BlockSpec((d, tn), lambda i, j: (0, j)),
            pl.BlockSpec((d, tn), lambda i, j: (0, nj + j)),
            pl.BlockSpec((1, tn), lambda i, j: (0, j)),
            pl.BlockSpec((1, tn), lambda i, j: (0, nj + j)),
            pl.BlockSpec((yw, tn), lambda i, j: (0, j)),
            pl.BlockSpec((yw, tn), lambda i, j: (0, j)),
            pl.BlockSpec((tn, d), lambda i, j: (j, 0)),
        ],
        out_specs=pl.BlockSpec((1, tm, d), lambda i, j: (i // nt, i % nt, 0)),
        scratch_shapes=[pltpu.VMEM((tm, d), BF16)],
        compiler_params=_params(("parallel", "arbitrary"), vmem),
        name="gated_merge_out_proj",
    )(h1, pre_g, post_g, y_da, y_mla, w_gate, w_gate, b_gate, b_gate, wb_da, wb_mla, w_out)


def _rope_tables(s_len):
    half = MLA_ROPE // 2
    pos = jnp.concatenate([N_META + jnp.arange(s_len), jnp.arange(N_META),
                           jnp.zeros((META_BLK - N_META,), jnp.int32)]).astype(F32)
    inv = ROPE_THETA ** (-jnp.arange(half, dtype=F32) * 2.0 / MLA_ROPE)
    ang = pos[:, None] * inv[None, :]
    cos, sin = jnp.cos(ang), jnp.sin(ang)
    pad = jnp.zeros((pos.shape[0], HEAD_W - MLA_ROPE), F32)
    return (jnp.concatenate([cos, cos, pad], axis=-1),
            jnp.concatenate([-sin, sin, pad], axis=-1))


def _swap_halves(w):
    half = w.shape[-1] // 2
    return jnp.concatenate([w[..., half:], w[..., :half]], axis=-1)


def _pad_lanes(w, width):
    return jnp.concatenate([w, jnp.zeros(w.shape[:-1] + (width - w.shape[-1],), w.dtype)], axis=-1)


def kernel(x, meta_tokens, rel_bias_table, ffn1_pre_g, ffn1_post_g, ffn1_w_gate, ffn1_w_up, ffn1_w_down, mix_pre_g, mix_post_g, w_in, b_gate, da_lambda_q1, da_lambda_k1, da_lambda_q2, da_lambda_k2, da_sub_g, mla_q_norm_g, mla_kv_norm_g, mla_w_uq, mla_w_ukv, w_branch_da, w_branch_mla, w_out, ffn2_pre_g, ffn2_post_g, ffn2_w_gate, ffn2_w_up, ffn2_w_down):
    b, s_len, d = x.shape
    assert ffn1_pre_g.shape[0] == 1, "single-layer trunk"
    lp = s_len + META_BLK
    t = _col_tile(s_len, 512)
    row = lambda g: g.reshape(1, -1).astype(F32)

    meta = jnp.broadcast_to(meta_tokens.astype(x.dtype)[None], (b, N_META, d))
    h0 = jnp.concatenate([x, meta, jnp.zeros((b, META_BLK - N_META, d), x.dtype)], axis=1)

    h1 = _ffn(h0.reshape(b * lp, d), row(ffn1_pre_g), row(ffn1_post_g),
              ffn1_w_gate[0].astype(BF16), ffn1_w_up[0].astype(BF16), ffn1_w_down[0].astype(BF16),
              name="ffn1").reshape(b, lp, d)

    w = w_in[0]
    o_k, o_v, o_cq = DA_W, 2 * DA_W, 3 * DA_W
    o_kr = o_cq + MLA_Q_RANK + MLA_KV_RANK
    o_gate = o_kr + MLA_ROPE
    qt_da, k_da, vt_da = _da_proj(
        h1, row(mix_pre_g), w[:, :o_k].T.astype(BF16), w[:, o_k:o_v].astype(BF16),
        w[:, o_v:o_cq].T.astype(BF16))

    kr = w[:, o_kr:o_gate]
    w_c = jnp.concatenate([w[:, o_cq:o_kr], _pad_lanes(kr, HEAD_W),
                           _pad_lanes(_swap_halves(kr), HEAD_W)], axis=-1).astype(BF16)
    uq = mla_w_uq[0].reshape(MLA_Q_RANK, HEADS, MLA_NOPE + MLA_ROPE)
    uq_rope = uq[..., MLA_NOPE:]
    w_uqt = jnp.concatenate([uq[..., :MLA_NOPE], _pad_lanes(uq_rope, HEAD_W),
                             _pad_lanes(_swap_halves(uq_rope), HEAD_W)],
                            axis=-1).reshape(MLA_Q_RANK, HEADS * _UQ_HEAD_W).T.astype(BF16)
    ukv = mla_w_ukv[0].reshape(MLA_KV_RANK, HEADS, 2 * HEAD_W)
    w_uk = ukv[..., :HEAD_W].reshape(MLA_KV_RANK, DA_W).astype(BF16)
    w_uvt = ukv[..., HEAD_W:].reshape(MLA_KV_RANK, DA_W).T.astype(BF16)
    cos, sin = _rope_tables(s_len)
    qt_mla, k_mla, vt_mla = _mla_proj(
        h1, row(mix_pre_g), cos, sin, cos.T, sin.T, w_c, row(mla_q_norm_g), row(mla_kv_norm_g),
        w_uqt, w_uk, w_uvt)

    bias_d, bias_s, bias_m = _bias_tiles(rel_bias_table.astype(F32), t)
    lam_p = jnp.concatenate([da_lambda_q1, da_lambda_k1, da_lambda_q2, da_lambda_k2], axis=0).astype(F32)
    y_da = _da_attention(qt_da, k_da, vt_da, bias_d, bias_s, bias_m, lam_p, row(da_sub_g),
                         t=t, s_len=s_len)
    y_mla = _mla_attention(qt_mla, k_mla, vt_mla, tq=_col_tile(s_len, 2 * t), tk=t, s_len=s_len)

    h2 = _merge(h1, row(mix_pre_g), row(mix_post_g), y_da, y_mla,
                w[:, o_gate:].astype(BF16), row(b_gate),
                w_branch_da[0].astype(BF16), w_branch_mla[0].astype(BF16), w_out[0].astype(BF16),
                s_len=s_len)

    out = _ffn(h2.reshape(b * s_len, d), row(ffn2_pre_g), row(ffn2_post_g),
               ffn2_w_gate[0].astype(BF16), ffn2_w_up[0].astype(BF16), ffn2_w_down[0].astype(BF16),
               name="ffn2")
    return out.reshape(b, s_len, d)
```

```python
import functools
import math

import numpy as np
import jax
import jax.numpy as jnp
from jax import lax
from jax.experimental import pallas as pl
from jax.experimental.pallas import tpu as pltpu

F32 = jnp.float32
BF16 = jnp.bfloat16

N_META = 16
RMS_EPS = 1e-6
NEG_INF = -1e30
HEADS = 8
HEAD_W = 128
DA_HEAD_DIM = 64
DA_W = HEADS * HEAD_W
MLA_Q_RANK = 768
MLA_KV_RANK = 512
MLA_NOPE = 128
MLA_ROPE = 64
MLA_QK_W = 256
ROPE_THETA = 10000.0
REL_BUCKETS = 32
REL_MAX_DIST = 128
LAMBDA_INIT = 0.8 - 0.6 * math.exp(-0.3 * 0)
LOG2_E = math.log2(math.e)

V7X_LANES = 128
V7X_BF16_SUBLANES = 16
V7X_VMEM_BYTES = 64 * 1024 * 1024
VMEM_CAP_BYTES = V7X_VMEM_BYTES - 6 * 1024 * 1024

META_BLK = V7X_LANES


def _row_tile(rows, max_tile, multiple=V7X_BF16_SUBLANES):
    best = None
    for t in range(multiple, max_tile + 1, multiple):
        if rows % t == 0:
            best = t
    assert best is not None, (rows, max_tile)
    return best


def _col_tile(cols, max_tile):
    return _row_tile(cols, max_tile, V7X_LANES)


def _params(semantics, vmem_bytes):
    return pltpu.CompilerParams(
        dimension_semantics=semantics,
        vmem_limit_bytes=int(min(max(vmem_bytes, 16 * 1024 * 1024), VMEM_CAP_BYTES)))


def _resident(block_shape, index_map):
    return pl.BlockSpec(block_shape, index_map, pipeline_mode=pl.Buffered(1))


def _rms(x, g):
    ms = jnp.mean(x * x, axis=-1, keepdims=True)
    return x * lax.rsqrt(ms + RMS_EPS) * g


def _dot(a, b):
    return jnp.dot(a, b, preferred_element_type=F32)


def _dot_nt(a, b):
    return lax.dot_general(a, b, (((1,), (1,)), ((), ())), preferred_element_type=F32)


def _ffn_kernel(x_ref, pre_ref, post_ref, wg_ref, wu_ref, wd_ref, o_ref, xn_ref):
    k = pl.program_id(1)
    last = pl.num_programs(1) - 1

    def partial_out(xn):
        g = _dot(xn, wg_ref[...])
        u = _dot(xn, wu_ref[...])
        a = (g * jax.nn.sigmoid(g) * u).astype(BF16)
        return _dot(a, wd_ref[...])

    @pl.when(k == 0)
    def _():
        xn = _rms(x_ref[...], pre_ref[...]).astype(BF16)
        xn_ref[...] = xn
        o_ref[...] = partial_out(xn)

    @pl.when(jnp.logical_and(k > 0, k < last))
    def _():
        o_ref[...] += partial_out(xn_ref[...])

    @pl.when(k == last)
    def _():
        f = o_ref[...] + partial_out(xn_ref[...])
        o_ref[...] = x_ref[...] + 0.5 * _rms(f, post_ref[...])


def _ffn(x, pre_g, post_g, wg, wu, wd, *, name):
    rows, d = x.shape
    f = wg.shape[1]
    tm = _row_tile(rows, 704)
    tf = _col_tile(f, 512)
    assert f // tf >= 2, "the kernel treats the first and the last d_ff tile as different steps"
    vmem = (2 * 2 * tm * d * 4
            + tm * d * 2
            + 2 * 3 * d * tf * 2
            + 4 * tm * tf * 4
            + tm * d * 4)
    return pl.pallas_call(
        _ffn_kernel,
        out_shape=jax.ShapeDtypeStruct((rows, d), F32),
        grid=(rows // tm, f // tf),
        in_specs=[
            pl.BlockSpec((tm, d), lambda i, k: (i, 0)),
            _resident((1, d), lambda i, k: (0, 0)),
            _resident((1, d), lambda i, k: (0, 0)),
            pl.BlockSpec((d, tf), lambda i, k: (0, k)),
            pl.BlockSpec((d, tf), lambda i, k: (0, k)),
            pl.BlockSpec((tf, d), lambda i, k: (k, 0)),
        ],
        out_specs=pl.BlockSpec((tm, d), lambda i, k: (i, 0)),
        scratch_shapes=[pltpu.VMEM((tm, d), BF16)],
        compiler_params=_params(("parallel", "arbitrary"), vmem),
        name=name,
    )(x, pre_g, post_g, wg, wu, wd)


VT_ROWS = HEAD_W + V7X_BF16_SUBLANES


def _store_vt(vt_ref, vt):
    tm = vt.shape[1]
    vt_ref[0, :, 0:HEAD_W, :] = vt.reshape(HEADS, HEAD_W, tm).astype(BF16)
    extra = lax.broadcasted_iota(jnp.int32, (HEADS, VT_ROWS - HEAD_W, tm), 1)
    vt_ref[0, :, HEAD_W:VT_ROWS, :] = jnp.where(extra == 0, 1.0, 0.0).astype(BF16)


def _da_proj_kernel(h_ref, g_ref, wqt_ref, wk_ref, wvt_ref, qt_ref, k_ref, vt_ref):
    hn = _rms(h_ref[...], g_ref[...]).astype(BF16)
    tm = hn.shape[0]
    qt = _dot_nt(wqt_ref[...], hn) * (DA_HEAD_DIM ** -0.5 * LOG2_E)
    qt_ref[0] = qt.reshape(HEADS, HEAD_W, tm).astype(BF16)
    k = _dot(hn, wk_ref[...])
    for h in range(HEADS):
        k_ref[0, h] = k[:, h * HEAD_W:(h + 1) * HEAD_W].astype(BF16)
    _store_vt(vt_ref, _dot_nt(wvt_ref[...], hn))


_C_CKV = MLA_Q_RANK
_C_KR1 = _C_CKV + MLA_KV_RANK
_C_KR2 = _C_KR1 + HEAD_W
_C_END = _C_KR2 + HEAD_W
_UQ_HEAD_W = 3 * HEAD_W


def _mla_proj_kernel(h_ref, g_ref, cos_ref, sin_ref, cost_ref, sint_ref, wc_ref, gq_ref, gkv_ref,
                     wuqt_ref, wuk_ref, wuvt_ref, qt_ref, k_ref, vt_ref):
    hn = _rms(h_ref[...], g_ref[...]).astype(BF16)
    tm = hn.shape[0]
    scale = (MLA_NOPE + MLA_ROPE) ** -0.5 * LOG2_E

    c_all = _dot(hn, wc_ref[...])

    def proj(lo, hi):
        return c_all[:, lo:hi]

    cq = _rms(proj(0, _C_CKV), gq_ref[...]).astype(BF16)
    qt = _dot_nt(wuqt_ref[...], cq)
    cos_t = cost_ref[...]
    sin_t = sint_ref[...]
    for h in range(HEADS):
        base = h * _UQ_HEAD_W
        nope = qt[base:base + HEAD_W]
        r1 = qt[base + HEAD_W:base + 2 * HEAD_W]
        r2 = qt[base + 2 * HEAD_W:base + 3 * HEAD_W]
        qt_ref[0, h, 0:HEAD_W, :] = (nope * scale).astype(BF16)
        qt_ref[0, h, HEAD_W:MLA_QK_W, :] = ((r1 * cos_t + r2 * sin_t) * scale).astype(BF16)

    ckv = _rms(proj(_C_CKV, _C_KR1), gkv_ref[...]).astype(BF16)
    k_nope = _dot(ckv, wuk_ref[...])
    k_rope = (proj(_C_KR1, _C_KR2) * cos_ref[...] + proj(_C_KR2, _C_END) * sin_ref[...]).astype(BF16)
    for h in range(HEADS):
        k_ref[0, h, :, 0:HEAD_W] = k_nope[:, h * HEAD_W:(h + 1) * HEAD_W].astype(BF16)
        k_ref[0, h, :, HEAD_W:MLA_QK_W] = k_rope
    _store_vt(vt_ref, _dot_nt(wuvt_ref[...], ckv))


def _proj_specs(b, lp, tm, q_w):
    nt = lp // tm
    rows = lambda w: pl.BlockSpec((1, HEADS, tm, w), lambda i: (i // nt, 0, i % nt, 0))
    cols = lambda w: pl.BlockSpec((1, HEADS, w, tm), lambda i: (i // nt, 0, 0, i % nt))
    shapes = [jax.ShapeDtypeStruct((b, HEADS, q_w, lp), BF16),
              jax.ShapeDtypeStruct((b, HEADS, lp, q_w), BF16),
              jax.ShapeDtypeStruct((b, HEADS, VT_ROWS, lp), BF16)]
    return shapes, [cols(q_w), rows(q_w), cols(VT_ROWS)]


def _da_proj(h1, g, wqt, wk, wvt):
    b, lp, d = h1.shape
    tm = _col_tile(lp, 704)
    nt = lp // tm
    shapes, out_specs = _proj_specs(b, lp, tm, HEAD_W)
    vmem = (2 * tm * d * 4 + tm * d * 2 + 3 * DA_W * d * 2
            + 2 * 3 * tm * DA_W * 2 + 3 * tm * DA_W * 4)
    return pl.pallas_call(
        _da_proj_kernel,
        out_shape=shapes,
        grid=(b * nt,),
        in_specs=[
            pl.BlockSpec((None, tm, d), lambda i: (i // nt, i % nt, 0)),
            _resident((1, d), lambda i: (0, 0)),
            _resident(wqt.shape, lambda i: (0, 0)),
            _resident(wk.shape, lambda i: (0, 0)),
            _resident(wvt.shape, lambda i: (0, 0)),
        ],
        out_specs=out_specs,
        compiler_params=_params(("parallel",), vmem),
        name="da_in_proj",
    )(h1, g, wqt, wk, wvt)


def _mla_proj(h1, g, cos, sin, cos_t, sin_t, wc, gq, gkv, wuqt, wuk, wuvt):
    b, lp, d = h1.shape
    tm = _col_tile(lp, 704)
    nt = lp // tm
    shapes, out_specs = _proj_specs(b, lp, tm, MLA_QK_W)
    vmem = (2 * tm * d * 4 + tm * d * 2
            + (d * _C_END + HEADS * _UQ_HEAD_W * MLA_Q_RANK + 2 * DA_W * MLA_KV_RANK) * 2
            + 2 * tm * HEADS * (2 * MLA_QK_W + HEAD_W) * 2
            + 2 * tm * HEADS * _UQ_HEAD_W * 4)
    return pl.pallas_call(
        _mla_proj_kernel,
        out_shape=shapes,
        grid=(b * nt,),
        in_specs=[
            pl.BlockSpec((None, tm, d), lambda i: (i // nt, i % nt, 0)),
            _resident((1, d), lambda i: (0, 0)),
            pl.BlockSpec((tm, HEAD_W), lambda i: (i % nt, 0)),
            pl.BlockSpec((tm, HEAD_W), lambda i: (i % nt, 0)),
            pl.BlockSpec((HEAD_W, tm), lambda i: (0, i % nt)),
            pl.BlockSpec((HEAD_W, tm), lambda i: (0, i % nt)),
            _resident(wc.shape, lambda i: (0, 0)),
            _resident((1, MLA_Q_RANK), lambda i: (0, 0)),
            _resident((1, MLA_KV_RANK), lambda i: (0, 0)),
            _resident(wuqt.shape, lambda i: (0, 0)),
            _resident(wuk.shape, lambda i: (0, 0)),
            _resident(wuvt.shape, lambda i: (0, 0)),
        ],
        out_specs=out_specs,
        compiler_params=_params(("parallel",), vmem),
        name="mla_in_proj",
    )(h1, g, cos, sin, cos_t, sin_t, wc, gq, gkv, wuqt, wuk, wuvt)


def _t5_bucket(rel):
    n = np.maximum(rel, 0)
    max_exact = REL_BUCKETS // 2
    n_f = np.maximum(n, 1).astype(np.float64)
    large = max_exact + (np.log(n_f / max_exact) / math.log(REL_MAX_DIST / max_exact)
                         * (REL_BUCKETS - max_exact)).astype(np.int32)
    large = np.minimum(large, REL_BUCKETS - 1)
    return np.where(n < max_exact, n, large).astype(np.int32)


def _bucket_tiles(t):
    j = np.arange(t)[:, None]
    i = np.arange(t)[None, :]
    diag = np.where(j <= i, _t5_bucket(i - j), -1)
    sub = _t5_bucket(t + i - j)
    assert _t5_bucket(np.array([t + 1]))[0] == REL_BUCKETS - 1
    jm = np.arange(META_BLK)[:, None]
    meta_first = np.where(jm < N_META, _t5_bucket(N_META + i - jm), -1)
    meta_rest = np.where(jm < N_META, REL_BUCKETS - 1, -1) + 0 * i
    return (diag.astype(np.int32), np.stack([np.full_like(sub, -1), sub]).astype(np.int32),
            np.stack([meta_first, meta_rest]).astype(np.int32))


def _bias_kernel(table_ref, bd_ref, bs_ref, bm_ref, od_ref, os_ref, om_ref):
    h = pl.program_id(0)
    far = table_ref[REL_BUCKETS - 1, h]

    def lookup(bucket):
        acc = jnp.zeros(bucket.shape, F32)
        for b in range(REL_BUCKETS - 1):
            acc = jnp.where(bucket == b, (table_ref[b, h] - far) * LOG2_E, acc)
        return jnp.where(bucket < 0, NEG_INF, acc)

    od_ref[0] = lookup(bd_ref[...])
    os_ref[0, 0] = lookup(bs_ref[0])
    os_ref[0, 1] = lookup(bs_ref[1])
    om_ref[0, 0] = lookup(bm_ref[0])
    om_ref[0, 1] = lookup(bm_ref[1])


def _bias_tiles(table, t):
    bd, bs, bm = _bucket_tiles(t)
    return pl.pallas_call(
        _bias_kernel,
        out_shape=[jax.ShapeDtypeStruct((HEADS, t, t), F32),
                   jax.ShapeDtypeStruct((HEADS, 2, t, t), F32),
                   jax.ShapeDtypeStruct((HEADS, 2, META_BLK, t), F32)],
        grid=(HEADS,),
        in_specs=[
            pl.BlockSpec(memory_space=pltpu.SMEM),
            pl.BlockSpec((t, t), lambda h: (0, 0)),
            pl.BlockSpec((2, t, t), lambda h: (0, 0, 0)),
            pl.BlockSpec((2, META_BLK, t), lambda h: (0, 0, 0)),
        ],
        out_specs=[pl.BlockSpec((1, t, t), lambda h: (h, 0, 0)),
                   pl.BlockSpec((1, 2, t, t), lambda h: (h, 0, 0, 0)),
                   pl.BlockSpec((1, 2, META_BLK, t), lambda h: (h, 0, 0, 0))],
        compiler_params=_params(("parallel",), 12 * t * t * 4),
        name="rel_bias_tiles",
    )(table, jnp.asarray(bd), jnp.asarray(bs), jnp.asarray(bm))


ATTN_QUERY_CHUNK = 256


ATTN_SCORE_BUFFERS = 4


def _chunking(qt_ref, s_ref):
    n_chunks = qt_ref.shape[1] // s_ref.shape[2]
    return n_chunks, min(s_ref.shape[0] - 1, n_chunks)


def _slot(s_ref, ordinal, n_chunks, c):
    n_buf = s_ref.shape[0]
    if n_chunks % n_buf == 0:
        return c % n_buf
    return (ordinal * n_chunks + c) & (n_buf - 1)


def _issue_scores(qt_ref, s_ref, k_blk, c, slot):
    qc = s_ref.shape[2]
    s_ref[slot, 0:k_blk.shape[0], :] = _dot(k_blk, qt_ref[:, pl.ds(c * qc, qc)])


def _attend_prologue(qt_ref, s_ref, k_blk):
    n_chunks, ahead = _chunking(qt_ref, s_ref)
    for c in range(ahead):
        _issue_scores(qt_ref, s_ref, k_blk, c, _slot(s_ref, 0, n_chunks, c))


def _attend_block(qt_ref, s_ref, ordinal, k_blk, vt_blk, bias_fn, next_k, m_ref, acc_ref):
    qc = s_ref.shape[2]
    keys = k_blk.shape[0]
    n_chunks, ahead = _chunking(qt_ref, s_ref)
    for c in range(n_chunks):
        sl = pl.ds(c * qc, qc)
        nxt = c + ahead
        if nxt < n_chunks:
            _issue_scores(qt_ref, s_ref, k_blk, nxt, _slot(s_ref, ordinal, n_chunks, nxt))
        elif next_k is not None:
            _issue_scores(qt_ref, s_ref, next_k, nxt - n_chunks, _slot(s_ref, ordinal, n_chunks, nxt))
        s = s_ref[_slot(s_ref, ordinal, n_chunks, c), 0:keys, :]
        bias = bias_fn(c) if bias_fn is not None else None
        if bias is not None:
            s = s + bias
        m_prev = m_ref[:, sl]
        m_new = jnp.maximum(m_prev, jnp.max(s, axis=0, keepdims=True))
        alpha = jnp.exp2(m_prev - m_new)
        p = jnp.exp2(s - m_new)
        acc_ref[:, sl] = alpha * acc_ref[:, sl] + _dot(vt_blk, p.astype(BF16))
        m_ref[:, sl] = m_new


def _attend_init(m_ref, acc_ref):
    m_ref[...] = jnp.full(m_ref.shape, NEG_INF, F32)
    acc_ref[...] = jnp.zeros(acc_ref.shape, F32)


FAR_UNROLL = 4


def _walk_far_blocks(n_far, attend_one):
    n_far = jnp.maximum(n_far, 0)

    def body(first, count):
        for u in range(count):
            attend_one(first + u)

    def loop_body(i, carry):
        body(i * FAR_UNROLL, FAR_UNROLL)
        return carry

    lax.fori_loop(0, n_far >> (FAR_UNROLL.bit_length() - 1), loop_body, 0)
    count = FAR_UNROLL // 2
    while count >= 1:
        first = n_far - (n_far & (2 * count - 1))

        @pl.when((n_far & count) != 0)
        def _(first=first, count=count):
            body(first, count)

        count //= 2


def _normalised(acc_ref):
    acc = acc_ref[...]
    return acc[0:HEAD_W] / acc[HEAD_W:HEAD_W + 1]


def _da_kernel(qt_ref, k_ref, vt_ref, bd_ref, bs_ref, bm_ref, lam_ref, subg_ref, o_ref,
               qq_ref, s_ref, m_ref, acc_ref, *, t, s_len):
    qi = pl.program_id(2)
    qt = qt_ref[0, 0]
    feat = lax.broadcasted_iota(jnp.int32, (HEAD_W, 1), 0)
    zero = jnp.zeros_like(qt)
    qq_ref[:, 0:t] = jnp.where(feat < DA_HEAD_DIM, qt, zero)
    qq_ref[:, t:2 * t] = jnp.where(feat >= DA_HEAD_DIM, qt, zero)
    qc = min(ATTN_QUERY_CHUNK, t)
    chunks_per_map = t // qc

    def bias_cols(read):
        return lambda c: read(pl.ds((c % chunks_per_map) * qc, qc))

    def k_block(kb):
        return k_ref[0, 0, pl.ds(pl.multiple_of(kb * t, t), t), :]

    def vt_block(kb):
        return vt_ref[0, 0, :, pl.ds(pl.multiple_of(kb * t, t), t)]

    state = (m_ref, acc_ref)
    n_far = jnp.maximum(qi - 1, 0)
    near_sel = jnp.minimum(qi, 1)
    prev_kb = jnp.maximum(qi - 1, 0)
    k_meta = k_ref[0, 0, pl.ds(s_len, META_BLK), :]
    _attend_init(*state)
    _attend_prologue(qq_ref, s_ref, k_block(jnp.where(qi >= 2, 0, qi)))

    def far(kb):
        next_kb = jnp.where(kb == n_far - 1, qi, kb + 1)
        _attend_block(qq_ref, s_ref, kb, k_block(kb), vt_block(kb), None, k_block(next_kb), *state)

    _walk_far_blocks(n_far, far)
    _attend_block(qq_ref, s_ref, n_far, k_block(qi), vt_block(qi),
                  bias_cols(lambda cs: bd_ref[0, :, cs]), k_block(prev_kb), *state)
    _attend_block(qq_ref, s_ref, n_far + 1, k_block(prev_kb), vt_block(prev_kb),
                  bias_cols(lambda cs: bs_ref[0, near_sel, :, cs]), k_meta, *state)
    _attend_block(qq_ref, s_ref, n_far + 2, k_meta, vt_ref[0, 0, :, pl.ds(s_len, META_BLK)],
                  bias_cols(lambda cs: bm_ref[0, near_sel, :, cs]), None, *state)

    ot = _normalised(acc_ref)
    lp = lam_ref[...]
    lam = (jnp.exp(jnp.sum(lp[0:1] * lp[1:2], axis=-1, keepdims=True))
           - jnp.exp(jnp.sum(lp[2:3] * lp[3:4], axis=-1, keepdims=True)) + LAMBDA_INIT)
    d = (ot[:, :t] - lam * ot[:, t:]).T
    o_ref[0] = (_rms(d, subg_ref[...]) * (1.0 - LAMBDA_INIT)).astype(BF16)


def _attn_specs(t, lp, q_w):
    return [
        pl.BlockSpec((1, 1, q_w, t), lambda bi, h, qi: (bi, h, 0, qi)),
        pl.BlockSpec((1, 1, lp, q_w), lambda bi, h, qi: (bi, h, 0, 0)),
        pl.BlockSpec((1, 1, VT_ROWS, lp), lambda bi, h, qi: (bi, h, 0, 0)),
    ]


def _da_attention(qt, k, vt, bias_d, bias_s, bias_m, lam_p, sub_g, *, t, s_len):
    b, _, lp, _ = k.shape
    nq = s_len // t
    vmem = (2 * 2 * lp * HEAD_W * 2 + 2 * 2 * t * t * 4 + 2 * 2 * t * META_BLK * 4
            + 2 * 2 * t * HEAD_W * 2 + 3 * 2 * t * HEAD_W * 4
            + 6 * t * ATTN_QUERY_CHUNK * 4)
    return pl.pallas_call(
        functools.partial(_da_kernel, t=t, s_len=s_len),
        out_shape=jax.ShapeDtypeStruct((b, s_len, DA_W), BF16),
        grid=(b, HEADS, nq),
        in_specs=_attn_specs(t, lp, HEAD_W) + [
            pl.BlockSpec((1, t, t), lambda bi, h, qi: (h, 0, 0)),
            pl.BlockSpec((1, 2, t, t), lambda bi, h, qi: (h, 0, 0, 0)),
            pl.BlockSpec((1, 2, META_BLK, t), lambda bi, h, qi: (h, 0, 0, 0)),
            _resident((4, DA_HEAD_DIM), lambda bi, h, qi: (0, 0)),
            _resident((1, HEAD_W), lambda bi, h, qi: (0, 0)),
        ],
        out_specs=pl.BlockSpec((1, t, HEAD_W), lambda bi, h, qi: (bi, qi, h)),
        scratch_shapes=[pltpu.VMEM((HEAD_W, 2 * t), BF16),
                        pltpu.VMEM((ATTN_SCORE_BUFFERS, t, min(ATTN_QUERY_CHUNK, t)), F32),
                        pltpu.VMEM((1, 2 * t), F32), pltpu.VMEM((VT_ROWS, 2 * t), F32)],
        compiler_params=_params(("parallel", "parallel", "arbitrary"), vmem),
        name="diff_attention",
    )(qt, k, vt, bias_d, bias_s, bias_m, lam_p, sub_g)


def _mla_kernel(qt_ref, k_ref, vt_ref, o_ref, s_ref, m_ref, acc_ref, *, tq, tk, s_len):
    qi = pl.program_id(2)
    qv_ref = qt_ref.at[0, 0]
    qc = s_ref.shape[2]
    diag_blocks = tq // tk

    def k_block(kb):
        return k_ref[0, 0, pl.ds(pl.multiple_of(kb * tk, tk), tk), :]

    def vt_block(kb):
        return vt_ref[0, 0, :, pl.ds(pl.multiple_of(kb * tk, tk), tk)]

    state = (m_ref, acc_ref)
    k_meta = k_ref[0, 0, pl.ds(s_len, META_BLK), :]
    key = lax.broadcasted_iota(jnp.int32, (META_BLK, qc), 0)
    meta_mask = jnp.where(key < N_META, 0.0, NEG_INF)
    _attend_init(*state)
    _attend_prologue(qv_ref, s_ref, k_block(0))

    def far(kb):
        _attend_block(qv_ref, s_ref, kb, k_block(kb), vt_block(kb), None, k_block(kb + 1), *state)

    first_diag = qi * diag_blocks
    _walk_far_blocks(first_diag, far)

    key = lax.broadcasted_iota(jnp.int32, (tk, qc), 0)
    query = lax.broadcasted_iota(jnp.int32, (tk, qc), 1)
    for d in range(diag_blocks):
        kb = first_diag + d
        causal = lambda c, d=d: jnp.where(key + d * tk <= query + c * qc, 0.0, NEG_INF)
        _attend_block(qv_ref, s_ref, kb, k_block(kb), vt_block(kb), causal,
                      k_block(kb + 1) if d + 1 < diag_blocks else k_meta, *state)
    _attend_block(qv_ref, s_ref, first_diag + diag_blocks, k_meta,
                  vt_ref[0, 0, :, pl.ds(s_len, META_BLK)], lambda c: meta_mask, None, *state)

    o_ref[0] = _normalised(acc_ref).T.astype(BF16)


def _mla_attention(qt, k, vt, *, tq, tk, s_len):
    b, _, lp, _ = k.shape
    nq = s_len // tq
    qc = min(ATTN_QUERY_CHUNK, tq)
    vmem = (2 * lp * (MLA_QK_W + HEAD_W) * 2 + 2 * tq * MLA_QK_W * 2
            + 2 * tq * HEAD_W * 2 + 3 * tq * HEAD_W * 4 + (ATTN_SCORE_BUFFERS + 4) * tk * qc * 4)
    return pl.pallas_call(
        functools.partial(_mla_kernel, tq=tq, tk=tk, s_len=s_len),
        out_shape=jax.ShapeDtypeStruct((b, s_len, DA_W), BF16),
        grid=(b, HEADS, nq),
        in_specs=_attn_specs(tq, lp, MLA_QK_W),
        out_specs=pl.BlockSpec((1, tq, HEAD_W), lambda bi, h, qi: (bi, qi, h)),
        scratch_shapes=[pltpu.VMEM((ATTN_SCORE_BUFFERS, tk, qc), F32),
                        pltpu.VMEM((1, tq), F32), pltpu.VMEM((VT_ROWS, tq), F32)],
        compiler_params=_params(("parallel", "parallel", "arbitrary"), vmem),
        name="mla_attention",
    )(qt, k, vt)


def _merge_kernel(h_ref, pre_ref, post_ref, yda_ref, ymla_ref, wg0_ref, wg1_ref, b0_ref, b1_ref,
                  wbd_ref, wbm_ref, wo_ref, o_ref, hn_ref):
    j = pl.program_id(1)
    last = pl.num_programs(1) - 1

    def partial_out(hn):
        g0 = jax.nn.sigmoid(_dot(hn, wg0_ref[...]) + b0_ref[...])
        g1 = jax.nn.sigmoid(_dot(hn, wg1_ref[...]) + b1_ref[...])
        merged = g0 * _dot(yda_ref[0], wbd_ref[...]) + g1 * _dot(ymla_ref[0], wbm_ref[...])
        return _dot(merged.astype(BF16), wo_ref[...])

    @pl.when(j == 0)
    def _():
        hn = _rms(h_ref[0], pre_ref[...]).astype(BF16)
        hn_ref[...] = hn
        o_ref[0] = partial_out(hn)

    @pl.when(jnp.logical_and(j > 0, j < last))
    def _():
        o_ref[0] += partial_out(hn_ref[...])

    @pl.when(j == last)
    def _():
        m = o_ref[0] + partial_out(hn_ref[...])
        o_ref[0] = h_ref[0] + _rms(m, post_ref[...])


def _merge(h1, pre_g, post_g, y_da, y_mla, w_gate, b_gate, wb_da, wb_mla, w_out, *, s_len):
    b, _, d = h1.shape
    yw = y_da.shape[-1]
    tm = _row_tile(s_len, 512)
    tn = _col_tile(d, 512)
    nt = s_len // tm
    nj = d // tn
    assert nj >= 2, "the kernel treats the first and the last column tile as different steps"
    vmem = (2 * 2 * tm * d * 4 + tm * d * 2 + 2 * 2 * tm * yw * 2
            + 2 * (3 * d * tn + 2 * yw * tn) * 2 + 6 * tm * tn * 4 + tm * d * 4)
    return pl.pallas_call(
        _merge_kernel,
        out_shape=jax.ShapeDtypeStruct((b, s_len, d), F32),
        grid=(b * nt, nj),
        in_specs=[
            pl.BlockSpec((1, tm, d), lambda i, j: (i // nt, i % nt, 0)),
            _resident((1, d), lambda i, j: (0, 0)),
            _resident((1, d), lambda i, j: (0, 0)),
            pl.BlockSpec((1, tm, yw), lambda i, j: (i // nt, i % nt, 0)),
            pl.BlockSpec((1, tm, yw), lambda i, j: (i // nt, i % nt, 0)),
            pl.BlockSpec((d, tn), lambda i, j: (0, j)),
            pl.BlockSpec((d, tn), lambda i, j: (0, nj + j)),
            pl.BlockSpec((1, tn), lambda i, j: (0, j)),
            pl.BlockSpec((1, tn), lambda i, j: (0, nj + j)),
            pl.BlockSpec((yw, tn), lambda i, j: (0, j)),
            pl.BlockSpec((yw, tn), lambda i, j: (0, j)),
            pl.BlockSpec((tn, d), lambda i, j: (j, 0)),
        ],
        out_specs=pl.BlockSpec((1, tm, d), lambda i, j: (i // nt, i % nt, 0)),
        scratch_shapes=[pltpu.VMEM((tm, d), BF16)],
        compiler_params=_params(("parallel", "arbitrary"), vmem),
        name="gated_merge_out_proj",
    )(h1, pre_g, post_g, y_da, y_mla, w_gate, w_gate, b_gate, b_gate, wb_da, wb_mla, w_out)


def _rope_tables(s_len):
    half = MLA_ROPE // 2
    pos = jnp.concatenate([N_META + jnp.arange(s_len), jnp.arange(N_META),
                           jnp.zeros((META_BLK - N_META,), jnp.int32)]).astype(F32)
    inv = ROPE_THETA ** (-jnp.arange(half, dtype=F32) * 2.0 / MLA_ROPE)
    ang = pos[:, None] * inv[None, :]
    cos, sin = jnp.cos(ang), jnp.sin(ang)
    pad = jnp.zeros((pos.shape[0], HEAD_W - MLA_ROPE), F32)
    return (jnp.concatenate([cos, cos, pad], axis=-1),
            jnp.concatenate([-sin, sin, pad], axis=-1))


def _swap_halves(w):
    half = w.shape[-1] // 2
    return jnp.concatenate([w[..., half:], w[..., :half]], axis=-1)


def _pad_lanes(w, width):
    return jnp.concatenate([w, jnp.zeros(w.shape[:-1] + (width - w.shape[-1],), w.dtype)], axis=-1)


def kernel(x, meta_tokens, rel_bias_table, ffn1_pre_g, ffn1_post_g, ffn1_w_gate, ffn1_w_up, ffn1_w_down, mix_pre_g, mix_post_g, w_in, b_gate, da_lambda_q1, da_lambda_k1, da_lambda_q2, da_lambda_k2, da_sub_g, mla_q_norm_g, mla_kv_norm_g, mla_w_uq, mla_w_ukv, w_branch_da, w_branch_mla, w_out, ffn2_pre_g, ffn2_post_g, ffn2_w_gate, ffn2_w_up, ffn2_w_down):
    b, s_len, d = x.shape
    assert ffn1_pre_g.shape[0] == 1, "single-layer trunk"
    lp = s_len + META_BLK
    t = _col_tile(s_len, 512)
    row = lambda g: g.reshape(1, -1).astype(F32)

    meta = jnp.broadcast_to(meta_tokens.astype(x.dtype)[None], (b, N_META, d))
    h0 = jnp.concatenate([x, meta, jnp.zeros((b, META_BLK - N_META, d), x.dtype)], axis=1)

    h1 = _ffn(h0.reshape(b * lp, d), row(ffn1_pre_g), row(ffn1_post_g),
              ffn1_w_gate[0].astype(BF16), ffn1_w_up[0].astype(BF16), ffn1_w_down[0].astype(BF16),
              name="ffn1").reshape(b, lp, d)

    w = w_in[0]
    o_k, o_v, o_cq = DA_W, 2 * DA_W, 3 * DA_W
    o_kr = o_cq + MLA_Q_RANK + MLA_KV_RANK
    o_gate = o_kr + MLA_ROPE
    qt_da, k_da, vt_da = _da_proj(
        h1, row(mix_pre_g), w[:, :o_k].T.astype(BF16), w[:, o_k:o_v].astype(BF16),
        w[:, o_v:o_cq].T.astype(BF16))

    kr = w[:, o_kr:o_gate]
    w_c = jnp.concatenate([w[:, o_cq:o_kr], _pad_lanes(kr, HEAD_W),
                           _pad_lanes(_swap_halves(kr), HEAD_W)], axis=-1).astype(BF16)
    uq = mla_w_uq[0].reshape(MLA_Q_RANK, HEADS, MLA_NOPE + MLA_ROPE)
    uq_rope = uq[..., MLA_NOPE:]
    w_uqt = jnp.concatenate([uq[..., :MLA_NOPE], _pad_lanes(uq_rope, HEAD_W),
                             _pad_lanes(_swap_halves(uq_rope), HEAD_W)],
                            axis=-1).reshape(MLA_Q_RANK, HEADS * _UQ_HEAD_W).T.astype(BF16)
    ukv = mla_w_ukv[0].reshape(MLA_KV_RANK, HEADS, 2 * HEAD_W)
    w_uk = ukv[..., :HEAD_W].reshape(MLA_KV_RANK, DA_W).astype(BF16)
    w_uvt = ukv[..., HEAD_W:].reshape(MLA_KV_RANK, DA_W).T.astype(BF16)
    cos, sin = _rope_tables(s_len)
    qt_mla, k_mla, vt_mla = _mla_proj(
        h1, row(mix_pre_g), cos, sin, cos.T, sin.T, w_c, row(mla_q_norm_g), row(mla_kv_norm_g),
        w_uqt, w_uk, w_uvt)

    bias_d, bias_s, bias_m = _bias_tiles(rel_bias_table.astype(F32), t)
    lam_p = jnp.concatenate([da_lambda_q1, da_lambda_k1, da_lambda_q2, da_lambda_k2], axis=0).astype(F32)
    y_da = _da_attention(qt_da, k_da, vt_da, bias_d, bias_s, bias_m, lam_p, row(da_sub_g),
                         t=t, s_len=s_len)
    y_mla = _mla_attention(qt_mla, k_mla, vt_mla, tq=_col_tile(s_len, 2 * t), tk=t, s_len=s_len)

    h2 = _merge(h1, row(mix_pre_g), row(mix_post_g), y_da, y_mla,
                w[:, o_gate:].astype(BF16), row(b_gate),
                w_branch_da[0].astype(BF16), w_branch_mla[0].astype(BF16), w_out[0].astype(BF16),
                s_len=s_len)

    out = _ffn(h2.reshape(b * s_len, d), row(ffn2_pre_g), row(ffn2_post_g),
               ffn2_w_gate[0].astype(BF16), ffn2_w_up[0].astype(BF16), ffn2_w_down[0].astype(BF16),
               name="ffn2")
    return out.reshape(b, s_len, d)
```

```python
import functools
import math

import numpy as np
import jax
import jax.numpy as jnp
from jax import lax
from jax.experimental import pallas as pl
from jax.experimental.pallas import tpu as pltpu

F32 = jnp.float32
BF16 = jnp.bfloat16

N_META = 16
RMS_EPS = 1e-6
NEG_INF = -1e30
HEADS = 8
HEAD_W = 128
DA_HEAD_DIM = 64
DA_W = HEADS * HEAD_W
MLA_Q_RANK = 768
MLA_KV_RANK = 512
MLA_NOPE = 128
MLA_ROPE = 64
MLA_QK_W = 256
ROPE_THETA = 10000.0
REL_BUCKETS = 32
REL_MAX_DIST = 128
LAMBDA_INIT = 0.8 - 0.6 * math.exp(-0.3 * 0)
LOG2_E = math.log2(math.e)

V7X_LANES = 128
V7X_BF16_SUBLANES = 16
V7X_VMEM_BYTES = 64 * 1024 * 1024
VMEM_CAP_BYTES = V7X_VMEM_BYTES - 6 * 1024 * 1024

META_BLK = V7X_LANES


def _row_tile(rows, max_tile, multiple=V7X_BF16_SUBLANES):
    best = None
    for t in range(multiple, max_tile + 1, multiple):
        if rows % t == 0:
            best = t
    assert best is not None, (rows, max_tile)
    return best


def _col_tile(cols, max_tile):
    return _row_tile(cols, max_tile, V7X_LANES)


def _params(semantics, vmem_bytes):
    return pltpu.CompilerParams(
        dimension_semantics=semantics,
        vmem_limit_bytes=int(min(max(vmem_bytes, 16 * 1024 * 1024), VMEM_CAP_BYTES)))


def _resident(block_shape, index_map):
    return pl.BlockSpec(block_shape, index_map, pipeline_mode=pl.Buffered(1))


def _rms(x, g):
    ms = jnp.mean(x * x, axis=-1, keepdims=True)
    return x * lax.rsqrt(ms + RMS_EPS) * g


def _dot(a, b):
    return jnp.dot(a, b, preferred_element_type=F32)


def _dot_nt(a, b):
    return lax.dot_general(a, b, (((1,), (1,)), ((), ())), preferred_element_type=F32)


def _ffn_kernel(x_ref, pre_ref, post_ref, wg_ref, wu_ref, wd_ref, o_ref, xn_ref):
    k = pl.program_id(1)
    last = pl.num_programs(1) - 1

    def partial_out(xn):
        g = _dot(xn, wg_ref[...])
        u = _dot(xn, wu_ref[...])
        a = (g * jax.nn.sigmoid(g) * u).astype(BF16)
        return _dot(a, wd_ref[...])

    @pl.when(k == 0)
    def _():
        xn = _rms(x_ref[...], pre_ref[...]).astype(BF16)
        xn_ref[...] = xn
        o_ref[...] = partial_out(xn)

    @pl.when(jnp.logical_and(k > 0, k < last))
    def _():
        o_ref[...] += partial_out(xn_ref[...])

    @pl.when(k == last)
    def _():
        f = o_ref[...] + partial_out(xn_ref[...])
        o_ref[...] = x_ref[...] + 0.5 * _rms(f, post_ref[...])


def _ffn(x, pre_g, post_g, wg, wu, wd, *, name):
    rows, d = x.shape
    f = wg.shape[1]
    tm = _row_tile(rows, 896)
    tf = _col_tile(f, 512)
    assert f // tf >= 2, "the kernel treats the first and the last d_ff tile as different steps"
    vmem = (2 * 2 * tm * d * 4
            + tm * d * 2
            + 2 * 3 * d * tf * 2
            + 4 * tm * tf * 4
            + tm * d * 4)
    return pl.pallas_call(
        _ffn_kernel,
        out_shape=jax.ShapeDtypeStruct((rows, d), F32),
        grid=(rows // tm, f // tf),
        in_specs=[
            pl.BlockSpec((tm, d), lambda i, k: (i, 0)),
            _resident((1, d), lambda i, k: (0, 0)),
            _resident((1, d), lambda i, k: (0, 0)),
            pl.BlockSpec((d, tf), lambda i, k: (0, k)),
            pl.BlockSpec((d, tf), lambda i, k: (0, k)),
            pl.BlockSpec((tf, d), lambda i, k: (k, 0)),
        ],
        out_specs=pl.BlockSpec((tm, d), lambda i, k: (i, 0)),
        scratch_shapes=[pltpu.VMEM((tm, d), BF16)],
        compiler_params=_params(("parallel", "arbitrary"), vmem),
        name=name,
    )(x, pre_g, post_g, wg, wu, wd)


VT_ROWS = HEAD_W + V7X_BF16_SUBLANES


def _store_vt(vt_ref, vt):
    tm = vt.shape[1]
    vt_ref[0, :, 0:HEAD_W, :] = vt.reshape(HEADS, HEAD_W, tm).astype(BF16)
    extra = lax.broadcasted_iota(jnp.int32, (HEADS, VT_ROWS - HEAD_W, tm), 1)
    vt_ref[0, :, HEAD_W:VT_ROWS, :] = jnp.where(extra == 0, 1.0, 0.0).astype(BF16)


def _da_proj_kernel(h_ref, g_ref, wqt_ref, wk_ref, wvt_ref, qt_ref, k_ref, vt_ref):
    hn = _rms(h_ref[...], g_ref[...]).astype(BF16)
    tm = hn.shape[0]
    qt = _dot_nt(wqt_ref[...], hn) * (DA_HEAD_DIM ** -0.5 * LOG2_E)
    qt_ref[0] = qt.reshape(HEADS, HEAD_W, tm).astype(BF16)
    k = _dot(hn, wk_ref[...])
    for h in range(HEADS):
        k_ref[0, h] = k[:, h * HEAD_W:(h + 1) * HEAD_W].astype(BF16)
    _store_vt(vt_ref, _dot_nt(wvt_ref[...], hn))


_C_CKV = MLA_Q_RANK
_C_KR1 = _C_CKV + MLA_KV_RANK
_C_KR2 = _C_KR1 + HEAD_W
_C_END = _C_KR2 + HEAD_W
_UQ_HEAD_W = 3 * HEAD_W


def _mla_proj_kernel(h_ref, g_ref, cos_ref, sin_ref, cost_ref, sint_ref, wc_ref, gq_ref, gkv_ref,
                     wuqt_ref, wuk_ref, wuvt_ref, qt_ref, k_ref, vt_ref):
    hn = _rms(h_ref[...], g_ref[...]).astype(BF16)
    tm = hn.shape[0]
    scale = (MLA_NOPE + MLA_ROPE) ** -0.5 * LOG2_E

    c_all = _dot(hn, wc_ref[...])

    def proj(lo, hi):
        return c_all[:, lo:hi]

    cq = _rms(proj(0, _C_CKV), gq_ref[...]).astype(BF16)
    qt = _dot_nt(wuqt_ref[...], cq)
    cos_t = cost_ref[...]
    sin_t = sint_ref[...]
    for h in range(HEADS):
        base = h * _UQ_HEAD_W
        nope = qt[base:base + HEAD_W]
        r1 = qt[base + HEAD_W:base + 2 * HEAD_W]
        r2 = qt[base + 2 * HEAD_W:base + 3 * HEAD_W]
        qt_ref[0, h, 0:HEAD_W, :] = (nope * scale).astype(BF16)
        qt_ref[0, h, HEAD_W:MLA_QK_W, :] = ((r1 * cos_t + r2 * sin_t) * scale).astype(BF16)

    ckv = _rms(proj(_C_CKV, _C_KR1), gkv_ref[...]).astype(BF16)
    k_nope = _dot(ckv, wuk_ref[...])
    k_rope = (proj(_C_KR1, _C_KR2) * cos_ref[...] + proj(_C_KR2, _C_END) * sin_ref[...]).astype(BF16)
    for h in range(HEADS):
        k_ref[0, h, :, 0:HEAD_W] = k_nope[:, h * HEAD_W:(h + 1) * HEAD_W].astype(BF16)
        k_ref[0, h, :, HEAD_W:MLA_QK_W] = k_rope
    _store_vt(vt_ref, _dot_nt(wuvt_ref[...], ckv))


def _proj_specs(b, lp, tm, q_w):
    nt = lp // tm
    rows = lambda w: pl.BlockSpec((1, HEADS, tm, w), lambda i: (i // nt, 0, i % nt, 0))
    cols = lambda w: pl.BlockSpec((1, HEADS, w, tm), lambda i: (i // nt, 0, 0, i % nt))
    shapes = [jax.ShapeDtypeStruct((b, HEADS, q_w, lp), BF16),
              jax.ShapeDtypeStruct((b, HEADS, lp, q_w), BF16),
              jax.ShapeDtypeStruct((b, HEADS, VT_ROWS, lp), BF16)]
    return shapes, [cols(q_w), rows(q_w), cols(VT_ROWS)]


def _da_proj(h1, g, wqt, wk, wvt):
    b, lp, d = h1.shape
    tm = _col_tile(lp, 704)
    nt = lp // tm
    shapes, out_specs = _proj_specs(b, lp, tm, HEAD_W)
    vmem = (2 * tm * d * 4 + tm * d * 2 + 3 * DA_W * d * 2
            + 2 * 3 * tm * DA_W * 2 + 3 * tm * DA_W * 4)
    return pl.pallas_call(
        _da_proj_kernel,
        out_shape=shapes,
        grid=(b * nt,),
        in_specs=[
            pl.BlockSpec((None, tm, d), lambda i: (i // nt, i % nt, 0)),
            _resident((1, d), lambda i: (0, 0)),
            _resident(wqt.shape, lambda i: (0, 0)),
            _resident(wk.shape, lambda i: (0, 0)),
            _resident(wvt.shape, lambda i: (0, 0)),
        ],
        out_specs=out_specs,
        compiler_params=_params(("parallel",), vmem),
        name="da_in_proj",
    )(h1, g, wqt, wk, wvt)


def _mla_proj(h1, g, cos, sin, cos_t, sin_t, wc, gq, gkv, wuqt, wuk, wuvt):
    b, lp, d = h1.shape
    tm = _col_tile(lp, 704)
    nt = lp // tm
    shapes, out_specs = _proj_specs(b, lp, tm, MLA_QK_W)
    vmem = (2 * tm * d * 4 + tm * d * 2
            + (d * _C_END + HEADS * _UQ_HEAD_W * MLA_Q_RANK + 2 * DA_W * MLA_KV_RANK) * 2
            + 2 * tm * HEADS * (2 * MLA_QK_W + HEAD_W) * 2
            + 2 * tm * HEADS * _UQ_HEAD_W * 4)
    return pl.pallas_call(
        _mla_proj_kernel,
        out_shape=shapes,
        grid=(b * nt,),
        in_specs=[
            pl.BlockSpec((None, tm, d), lambda i: (i // nt, i % nt, 0)),
            _resident((1, d), lambda i: (0, 0)),
            pl.BlockSpec((tm, HEAD_W), lambda i: (i % nt, 0)),
            pl.BlockSpec((tm, HEAD_W), lambda i: (i % nt, 0)),
            pl.BlockSpec((HEAD_W, tm), lambda i: (0, i % nt)),
            pl.BlockSpec((HEAD_W, tm), lambda i: (0, i % nt)),
            _resident(wc.shape, lambda i: (0, 0)),
            _resident((1, MLA_Q_RANK), lambda i: (0, 0)),
            _resident((1, MLA_KV_RANK), lambda i: (0, 0)),
            _resident(wuqt.shape, lambda i: (0, 0)),
            _resident(wuk.shape, lambda i: (0, 0)),
            _resident(wuvt.shape, lambda i: (0, 0)),
        ],
        out_specs=out_specs,
        compiler_params=_params(("parallel",), vmem),
        name="mla_in_proj",
    )(h1, g, cos, sin, cos_t, sin_t, wc, gq, gkv, wuqt, wuk, wuvt)


def _t5_bucket(rel):
    n = np.maximum(rel, 0)
    max_exact = REL_BUCKETS // 2
    n_f = np.maximum(n, 1).astype(np.float64)
    large = max_exact + (np.log(n_f / max_exact) / math.log(REL_MAX_DIST / max_exact)
                         * (REL_BUCKETS - max_exact)).astype(np.int32)
    large = np.minimum(large, REL_BUCKETS - 1)
    return np.where(n < max_exact, n, large).astype(np.int32)


def _bucket_tiles(t):
    j = np.arange(t)[:, None]
    i = np.arange(t)[None, :]
    diag = np.where(j <= i, _t5_bucket(i - j), -1)
    sub = _t5_bucket(t + i - j)
    assert _t5_bucket(np.array([t + 1]))[0] == REL_BUCKETS - 1
    jm = np.arange(META_BLK)[:, None]
    meta_first = np.where(jm < N_META, _t5_bucket(N_META + i - jm), -1)
    meta_rest = np.where(jm < N_META, REL_BUCKETS - 1, -1) + 0 * i
    return (diag.astype(np.int32), np.stack([np.full_like(sub, -1), sub]).astype(np.int32),
            np.stack([meta_first, meta_rest]).astype(np.int32))


def _bias_kernel(table_ref, bd_ref, bs_ref, bm_ref, od_ref, os_ref, om_ref, *, buckets):
    h = pl.program_id(0)
    far = table_ref[REL_BUCKETS - 1, h]

    def lookup(bucket_ref, out_ref, known):
        rows, cols = known.shape
        for r0 in range(0, rows, V7X_LANES):
            for c0 in range(0, cols, V7X_LANES):
                patch = (slice(r0, r0 + V7X_LANES), slice(c0, c0 + V7X_LANES))
                present = np.unique(known[patch])
                acc = jnp.zeros((V7X_LANES, V7X_LANES), F32)
                if present.size > 1 or 0 <= present[0] < REL_BUCKETS - 1:
                    bucket = bucket_ref[patch]
                    for b in present:
                        if 0 <= b < REL_BUCKETS - 1:
                            acc = jnp.where(bucket == int(b), (table_ref[int(b), h] - far) * LOG2_E, acc)
                    if present[0] < 0:
                        acc = jnp.where(bucket < 0, NEG_INF, acc)
                elif present[0] < 0:
                    acc = jnp.full((V7X_LANES, V7X_LANES), NEG_INF, F32)
                out_ref[patch] = acc

    bd, bs, bm = buckets
    lookup(bd_ref, od_ref.at[0], bd)
    for i in range(2):
        lookup(bs_ref.at[i], os_ref.at[0, i], bs[i])
        lookup(bm_ref.at[i], om_ref.at[0, i], bm[i])


def _bias_tiles(table, t):
    bd, bs, bm = _bucket_tiles(t)
    return pl.pallas_call(
        functools.partial(_bias_kernel, buckets=(bd, bs, bm)),
        out_shape=[jax.ShapeDtypeStruct((HEADS, t, t), F32),
                   jax.ShapeDtypeStruct((HEADS, 2, t, t), F32),
                   jax.ShapeDtypeStruct((HEADS, 2, META_BLK, t), F32)],
        grid=(HEADS,),
        in_specs=[
            pl.BlockSpec(memory_space=pltpu.SMEM),
            pl.BlockSpec((t, t), lambda h: (0, 0)),
            pl.BlockSpec((2, t, t), lambda h: (0, 0, 0)),
            pl.BlockSpec((2, META_BLK, t), lambda h: (0, 0, 0)),
        ],
        out_specs=[pl.BlockSpec((1, t, t), lambda h: (h, 0, 0)),
                   pl.BlockSpec((1, 2, t, t), lambda h: (h, 0, 0, 0)),
                   pl.BlockSpec((1, 2, META_BLK, t), lambda h: (h, 0, 0, 0))],
        compiler_params=_params(("parallel",), 12 * t * t * 4),
        name="rel_bias_tiles",
    )(table, jnp.asarray(bd), jnp.asarray(bs), jnp.asarray(bm))


ATTN_QUERY_CHUNK = 256


ATTN_SCORE_BUFFERS = 4


def _chunking(qt_ref, s_ref):
    n_chunks = qt_ref.shape[1] // s_ref.shape[2]
    return n_chunks, min(s_ref.shape[0] - 1, n_chunks)


def _slot(s_ref, ordinal, n_chunks, c):
    n_buf = s_ref.shape[0]
    if n_chunks % n_buf == 0:
        return c % n_buf
    return (ordinal * n_chunks + c) & (n_buf - 1)


def _issue_scores(qt_ref, s_ref, k_blk, c, slot):
    qc = s_ref.shape[2]
    s_ref[slot, 0:k_blk.shape[0], :] = _dot(k_blk, qt_ref[:, pl.ds(c * qc, qc)])


def _attend_prologue(qt_ref, s_ref, k_blk):
    n_chunks, ahead = _chunking(qt_ref, s_ref)
    for c in range(ahead):
        _issue_scores(qt_ref, s_ref, k_blk, c, _slot(s_ref, 0, n_chunks, c))


def _attend_block(qt_ref, s_ref, ordinal, k_blk, vt_blk, bias_fn, next_k, m_ref, acc_ref):
    qc = s_ref.shape[2]
    keys = k_blk.shape[0]
    n_chunks, ahead = _chunking(qt_ref, s_ref)
    for c in range(n_chunks):
        sl = pl.ds(c * qc, qc)
        nxt = c + ahead
        if nxt < n_chunks:
            _issue_scores(qt_ref, s_ref, k_blk, nxt, _slot(s_ref, ordinal, n_chunks, nxt))
        elif next_k is not None:
            _issue_scores(qt_ref, s_ref, next_k, nxt - n_chunks, _slot(s_ref, ordinal, n_chunks, nxt))
        s = s_ref[_slot(s_ref, ordinal, n_chunks, c), 0:keys, :]
        bias = bias_fn(c) if bias_fn is not None else None
        if bias is not None:
            s = s + bias
        m_prev = m_ref[:, sl]
        m_new = jnp.maximum(m_prev, jnp.max(s, axis=0, keepdims=True))
        alpha = jnp.exp2(m_prev - m_new)
        p = jnp.exp2(s - m_new)
        acc_ref[:, sl] = alpha * acc_ref[:, sl] + _dot(vt_blk, p.astype(BF16))
        m_ref[:, sl] = m_new


def _attend_init(m_ref, acc_ref):
    m_ref[...] = jnp.full(m_ref.shape, NEG_INF, F32)
    acc_ref[...] = jnp.zeros(acc_ref.shape, F32)


FAR_UNROLL = 8


def _walk_far_blocks(n_far, attend_one):
    n_far = jnp.maximum(n_far, 0)

    def body(first, count):
        for u in range(count):
            attend_one(first + u)

    def loop_body(i, carry):
        body(i * FAR_UNROLL, FAR_UNROLL)
        return carry

    lax.fori_loop(0, n_far >> (FAR_UNROLL.bit_length() - 1), loop_body, 0)
    count = FAR_UNROLL // 2
    while count >= 1:
        first = n_far - (n_far & (2 * count - 1))

        @pl.when((n_far & count) != 0)
        def _(first=first, count=count):
            body(first, count)

        count //= 2


def _normalised(acc_ref):
    acc = acc_ref[...]
    return acc[0:HEAD_W] * (1.0 / acc[HEAD_W:HEAD_W + 1])


def _da_kernel(qt_ref, k_ref, vt_ref, bd_ref, bs_ref, bm_ref, lam_ref, subg_ref, o_ref,
               qq_ref, s_ref, m_ref, acc_ref, *, t, s_len):
    qi = pl.program_id(2)
    qt = qt_ref[0, 0]
    feat = lax.broadcasted_iota(jnp.int32, (HEAD_W, 1), 0)
    zero = jnp.zeros_like(qt)
    qq_ref[:, 0:t] = jnp.where(feat < DA_HEAD_DIM, qt, zero)
    qq_ref[:, t:2 * t] = jnp.where(feat >= DA_HEAD_DIM, qt, zero)
    qc = min(ATTN_QUERY_CHUNK, t)
    chunks_per_map = t // qc

    def bias_cols(read):
        return lambda c: read(pl.ds((c % chunks_per_map) * qc, qc))

    def k_block(kb):
        return k_ref[0, 0, pl.ds(pl.multiple_of(kb * t, t), t), :]

    def vt_block(kb):
        return vt_ref[0, 0, :, pl.ds(pl.multiple_of(kb * t, t), t)]

    state = (m_ref, acc_ref)
    n_far = jnp.maximum(qi - 1, 0)
    near_sel = jnp.minimum(qi, 1)
    prev_kb = jnp.maximum(qi - 1, 0)
    k_meta = k_ref[0, 0, pl.ds(s_len, META_BLK), :]
    _attend_init(*state)
    _attend_prologue(qq_ref, s_ref, k_block(jnp.where(qi >= 2, 0, qi)))

    def far(kb):
        next_kb = jnp.where(kb == n_far - 1, qi, kb + 1)
        _attend_block(qq_ref, s_ref, kb, k_block(kb), vt_block(kb), None, k_block(next_kb), *state)

    _walk_far_blocks(n_far, far)
    _attend_block(qq_ref, s_ref, n_far, k_block(qi), vt_block(qi),
                  bias_cols(lambda cs: bd_ref[0, :, cs]), k_block(prev_kb), *state)
    _attend_block(qq_ref, s_ref, n_far + 1, k_block(prev_kb), vt_block(prev_kb),
                  bias_cols(lambda cs: bs_ref[0, near_sel, :, cs]), k_meta, *state)
    _attend_block(qq_ref, s_ref, n_far + 2, k_meta, vt_ref[0, 0, :, pl.ds(s_len, META_BLK)],
                  bias_cols(lambda cs: bm_ref[0, near_sel, :, cs]), None, *state)

    ot = _normalised(acc_ref)
    lp = lam_ref[...]
    lam = (jnp.exp(jnp.sum(lp[0:1] * lp[1:2], axis=-1, keepdims=True))
           - jnp.exp(jnp.sum(lp[2:3] * lp[3:4], axis=-1, keepdims=True)) + LAMBDA_INIT)
    d = ot[:, :t] - lam * ot[:, t:]
    ms = jnp.mean(d * d, axis=0, keepdims=True)
    y = d * lax.rsqrt(ms + RMS_EPS) * subg_ref[...] * (1.0 - LAMBDA_INIT)
    o_ref[0] = y.T.astype(BF16)


def _attn_specs(t, lp, q_w):
    return [
        pl.BlockSpec((1, 1, q_w, t), lambda bi, h, qi: (bi, h, 0, qi)),
        pl.BlockSpec((1, 1, lp, q_w), lambda bi, h, qi: (bi, h, 0, 0)),
        pl.BlockSpec((1, 1, VT_ROWS, lp), lambda bi, h, qi: (bi, h, 0, 0)),
    ]


def _da_attention(qt, k, vt, bias_d, bias_s, bias_m, lam_p, sub_g, *, t, s_len):
    b, _, lp, _ = k.shape
    nq = s_len // t
    vmem = (2 * 2 * lp * HEAD_W * 2 + 2 * 2 * t * t * 4 + 2 * 2 * t * META_BLK * 4
            + 2 * 2 * t * HEAD_W * 2 + 3 * 2 * t * HEAD_W * 4
            + 6 * t * ATTN_QUERY_CHUNK * 4)
    return pl.pallas_call(
        functools.partial(_da_kernel, t=t, s_len=s_len),
        out_shape=jax.ShapeDtypeStruct((b, s_len, DA_W), BF16),
        grid=(b, HEADS, nq),
        in_specs=_attn_specs(t, lp, HEAD_W) + [
            pl.BlockSpec((1, t, t), lambda bi, h, qi: (h, 0, 0)),
            pl.BlockSpec((1, 2, t, t), lambda bi, h, qi: (h, 0, 0, 0)),
            pl.BlockSpec((1, 2, META_BLK, t), lambda bi, h, qi: (h, 0, 0, 0)),
            _resident((4, DA_HEAD_DIM), lambda bi, h, qi: (0, 0)),
            _resident((HEAD_W, 1), lambda bi, h, qi: (0, 0)),
        ],
        out_specs=pl.BlockSpec((1, t, HEAD_W), lambda bi, h, qi: (bi, qi, h)),
        scratch_shapes=[pltpu.VMEM((HEAD_W, 2 * t), BF16),
                        pltpu.VMEM((ATTN_SCORE_BUFFERS, t, min(ATTN_QUERY_CHUNK, t)), F32),
                        pltpu.VMEM((1, 2 * t), F32), pltpu.VMEM((VT_ROWS, 2 * t), F32)],
        compiler_params=_params(("parallel", "parallel", "arbitrary"), vmem),
        name="diff_attention",
    )(qt, k, vt, bias_d, bias_s, bias_m, lam_p, sub_g)


def _mla_kernel(qt_ref, k_ref, vt_ref, o_ref, s_ref, m_ref, acc_ref, *, tq, tk, s_len):
    qi = pl.program_id(2)
    qv_ref = qt_ref.at[0, 0]
    qc = s_ref.shape[2]
    diag_blocks = tq // tk

    def k_block(kb):
        return k_ref[0, 0, pl.ds(pl.multiple_of(kb * tk, tk), tk), :]

    def vt_block(kb):
        return vt_ref[0, 0, :, pl.ds(pl.multiple_of(kb * tk, tk), tk)]

    state = (m_ref, acc_ref)
    k_meta = k_ref[0, 0, pl.ds(s_len, META_BLK), :]
    key = lax.broadcasted_iota(jnp.int32, (META_BLK, qc), 0)
    meta_mask = jnp.where(key < N_META, 0.0, NEG_INF)
    _attend_init(*state)
    _attend_prologue(qv_ref, s_ref, k_block(0))

    def far(kb):
        _attend_block(qv_ref, s_ref, kb, k_block(kb), vt_block(kb), None, k_block(kb + 1), *state)

    first_diag = qi * diag_blocks
    _walk_far_blocks(first_diag, far)

    key = lax.broadcasted_iota(jnp.int32, (tk, qc), 0)
    query = lax.broadcasted_iota(jnp.int32, (tk, qc), 1)
    for d in range(diag_blocks):
        kb = first_diag + d
        causal = lambda c, d=d: jnp.where(key + d * tk <= query + c * qc, 0.0, NEG_INF)
        _attend_block(qv_ref, s_ref, kb, k_block(kb), vt_block(kb), causal,
                      k_block(kb + 1) if d + 1 < diag_blocks else k_meta, *state)
    _attend_block(qv_ref, s_ref, first_diag + diag_blocks, k_meta,
                  vt_ref[0, 0, :, pl.ds(s_len, META_BLK)], lambda c: meta_mask, None, *state)

    o_ref[0] = _normalised(acc_ref).T.astype(BF16)


def _mla_attention(qt, k, vt, *, tq, tk, s_len):
    b, _, lp, _ = k.shape
    nq = s_len // tq
    qc = min(ATTN_QUERY_CHUNK, tq)
    vmem = (2 * lp * (MLA_QK_W + HEAD_W) * 2 + 2 * tq * MLA_QK_W * 2
            + 2 * tq * HEAD_W * 2 + 3 * tq * HEAD_W * 4 + (ATTN_SCORE_BUFFERS + 4) * tk * qc * 4)
    return pl.pallas_call(
        functools.partial(_mla_kernel, tq=tq, tk=tk, s_len=s_len),
        out_shape=jax.ShapeDtypeStruct((b, s_len, DA_W), BF16),
        grid=(b, HEADS, nq),
        in_specs=_attn_specs(tq, lp, MLA_QK_W),
        out_specs=pl.BlockSpec((1, tq, HEAD_W), lambda bi, h, qi: (bi, qi, h)),
        scratch_shapes=[pltpu.VMEM((ATTN_SCORE_BUFFERS, tk, qc), F32),
                        pltpu.VMEM((1, tq), F32), pltpu.VMEM((VT_ROWS, tq), F32)],
        compiler_params=_params(("parallel", "parallel", "arbitrary"), vmem),
        name="mla_attention",
    )(qt, k, vt)


def _merge_kernel(h_ref, pre_ref, post_ref, yda_ref, ymla_ref, wg0_ref, wg1_ref, b0_ref, b1_ref,
                  wbd_ref, wbm_ref, wo_ref, o_ref, hn_ref):
    j = pl.program_id(1)
    last = pl.num_programs(1) - 1

    def partial_out(hn):
        g0 = jax.nn.sigmoid(_dot(hn, wg0_ref[...]) + b0_ref[...])
        g1 = jax.nn.sigmoid(_dot(hn, wg1_ref[...]) + b1_ref[...])
        merged = g0 * _dot(yda_ref[0], wbd_ref[...]) + g1 * _dot(ymla_ref[0], wbm_ref[...])
        return _dot(merged.astype(BF16), wo_ref[...])

    @pl.when(j == 0)
    def _():
        hn = _rms(h_ref[0], pre_ref[...]).astype(BF16)
        hn_ref[...] = hn
        o_ref[0] = partial_out(hn)

    @pl.when(jnp.logical_and(j > 0, j < last))
    def _():
        o_ref[0] += partial_out(hn_ref[...])

    @pl.when(j == last)
    def _():
        m = o_ref[0] + partial_out(hn_ref[...])
        o_ref[0] = h_ref[0] + _rms(m, post_ref[...])


def _merge(h1, pre_g, post_g, y_da, y_mla, w_gate, b_gate, wb_da, wb_mla, w_out, *, s_len):
    b, _, d = h1.shape
    yw = y_da.shape[-1]
    tm = _row_tile(s_len, 512)
    tn = _col_tile(d, 512)
    nt = s_len // tm
    nj = d // tn
    assert nj >= 2, "the kernel treats the first and the last column tile as different steps"
    vmem = (2 * 2 * tm * d * 4 + tm * d * 2 + 2 * 2 * tm * yw * 2
            + 2 * (3 * d * tn + 2 * yw * tn) * 2 + 6 * tm * tn * 4 + tm * d * 4)
    return pl.pallas_call(
        _merge_kernel,
        out_shape=jax.ShapeDtypeStruct((b, s_len, d), F32),
        grid=(b * nt, nj),
        in_specs=[
            pl.BlockSpec((1, tm, d), lambda i, j: (i // nt, i % nt, 0)),
            _resident((1, d), lambda i, j: (0, 0)),
            _resident((1, d), lambda i, j: (0, 0)),
            pl.BlockSpec((1, tm, yw), lambda i, j: (i // nt, i % nt, 0)),
            pl.BlockSpec((1, tm, yw), lambda i, j: (i // nt, i % nt, 0)),
            pl.BlockSpec((d, tn), lambda i, j: (0, j)),
            pl.BlockSpec((d, tn), lambda i, j: (0, nj + j)),
            pl.BlockSpec((1, tn), lambda i, j: (0, j)),
            pl.BlockSpec((1, tn), lambda i, j: (0, nj + j)),
            pl.BlockSpec((yw, tn), lambda i, j: (0, j)),
            pl.BlockSpec((yw, tn), lambda i, j: (0, j)),
            pl.BlockSpec((tn, d), lambda i, j: (j, 0)),
        ],
        out_specs=pl.BlockSpec((1, tm, d), lambda i, j: (i // nt, i % nt, 0)),
        scratch_shapes=[pltpu.VMEM((tm, d), BF16)],
        compiler_params=_params(("parallel", "arbitrary"), vmem),
        name="gated_merge_out_proj",
    )(h1, pre_g, post_g, y_da, y_mla, w_gate, w_gate, b_gate, b_gate, wb_da, wb_mla, w_out)


def _rope_tables(s_len):
    half = MLA_ROPE // 2
    pos = jnp.concatenate([N_META + jnp.arange(s_len), jnp.arange(N_META),
                           jnp.zeros((META_BLK - N_META,), jnp.int32)]).astype(F32)
    inv = ROPE_THETA ** (-jnp.arange(half, dtype=F32) * 2.0 / MLA_ROPE)
    ang = pos[:, None] * inv[None, :]
    cos, sin = jnp.cos(ang), jnp.sin(ang)
    pad = jnp.zeros((pos.shape[0], HEAD_W - MLA_ROPE), F32)
    return (jnp.concatenate([cos, cos, pad], axis=-1),
            jnp.concatenate([-sin, sin, pad], axis=-1))


def _swap_halves(w):
    half = w.shape[-1] // 2
    return jnp.concatenate([w[..., half:], w[..., :half]], axis=-1)


def _pad_lanes(w, width):
    return jnp.concatenate([w, jnp.zeros(w.shape[:-1] + (width - w.shape[-1],), w.dtype)], axis=-1)


def kernel(x, meta_tokens, rel_bias_table, ffn1_pre_g, ffn1_post_g, ffn1_w_gate, ffn1_w_up, ffn1_w_down, mix_pre_g, mix_post_g, w_in, b_gate, da_lambda_q1, da_lambda_k1, da_lambda_q2, da_lambda_k2, da_sub_g, mla_q_norm_g, mla_kv_norm_g, mla_w_uq, mla_w_ukv, w_branch_da, w_branch_mla, w_out, ffn2_pre_g, ffn2_post_g, ffn2_w_gate, ffn2_w_up, ffn2_w_down):
    b, s_len, d = x.shape
    assert ffn1_pre_g.shape[0] == 1, "single-layer trunk"
    lp = s_len + META_BLK
    t = _col_tile(s_len, 512)
    row = lambda g: g.reshape(1, -1).astype(F32)

    meta = jnp.broadcast_to(meta_tokens.astype(x.dtype)[None], (b, N_META, d))
    h0 = jnp.concatenate([x, meta, jnp.zeros((b, META_BLK - N_META, d), x.dtype)], axis=1)

    h1 = _ffn(h0.reshape(b * lp, d), row(ffn1_pre_g), row(ffn1_post_g),
              ffn1_w_gate[0].astype(BF16), ffn1_w_up[0].astype(BF16), ffn1_w_down[0].astype(BF16),
              name="ffn1").reshape(b, lp, d)

    w = w_in[0]
    o_k, o_v, o_cq = DA_W, 2 * DA_W, 3 * DA_W
    o_kr = o_cq + MLA_Q_RANK + MLA_KV_RANK
    o_gate = o_kr + MLA_ROPE
    qt_da, k_da, vt_da = _da_proj(
        h1, row(mix_pre_g), w[:, :o_k].T.astype(BF16), w[:, o_k:o_v].astype(BF16),
        w[:, o_v:o_cq].T.astype(BF16))

    kr = w[:, o_kr:o_gate]
    w_c = jnp.concatenate([w[:, o_cq:o_kr], _pad_lanes(kr, HEAD_W),
                           _pad_lanes(_swap_halves(kr), HEAD_W)], axis=-1).astype(BF16)
    uq = mla_w_uq[0].reshape(MLA_Q_RANK, HEADS, MLA_NOPE + MLA_ROPE)
    uq_rope = uq[..., MLA_NOPE:]
    w_uqt = jnp.concatenate([uq[..., :MLA_NOPE], _pad_lanes(uq_rope, HEAD_W),
                             _pad_lanes(_swap_halves(uq_rope), HEAD_W)],
                            axis=-1).reshape(MLA_Q_RANK, HEADS * _UQ_HEAD_W).T.astype(BF16)
    ukv = mla_w_ukv[0].reshape(MLA_KV_RANK, HEADS, 2 * HEAD_W)
    w_uk = ukv[..., :HEAD_W].reshape(MLA_KV_RANK, DA_W).astype(BF16)
    w_uvt = ukv[..., HEAD_W:].reshape(MLA_KV_RANK, DA_W).T.astype(BF16)
    cos, sin = _rope_tables(s_len)
    qt_mla, k_mla, vt_mla = _mla_proj(
        h1, row(mix_pre_g), cos, sin, cos.T, sin.T, w_c, row(mla_q_norm_g), row(mla_kv_norm_g),
        w_uqt, w_uk, w_uvt)

    bias_d, bias_s, bias_m = _bias_tiles(rel_bias_table.astype(F32), t)
    lam_p = jnp.concatenate([da_lambda_q1, da_lambda_k1, da_lambda_q2, da_lambda_k2], axis=0).astype(F32)
    y_da = _da_attention(qt_da, k_da, vt_da, bias_d, bias_s, bias_m, lam_p, da_sub_g.reshape(HEAD_W, 1).astype(F32),
                         t=t, s_len=s_len)
    y_mla = _mla_attention(qt_mla, k_mla, vt_mla, tq=_col_tile(s_len, 2 * t), tk=t, s_len=s_len)

    h2 = _merge(h1, row(mix_pre_g), row(mix_post_g), y_da, y_mla,
                w[:, o_gate:].astype(BF16), row(b_gate),
                w_branch_da[0].astype(BF16), w_branch_mla[0].astype(BF16), w_out[0].astype(BF16),
                s_len=s_len)

    out = _ffn(h2.reshape(b * s_len, d), row(ffn2_pre_g), row(ffn2_post_g),
               ffn2_w_gate[0].astype(BF16), ffn2_w_up[0].astype(BF16), ffn2_w_down[0].astype(BF16),
               name="ffn2")
    return out.reshape(b, s_len, d)
```

```python
import functools
import math
from typing import Any, Callable, NamedTuple, Optional

import numpy as np
import jax
import jax.numpy as jnp
from jax import lax
from jax.experimental import pallas as pl
from jax.experimental.pallas import tpu as pltpu

F32 = jnp.float32
BF16 = jnp.bfloat16

N_META = 16
RMS_EPS = 1e-6
NEG_INF = -1e30
HEADS = 8
HEAD_W = 128
DA_HEAD_DIM = 64
DA_W = HEADS * HEAD_W
MLA_Q_RANK = 768
MLA_KV_RANK = 512
MLA_NOPE = 128
MLA_ROPE = 64
MLA_QK_W = 256
ROPE_THETA = 10000.0
REL_BUCKETS = 32
REL_MAX_DIST = 128
LAMBDA_INIT = 0.8 - 0.6 * math.exp(-0.3 * 0)
LOG2_E = math.log2(math.e)

V7X_LANES = 128
V7X_BF16_SUBLANES = 16
V7X_VMEM_BYTES = 64 * 1024 * 1024
VMEM_CAP_BYTES = V7X_VMEM_BYTES - 6 * 1024 * 1024

META_BLK = V7X_LANES


def _row_tile(rows, max_tile, multiple=V7X_BF16_SUBLANES):
    best = None
    for t in range(multiple, max_tile + 1, multiple):
        if rows % t == 0:
            best = t
    assert best is not None, (rows, max_tile)
    return best


def _col_tile(cols, max_tile):
    return _row_tile(cols, max_tile, V7X_LANES)


def _params(semantics, vmem_bytes):
    return pltpu.CompilerParams(
        dimension_semantics=semantics,
        vmem_limit_bytes=int(min(max(vmem_bytes, 16 * 1024 * 1024), VMEM_CAP_BYTES)))


def _resident(block_shape, index_map):
    return pl.BlockSpec(block_shape, index_map, pipeline_mode=pl.Buffered(1))


def _rms(x, g):
    ms = jnp.mean(x * x, axis=-1, keepdims=True)
    return x * lax.rsqrt(ms + RMS_EPS) * g


def _dot(a, b):
    return jnp.dot(a, b, preferred_element_type=F32)


def _dot_nt(a, b):
    return lax.dot_general(a, b, (((1,), (1,)), ((), ())), preferred_element_type=F32)


def _ffn_kernel(x_ref, pre_ref, post_ref, wg_ref, wu_ref, wd_ref, o_ref, xn_ref):
    k = pl.program_id(1)
    last = pl.num_programs(1) - 1

    def partial_out(xn):
        g = _dot(xn, wg_ref[...])
        u = _dot(xn, wu_ref[...])
        a = (g * jax.nn.sigmoid(g) * u).astype(BF16)
        return _dot(a, wd_ref[...])

    @pl.when(k == 0)
    def _():
        xn = _rms(x_ref[...], pre_ref[...]).astype(BF16)
        xn_ref[...] = xn
        o_ref[...] = partial_out(xn)

    @pl.when(jnp.logical_and(k > 0, k < last))
    def _():
        o_ref[...] += partial_out(xn_ref[...])

    @pl.when(k == last)
    def _():
        f = o_ref[...] + partial_out(xn_ref[...])
        o_ref[...] = x_ref[...] + 0.5 * _rms(f, post_ref[...])


def _ffn(x, pre_g, post_g, wg, wu, wd, *, name):
    rows, d = x.shape
    f = wg.shape[1]
    tm = _row_tile(rows, 896)
    tf = _col_tile(f, 512)
    assert f // tf >= 2, "the kernel treats the first and the last d_ff tile as different steps"
    vmem = (2 * 2 * tm * d * 4
            + tm * d * 2
            + 2 * 3 * d * tf * 2
            + 4 * tm * tf * 4
            + tm * d * 4)
    return pl.pallas_call(
        _ffn_kernel,
        out_shape=jax.ShapeDtypeStruct((rows, d), F32),
        grid=(rows // tm, f // tf),
        in_specs=[
            pl.BlockSpec((tm, d), lambda i, k: (i, 0)),
            _resident((1, d), lambda i, k: (0, 0)),
            _resident((1, d), lambda i, k: (0, 0)),
            pl.BlockSpec((d, tf), lambda i, k: (0, k)),
            pl.BlockSpec((d, tf), lambda i, k: (0, k)),
            pl.BlockSpec((tf, d), lambda i, k: (k, 0)),
        ],
        out_specs=pl.BlockSpec((tm, d), lambda i, k: (i, 0)),
        scratch_shapes=[pltpu.VMEM((tm, d), BF16)],
        compiler_params=_params(("parallel", "arbitrary"), vmem),
        name=name,
    )(x, pre_g, post_g, wg, wu, wd)


VT_ROWS = HEAD_W + V7X_BF16_SUBLANES


def _store_vt(vt_ref, vt):
    tm = vt.shape[1]
    vt_ref[0, :, 0:HEAD_W, :] = vt.reshape(HEADS, HEAD_W, tm).astype(BF16)
    extra = lax.broadcasted_iota(jnp.int32, (HEADS, VT_ROWS - HEAD_W, tm), 1)
    vt_ref[0, :, HEAD_W:VT_ROWS, :] = jnp.where(extra == 0, 1.0, 0.0).astype(BF16)


def _da_proj_kernel(h_ref, g_ref, wqt_ref, wk_ref, wvt_ref, qt_ref, k_ref, vt_ref):
    hn = _rms(h_ref[...], g_ref[...]).astype(BF16)
    tm = hn.shape[0]
    qt = _dot_nt(wqt_ref[...], hn) * (DA_HEAD_DIM ** -0.5 * LOG2_E)
    qt_ref[0] = qt.reshape(HEADS, HEAD_W, tm).astype(BF16)
    k = _dot(hn, wk_ref[...])
    for h in range(HEADS):
        k_ref[0, h] = k[:, h * HEAD_W:(h + 1) * HEAD_W].astype(BF16)
    _store_vt(vt_ref, _dot_nt(wvt_ref[...], hn))


_C_CKV = MLA_Q_RANK
_C_KR1 = _C_CKV + MLA_KV_RANK
_C_KR2 = _C_KR1 + HEAD_W
_C_END = _C_KR2 + HEAD_W
_UQ_HEAD_W = 3 * HEAD_W


def _mla_proj_kernel(h_ref, g_ref, cos_ref, sin_ref, cost_ref, sint_ref, wc_ref, gq_ref, gkv_ref,
                     wuqt_ref, wuk_ref, wuvt_ref, qt_ref, k_ref, vt_ref):
    hn = _rms(h_ref[...], g_ref[...]).astype(BF16)
    tm = hn.shape[0]
    scale = (MLA_NOPE + MLA_ROPE) ** -0.5 * LOG2_E

    c_all = _dot(hn, wc_ref[...])

    def proj(lo, hi):
        return c_all[:, lo:hi]

    cq = _rms(proj(0, _C_CKV), gq_ref[...]).astype(BF16)
    qt = _dot_nt(wuqt_ref[...], cq)
    cos_t = cost_ref[...]
    sin_t = sint_ref[...]
    for h in range(HEADS):
        base = h * _UQ_HEAD_W
        nope = qt[base:base + HEAD_W]
        r1 = qt[base + HEAD_W:base + 2 * HEAD_W]
        r2 = qt[base + 2 * HEAD_W:base + 3 * HEAD_W]
        qt_ref[0, h, 0:HEAD_W, :] = (nope * scale).astype(BF16)
        qt_ref[0, h, HEAD_W:MLA_QK_W, :] = ((r1 * cos_t + r2 * sin_t) * scale).astype(BF16)

    ckv = _rms(proj(_C_CKV, _C_KR1), gkv_ref[...]).astype(BF16)
    k_nope = _dot(ckv, wuk_ref[...])
    k_rope = (proj(_C_KR1, _C_KR2) * cos_ref[...] + proj(_C_KR2, _C_END) * sin_ref[...]).astype(BF16)
    for h in range(HEADS):
        k_ref[0, h, :, 0:HEAD_W] = k_nope[:, h * HEAD_W:(h + 1) * HEAD_W].astype(BF16)
        k_ref[0, h, :, HEAD_W:MLA_QK_W] = k_rope
    _store_vt(vt_ref, _dot_nt(wuvt_ref[...], ckv))


def _proj_specs(b, lp, tm, q_w):
    nt = lp // tm
    rows = lambda w: pl.BlockSpec((1, HEADS, tm, w), lambda i: (i // nt, 0, i % nt, 0))
    cols = lambda w: pl.BlockSpec((1, HEADS, w, tm), lambda i: (i // nt, 0, 0, i % nt))
    shapes = [jax.ShapeDtypeStruct((b, HEADS, q_w, lp), BF16),
              jax.ShapeDtypeStruct((b, HEADS, lp, q_w), BF16),
              jax.ShapeDtypeStruct((b, HEADS, VT_ROWS, lp), BF16)]
    return shapes, [cols(q_w), rows(q_w), cols(VT_ROWS)]


def _da_proj(h1, g, wqt, wk, wvt):
    b, lp, d = h1.shape
    tm = _col_tile(lp, 704)
    nt = lp // tm
    shapes, out_specs = _proj_specs(b, lp, tm, HEAD_W)
    vmem = (2 * tm * d * 4 + tm * d * 2 + 3 * DA_W * d * 2
            + 2 * 3 * tm * DA_W * 2 + 3 * tm * DA_W * 4)
    return pl.pallas_call(
        _da_proj_kernel,
        out_shape=shapes,
        grid=(b * nt,),
        in_specs=[
            pl.BlockSpec((None, tm, d), lambda i: (i // nt, i % nt, 0)),
            _resident((1, d), lambda i: (0, 0)),
            _resident(wqt.shape, lambda i: (0, 0)),
            _resident(wk.shape, lambda i: (0, 0)),
            _resident(wvt.shape, lambda i: (0, 0)),
        ],
        out_specs=out_specs,
        compiler_params=_params(("parallel",), vmem),
        name="da_in_proj",
    )(h1, g, wqt, wk, wvt)


def _mla_proj(h1, g, cos, sin, cos_t, sin_t, wc, gq, gkv, wuqt, wuk, wuvt):
    b, lp, d = h1.shape
    tm = _col_tile(lp, 704)
    nt = lp // tm
    shapes, out_specs = _proj_specs(b, lp, tm, MLA_QK_W)
    vmem = (2 * tm * d * 4 + tm * d * 2
            + (d * _C_END + HEADS * _UQ_HEAD_W * MLA_Q_RANK + 2 * DA_W * MLA_KV_RANK) * 2
            + 2 * tm * HEADS * (2 * MLA_QK_W + HEAD_W) * 2
            + 2 * tm * HEADS * _UQ_HEAD_W * 4)
    return pl.pallas_call(
        _mla_proj_kernel,
        out_shape=shapes,
        grid=(b * nt,),
        in_specs=[
            pl.BlockSpec((None, tm, d), lambda i: (i // nt, i % nt, 0)),
            _resident((1, d), lambda i: (0, 0)),
            pl.BlockSpec((tm, HEAD_W), lambda i: (i % nt, 0)),
            pl.BlockSpec((tm, HEAD_W), lambda i: (i % nt, 0)),
            pl.BlockSpec((HEAD_W, tm), lambda i: (0, i % nt)),
            pl.BlockSpec((HEAD_W, tm), lambda i: (0, i % nt)),
            _resident(wc.shape, lambda i: (0, 0)),
            _resident((1, MLA_Q_RANK), lambda i: (0, 0)),
            _resident((1, MLA_KV_RANK), lambda i: (0, 0)),
            _resident(wuqt.shape, lambda i: (0, 0)),
            _resident(wuk.shape, lambda i: (0, 0)),
            _resident(wuvt.shape, lambda i: (0, 0)),
        ],
        out_specs=out_specs,
        compiler_params=_params(("parallel",), vmem),
        name="mla_in_proj",
    )(h1, g, cos, sin, cos_t, sin_t, wc, gq, gkv, wuqt, wuk, wuvt)


def _t5_bucket(rel):
    n = np.maximum(rel, 0)
    max_exact = REL_BUCKETS // 2
    n_f = np.maximum(n, 1).astype(np.float64)
    large = max_exact + (np.log(n_f / max_exact) / math.log(REL_MAX_DIST / max_exact)
                         * (REL_BUCKETS - max_exact)).astype(np.int32)
    large = np.minimum(large, REL_BUCKETS - 1)
    return np.where(n < max_exact, n, large).astype(np.int32)


def _bucket_tiles(t):
    j = np.arange(t)[:, None]
    i = np.arange(t)[None, :]
    diag = np.where(j <= i, _t5_bucket(i - j), -1)
    sub = _t5_bucket(t + i - j)
    assert _t5_bucket(np.array([t + 1]))[0] == REL_BUCKETS - 1
    jm = np.arange(META_BLK)[:, None]
    meta_first = np.where(jm < N_META, _t5_bucket(N_META + i - jm), -1)
    meta_rest = np.where(jm < N_META, REL_BUCKETS - 1, -1) + 0 * i
    return (diag.astype(np.int32), np.stack([np.full_like(sub, -1), sub]).astype(np.int32),
            np.stack([meta_first, meta_rest]).astype(np.int32))


def _bias_kernel(table_ref, bd_ref, bs_ref, bm_ref, od_ref, os_ref, om_ref, *, buckets):
    h = pl.program_id(0)
    far = table_ref[REL_BUCKETS - 1, h]

    def lookup(bucket_ref, out_ref, known):
        rows, cols = known.shape
        for r0 in range(0, rows, V7X_LANES):
            for c0 in range(0, cols, V7X_LANES):
                patch = (slice(r0, r0 + V7X_LANES), slice(c0, c0 + V7X_LANES))
                present = np.unique(known[patch])
                acc = jnp.zeros((V7X_LANES, V7X_LANES), F32)
                if present.size > 1 or 0 <= present[0] < REL_BUCKETS - 1:
                    bucket = bucket_ref[patch]
                    for b in present:
                        if 0 <= b < REL_BUCKETS - 1:
                            acc = jnp.where(bucket == int(b), (table_ref[int(b), h] - far) * LOG2_E, acc)
                    if present[0] < 0:
                        acc = jnp.where(bucket < 0, NEG_INF, acc)
                elif present[0] < 0:
                    acc = jnp.full((V7X_LANES, V7X_LANES), NEG_INF, F32)
                out_ref[patch] = acc

    bd, bs, bm = buckets
    lookup(bd_ref, od_ref.at[0], bd)
    for i in range(2):
        lookup(bs_ref.at[i], os_ref.at[0, i], bs[i])
        lookup(bm_ref.at[i], om_ref.at[0, i], bm[i])


def _bias_tiles(table, t):
    bd, bs, bm = _bucket_tiles(t)
    return pl.pallas_call(
        functools.partial(_bias_kernel, buckets=(bd, bs, bm)),
        out_shape=[jax.ShapeDtypeStruct((HEADS, t, t), F32),
                   jax.ShapeDtypeStruct((HEADS, 2, t, t), F32),
                   jax.ShapeDtypeStruct((HEADS, 2, META_BLK, t), F32)],
        grid=(HEADS,),
        in_specs=[
            pl.BlockSpec(memory_space=pltpu.SMEM),
            pl.BlockSpec((t, t), lambda h: (0, 0)),
            pl.BlockSpec((2, t, t), lambda h: (0, 0, 0)),
            pl.BlockSpec((2, META_BLK, t), lambda h: (0, 0, 0)),
        ],
        out_specs=[pl.BlockSpec((1, t, t), lambda h: (h, 0, 0)),
                   pl.BlockSpec((1, 2, t, t), lambda h: (h, 0, 0, 0)),
                   pl.BlockSpec((1, 2, META_BLK, t), lambda h: (h, 0, 0, 0))],
        compiler_params=_params(("parallel",), 12 * t * t * 4),
        name="rel_bias_tiles",
    )(table, jnp.asarray(bd), jnp.asarray(bs), jnp.asarray(bm))


ATTN_QUERY_CHUNK = 256


ATTN_SCORE_BUFFERS = 4


class _Step(NamedTuple):
    k: Callable[[], Any]
    vt: Callable[[], Any]
    bias: Optional[Callable[[], Any]]
    chunk: int


def _block_steps(n_chunks, k, vt, bias=None):
    return [_Step(k, vt, (lambda c=c: bias(c)) if bias is not None else None, c)
            for c in range(n_chunks)]


def _issue_scores(qt_ref, s_ref, step, slot):
    qc = s_ref.shape[2]
    k = step.k()
    s_ref[slot, 0:k.shape[0], :] = _dot(k, qt_ref[:, pl.ds(step.chunk * qc, qc)])


def _prefetch_scores(qt_ref, s_ref, steps):
    for i, step in enumerate(steps[:s_ref.shape[0] - 1]):
        _issue_scores(qt_ref, s_ref, step, i)


def _run_steps(qt_ref, s_ref, steps, following, m_ref, acc_ref):
    n_buf, _, qc = s_ref.shape
    ahead = n_buf - 1
    assert not following or len(steps) % n_buf == 0
    seq = list(steps) + list(following)[:ahead]
    for i, step in enumerate(steps):
        if i + ahead < len(seq):
            _issue_scores(qt_ref, s_ref, seq[i + ahead], (i + ahead) % n_buf)
        vt = step.vt()
        s = s_ref[i % n_buf, 0:vt.shape[1], :]
        if step.bias is not None:
            s = s + step.bias()
        sl = pl.ds(step.chunk * qc, qc)
        m_prev = m_ref[:, sl]
        m_new = jnp.maximum(m_prev, jnp.max(s, axis=0, keepdims=True))
        alpha = jnp.exp2(m_prev - m_new)
        p = jnp.exp2(s - m_new)
        acc_ref[:, sl] = alpha * acc_ref[:, sl] + _dot(vt, p.astype(BF16))
        m_ref[:, sl] = m_new


def _attend_init(m_ref, acc_ref):
    m_ref[...] = jnp.full(m_ref.shape, NEG_INF, F32)
    acc_ref[...] = jnp.zeros(acc_ref.shape, F32)


FAR_UNROLL = 8


def _walk_far_blocks(n_far, body):
    def loop_body(i, carry):
        body(i * FAR_UNROLL, FAR_UNROLL)
        return carry

    lax.fori_loop(0, n_far >> (FAR_UNROLL.bit_length() - 1), loop_body, 0)
    count = FAR_UNROLL // 2
    while count >= 1:
        first = n_far - (n_far & (2 * count - 1))

        @pl.when((n_far & count) != 0)
        def _(first=first, count=count):
            body(first, count)

        count //= 2


def _normalised(acc_ref):
    acc = acc_ref[...]
    return acc[0:HEAD_W] * (1.0 / acc[HEAD_W:HEAD_W + 1])


def _da_kernel(qt_ref, k_ref, vt_ref, bd_ref, bs_ref, bm_ref, lam_ref, subg_ref, o_ref,
               qq_ref, s_ref, m_ref, acc_ref, *, t, s_len):
    qi = pl.program_id(2)
    qt = qt_ref[0, 0]
    feat = lax.broadcasted_iota(jnp.int32, (HEAD_W, 1), 0)
    zero = jnp.zeros_like(qt)
    qq_ref[:, 0:t] = jnp.where(feat < DA_HEAD_DIM, qt, zero)
    qq_ref[:, t:2 * t] = jnp.where(feat >= DA_HEAD_DIM, qt, zero)
    qc = s_ref.shape[2]
    chunks_per_map = t // qc
    n_chunks = 2 * chunks_per_map

    def cols(c):
        return pl.ds((c % chunks_per_map) * qc, qc)

    def block_steps(kb, bias=None, keys=t):
        rows = pl.ds(pl.multiple_of(kb * t, t), keys)
        return _block_steps(n_chunks, lambda: k_ref[0, 0, rows, :], lambda: vt_ref[0, 0, :, rows], bias)

    state = (m_ref, acc_ref)
    n_far = jnp.maximum(qi - 1, 0)
    near_sel = jnp.minimum(qi, 1)
    prev_kb = jnp.maximum(qi - 1, 0)
    meta_rows = pl.ds(s_len, META_BLK)
    meta_steps = _block_steps(n_chunks, lambda: k_ref[0, 0, meta_rows, :],
                              lambda: vt_ref[0, 0, :, meta_rows],
                              lambda c: bm_ref[0, near_sel, :, cols(c)])
    _attend_init(*state)
    _prefetch_scores(qq_ref, s_ref, meta_steps)
    _run_steps(qq_ref, s_ref, meta_steps, block_steps(jnp.where(qi >= 2, 0, qi)), *state)

    def far(first, count):
        steps = [st for u in range(count) for st in block_steps(first + u)]
        last = first + count - 1
        _run_steps(qq_ref, s_ref, steps, block_steps(jnp.where(last == n_far - 1, qi, last + 1)), *state)

    _walk_far_blocks(n_far, far)

    diag = []
    for c in range(n_chunks):
        keys = (c % chunks_per_map + 1) * qc
        diag.append(block_steps(qi, lambda c, keys=keys: bd_ref[0, 0:keys, cols(c)], keys)[c])
    prev = block_steps(prev_kb, lambda c: bs_ref[0, near_sel, :, cols(c)])
    _run_steps(qq_ref, s_ref, diag + prev, (), *state)

    ot = _normalised(acc_ref)
    lp = lam_ref[...]
    lam = (jnp.exp(jnp.sum(lp[0:1] * lp[1:2], axis=-1, keepdims=True))
           - jnp.exp(jnp.sum(lp[2:3] * lp[3:4], axis=-1, keepdims=True)) + LAMBDA_INIT)
    d = ot[:, :t] - lam * ot[:, t:]
    ms = jnp.mean(d * d, axis=0, keepdims=True)
    y = d * lax.rsqrt(ms + RMS_EPS) * subg_ref[...] * (1.0 - LAMBDA_INIT)
    o_ref[0] = y.T.astype(BF16)


def _attn_specs(t, lp, q_w):
    return [
        pl.BlockSpec((1, 1, q_w, t), lambda bi, h, qi: (bi, h, 0, qi)),
        pl.BlockSpec((1, 1, lp, q_w), lambda bi, h, qi: (bi, h, 0, 0)),
        pl.BlockSpec((1, 1, VT_ROWS, lp), lambda bi, h, qi: (bi, h, 0, 0)),
    ]


def _da_attention(qt, k, vt, bias_d, bias_s, bias_m, lam_p, sub_g, *, t, s_len):
    b, _, lp, _ = k.shape
    nq = s_len // t
    vmem = (2 * 2 * lp * HEAD_W * 2 + 2 * 2 * t * t * 4 + 2 * 2 * t * META_BLK * 4
            + 2 * 2 * t * HEAD_W * 2 + 3 * 2 * t * HEAD_W * 4
            + 6 * t * ATTN_QUERY_CHUNK * 4)
    return pl.pallas_call(
        functools.partial(_da_kernel, t=t, s_len=s_len),
        out_shape=jax.ShapeDtypeStruct((b, s_len, DA_W), BF16),
        grid=(b, HEADS, nq),
        in_specs=_attn_specs(t, lp, HEAD_W) + [
            pl.BlockSpec((1, t, t), lambda bi, h, qi: (h, 0, 0)),
            pl.BlockSpec((1, 2, t, t), lambda bi, h, qi: (h, 0, 0, 0)),
            pl.BlockSpec((1, 2, META_BLK, t), lambda bi, h, qi: (h, 0, 0, 0)),
            _resident((4, DA_HEAD_DIM), lambda bi, h, qi: (0, 0)),
            _resident((HEAD_W, 1), lambda bi, h, qi: (0, 0)),
        ],
        out_specs=pl.BlockSpec((1, t, HEAD_W), lambda bi, h, qi: (bi, qi, h)),
        scratch_shapes=[pltpu.VMEM((HEAD_W, 2 * t), BF16),
                        pltpu.VMEM((ATTN_SCORE_BUFFERS, t, min(ATTN_QUERY_CHUNK, t)), F32),
                        pltpu.VMEM((1, 2 * t), F32), pltpu.VMEM((VT_ROWS, 2 * t), F32)],
        compiler_params=_params(("parallel", "parallel", "arbitrary"), vmem),
        name="diff_attention",
    )(qt, k, vt, bias_d, bias_s, bias_m, lam_p, sub_g)


def _mla_kernel(qt_ref, k_ref, vt_ref, o_ref, s_ref, m_ref, acc_ref, *, tq, tk, s_len):
    qi = pl.program_id(2)
    qv_ref = qt_ref.at[0, 0]
    qc = s_ref.shape[2]
    diag_blocks = tq // tk

    n_chunks = tq // qc

    def block_steps(kb, bias=None, keys=tk):
        rows = pl.ds(pl.multiple_of(kb * tk, tk), keys)
        return _block_steps(n_chunks, lambda: k_ref[0, 0, rows, :], lambda: vt_ref[0, 0, :, rows], bias)

    state = (m_ref, acc_ref)
    meta_rows = pl.ds(s_len, META_BLK)

    def meta_mask(c):
        key = lax.broadcasted_iota(jnp.int32, (META_BLK, qc), 0)
        return jnp.where(key < N_META, 0.0, NEG_INF)

    meta_steps = _block_steps(n_chunks, lambda: k_ref[0, 0, meta_rows, :],
                              lambda: vt_ref[0, 0, :, meta_rows], meta_mask)
    _attend_init(*state)
    _prefetch_scores(qv_ref, s_ref, meta_steps)
    _run_steps(qv_ref, s_ref, meta_steps, block_steps(0), *state)

    def far(first, count):
        steps = [st for u in range(count) for st in block_steps(first + u)]
        _run_steps(qv_ref, s_ref, steps, block_steps(first + count), *state)

    first_diag = qi * diag_blocks
    _walk_far_blocks(first_diag, far)

    diag = []
    for d in range(diag_blocks):
        for c in range(n_chunks):
            keys = min(tk, (c + 1) * qc - d * tk)
            if keys <= 0:
                continue

            def causal(c, d=d, keys=keys):
                key = lax.broadcasted_iota(jnp.int32, (keys, qc), 0) + d * tk
                query = lax.broadcasted_iota(jnp.int32, (keys, qc), 1) + c * qc
                return jnp.where(key <= query, 0.0, NEG_INF)

            diag.append(block_steps(first_diag + d, causal, keys)[c])
    _run_steps(qv_ref, s_ref, diag, (), *state)

    o_ref[0] = _normalised(acc_ref).T.astype(BF16)


def _mla_attention(qt, k, vt, *, tq, tk, s_len):
    b, _, lp, _ = k.shape
    nq = s_len // tq
    qc = min(ATTN_QUERY_CHUNK, tq)
    vmem = (2 * lp * (MLA_QK_W + HEAD_W) * 2 + 2 * tq * MLA_QK_W * 2
            + 2 * tq * HEAD_W * 2 + 3 * tq * HEAD_W * 4 + (ATTN_SCORE_BUFFERS + 4) * tk * qc * 4)
    return pl.pallas_call(
        functools.partial(_mla_kernel, tq=tq, tk=tk, s_len=s_len),
        out_shape=jax.ShapeDtypeStruct((b, s_len, DA_W), BF16),
        grid=(b, HEADS, nq),
        in_specs=_attn_specs(tq, lp, MLA_QK_W),
        out_specs=pl.BlockSpec((1, tq, HEAD_W), lambda bi, h, qi: (bi, qi, h)),
        scratch_shapes=[pltpu.VMEM((ATTN_SCORE_BUFFERS, tk, qc), F32),
                        pltpu.VMEM((1, tq), F32), pltpu.VMEM((VT_ROWS, tq), F32)],
        compiler_params=_params(("parallel", "parallel", "arbitrary"), vmem),
        name="mla_attention",
    )(qt, k, vt)


def _merge_kernel(h_ref, pre_ref, post_ref, yda_ref, ymla_ref, wg0_ref, wg1_ref, b0_ref, b1_ref,
                  wbd_ref, wbm_ref, wo_ref, o_ref, hn_ref):
    j = pl.program_id(1)
    last = pl.num_programs(1) - 1

    def partial_out(hn):
        g0 = jax.nn.sigmoid(_dot(hn, wg0_ref[...]) + b0_ref[...])
        g1 = jax.nn.sigmoid(_dot(hn, wg1_ref[...]) + b1_ref[...])
        merged = g0 * _dot(yda_ref[0], wbd_ref[...]) + g1 * _dot(ymla_ref[0], wbm_ref[...])
        return _dot(merged.astype(BF16), wo_ref[...])

    @pl.when(j == 0)
    def _():
        hn = _rms(h_ref[0], pre_ref[...]).astype(BF16)
        hn_ref[...] = hn
        o_ref[0] = partial_out(hn)

    @pl.when(jnp.logical_and(j > 0, j < last))
    def _():
        o_ref[0] += partial_out(hn_ref[...])

    @pl.when(j == last)
    def _():
        m = o_ref[0] + partial_out(hn_ref[...])
        o_ref[0] = h_ref[0] + _rms(m, post_ref[...])


def _merge(h1, pre_g, post_g, y_da, y_mla, w_gate, b_gate, wb_da, wb_mla, w_out, *, s_len):
    b, _, d = h1.shape
    yw = y_da.shape[-1]
    tm = _row_tile(s_len, 512)
    tn = _col_tile(d, 512)
    nt = s_len // tm
    nj = d // tn
    assert nj >= 2, "the kernel treats the first and the last column tile as different steps"
    vmem = (2 * 2 * tm * d * 4 + tm * d * 2 + 2 * 2 * tm * yw * 2
            + 2 * (3 * d * tn + 2 * yw * tn) * 2 + 6 * tm * tn * 4 + tm * d * 4)
    return pl.pallas_call(
        _merge_kernel,
        out_shape=jax.ShapeDtypeStruct((b, s_len, d), F32),
        grid=(b * nt, nj),
        in_specs=[
            pl.BlockSpec((1, tm, d), lambda i, j: (i // nt, i % nt, 0)),
            _resident((1, d), lambda i, j: (0, 0)),
            _resident((1, d), lambda i, j: (0, 0)),
            pl.BlockSpec((1, tm, yw), lambda i, j: (i // nt, i % nt, 0)),
            pl.BlockSpec((1, tm, yw), lambda i, j: (i // nt, i % nt, 0)),
            pl.BlockSpec((d, tn), lambda i, j: (0, j)),
            pl.BlockSpec((d, tn), lambda i, j: (0, nj + j)),
            pl.BlockSpec((1, tn), lambda i, j: (0, j)),
            pl.BlockSpec((1, tn), lambda i, j: (0, nj + j)),
            pl.BlockSpec((yw, tn), lambda i, j: (0, j)),
            pl.BlockSpec((yw, tn), lambda i, j: (0, j)),
            pl.BlockSpec((tn, d), lambda i, j: (j, 0)),
        ],
        out_specs=pl.BlockSpec((1, tm, d), lambda i, j: (i // nt, i % nt, 0)),
        scratch_shapes=[pltpu.VMEM((tm, d), BF16)],
        compiler_params=_params(("parallel", "arbitrary"), vmem),
        name="gated_merge_out_proj",
    )(h1, pre_g, post_g, y_da, y_mla, w_gate, w_gate, b_gate, b_gate, wb_da, wb_mla, w_out)


def _rope_tables(s_len):
    half = MLA_ROPE // 2
    pos = jnp.concatenate([N_META + jnp.arange(s_len), jnp.arange(N_META),
                           jnp.zeros((META_BLK - N_META,), jnp.int32)]).astype(F32)
    inv = ROPE_THETA ** (-jnp.arange(half, dtype=F32) * 2.0 / MLA_ROPE)
    ang = pos[:, None] * inv[None, :]
    cos, sin = jnp.cos(ang), jnp.sin(ang)
    pad = jnp.zeros((pos.shape[0], HEAD_W - MLA_ROPE), F32)
    return (jnp.concatenate([cos, cos, pad], axis=-1),
            jnp.concatenate([-sin, sin, pad], axis=-1))


def _swap_halves(w):
    half = w.shape[-1] // 2
    return jnp.concatenate([w[..., half:], w[..., :half]], axis=-1)


def _pad_lanes(w, width):
    return jnp.concatenate([w, jnp.zeros(w.shape[:-1] + (width - w.shape[-1],), w.dtype)], axis=-1)


def kernel(x, meta_tokens, rel_bias_table, ffn1_pre_g, ffn1_post_g, ffn1_w_gate, ffn1_w_up, ffn1_w_down, mix_pre_g, mix_post_g, w_in, b_gate, da_lambda_q1, da_lambda_k1, da_lambda_q2, da_lambda_k2, da_sub_g, mla_q_norm_g, mla_kv_norm_g, mla_w_uq, mla_w_ukv, w_branch_da, w_branch_mla, w_out, ffn2_pre_g, ffn2_post_g, ffn2_w_gate, ffn2_w_up, ffn2_w_down):
    b, s_len, d = x.shape
    assert ffn1_pre_g.shape[0] == 1, "single-layer trunk"
    lp = s_len + META_BLK
    t = _col_tile(s_len, 512)
    row = lambda g: g.reshape(1, -1).astype(F32)

    meta = jnp.broadcast_to(meta_tokens.astype(x.dtype)[None], (b, N_META, d))
    h0 = jnp.concatenate([x, meta, jnp.zeros((b, META_BLK - N_META, d), x.dtype)], axis=1)

    h1 = _ffn(h0.reshape(b * lp, d), row(ffn1_pre_g), row(ffn1_post_g),
              ffn1_w_gate[0].astype(BF16), ffn1_w_up[0].astype(BF16), ffn1_w_down[0].astype(BF16),
              name="ffn1").reshape(b, lp, d)

    w = w_in[0]
    o_k, o_v, o_cq = DA_W, 2 * DA_W, 3 * DA_W
    o_kr = o_cq + MLA_Q_RANK + MLA_KV_RANK
    o_gate = o_kr + MLA_ROPE
    qt_da, k_da, vt_da = _da_proj(
        h1, row(mix_pre_g), w[:, :o_k].T.astype(BF16), w[:, o_k:o_v].astype(BF16),
        w[:, o_v:o_cq].T.astype(BF16))

    kr = w[:, o_kr:o_gate]
    w_c = jnp.concatenate([w[:, o_cq:o_kr], _pad_lanes(kr, HEAD_W),
                           _pad_lanes(_swap_halves(kr), HEAD_W)], axis=-1).astype(BF16)
    uq = mla_w_uq[0].reshape(MLA_Q_RANK, HEADS, MLA_NOPE + MLA_ROPE)
    uq_rope = uq[..., MLA_NOPE:]
    w_uqt = jnp.concatenate([uq[..., :MLA_NOPE], _pad_lanes(uq_rope, HEAD_W),
                             _pad_lanes(_swap_halves(uq_rope), HEAD_W)],
                            axis=-1).reshape(MLA_Q_RANK, HEADS * _UQ_HEAD_W).T.astype(BF16)
    ukv = mla_w_ukv[0].reshape(MLA_KV_RANK, HEADS, 2 * HEAD_W)
    w_uk = ukv[..., :HEAD_W].reshape(MLA_KV_RANK, DA_W).astype(BF16)
    w_uvt = ukv[..., HEAD_W:].reshape(MLA_KV_RANK, DA_W).T.astype(BF16)
    cos, sin = _rope_tables(s_len)
    qt_mla, k_mla, vt_mla = _mla_proj(
        h1, row(mix_pre_g), cos, sin, cos.T, sin.T, w_c, row(mla_q_norm_g), row(mla_kv_norm_g),
        w_uqt, w_uk, w_uvt)

    bias_d, bias_s, bias_m = _bias_tiles(rel_bias_table.astype(F32), t)
    lam_p = jnp.concatenate([da_lambda_q1, da_lambda_k1, da_lambda_q2, da_lambda_k2], axis=0).astype(F32)
    y_da = _da_attention(qt_da, k_da, vt_da, bias_d, bias_s, bias_m, lam_p, da_sub_g.reshape(HEAD_W, 1).astype(F32),
                         t=t, s_len=s_len)
    y_mla = _mla_attention(qt_mla, k_mla, vt_mla, tq=_col_tile(s_len, 2 * t), tk=t, s_len=s_len)

    h2 = _merge(h1, row(mix_pre_g), row(mix_post_g), y_da, y_mla,
                w[:, o_gate:].astype(BF16), row(b_gate),
                w_branch_da[0].astype(BF16), w_branch_mla[0].astype(BF16), w_out[0].astype(BF16),
                s_len=s_len)

    out = _ffn(h2.reshape(b * s_len, d), row(ffn2_pre_g), row(ffn2_post_g),
               ffn2_w_gate[0].astype(BF16), ffn2_w_up[0].astype(BF16), ffn2_w_down[0].astype(BF16),
               name="ffn2")
    return out.reshape(b, s_len, d)
```

```python
import functools
import math
from typing import Any, Callable, NamedTuple, Optional

import numpy as np
import jax
import jax.numpy as jnp
from jax import lax
from jax.experimental import pallas as pl
from jax.experimental.pallas import tpu as pltpu

F32 = jnp.float32
BF16 = jnp.bfloat16

N_META = 16
RMS_EPS = 1e-6
NEG_INF = -1e30
HEADS = 8
HEAD_W = 128
DA_HEAD_DIM = 64
DA_W = HEADS * HEAD_W
MLA_Q_RANK = 768
MLA_KV_RANK = 512
MLA_NOPE = 128
MLA_ROPE = 64
MLA_QK_W = 256
ROPE_THETA = 10000.0
REL_BUCKETS = 32
REL_MAX_DIST = 128
LAMBDA_INIT = 0.8 - 0.6 * math.exp(-0.3 * 0)
LOG2_E = math.log2(math.e)

V7X_LANES = 128
V7X_BF16_SUBLANES = 16
V7X_VMEM_BYTES = 64 * 1024 * 1024
VMEM_CAP_BYTES = V7X_VMEM_BYTES - 6 * 1024 * 1024

META_BLK = V7X_LANES


def _row_tile(rows, max_tile, multiple=V7X_BF16_SUBLANES):
    best = None
    for t in range(multiple, max_tile + 1, multiple):
        if rows % t == 0:
            best = t
    assert best is not None, (rows, max_tile)
    return best


def _col_tile(cols, max_tile):
    return _row_tile(cols, max_tile, V7X_LANES)


def _params(semantics, vmem_bytes):
    return pltpu.CompilerParams(
        dimension_semantics=semantics,
        vmem_limit_bytes=int(min(max(vmem_bytes, 16 * 1024 * 1024), VMEM_CAP_BYTES)))


def _resident(block_shape, index_map):
    return pl.BlockSpec(block_shape, index_map, pipeline_mode=pl.Buffered(1))


def _rms(x, g):
    ms = jnp.mean(x * x, axis=-1, keepdims=True)
    return x * lax.rsqrt(ms + RMS_EPS) * g


def _dot(a, b):
    return jnp.dot(a, b, preferred_element_type=F32)


def _dot_nt(a, b):
    return lax.dot_general(a, b, (((1,), (1,)), ((), ())), preferred_element_type=F32)


def _ffn_kernel(x_ref, pre_ref, post_ref, wg_ref, wu_ref, wd_ref, o_ref, xn_ref):
    k = pl.program_id(1)
    last = pl.num_programs(1) - 1

    def partial_out(xn):
        g = _dot(xn, wg_ref[...])
        u = _dot(xn, wu_ref[...])
        a = (g * jax.nn.sigmoid(g) * u).astype(BF16)
        return _dot(a, wd_ref[...])

    @pl.when(k == 0)
    def _():
        xn = _rms(x_ref[...], pre_ref[...]).astype(BF16)
        xn_ref[...] = xn
        o_ref[...] = partial_out(xn)

    @pl.when(jnp.logical_and(k > 0, k < last))
    def _():
        o_ref[...] += partial_out(xn_ref[...])

    @pl.when(k == last)
    def _():
        f = o_ref[...] + partial_out(xn_ref[...])
        o_ref[...] = x_ref[...] + 0.5 * _rms(f, post_ref[...])


def _ffn(x, pre_g, post_g, wg, wu, wd, *, name):
    rows, d = x.shape
    f = wg.shape[1]
    tm = _row_tile(rows, 896)
    tf = _col_tile(f, 512)
    assert f // tf >= 2, "the kernel treats the first and the last d_ff tile as different steps"
    vmem = (2 * 2 * tm * d * 4
            + tm * d * 2
            + 2 * 3 * d * tf * 2
            + 4 * tm * tf * 4
            + tm * d * 4)
    return pl.pallas_call(
        _ffn_kernel,
        out_shape=jax.ShapeDtypeStruct((rows, d), F32),
        grid=(rows // tm, f // tf),
        in_specs=[
            pl.BlockSpec((tm, d), lambda i, k: (i, 0)),
            _resident((1, d), lambda i, k: (0, 0)),
            _resident((1, d), lambda i, k: (0, 0)),
            pl.BlockSpec((d, tf), lambda i, k: (0, k)),
            pl.BlockSpec((d, tf), lambda i, k: (0, k)),
            pl.BlockSpec((tf, d), lambda i, k: (k, 0)),
        ],
        out_specs=pl.BlockSpec((tm, d), lambda i, k: (i, 0)),
        scratch_shapes=[pltpu.VMEM((tm, d), BF16)],
        compiler_params=_params(("parallel", "arbitrary"), vmem),
        name=name,
    )(x, pre_g, post_g, wg, wu, wd)


VT_ROWS = HEAD_W + V7X_BF16_SUBLANES


def _store_vt(vt_ref, vt):
    tm = vt.shape[1]
    vt_ref[0, :, 0:HEAD_W, :] = vt.reshape(HEADS, HEAD_W, tm).astype(BF16)
    extra = lax.broadcasted_iota(jnp.int32, (HEADS, VT_ROWS - HEAD_W, tm), 1)
    vt_ref[0, :, HEAD_W:VT_ROWS, :] = jnp.where(extra == 0, 1.0, 0.0).astype(BF16)


def _da_proj_kernel(h_ref, g_ref, wqt_ref, wk_ref, wvt_ref, qt_ref, k_ref, vt_ref):
    hn = _rms(h_ref[...], g_ref[...]).astype(BF16)
    tm = hn.shape[0]
    qt = _dot_nt(wqt_ref[...], hn) * (DA_HEAD_DIM ** -0.5 * LOG2_E)
    qt_ref[0] = qt.reshape(HEADS, HEAD_W, tm).astype(BF16)
    k = _dot(hn, wk_ref[...])
    for h in range(HEADS):
        k_ref[0, h] = k[:, h * HEAD_W:(h + 1) * HEAD_W].astype(BF16)
    _store_vt(vt_ref, _dot_nt(wvt_ref[...], hn))


_C_CKV = MLA_Q_RANK
_C_KR1 = _C_CKV + MLA_KV_RANK
_C_KR2 = _C_KR1 + HEAD_W
_C_END = _C_KR2 + HEAD_W
_UQ_HEAD_W = 3 * HEAD_W


def _mla_proj_kernel(h_ref, g_ref, cos_ref, sin_ref, cost_ref, sint_ref, wc_ref, gq_ref, gkv_ref,
                     wuqt_ref, wuk_ref, wuvt_ref, qt_ref, k_ref, vt_ref):
    hn = _rms(h_ref[...], g_ref[...]).astype(BF16)
    tm = hn.shape[0]
    scale = (MLA_NOPE + MLA_ROPE) ** -0.5 * LOG2_E

    c_all = _dot(hn, wc_ref[...])

    def proj(lo, hi):
        return c_all[:, lo:hi]

    cq = _rms(proj(0, _C_CKV), gq_ref[...]).astype(BF16)
    qt = _dot_nt(wuqt_ref[...], cq)
    cos_t = cost_ref[...]
    sin_t = sint_ref[...]
    for h in range(HEADS):
        base = h * _UQ_HEAD_W
        nope = qt[base:base + HEAD_W]
        r1 = qt[base + HEAD_W:base + 2 * HEAD_W]
        r2 = qt[base + 2 * HEAD_W:base + 3 * HEAD_W]
        qt_ref[0, h, 0:HEAD_W, :] = (nope * scale).astype(BF16)
        qt_ref[0, h, HEAD_W:MLA_QK_W, :] = ((r1 * cos_t + r2 * sin_t) * scale).astype(BF16)

    ckv = _rms(proj(_C_CKV, _C_KR1), gkv_ref[...]).astype(BF16)
    k_nope = _dot(ckv, wuk_ref[...])
    k_rope = (proj(_C_KR1, _C_KR2) * cos_ref[...] + proj(_C_KR2, _C_END) * sin_ref[...]).astype(BF16)
    for h in range(HEADS):
        k_ref[0, h, :, 0:HEAD_W] = k_nope[:, h * HEAD_W:(h + 1) * HEAD_W].astype(BF16)
        k_ref[0, h, :, HEAD_W:MLA_QK_W] = k_rope
    _store_vt(vt_ref, _dot_nt(wuvt_ref[...], ckv))


def _proj_specs(b, lp, tm, q_w):
    nt = lp // tm
    rows = lambda w: pl.BlockSpec((1, HEADS, tm, w), lambda i: (i // nt, 0, i % nt, 0))
    cols = lambda w: pl.BlockSpec((1, HEADS, w, tm), lambda i: (i // nt, 0, 0, i % nt))
    shapes = [jax.ShapeDtypeStruct((b, HEADS, q_w, lp), BF16),
              jax.ShapeDtypeStruct((b, HEADS, lp, q_w), BF16),
              jax.ShapeDtypeStruct((b, HEADS, VT_ROWS, lp), BF16)]
    return shapes, [cols(q_w), rows(q_w), cols(VT_ROWS)]


def _da_proj(h1, g, wqt, wk, wvt):
    b, lp, d = h1.shape
    tm = _col_tile(lp, 704)
    nt = lp // tm
    shapes, out_specs = _proj_specs(b, lp, tm, HEAD_W)
    vmem = (2 * tm * d * 4 + tm * d * 2 + 3 * DA_W * d * 2
            + 2 * 3 * tm * DA_W * 2 + 3 * tm * DA_W * 4)
    return pl.pallas_call(
        _da_proj_kernel,
        out_shape=shapes,
        grid=(b * nt,),
        in_specs=[
            pl.BlockSpec((None, tm, d), lambda i: (i // nt, i % nt, 0)),
            _resident((1, d), lambda i: (0, 0)),
            _resident(wqt.shape, lambda i: (0, 0)),
            _resident(wk.shape, lambda i: (0, 0)),
            _resident(wvt.shape, lambda i: (0, 0)),
        ],
        out_specs=out_specs,
        compiler_params=_params(("parallel",), vmem),
        name="da_in_proj",
    )(h1, g, wqt, wk, wvt)


def _mla_proj(h1, g, cos, sin, cos_t, sin_t, wc, gq, gkv, wuqt, wuk, wuvt):
    b, lp, d = h1.shape
    tm = _col_tile(lp, 704)
    nt = lp // tm
    shapes, out_specs = _proj_specs(b, lp, tm, MLA_QK_W)
    vmem = (2 * tm * d * 4 + tm * d * 2
            + (d * _C_END + HEADS * _UQ_HEAD_W * MLA_Q_RANK + 2 * DA_W * MLA_KV_RANK) * 2
            + 2 * tm * HEADS * (2 * MLA_QK_W + HEAD_W) * 2
            + 2 * tm * HEADS * _UQ_HEAD_W * 4)
    return pl.pallas_call(
        _mla_proj_kernel,
        out_shape=shapes,
        grid=(b * nt,),
        in_specs=[
            pl.BlockSpec((None, tm, d), lambda i: (i // nt, i % nt, 0)),
            _resident((1, d), lambda i: (0, 0)),
            pl.BlockSpec((tm, HEAD_W), lambda i: (i % nt, 0)),
            pl.BlockSpec((tm, HEAD_W), lambda i: (i % nt, 0)),
            pl.BlockSpec((HEAD_W, tm), lambda i: (0, i % nt)),
            pl.BlockSpec((HEAD_W, tm), lambda i: (0, i % nt)),
            _resident(wc.shape, lambda i: (0, 0)),
            _resident((1, MLA_Q_RANK), lambda i: (0, 0)),
            _resident((1, MLA_KV_RANK), lambda i: (0, 0)),
            _resident(wuqt.shape, lambda i: (0, 0)),
            _resident(wuk.shape, lambda i: (0, 0)),
            _resident(wuvt.shape, lambda i: (0, 0)),
        ],
        out_specs=out_specs,
        compiler_params=_params(("parallel",), vmem),
        name="mla_in_proj",
    )(h1, g, cos, sin, cos_t, sin_t, wc, gq, gkv, wuqt, wuk, wuvt)


def _t5_bucket(rel):
    n = np.maximum(rel, 0)
    max_exact = REL_BUCKETS // 2
    n_f = np.maximum(n, 1).astype(np.float64)
    large = max_exact + (np.log(n_f / max_exact) / math.log(REL_MAX_DIST / max_exact)
                         * (REL_BUCKETS - max_exact)).astype(np.int32)
    large = np.minimum(large, REL_BUCKETS - 1)
    return np.where(n < max_exact, n, large).astype(np.int32)


def _bucket_tiles(t):
    j = np.arange(t)[:, None]
    i = np.arange(t)[None, :]
    diag = np.where(j <= i, _t5_bucket(i - j), -1)
    sub = _t5_bucket(t + i - j)
    assert _t5_bucket(np.array([t + 1]))[0] == REL_BUCKETS - 1
    jm = np.arange(META_BLK)[:, None]
    meta_first = np.where(jm < N_META, _t5_bucket(N_META + i - jm), -1)
    meta_rest = np.where(jm < N_META, REL_BUCKETS - 1, -1) + 0 * i
    return (diag.astype(np.int32), np.stack([np.full_like(sub, -1), sub]).astype(np.int32),
            np.stack([meta_first, meta_rest]).astype(np.int32))


def _bias_kernel(table_ref, bd_ref, bs_ref, bm_ref, od_ref, os_ref, om_ref, *, buckets):
    h = pl.program_id(0)
    far = table_ref[REL_BUCKETS - 1, h]

    def lookup(bucket_ref, out_ref, known):
        rows, cols = known.shape
        for r0 in range(0, rows, V7X_LANES):
            for c0 in range(0, cols, V7X_LANES):
                patch = (slice(r0, r0 + V7X_LANES), slice(c0, c0 + V7X_LANES))
                present = np.unique(known[patch])
                acc = jnp.zeros((V7X_LANES, V7X_LANES), F32)
                if present.size > 1 or 0 <= present[0] < REL_BUCKETS - 1:
                    bucket = bucket_ref[patch]
                    for b in present:
                        if 0 <= b < REL_BUCKETS - 1:
                            acc = jnp.where(bucket == int(b), (table_ref[int(b), h] - far) * LOG2_E, acc)
                    if present[0] < 0:
                        acc = jnp.where(bucket < 0, NEG_INF, acc)
                elif present[0] < 0:
                    acc = jnp.full((V7X_LANES, V7X_LANES), NEG_INF, F32)
                out_ref[patch] = acc

    bd, bs, bm = buckets
    lookup(bd_ref, od_ref.at[0], bd)
    for i in range(2):
        lookup(bs_ref.at[i], os_ref.at[0, i], bs[i])
        lookup(bm_ref.at[i], om_ref.at[0, i], bm[i])


def _bias_tiles(table, t):
    bd, bs, bm = _bucket_tiles(t)
    return pl.pallas_call(
        functools.partial(_bias_kernel, buckets=(bd, bs, bm)),
        out_shape=[jax.ShapeDtypeStruct((HEADS, t, t), F32),
                   jax.ShapeDtypeStruct((HEADS, 2, t, t), F32),
                   jax.ShapeDtypeStruct((HEADS, 2, META_BLK, t), F32)],
        grid=(HEADS,),
        in_specs=[
            pl.BlockSpec(memory_space=pltpu.SMEM),
            pl.BlockSpec((t, t), lambda h: (0, 0)),
            pl.BlockSpec((2, t, t), lambda h: (0, 0, 0)),
            pl.BlockSpec((2, META_BLK, t), lambda h: (0, 0, 0)),
        ],
        out_specs=[pl.BlockSpec((1, t, t), lambda h: (h, 0, 0)),
                   pl.BlockSpec((1, 2, t, t), lambda h: (h, 0, 0, 0)),
                   pl.BlockSpec((1, 2, META_BLK, t), lambda h: (h, 0, 0, 0))],
        compiler_params=_params(("parallel",), 12 * t * t * 4),
        name="rel_bias_tiles",
    )(table, jnp.asarray(bd), jnp.asarray(bs), jnp.asarray(bm))


ATTN_QUERY_CHUNK = 256


ATTN_SCORE_BUFFERS = 4
ATTN_HEADS_PER_STEP = 2


class _Step(NamedTuple):
    q: Callable[[], Any]
    k: Callable[[], Any]
    vt: Callable[[], Any]
    bias: Optional[Callable[[], Any]]
    chunk: int


def _block_steps(n_chunks, q, k, vt, bias=None):
    return [_Step(functools.partial(q, c), functools.partial(k, c), functools.partial(vt, c),
                  functools.partial(bias, c) if bias is not None else None, c)
            for c in range(n_chunks)]


def _issue_scores(s_ref, step, slot):
    k = step.k()
    s_ref[slot, 0:k.shape[0], :] = _dot(k, step.q())


def _prefetch_scores(s_ref, steps):
    for i, step in enumerate(steps[:s_ref.shape[0] - 1]):
        _issue_scores(s_ref, step, i)


def _run_steps(s_ref, steps, following, m_ref, acc_ref):
    n_buf, _, qc = s_ref.shape
    ahead = n_buf - 1
    assert not following or len(steps) % n_buf == 0
    seq = list(steps) + list(following)[:ahead]
    for i, step in enumerate(steps):
        if i + ahead < len(seq):
            _issue_scores(s_ref, seq[i + ahead], (i + ahead) % n_buf)
        vt = step.vt()
        s = s_ref[i % n_buf, 0:vt.shape[1], :]
        if step.bias is not None:
            s = s + step.bias()
        sl = pl.ds(step.chunk * qc, qc)
        m_prev = m_ref[:, sl]
        m_new = jnp.maximum(m_prev, jnp.max(s, axis=0, keepdims=True))
        alpha = jnp.exp2(m_prev - m_new)
        p = jnp.exp2(s - m_new)
        acc_ref[:, sl] = alpha * acc_ref[:, sl] + _dot(vt, p.astype(BF16))
        m_ref[:, sl] = m_new


def _attend_init(m_ref, acc_ref):
    m_ref[...] = jnp.full(m_ref.shape, NEG_INF, F32)
    acc_ref[...] = jnp.zeros(acc_ref.shape, F32)


FAR_UNROLL = 4


def _walk_far_blocks(n_far, body):
    def loop_body(i, carry):
        body(i * FAR_UNROLL, FAR_UNROLL)
        return carry

    lax.fori_loop(0, n_far >> (FAR_UNROLL.bit_length() - 1), loop_body, 0)
    count = FAR_UNROLL // 2
    while count >= 1:
        first = n_far - (n_far & (2 * count - 1))

        @pl.when((n_far & count) != 0)
        def _(first=first, count=count):
            body(first, count)

        count //= 2


def _normalised(acc_ref):
    acc = acc_ref[...]
    return acc[0:HEAD_W] * (1.0 / acc[HEAD_W:HEAD_W + 1])


def _da_kernel(qt_ref, k_ref, vt_ref, bd_ref, bs_ref, bm_ref, lam_ref, subg_ref, o_ref,
               qq_ref, s_ref, m_ref, acc_ref, *, t, s_len):
    qi = pl.program_id(2)
    feat = lax.broadcasted_iota(jnp.int32, (HEAD_W, 1), 0)
    for hd in range(ATTN_HEADS_PER_STEP):
        qt = qt_ref[0, hd]
        zero = jnp.zeros_like(qt)
        qq_ref[:, pl.ds(2 * t * hd, t)] = jnp.where(feat < DA_HEAD_DIM, qt, zero)
        qq_ref[:, pl.ds(2 * t * hd + t, t)] = jnp.where(feat >= DA_HEAD_DIM, qt, zero)
    qc = s_ref.shape[2]
    chunks_per_map = t // qc
    chunks_per_head = 2 * chunks_per_map
    n_chunks = ATTN_HEADS_PER_STEP * chunks_per_head

    def head(c):
        return c // chunks_per_head

    def cols(c):
        return pl.ds((c % chunks_per_map) * qc, qc)

    def q_chunk(c):
        return qq_ref[:, pl.ds(c * qc, qc)]

    def block_steps(kb, bias=None, keys=t):
        rows = pl.ds(pl.multiple_of(kb * t, t), keys)
        return _block_steps(n_chunks, q_chunk, lambda c: k_ref[0, head(c), rows, :],
                            lambda c: vt_ref[0, head(c), :, rows], bias)

    state = (m_ref, acc_ref)
    n_far = jnp.maximum(qi - 1, 0)
    near_sel = jnp.minimum(qi, 1)
    prev_kb = jnp.maximum(qi - 1, 0)
    meta_rows = pl.ds(s_len, META_BLK)
    meta_steps = _block_steps(n_chunks, q_chunk, lambda c: k_ref[0, head(c), meta_rows, :],
                              lambda c: vt_ref[0, head(c), :, meta_rows],
                              lambda c: bm_ref[head(c), near_sel, :, cols(c)])
    _attend_init(*state)
    _prefetch_scores(s_ref, meta_steps)
    _run_steps(s_ref, meta_steps, block_steps(jnp.where(qi >= 2, 0, qi)), *state)

    def far(first, count):
        steps = [st for u in range(count) for st in block_steps(first + u)]
        last = first + count - 1
        _run_steps(s_ref, steps, block_steps(jnp.where(last == n_far - 1, qi, last + 1)), *state)

    _walk_far_blocks(n_far, far)

    diag = []
    for c in range(n_chunks):
        keys = (c % chunks_per_map + 1) * qc
        diag.append(block_steps(qi, lambda c, keys=keys: bd_ref[head(c), 0:keys, cols(c)], keys)[c])
    prev = block_steps(prev_kb, lambda c: bs_ref[head(c), near_sel, :, cols(c)])
    _run_steps(s_ref, diag + prev, (), *state)

    ot = _normalised(acc_ref)
    lp = lam_ref[...]
    lam = (jnp.exp(jnp.sum(lp[0:1] * lp[1:2], axis=-1, keepdims=True))
           - jnp.exp(jnp.sum(lp[2:3] * lp[3:4], axis=-1, keepdims=True)) + LAMBDA_INIT)
    for hd in range(ATTN_HEADS_PER_STEP):
        o1 = ot[:, 2 * t * hd:2 * t * hd + t]
        o2 = ot[:, 2 * t * hd + t:2 * t * (hd + 1)]
        d = o1 - lam * o2
        ms = jnp.mean(d * d, axis=0, keepdims=True)
        y = d * lax.rsqrt(ms + RMS_EPS) * subg_ref[...] * (1.0 - LAMBDA_INIT)
        o_ref[0, :, hd * HEAD_W:(hd + 1) * HEAD_W] = y.T.astype(BF16)


def _attn_specs(t, lp, q_w):
    hp = ATTN_HEADS_PER_STEP
    return [
        pl.BlockSpec((1, hp, q_w, t), lambda bi, h, qi: (bi, h, 0, qi)),
        pl.BlockSpec((1, hp, lp, q_w), lambda bi, h, qi: (bi, h, 0, 0)),
        pl.BlockSpec((1, hp, VT_ROWS, lp), lambda bi, h, qi: (bi, h, 0, 0)),
    ]


def _da_attention(qt, k, vt, bias_d, bias_s, bias_m, lam_p, sub_g, *, t, s_len):
    b, _, lp, _ = k.shape
    nq = s_len // t
    hp = ATTN_HEADS_PER_STEP
    qc = min(ATTN_QUERY_CHUNK, t)
    vmem = hp * (2 * lp * (HEAD_W + VT_ROWS) * 2 + 2 * 3 * t * t * 4 + 2 * 2 * t * META_BLK * 4
                 + 2 * 2 * t * HEAD_W * 2 + 2 * t * HEAD_W * 2 + (VT_ROWS + 8) * 2 * t * 4
                 ) + (ATTN_SCORE_BUFFERS + 4) * t * qc * 4
    return pl.pallas_call(
        functools.partial(_da_kernel, t=t, s_len=s_len),
        out_shape=jax.ShapeDtypeStruct((b, s_len, DA_W), BF16),
        grid=(b, HEADS // hp, nq),
        in_specs=_attn_specs(t, lp, HEAD_W) + [
            pl.BlockSpec((hp, t, t), lambda bi, h, qi: (h, 0, 0)),
            pl.BlockSpec((hp, 2, t, t), lambda bi, h, qi: (h, 0, 0, 0)),
            pl.BlockSpec((hp, 2, META_BLK, t), lambda bi, h, qi: (h, 0, 0, 0)),
            _resident((4, DA_HEAD_DIM), lambda bi, h, qi: (0, 0)),
            _resident((HEAD_W, 1), lambda bi, h, qi: (0, 0)),
        ],
        out_specs=pl.BlockSpec((1, t, hp * HEAD_W), lambda bi, h, qi: (bi, qi, h)),
        scratch_shapes=[pltpu.VMEM((HEAD_W, hp * 2 * t), BF16),
                        pltpu.VMEM((ATTN_SCORE_BUFFERS, t, qc), F32),
                        pltpu.VMEM((1, hp * 2 * t), F32), pltpu.VMEM((VT_ROWS, hp * 2 * t), F32)],
        compiler_params=_params(("parallel", "parallel", "arbitrary"), vmem),
        name="diff_attention",
    )(qt, k, vt, bias_d, bias_s, bias_m, lam_p, sub_g)


def _mla_kernel(qt_ref, k_ref, vt_ref, o_ref, s_ref, m_ref, acc_ref, *, tq, tk, s_len):
    qi = pl.program_id(2)
    qc = s_ref.shape[2]
    diag_blocks = tq // tk
    chunks_per_head = tq // qc
    n_chunks = ATTN_HEADS_PER_STEP * chunks_per_head

    def head(c):
        return c // chunks_per_head

    def q_chunk(c):
        return qt_ref[0, head(c), :, pl.ds((c % chunks_per_head) * qc, qc)]

    def block_steps(kb, bias=None, keys=tk):
        rows = pl.ds(pl.multiple_of(kb * tk, tk), keys)
        return _block_steps(n_chunks, q_chunk, lambda c: k_ref[0, head(c), rows, :],
                            lambda c: vt_ref[0, head(c), :, rows], bias)

    state = (m_ref, acc_ref)
    meta_rows = pl.ds(s_len, META_BLK)

    def meta_mask(c):
        key = lax.broadcasted_iota(jnp.int32, (META_BLK, qc), 0)
        return jnp.where(key < N_META, 0.0, NEG_INF)

    meta_steps = _block_steps(n_chunks, q_chunk, lambda c: k_ref[0, head(c), meta_rows, :],
                              lambda c: vt_ref[0, head(c), :, meta_rows], meta_mask)
    _attend_init(*state)
    _prefetch_scores(s_ref, meta_steps)
    _run_steps(s_ref, meta_steps, block_steps(0), *state)

    def far(first, count):
        steps = [st for u in range(count) for st in block_steps(first + u)]
        _run_steps(s_ref, steps, block_steps(first + count), *state)

    first_diag = qi * diag_blocks
    _walk_far_blocks(first_diag, far)

    diag = []
    for d in range(diag_blocks):
        for c in range(n_chunks):
            cq = (c % chunks_per_head) * qc
            keys = min(tk, cq + qc - d * tk)
            if keys <= 0:
                continue

            def causal(c, d=d, keys=keys, cq=cq):
                key = lax.broadcasted_iota(jnp.int32, (keys, qc), 0) + d * tk
                query = lax.broadcasted_iota(jnp.int32, (keys, qc), 1) + cq
                return jnp.where(key <= query, 0.0, NEG_INF)

            diag.append(block_steps(first_diag + d, causal, keys)[c])
    _run_steps(s_ref, diag, (), *state)

    ot = _normalised(acc_ref)
    for hd in range(ATTN_HEADS_PER_STEP):
        o_ref[0, :, hd * HEAD_W:(hd + 1) * HEAD_W] = ot[:, hd * tq:(hd + 1) * tq].T.astype(BF16)


def _mla_attention(qt, k, vt, *, tq, tk, s_len):
    b, _, lp, _ = k.shape
    nq = s_len // tq
    hp = ATTN_HEADS_PER_STEP
    qc = min(ATTN_QUERY_CHUNK, tq)
    vmem = hp * (2 * lp * (MLA_QK_W + VT_ROWS) * 2 + 2 * tq * MLA_QK_W * 2
                 + 2 * tq * HEAD_W * 2 + (VT_ROWS + 8) * tq * 4 + tq * HEAD_W * 4
                 ) + (ATTN_SCORE_BUFFERS + 4) * tk * qc * 4
    return pl.pallas_call(
        functools.partial(_mla_kernel, tq=tq, tk=tk, s_len=s_len),
        out_shape=jax.ShapeDtypeStruct((b, s_len, DA_W), BF16),
        grid=(b, HEADS // hp, nq),
        in_specs=_attn_specs(tq, lp, MLA_QK_W),
        out_specs=pl.BlockSpec((1, tq, hp * HEAD_W), lambda bi, h, qi: (bi, qi, h)),
        scratch_shapes=[pltpu.VMEM((ATTN_SCORE_BUFFERS, tk, qc), F32),
                        pltpu.VMEM((1, hp * tq), F32), pltpu.VMEM((VT_ROWS, hp * tq), F32)],
        compiler_params=_params(("parallel", "parallel", "arbitrary"), vmem),
        name="mla_attention",
    )(qt, k, vt)


def _merge_kernel(h_ref, pre_ref, post_ref, yda_ref, ymla_ref, wg0_ref, wg1_ref, b0_ref, b1_ref,
                  wbd_ref, wbm_ref, wo_ref, o_ref, hn_ref):
    j = pl.program_id(1)
    last = pl.num_programs(1) - 1

    def partial_out(hn):
        g0 = jax.nn.sigmoid(_dot(hn, wg0_ref[...]) + b0_ref[...])
        g1 = jax.nn.sigmoid(_dot(hn, wg1_ref[...]) + b1_ref[...])
        merged = g0 * _dot(yda_ref[0], wbd_ref[...]) + g1 * _dot(ymla_ref[0], wbm_ref[...])
        return _dot(merged.astype(BF16), wo_ref[...])

    @pl.when(j == 0)
    def _():
        hn = _rms(h_ref[0], pre_ref[...]).astype(BF16)
        hn_ref[...] = hn
        o_ref[0] = partial_out(hn)

    @pl.when(jnp.logical_and(j > 0, j < last))
    def _():
        o_ref[0] += partial_out(hn_ref[...])

    @pl.when(j == last)
    def _():
        m = o_ref[0] + partial_out(hn_ref[...])
        o_ref[0] = h_ref[0] + _rms(m, post_ref[...])


def _merge(h1, pre_g, post_g, y_da, y_mla, w_gate, b_gate, wb_da, wb_mla, w_out, *, s_len):
    b, _, d = h1.shape
    yw = y_da.shape[-1]
    tm = _row_tile(s_len, 512)
    tn = _col_tile(d, 512)
    nt = s_len // tm
    nj = d // tn
    assert nj >= 2, "the kernel treats the first and the last column tile as different steps"
    vmem = (2 * 2 * tm * d * 4 + tm * d * 2 + 2 * 2 * tm * yw * 2
            + 2 * (3 * d * tn + 2 * yw * tn) * 2 + 6 * tm * tn * 4 + tm * d * 4)
    return pl.pallas_call(
        _merge_kernel,
        out_shape=jax.ShapeDtypeStruct((b, s_len, d), F32),
        grid=(b * nt, nj),
        in_specs=[
            pl.BlockSpec((1, tm, d), lambda i, j: (i // nt, i % nt, 0)),
            _resident((1, d), lambda i, j: (0, 0)),
            _resident((1, d), lambda i, j: (0, 0)),
            pl.BlockSpec((1, tm, yw), lambda i, j: (i // nt, i % nt, 0)),
            pl.BlockSpec((1, tm, yw), lambda i, j: (i // nt, i % nt, 0)),
            pl.BlockSpec((d, tn), lambda i, j: (0, j)),
            pl.BlockSpec((d, tn), lambda i, j: (0, nj + j)),
            pl.BlockSpec((1, tn), lambda i, j: (0, j)),
            pl.BlockSpec((1, tn), lambda i, j: (0, nj + j)),
            pl.BlockSpec((yw, tn), lambda i, j: (0, j)),
            pl.BlockSpec((yw, tn), lambda i, j: (0, j)),
            pl.BlockSpec((tn, d), lambda i, j: (j, 0)),
        ],
        out_specs=pl.BlockSpec((1, tm, d), lambda i, j: (i // nt, i % nt, 0)),
        scratch_shapes=[pltpu.VMEM((tm, d), BF16)],
        compiler_params=_params(("parallel", "arbitrary"), vmem),
        name="gated_merge_out_proj",
    )(h1, pre_g, post_g, y_da, y_mla, w_gate, w_gate, b_gate, b_gate, wb_da, wb_mla, w_out)


def _rope_tables(s_len):
    half = MLA_ROPE // 2
    pos = jnp.concatenate([N_META + jnp.arange(s_len), jnp.arange(N_META),
                           jnp.zeros((META_BLK - N_META,), jnp.int32)]).astype(F32)
    inv = ROPE_THETA ** (-jnp.arange(half, dtype=F32) * 2.0 / MLA_ROPE)
    ang = pos[:, None] * inv[None, :]
    cos, sin = jnp.cos(ang), jnp.sin(ang)
    pad = jnp.zeros((pos.shape[0], HEAD_W - MLA_ROPE), F32)
    return (jnp.concatenate([cos, cos, pad], axis=-1),
            jnp.concatenate([-sin, sin, pad], axis=-1))


def _swap_halves(w):
    half = w.shape[-1] // 2
    return jnp.concatenate([w[..., half:], w[..., :half]], axis=-1)


def _pad_lanes(w, width):
    return jnp.concatenate([w, jnp.zeros(w.shape[:-1] + (width - w.shape[-1],), w.dtype)], axis=-1)


def kernel(x, meta_tokens, rel_bias_table, ffn1_pre_g, ffn1_post_g, ffn1_w_gate, ffn1_w_up, ffn1_w_down, mix_pre_g, mix_post_g, w_in, b_gate, da_lambda_q1, da_lambda_k1, da_lambda_q2, da_lambda_k2, da_sub_g, mla_q_norm_g, mla_kv_norm_g, mla_w_uq, mla_w_ukv, w_branch_da, w_branch_mla, w_out, ffn2_pre_g, ffn2_post_g, ffn2_w_gate, ffn2_w_up, ffn2_w_down):
    b, s_len, d = x.shape
    assert ffn1_pre_g.shape[0] == 1, "single-layer trunk"
    lp = s_len + META_BLK
    t = _col_tile(s_len, 512)
    row = lambda g: g.reshape(1, -1).astype(F32)

    meta = jnp.broadcast_to(meta_tokens.astype(x.dtype)[None], (b, N_META, d))
    h0 = jnp.concatenate([x, meta, jnp.zeros((b, META_BLK - N_META, d), x.dtype)], axis=1)

    h1 = _ffn(h0.reshape(b * lp, d), row(ffn1_pre_g), row(ffn1_post_g),
              ffn1_w_gate[0].astype(BF16), ffn1_w_up[0].astype(BF16), ffn1_w_down[0].astype(BF16),
              name="ffn1").reshape(b, lp, d)

    w = w_in[0]
    o_k, o_v, o_cq = DA_W, 2 * DA_W, 3 * DA_W
    o_kr = o_cq + MLA_Q_RANK + MLA_KV_RANK
    o_gate = o_kr + MLA_ROPE
    qt_da, k_da, vt_da = _da_proj(
        h1, row(mix_pre_g), w[:, :o_k].T.astype(BF16), w[:, o_k:o_v].astype(BF16),
        w[:, o_v:o_cq].T.astype(BF16))

    kr = w[:, o_kr:o_gate]
    w_c = jnp.concatenate([w[:, o_cq:o_kr], _pad_lanes(kr, HEAD_W),
                           _pad_lanes(_swap_halves(kr), HEAD_W)], axis=-1).astype(BF16)
    uq = mla_w_uq[0].reshape(MLA_Q_RANK, HEADS, MLA_NOPE + MLA_ROPE)
    uq_rope = uq[..., MLA_NOPE:]
    w_uqt = jnp.concatenate([uq[..., :MLA_NOPE], _pad_lanes(uq_rope, HEAD_W),
                             _pad_lanes(_swap_halves(uq_rope), HEAD_W)],
                            axis=-1).reshape(MLA_Q_RANK, HEADS * _UQ_HEAD_W).T.astype(BF16)
    ukv = mla_w_ukv[0].reshape(MLA_KV_RANK, HEADS, 2 * HEAD_W)
    w_uk = ukv[..., :HEAD_W].reshape(MLA_KV_RANK, DA_W).astype(BF16)
    w_uvt = ukv[..., HEAD_W:].reshape(MLA_KV_RANK, DA_W).T.astype(BF16)
    cos, sin = _rope_tables(s_len)
    qt_mla, k_mla, vt_mla = _mla_proj(
        h1, row(mix_pre_g), cos, sin, cos.T, sin.T, w_c, row(mla_q_norm_g), row(mla_kv_norm_g),
        w_uqt, w_uk, w_uvt)

    bias_d, bias_s, bias_m = _bias_tiles(rel_bias_table.astype(F32), t)
    lam_p = jnp.concatenate([da_lambda_q1, da_lambda_k1, da_lambda_q2, da_lambda_k2], axis=0).astype(F32)
    y_da = _da_attention(qt_da, k_da, vt_da, bias_d, bias_s, bias_m, lam_p, da_sub_g.reshape(HEAD_W, 1).astype(F32),
                         t=t, s_len=s_len)
    y_mla = _mla_attention(qt_mla, k_mla, vt_mla, tq=_col_tile(s_len, 2 * t), tk=t, s_len=s_len)

    h2 = _merge(h1, row(mix_pre_g), row(mix_post_g), y_da, y_mla,
                w[:, o_gate:].astype(BF16), row(b_gate),
                w_branch_da[0].astype(BF16), w_branch_mla[0].astype(BF16), w_out[0].astype(BF16),
                s_len=s_len)

    out = _ffn(h2.reshape(b * s_len, d), row(ffn2_pre_g), row(ffn2_post_g),
               ffn2_w_gate[0].astype(BF16), ffn2_w_up[0].astype(BF16), ffn2_w_down[0].astype(BF16),
               name="ffn2")
    return out.reshape(b, s_len, d)
```

```python
import functools
import math
from typing import Any, Callable, NamedTuple, Optional

import numpy as np
import jax
import jax.numpy as jnp
from jax import lax
from jax.experimental import pallas as pl
from jax.experimental.pallas import tpu as pltpu

F32 = jnp.float32
BF16 = jnp.bfloat16

N_META = 16
RMS_EPS = 1e-6
NEG_INF = -1e30
HEADS = 8
HEAD_W = 128
DA_HEAD_DIM = 64
DA_W = HEADS * HEAD_W
MLA_Q_RANK = 768
MLA_KV_RANK = 512
MLA_NOPE = 128
MLA_ROPE = 64
MLA_QK_W = 256
ROPE_THETA = 10000.0
REL_BUCKETS = 32
REL_MAX_DIST = 128
LAMBDA_INIT = 0.8 - 0.6 * math.exp(-0.3 * 0)
LOG2_E = math.log2(math.e)

V7X_LANES = 128
V7X_BF16_SUBLANES = 16
V7X_VMEM_BYTES = 64 * 1024 * 1024
VMEM_CAP_BYTES = V7X_VMEM_BYTES - 6 * 1024 * 1024

META_BLK = V7X_LANES


def _row_tile(rows, max_tile, multiple=V7X_BF16_SUBLANES):
    best = None
    for t in range(multiple, max_tile + 1, multiple):
        if rows % t == 0:
            best = t
    assert best is not None, (rows, max_tile)
    return best


def _col_tile(cols, max_tile):
    return _row_tile(cols, max_tile, V7X_LANES)


def _params(semantics, vmem_bytes):
    return pltpu.CompilerParams(
        dimension_semantics=semantics,
        vmem_limit_bytes=int(min(max(vmem_bytes, 16 * 1024 * 1024), VMEM_CAP_BYTES)))


def _resident(block_shape, index_map):
    return pl.BlockSpec(block_shape, index_map, pipeline_mode=pl.Buffered(1))


def _rms(x, g):
    ms = jnp.mean(x * x, axis=-1, keepdims=True)
    return x * lax.rsqrt(ms + RMS_EPS) * g


def _dot(a, b):
    return jnp.dot(a, b, preferred_element_type=F32)


def _dot_nt(a, b):
    return lax.dot_general(a, b, (((1,), (1,)), ((), ())), preferred_element_type=F32)


def _ffn_kernel(x_ref, pre_ref, post_ref, wg_ref, wu_ref, wd_ref, o_ref, xn_ref):
    k = pl.program_id(1)
    last = pl.num_programs(1) - 1

    def partial_out(xn):
        g = _dot(xn, wg_ref[...])
        u = _dot(xn, wu_ref[...])
        a = (g * jax.nn.sigmoid(g) * u).astype(BF16)
        return _dot(a, wd_ref[...])

    @pl.when(k == 0)
    def _():
        xn = _rms(x_ref[...], pre_ref[...]).astype(BF16)
        xn_ref[...] = xn
        o_ref[...] = partial_out(xn)

    @pl.when(jnp.logical_and(k > 0, k < last))
    def _():
        o_ref[...] += partial_out(xn_ref[...])

    @pl.when(k == last)
    def _():
        f = o_ref[...] + partial_out(xn_ref[...])
        o_ref[...] = x_ref[...] + 0.5 * _rms(f, post_ref[...])


def _ffn(x, pre_g, post_g, wg, wu, wd, *, name):
    rows, d = x.shape
    f = wg.shape[1]
    tm = _row_tile(rows, 896)
    tf = _col_tile(f, 512)
    assert f // tf >= 2, "the kernel treats the first and the last d_ff tile as different steps"
    vmem = (2 * 2 * tm * d * 4
            + tm * d * 2
            + 2 * 3 * d * tf * 2
            + 4 * tm * tf * 4
            + tm * d * 4)
    return pl.pallas_call(
        _ffn_kernel,
        out_shape=jax.ShapeDtypeStruct((rows, d), F32),
        grid=(rows // tm, f // tf),
        in_specs=[
            pl.BlockSpec((tm, d), lambda i, k: (i, 0)),
            _resident((1, d), lambda i, k: (0, 0)),
            _resident((1, d), lambda i, k: (0, 0)),
            pl.BlockSpec((d, tf), lambda i, k: (0, k)),
            pl.BlockSpec((d, tf), lambda i, k: (0, k)),
            pl.BlockSpec((tf, d), lambda i, k: (k, 0)),
        ],
        out_specs=pl.BlockSpec((tm, d), lambda i, k: (i, 0)),
        scratch_shapes=[pltpu.VMEM((tm, d), BF16)],
        compiler_params=_params(("parallel", "arbitrary"), vmem),
        name=name,
    )(x, pre_g, post_g, wg, wu, wd)


VT_ROWS = HEAD_W + V7X_BF16_SUBLANES


def _store_vt(vt_ref, vt):
    tm = vt.shape[1]
    vt_ref[0, :, 0:HEAD_W, :] = vt.reshape(HEADS, HEAD_W, tm).astype(BF16)
    extra = lax.broadcasted_iota(jnp.int32, (HEADS, VT_ROWS - HEAD_W, tm), 1)
    vt_ref[0, :, HEAD_W:VT_ROWS, :] = jnp.where(extra == 0, 1.0, 0.0).astype(BF16)


def _da_proj_kernel(h_ref, g_ref, wqt_ref, wk_ref, wvt_ref, qt_ref, k_ref, vt_ref):
    hn = _rms(h_ref[...], g_ref[...]).astype(BF16)
    tm = hn.shape[0]
    qt = _dot_nt(wqt_ref[...], hn) * (DA_HEAD_DIM ** -0.5 * LOG2_E)
    qt_ref[0] = qt.reshape(HEADS, HEAD_W, tm).astype(BF16)
    k = _dot(hn, wk_ref[...])
    for h in range(HEADS):
        k_ref[0, h] = k[:, h * HEAD_W:(h + 1) * HEAD_W].astype(BF16)
    _store_vt(vt_ref, _dot_nt(wvt_ref[...], hn))


_C_CKV = MLA_Q_RANK
_C_KR1 = _C_CKV + MLA_KV_RANK
_C_KR2 = _C_KR1 + HEAD_W
_C_END = _C_KR2 + HEAD_W
_UQ_HEAD_W = 3 * HEAD_W


def _mla_proj_kernel(h_ref, g_ref, cos_ref, sin_ref, cost_ref, sint_ref, wc_ref, gq_ref, gkv_ref,
                     wuqt_ref, wuk_ref, wuvt_ref, qt_ref, k_ref, vt_ref):
    hn = _rms(h_ref[...], g_ref[...]).astype(BF16)
    tm = hn.shape[0]
    scale = (MLA_NOPE + MLA_ROPE) ** -0.5 * LOG2_E

    c_all = _dot(hn, wc_ref[...])

    def proj(lo, hi):
        return c_all[:, lo:hi]

    cq = _rms(proj(0, _C_CKV), gq_ref[...]).astype(BF16)
    qt = _dot_nt(wuqt_ref[...], cq)
    cos_t = cost_ref[...]
    sin_t = sint_ref[...]
    for h in range(HEADS):
        base = h * _UQ_HEAD_W
        nope = qt[base:base + HEAD_W]
        r1 = qt[base + HEAD_W:base + 2 * HEAD_W]
        r2 = qt[base + 2 * HEAD_W:base + 3 * HEAD_W]
        qt_ref[0, h, 0:HEAD_W, :] = (nope * scale).astype(BF16)
        qt_ref[0, h, HEAD_W:MLA_QK_W, :] = ((r1 * cos_t + r2 * sin_t) * scale).astype(BF16)

    ckv = _rms(proj(_C_CKV, _C_KR1), gkv_ref[...]).astype(BF16)
    k_nope = _dot(ckv, wuk_ref[...])
    k_rope = (proj(_C_KR1, _C_KR2) * cos_ref[...] + proj(_C_KR2, _C_END) * sin_ref[...]).astype(BF16)
    for h in range(HEADS):
        k_ref[0, h, :, 0:HEAD_W] = k_nope[:, h * HEAD_W:(h + 1) * HEAD_W].astype(BF16)
        k_ref[0, h, :, HEAD_W:MLA_QK_W] = k_rope
    _store_vt(vt_ref, _dot_nt(wuvt_ref[...], ckv))


def _proj_specs(b, lp, tm, q_w):
    nt = lp // tm
    rows = lambda w: pl.BlockSpec((1, HEADS, tm, w), lambda i: (i // nt, 0, i % nt, 0))
    cols = lambda w: pl.BlockSpec((1, HEADS, w, tm), lambda i: (i // nt, 0, 0, i % nt))
    shapes = [jax.ShapeDtypeStruct((b, HEADS, q_w, lp), BF16),
              jax.ShapeDtypeStruct((b, HEADS, lp, q_w), BF16),
              jax.ShapeDtypeStruct((b, HEADS, VT_ROWS, lp), BF16)]
    return shapes, [cols(q_w), rows(q_w), cols(VT_ROWS)]


def _da_proj(h1, g, wqt, wk, wvt, *, name):
    b, lp, d = h1.shape
    tm = _col_tile(lp, 704)
    nt = lp // tm
    shapes, out_specs = _proj_specs(b, lp, tm, HEAD_W)
    vmem = (2 * tm * d * 4 + tm * d * 2 + 3 * DA_W * d * 2
            + 2 * 3 * tm * DA_W * 2 + 3 * tm * DA_W * 4)
    return pl.pallas_call(
        _da_proj_kernel,
        out_shape=shapes,
        grid=(b * nt,),
        in_specs=[
            pl.BlockSpec((None, tm, d), lambda i: (i // nt, i % nt, 0)),
            _resident((1, d), lambda i: (0, 0)),
            _resident(wqt.shape, lambda i: (0, 0)),
            _resident(wk.shape, lambda i: (0, 0)),
            _resident(wvt.shape, lambda i: (0, 0)),
        ],
        out_specs=out_specs,
        compiler_params=_params(("parallel",), vmem),
        name=name,
    )(h1, g, wqt, wk, wvt)


def _mla_proj(h1, g, cos, sin, cos_t, sin_t, wc, gq, gkv, wuqt, wuk, wuvt, *, name):
    b, lp, d = h1.shape
    tm = _col_tile(lp, 704)
    nt = lp // tm
    shapes, out_specs = _proj_specs(b, lp, tm, MLA_QK_W)
    vmem = (2 * tm * d * 4 + tm * d * 2
            + (d * _C_END + HEADS * _UQ_HEAD_W * MLA_Q_RANK + 2 * DA_W * MLA_KV_RANK) * 2
            + 2 * tm * HEADS * (2 * MLA_QK_W + HEAD_W) * 2
            + 2 * tm * HEADS * _UQ_HEAD_W * 4)
    return pl.pallas_call(
        _mla_proj_kernel,
        out_shape=shapes,
        grid=(b * nt,),
        in_specs=[
            pl.BlockSpec((None, tm, d), lambda i: (i // nt, i % nt, 0)),
            _resident((1, d), lambda i: (0, 0)),
            pl.BlockSpec((tm, HEAD_W), lambda i: (i % nt, 0)),
            pl.BlockSpec((tm, HEAD_W), lambda i: (i % nt, 0)),
            pl.BlockSpec((HEAD_W, tm), lambda i: (0, i % nt)),
            pl.BlockSpec((HEAD_W, tm), lambda i: (0, i % nt)),
            _resident(wc.shape, lambda i: (0, 0)),
            _resident((1, MLA_Q_RANK), lambda i: (0, 0)),
            _resident((1, MLA_KV_RANK), lambda i: (0, 0)),
            _resident(wuqt.shape, lambda i: (0, 0)),
            _resident(wuk.shape, lambda i: (0, 0)),
            _resident(wuvt.shape, lambda i: (0, 0)),
        ],
        out_specs=out_specs,
        compiler_params=_params(("parallel",), vmem),
        name=name,
    )(h1, g, cos, sin, cos_t, sin_t, wc, gq, gkv, wuqt, wuk, wuvt)


def _t5_bucket(rel):
    n = np.maximum(rel, 0)
    max_exact = REL_BUCKETS // 2
    n_f = np.maximum(n, 1).astype(np.float64)
    large = max_exact + (np.log(n_f / max_exact) / math.log(REL_MAX_DIST / max_exact)
                         * (REL_BUCKETS - max_exact)).astype(np.int32)
    large = np.minimum(large, REL_BUCKETS - 1)
    return np.where(n < max_exact, n, large).astype(np.int32)


def _bucket_tiles(t):
    j = np.arange(t)[:, None]
    i = np.arange(t)[None, :]
    diag = np.where(j <= i, _t5_bucket(i - j), -1)
    sub = _t5_bucket(t + i - j)
    assert _t5_bucket(np.array([t + 1]))[0] == REL_BUCKETS - 1
    jm = np.arange(META_BLK)[:, None]
    meta_first = np.where(jm < N_META, _t5_bucket(N_META + i - jm), -1)
    meta_rest = np.where(jm < N_META, REL_BUCKETS - 1, -1) + 0 * i
    return (diag.astype(np.int32), np.stack([np.full_like(sub, -1), sub]).astype(np.int32),
            np.stack([meta_first, meta_rest]).astype(np.int32))


def _bias_kernel(table_ref, bd_ref, bs_ref, bm_ref, od_ref, os_ref, om_ref, *, buckets):
    h = pl.program_id(0)
    far = table_ref[REL_BUCKETS - 1, h]

    def lookup(bucket_ref, out_ref, known):
        rows, cols = known.shape
        for r0 in range(0, rows, V7X_LANES):
            for c0 in range(0, cols, V7X_LANES):
                patch = (slice(r0, r0 + V7X_LANES), slice(c0, c0 + V7X_LANES))
                present = np.unique(known[patch])
                acc = jnp.zeros((V7X_LANES, V7X_LANES), F32)
                if present.size > 1 or 0 <= present[0] < REL_BUCKETS - 1:
                    bucket = bucket_ref[patch]
                    for b in present:
                        if 0 <= b < REL_BUCKETS - 1:
                            acc = jnp.where(bucket == int(b), (table_ref[int(b), h] - far) * LOG2_E, acc)
                    if present[0] < 0:
                        acc = jnp.where(bucket < 0, NEG_INF, acc)
                elif present[0] < 0:
                    acc = jnp.full((V7X_LANES, V7X_LANES), NEG_INF, F32)
                out_ref[patch] = acc

    bd, bs, bm = buckets
    lookup(bd_ref, od_ref.at[0], bd)
    for i in range(2):
        lookup(bs_ref.at[i], os_ref.at[0, i], bs[i])
        lookup(bm_ref.at[i], om_ref.at[0, i], bm[i])


def _bias_tiles(table, t):
    bd, bs, bm = _bucket_tiles(t)
    return pl.pallas_call(
        functools.partial(_bias_kernel, buckets=(bd, bs, bm)),
        out_shape=[jax.ShapeDtypeStruct((HEADS, t, t), F32),
                   jax.ShapeDtypeStruct((HEADS, 2, t, t), F32),
                   jax.ShapeDtypeStruct((HEADS, 2, META_BLK, t), F32)],
        grid=(HEADS,),
        in_specs=[
            pl.BlockSpec(memory_space=pltpu.SMEM),
            pl.BlockSpec((t, t), lambda h: (0, 0)),
            pl.BlockSpec((2, t, t), lambda h: (0, 0, 0)),
            pl.BlockSpec((2, META_BLK, t), lambda h: (0, 0, 0)),
        ],
        out_specs=[pl.BlockSpec((1, t, t), lambda h: (h, 0, 0)),
                   pl.BlockSpec((1, 2, t, t), lambda h: (h, 0, 0, 0)),
                   pl.BlockSpec((1, 2, META_BLK, t), lambda h: (h, 0, 0, 0))],
        compiler_params=_params(("parallel",), 12 * t * t * 4),
        name="rel_bias_tiles",
    )(table, jnp.asarray(bd), jnp.asarray(bs), jnp.asarray(bm))


ATTN_QUERY_CHUNK = 256


ATTN_SCORE_BUFFERS = 4
ATTN_HEADS_PER_STEP = 2


class _Step(NamedTuple):
    q: Callable[[], Any]
    k: Callable[[], Any]
    vt: Callable[[], Any]
    bias: Optional[Callable[[], Any]]
    chunk: int


def _block_steps(n_chunks, q, k, vt, bias=None):
    return [_Step(functools.partial(q, c), functools.partial(k, c), functools.partial(vt, c),
                  functools.partial(bias, c) if bias is not None else None, c)
            for c in range(n_chunks)]


def _issue_scores(s_ref, step, slot):
    k = step.k()
    s_ref[slot, 0:k.shape[0], :] = _dot(k, step.q())


def _prefetch_scores(s_ref, steps):
    for i, step in enumerate(steps[:s_ref.shape[0] - 1]):
        _issue_scores(s_ref, step, i)


def _run_steps(s_ref, steps, following, m_ref, acc_ref):
    n_buf, _, qc = s_ref.shape
    ahead = n_buf - 1
    assert not following or len(steps) % n_buf == 0
    seq = list(steps) + list(following)[:ahead]
    for i, step in enumerate(steps):
        if i + ahead < len(seq):
            _issue_scores(s_ref, seq[i + ahead], (i + ahead) % n_buf)
        vt = step.vt()
        s = s_ref[i % n_buf, 0:vt.shape[1], :]
        if step.bias is not None:
            s = s + step.bias()
        sl = pl.ds(step.chunk * qc, qc)
        m_prev = m_ref[:, sl]
        m_new = jnp.maximum(m_prev, jnp.max(s, axis=0, keepdims=True))
        alpha = jnp.exp2(m_prev - m_new)
        p = jnp.exp2(s - m_new)
        acc_ref[:, sl] = alpha * acc_ref[:, sl] + _dot(vt, p.astype(BF16))
        m_ref[:, sl] = m_new


def _attend_init(m_ref, acc_ref):
    m_ref[...] = jnp.full(m_ref.shape, NEG_INF, F32)
    acc_ref[...] = jnp.zeros(acc_ref.shape, F32)


FAR_UNROLL = 4


def _walk_far_blocks(n_far, body):
    def loop_body(i, carry):
        body(i * FAR_UNROLL, FAR_UNROLL)
        return carry

    lax.fori_loop(0, n_far >> (FAR_UNROLL.bit_length() - 1), loop_body, 0)
    count = FAR_UNROLL // 2
    while count >= 1:
        first = n_far - (n_far & (2 * count - 1))

        @pl.when((n_far & count) != 0)
        def _(first=first, count=count):
            body(first, count)

        count //= 2


def _normalised(acc_ref):
    acc = acc_ref[...]
    return acc[0:HEAD_W] * (1.0 / acc[HEAD_W:HEAD_W + 1])


def _da_kernel(qt_ref, k_ref, vt_ref, km_ref, vtm_ref, bd_ref, bs_ref, bm_ref, lam_ref, subg_ref,
               o_ref, qq_ref, s_ref, m_ref, acc_ref, *, t):
    qi = pl.program_id(2)
    feat = lax.broadcasted_iota(jnp.int32, (HEAD_W, 1), 0)
    for hd in range(ATTN_HEADS_PER_STEP):
        qt = qt_ref[0, hd]
        zero = jnp.zeros_like(qt)
        qq_ref[:, pl.ds(2 * t * hd, t)] = jnp.where(feat < DA_HEAD_DIM, qt, zero)
        qq_ref[:, pl.ds(2 * t * hd + t, t)] = jnp.where(feat >= DA_HEAD_DIM, qt, zero)
    qc = s_ref.shape[2]
    chunks_per_map = t // qc
    chunks_per_head = 2 * chunks_per_map
    n_chunks = ATTN_HEADS_PER_STEP * chunks_per_head

    def head(c):
        return c // chunks_per_head

    def cols(c):
        return pl.ds((c % chunks_per_map) * qc, qc)

    def q_chunk(c):
        return qq_ref[:, pl.ds(c * qc, qc)]

    def block_steps(kb, bias=None, keys=t):
        rows = pl.ds(pl.multiple_of(kb * t, t), keys)
        return _block_steps(n_chunks, q_chunk, lambda c: k_ref[0, head(c), rows, :],
                            lambda c: vt_ref[0, head(c), :, rows], bias)

    state = (m_ref, acc_ref)
    n_far = jnp.maximum(qi - 1, 0)
    near_sel = jnp.minimum(qi, 1)
    prev_kb = jnp.maximum(qi - 1, 0)
    meta_steps = _block_steps(n_chunks, q_chunk, lambda c: km_ref[0, head(c)],
                              lambda c: vtm_ref[0, head(c)],
                              lambda c: bm_ref[head(c), near_sel, :, cols(c)])
    _attend_init(*state)
    _prefetch_scores(s_ref, meta_steps)
    _run_steps(s_ref, meta_steps, block_steps(jnp.where(qi >= 2, 0, qi)), *state)

    def far(first, count):
        steps = [st for u in range(count) for st in block_steps(first + u)]
        last = first + count - 1
        _run_steps(s_ref, steps, block_steps(jnp.where(last == n_far - 1, qi, last + 1)), *state)

    _walk_far_blocks(n_far, far)

    diag = []
    for c in range(n_chunks):
        keys = (c % chunks_per_map + 1) * qc
        diag.append(block_steps(qi, lambda c, keys=keys: bd_ref[head(c), 0:keys, cols(c)], keys)[c])
    prev = block_steps(prev_kb, lambda c: bs_ref[head(c), near_sel, :, cols(c)])
    _run_steps(s_ref, diag + prev, (), *state)

    ot = _normalised(acc_ref)
    lp = lam_ref[...]
    lam = (jnp.exp(jnp.sum(lp[0:1] * lp[1:2], axis=-1, keepdims=True))
           - jnp.exp(jnp.sum(lp[2:3] * lp[3:4], axis=-1, keepdims=True)) + LAMBDA_INIT)
    for hd in range(ATTN_HEADS_PER_STEP):
        o1 = ot[:, 2 * t * hd:2 * t * hd + t]
        o2 = ot[:, 2 * t * hd + t:2 * t * (hd + 1)]
        d = o1 - lam * o2
        ms = jnp.mean(d * d, axis=0, keepdims=True)
        y = d * lax.rsqrt(ms + RMS_EPS) * subg_ref[...] * (1.0 - LAMBDA_INIT)
        o_ref[0, :, hd * HEAD_W:(hd + 1) * HEAD_W] = y.T.astype(BF16)


def _attn_specs(t, s_len, q_w):
    hp = ATTN_HEADS_PER_STEP
    return [
        pl.BlockSpec((1, hp, q_w, t), lambda bi, h, qi: (bi, h, 0, qi)),
        pl.BlockSpec((1, hp, s_len, q_w), lambda bi, h, qi: (bi, h, 0, 0)),
        pl.BlockSpec((1, hp, VT_ROWS, s_len), lambda bi, h, qi: (bi, h, 0, 0)),
        pl.BlockSpec((1, hp, META_BLK, q_w), lambda bi, h, qi: (0, h, 0, 0)),
        pl.BlockSpec((1, hp, VT_ROWS, META_BLK), lambda bi, h, qi: (0, h, 0, 0)),
    ]


def _da_attention(qt, k, vt, k_meta, vt_meta, bias_d, bias_s, bias_m, lam_p, sub_g, *, t):
    b, _, s_len, _ = k.shape
    nq = s_len // t
    hp = ATTN_HEADS_PER_STEP
    qc = min(ATTN_QUERY_CHUNK, t)
    vmem = hp * (2 * (s_len + META_BLK) * (HEAD_W + VT_ROWS) * 2 + 2 * 3 * t * t * 4
                 + 2 * 2 * t * META_BLK * 4
                 + 2 * 2 * t * HEAD_W * 2 + 2 * t * HEAD_W * 2 + (VT_ROWS + 8) * 2 * t * 4
                 ) + (ATTN_SCORE_BUFFERS + 4) * t * qc * 4
    return pl.pallas_call(
        functools.partial(_da_kernel, t=t),
        out_shape=jax.ShapeDtypeStruct((b, s_len, DA_W), BF16),
        grid=(b, HEADS // hp, nq),
        in_specs=_attn_specs(t, s_len, HEAD_W) + [
            pl.BlockSpec((hp, t, t), lambda bi, h, qi: (h, 0, 0)),
            pl.BlockSpec((hp, 2, t, t), lambda bi, h, qi: (h, 0, 0, 0)),
            pl.BlockSpec((hp, 2, META_BLK, t), lambda bi, h, qi: (h, 0, 0, 0)),
            _resident((4, DA_HEAD_DIM), lambda bi, h, qi: (0, 0)),
            _resident((HEAD_W, 1), lambda bi, h, qi: (0, 0)),
        ],
        out_specs=pl.BlockSpec((1, t, hp * HEAD_W), lambda bi, h, qi: (bi, qi, h)),
        scratch_shapes=[pltpu.VMEM((HEAD_W, hp * 2 * t), BF16),
                        pltpu.VMEM((ATTN_SCORE_BUFFERS, t, qc), F32),
                        pltpu.VMEM((1, hp * 2 * t), F32), pltpu.VMEM((VT_ROWS, hp * 2 * t), F32)],
        compiler_params=_params(("parallel", "parallel", "arbitrary"), vmem),
        name="diff_attention",
    )(qt, k, vt, k_meta, vt_meta, bias_d, bias_s, bias_m, lam_p, sub_g)


def _mla_kernel(qt_ref, k_ref, vt_ref, km_ref, vtm_ref, o_ref, s_ref, m_ref, acc_ref, *, tq, tk):
    qi = pl.program_id(2)
    qc = s_ref.shape[2]
    diag_blocks = tq // tk
    chunks_per_head = tq // qc
    n_chunks = ATTN_HEADS_PER_STEP * chunks_per_head

    def head(c):
        return c // chunks_per_head

    def q_chunk(c):
        return qt_ref[0, head(c), :, pl.ds((c % chunks_per_head) * qc, qc)]

    def block_steps(kb, bias=None, keys=tk):
        rows = pl.ds(pl.multiple_of(kb * tk, tk), keys)
        return _block_steps(n_chunks, q_chunk, lambda c: k_ref[0, head(c), rows, :],
                            lambda c: vt_ref[0, head(c), :, rows], bias)

    state = (m_ref, acc_ref)

    def meta_mask(c):
        key = lax.broadcasted_iota(jnp.int32, (META_BLK, qc), 0)
        return jnp.where(key < N_META, 0.0, NEG_INF)

    meta_steps = _block_steps(n_chunks, q_chunk, lambda c: km_ref[0, head(c)],
                              lambda c: vtm_ref[0, head(c)], meta_mask)
    _attend_init(*state)
    _prefetch_scores(s_ref, meta_steps)
    _run_steps(s_ref, meta_steps, block_steps(0), *state)

    def far(first, count):
        steps = [st for u in range(count) for st in block_steps(first + u)]
        _run_steps(s_ref, steps, block_steps(first + count), *state)

    first_diag = qi * diag_blocks
    _walk_far_blocks(first_diag, far)

    diag = []
    for d in range(diag_blocks):
        for c in range(n_chunks):
            cq = (c % chunks_per_head) * qc
            keys = min(tk, cq + qc - d * tk)
            if keys <= 0:
                continue

            def causal(c, d=d, keys=keys, cq=cq):
                key = lax.broadcasted_iota(jnp.int32, (keys, qc), 0) + d * tk
                query = lax.broadcasted_iota(jnp.int32, (keys, qc), 1) + cq
                return jnp.where(key <= query, 0.0, NEG_INF)

            diag.append(block_steps(first_diag + d, causal, keys)[c])
    _run_steps(s_ref, diag, (), *state)

    ot = _normalised(acc_ref)
    for hd in range(ATTN_HEADS_PER_STEP):
        o_ref[0, :, hd * HEAD_W:(hd + 1) * HEAD_W] = ot[:, hd * tq:(hd + 1) * tq].T.astype(BF16)


def _mla_attention(qt, k, vt, k_meta, vt_meta, *, tq, tk):
    b, _, s_len, _ = k.shape
    nq = s_len // tq
    hp = ATTN_HEADS_PER_STEP
    qc = min(ATTN_QUERY_CHUNK, tq)
    vmem = hp * (2 * (s_len + META_BLK) * (MLA_QK_W + VT_ROWS) * 2 + 2 * tq * MLA_QK_W * 2
                 + 2 * tq * HEAD_W * 2 + (VT_ROWS + 8) * tq * 4 + tq * HEAD_W * 4
                 ) + (ATTN_SCORE_BUFFERS + 4) * tk * qc * 4
    return pl.pallas_call(
        functools.partial(_mla_kernel, tq=tq, tk=tk),
        out_shape=jax.ShapeDtypeStruct((b, s_len, DA_W), BF16),
        grid=(b, HEADS // hp, nq),
        in_specs=_attn_specs(tq, s_len, MLA_QK_W),
        out_specs=pl.BlockSpec((1, tq, hp * HEAD_W), lambda bi, h, qi: (bi, qi, h)),
        scratch_shapes=[pltpu.VMEM((ATTN_SCORE_BUFFERS, tk, qc), F32),
                        pltpu.VMEM((1, hp * tq), F32), pltpu.VMEM((VT_ROWS, hp * tq), F32)],
        compiler_params=_params(("parallel", "parallel", "arbitrary"), vmem),
        name="mla_attention",
    )(qt, k, vt, k_meta, vt_meta)


def _merge_kernel(h_ref, pre_ref, post_ref, yda_ref, ymla_ref, wg0_ref, wg1_ref, b0_ref, b1_ref,
                  wbd_ref, wbm_ref, wo_ref, o_ref, hn_ref):
    j = pl.program_id(1)
    last = pl.num_programs(1) - 1

    def partial_out(hn):
        g0 = jax.nn.sigmoid(_dot(hn, wg0_ref[...]) + b0_ref[...])
        g1 = jax.nn.sigmoid(_dot(hn, wg1_ref[...]) + b1_ref[...])
        merged = g0 * _dot(yda_ref[0], wbd_ref[...]) + g1 * _dot(ymla_ref[0], wbm_ref[...])
        return _dot(merged.astype(BF16), wo_ref[...])

    @pl.when(j == 0)
    def _():
        hn = _rms(h_ref[0], pre_ref[...]).astype(BF16)
        hn_ref[...] = hn
        o_ref[0] = partial_out(hn)

    @pl.when(jnp.logical_and(j > 0, j < last))
    def _():
        o_ref[0] += partial_out(hn_ref[...])

    @pl.when(j == last)
    def _():
        m = o_ref[0] + partial_out(hn_ref[...])
        o_ref[0] = h_ref[0] + _rms(m, post_ref[...])


def _merge(h1, pre_g, post_g, y_da, y_mla, w_gate, b_gate, wb_da, wb_mla, w_out):
    b, s_len, d = h1.shape
    yw = y_da.shape[-1]
    tm = _row_tile(s_len, 512)
    tn = _col_tile(d, 512)
    nt = s_len // tm
    nj = d // tn
    assert nj >= 2, "the kernel treats the first and the last column tile as different steps"
    vmem = (2 * 2 * tm * d * 4 + tm * d * 2 + 2 * 2 * tm * yw * 2
            + 2 * (3 * d * tn + 2 * yw * tn) * 2 + 6 * tm * tn * 4 + tm * d * 4)
    return pl.pallas_call(
        _merge_kernel,
        out_shape=jax.ShapeDtypeStruct((b, s_len, d), F32),
        grid=(b * nt, nj),
        in_specs=[
            pl.BlockSpec((1, tm, d), lambda i, j: (i // nt, i % nt, 0)),
            _resident((1, d), lambda i, j: (0, 0)),
            _resident((1, d), lambda i, j: (0, 0)),
            pl.BlockSpec((1, tm, yw), lambda i, j: (i // nt, i % nt, 0)),
            pl.BlockSpec((1, tm, yw), lambda i, j: (i // nt, i % nt, 0)),
            pl.BlockSpec((d, tn), lambda i, j: (0, j)),
            pl.BlockSpec((d, tn), lambda i, j: (0, nj + j)),
            pl.BlockSpec((1, tn), lambda i, j: (0, j)),
            pl.BlockSpec((1, tn), lambda i, j: (0, nj + j)),
            pl.BlockSpec((yw, tn), lambda i, j: (0, j)),
            pl.BlockSpec((yw, tn), lambda i, j: (0, j)),
            pl.BlockSpec((tn, d), lambda i, j: (j, 0)),
        ],
        out_specs=pl.BlockSpec((1, tm, d), lambda i, j: (i // nt, i % nt, 0)),
        scratch_shapes=[pltpu.VMEM((tm, d), BF16)],
        compiler_params=_params(("parallel", "arbitrary"), vmem),
        name="gated_merge_out_proj",
    )(h1, pre_g, post_g, y_da, y_mla, w_gate, w_gate, b_gate, b_gate, wb_da, wb_mla, w_out)


def _rope_tables(pos):
    half = MLA_ROPE // 2
    pos = pos.astype(F32)
    inv = ROPE_THETA ** (-jnp.arange(half, dtype=F32) * 2.0 / MLA_ROPE)
    ang = pos[:, None] * inv[None, :]
    cos, sin = jnp.cos(ang), jnp.sin(ang)
    pad = jnp.zeros((pos.shape[0], HEAD_W - MLA_ROPE), F32)
    return (jnp.concatenate([cos, cos, pad], axis=-1),
            jnp.concatenate([-sin, sin, pad], axis=-1))


def _swap_halves(w):
    half = w.shape[-1] // 2
    return jnp.concatenate([w[..., half:], w[..., :half]], axis=-1)


def _pad_lanes(w, width):
    return jnp.concatenate([w, jnp.zeros(w.shape[:-1] + (width - w.shape[-1],), w.dtype)], axis=-1)


def kernel(x, meta_tokens, rel_bias_table, ffn1_pre_g, ffn1_post_g, ffn1_w_gate, ffn1_w_up, ffn1_w_down, mix_pre_g, mix_post_g, w_in, b_gate, da_lambda_q1, da_lambda_k1, da_lambda_q2, da_lambda_k2, da_sub_g, mla_q_norm_g, mla_kv_norm_g, mla_w_uq, mla_w_ukv, w_branch_da, w_branch_mla, w_out, ffn2_pre_g, ffn2_post_g, ffn2_w_gate, ffn2_w_up, ffn2_w_down):
    b, s_len, d = x.shape
    assert ffn1_pre_g.shape[0] == 1, "single-layer trunk"
    t = _col_tile(s_len, 512)
    row = lambda g: g.reshape(1, -1).astype(F32)

    meta_in = jnp.concatenate([meta_tokens.astype(x.dtype),
                               jnp.zeros((META_BLK - N_META, d), x.dtype)], axis=0)
    ffn1_w = (row(ffn1_pre_g), row(ffn1_post_g), ffn1_w_gate[0].astype(BF16),
              ffn1_w_up[0].astype(BF16), ffn1_w_down[0].astype(BF16))
    h1 = _ffn(x.reshape(b * s_len, d), *ffn1_w, name="ffn1").reshape(b, s_len, d)
    h1_meta = _ffn(meta_in, *ffn1_w, name="ffn1_meta").reshape(1, META_BLK, d)

    w = w_in[0]
    o_k, o_v, o_cq = DA_W, 2 * DA_W, 3 * DA_W
    o_kr = o_cq + MLA_Q_RANK + MLA_KV_RANK
    o_gate = o_kr + MLA_ROPE
    da_w = (row(mix_pre_g), w[:, :o_k].T.astype(BF16), w[:, o_k:o_v].astype(BF16),
            w[:, o_v:o_cq].T.astype(BF16))
    qt_da, k_da, vt_da = _da_proj(h1, *da_w, name="da_in_proj")
    _, k_da_meta, vt_da_meta = _da_proj(h1_meta, *da_w, name="da_in_proj_meta")

    kr = w[:, o_kr:o_gate]
    w_c = jnp.concatenate([w[:, o_cq:o_kr], _pad_lanes(kr, HEAD_W),
                           _pad_lanes(_swap_halves(kr), HEAD_W)], axis=-1).astype(BF16)
    uq = mla_w_uq[0].reshape(MLA_Q_RANK, HEADS, MLA_NOPE + MLA_ROPE)
    uq_rope = uq[..., MLA_NOPE:]
    w_uqt = jnp.concatenate([uq[..., :MLA_NOPE], _pad_lanes(uq_rope, HEAD_W),
                             _pad_lanes(_swap_halves(uq_rope), HEAD_W)],
                            axis=-1).reshape(MLA_Q_RANK, HEADS * _UQ_HEAD_W).T.astype(BF16)
    ukv = mla_w_ukv[0].reshape(MLA_KV_RANK, HEADS, 2 * HEAD_W)
    w_uk = ukv[..., :HEAD_W].reshape(MLA_KV_RANK, DA_W).astype(BF16)
    w_uvt = ukv[..., HEAD_W:].reshape(MLA_KV_RANK, DA_W).T.astype(BF16)
    mla_w = (w_c, row(mla_q_norm_g), row(mla_kv_norm_g), w_uqt, w_uk, w_uvt)

    def mla_proj(h, pos, name):
        cos, sin = _rope_tables(pos)
        return _mla_proj(h, row(mix_pre_g), cos, sin, cos.T, sin.T, *mla_w, name=name)

    qt_mla, k_mla, vt_mla = mla_proj(h1, N_META + jnp.arange(s_len), "mla_in_proj")
    _, k_mla_meta, vt_mla_meta = mla_proj(h1_meta, jnp.arange(META_BLK), "mla_in_proj_meta")

    bias_d, bias_s, bias_m = _bias_tiles(rel_bias_table.astype(F32), t)
    lam_p = jnp.concatenate([da_lambda_q1, da_lambda_k1, da_lambda_q2, da_lambda_k2], axis=0).astype(F32)
    y_da = _da_attention(qt_da, k_da, vt_da, k_da_meta, vt_da_meta, bias_d, bias_s, bias_m, lam_p,
                         da_sub_g.reshape(HEAD_W, 1).astype(F32), t=t)
    y_mla = _mla_attention(qt_mla, k_mla, vt_mla, k_mla_meta, vt_mla_meta,
                           tq=_col_tile(s_len, 2 * t), tk=t)

    h2 = _merge(h1, row(mix_pre_g), row(mix_post_g), y_da, y_mla,
                w[:, o_gate:].astype(BF16), row(b_gate),
                w_branch_da[0].astype(BF16), w_branch_mla[0].astype(BF16), w_out[0].astype(BF16))

    out = _ffn(h2.reshape(b * s_len, d), row(ffn2_pre_g), row(ffn2_post_g),
               ffn2_w_gate[0].astype(BF16), ffn2_w_up[0].astype(BF16), ffn2_w_down[0].astype(BF16),
               name="ffn2")
    return out.reshape(b, s_len, d)
```

```python
import functools
import math
from typing import Any, Callable, NamedTuple, Optional

import numpy as np
import jax
import jax.numpy as jnp
from jax import lax
from jax.experimental import pallas as pl
from jax.experimental.pallas import tpu as pltpu

F32 = jnp.float32
BF16 = jnp.bfloat16

N_META = 16
RMS_EPS = 1e-6
NEG_INF = -1e30
HEADS = 8
HEAD_W = 128
DA_HEAD_DIM = 64
DA_W = HEADS * HEAD_W
MLA_Q_RANK = 768
MLA_KV_RANK = 512
MLA_NOPE = 128
MLA_ROPE = 64
MLA_QK_W = 256
ROPE_THETA = 10000.0
REL_BUCKETS = 32
REL_MAX_DIST = 128
LAMBDA_INIT = 0.8 - 0.6 * math.exp(-0.3 * 0)
LOG2_E = math.log2(math.e)

V7X_LANES = 128
V7X_BF16_SUBLANES = 16
V7X_VMEM_BYTES = 64 * 1024 * 1024
VMEM_CAP_BYTES = V7X_VMEM_BYTES - 6 * 1024 * 1024

META_BLK = V7X_LANES


def _row_tile(rows, max_tile, multiple=V7X_BF16_SUBLANES):
    best = None
    for t in range(multiple, max_tile + 1, multiple):
        if rows % t == 0:
            best = t
    assert best is not None, (rows, max_tile)
    return best


def _col_tile(cols, max_tile):
    return _row_tile(cols, max_tile, V7X_LANES)


def _params(semantics, vmem_bytes):
    return pltpu.CompilerParams(
        dimension_semantics=semantics,
        vmem_limit_bytes=int(min(max(vmem_bytes, 16 * 1024 * 1024), VMEM_CAP_BYTES)))


def _resident(block_shape, index_map):
    return pl.BlockSpec(block_shape, index_map, pipeline_mode=pl.Buffered(1))


def _rms(x, g):
    ms = jnp.mean(x * x, axis=-1, keepdims=True)
    return x * lax.rsqrt(ms + RMS_EPS) * g


def _dot(a, b):
    return jnp.dot(a, b, preferred_element_type=F32)


def _dot_nt(a, b):
    return lax.dot_general(a, b, (((1,), (1,)), ((), ())), preferred_element_type=F32)


class _CastJob(NamedTuple):
    src: Any
    chunk_rows: int


def _cast_job_specs(jobs, n_steps, flat_step):
    specs, shapes, start = [], [], 0
    for job in jobs:
        rows, cols = job.src.shape
        assert rows % job.chunk_rows == 0 and job.chunk_rows % V7X_BF16_SUBLANES == 0
        n = rows // job.chunk_rows

        def index_map(*ids, start=start, n=n):
            return (jnp.clip(flat_step(*ids) - start, 0, n - 1), 0)

        specs.append(pl.BlockSpec((job.chunk_rows, cols), index_map))
        shapes.append(jax.ShapeDtypeStruct((rows, cols), BF16))
        start += n
    assert start <= n_steps, (start, n_steps)
    return specs, shapes


def _plan_cast_jobs(weights, cols, n_steps):
    views = [w.reshape(-1, cols) for w in weights]
    chunk = 4 * V7X_BF16_SUBLANES
    while True:
        jobs = [_CastJob(v, _row_tile(v.shape[0], chunk)) for v in views]
        if sum(j.src.shape[0] // j.chunk_rows for j in jobs) <= n_steps:
            return jobs
        chunk += V7X_BF16_SUBLANES
        assert chunk <= max(v.shape[0] for v in views), "not enough grid steps for the cast jobs"


def _with_cast_jobs(body, n_in, n_out, n_jobs):
    def kernel(*refs):
        ins, refs = refs[:n_in], refs[n_in:]
        job_in, refs = refs[:n_jobs], refs[n_jobs:]
        outs, refs = refs[:n_out], refs[n_out:]
        job_out, scratch = refs[:n_jobs], refs[n_jobs:]
        for src, dst in zip(job_in, job_out):
            dst[...] = src[...].astype(BF16)
        body(*ins, *outs, *scratch)
    return kernel


def _ffn_kernel(x_ref, pre_ref, post_ref, wg_ref, wu_ref, wd_ref, o_ref, *rest):
    *bf16_out, xn_ref = rest
    k = pl.program_id(1)
    last = pl.num_programs(1) - 1

    def partial_out(xn):
        wg, wu, wd = wg_ref[...], wu_ref[...], wd_ref[...]
        if bf16_out:
            wg, wu, wd = wg.astype(BF16), wu.astype(BF16), wd.astype(BF16)
            for ref, tile in zip(bf16_out, (wg, wu, wd)):
                ref[...] = tile
        g = _dot(xn, wg)
        u = _dot(xn, wu)
        a = (g * jax.nn.sigmoid(g) * u).astype(BF16)
        return _dot(a, wd)

    @pl.when(k == 0)
    def _():
        xn = _rms(x_ref[...], pre_ref[...]).astype(BF16)
        xn_ref[...] = xn
        o_ref[...] = partial_out(xn)

    @pl.when(jnp.logical_and(k > 0, k < last))
    def _():
        o_ref[...] += partial_out(xn_ref[...])

    @pl.when(k == last)
    def _():
        f = o_ref[...] + partial_out(xn_ref[...])
        o_ref[...] = x_ref[...] + 0.5 * _rms(f, post_ref[...])


def _ffn(x, pre_g, post_g, wg, wu, wd, *, name, cast_weights=()):
    rows, d = x.shape
    f = wg.shape[1]
    tm = _row_tile(rows, 896)
    tf = _col_tile(f, 512)
    n_i, n_k = rows // tm, f // tf
    assert n_k >= 2, "the kernel treats the first and the last d_ff tile as different steps"
    cast_jobs = _plan_cast_jobs(cast_weights, d, n_i * n_k) if cast_weights else ()
    emit = wg.dtype == F32
    assert not emit or n_i == 1
    w_bytes = 4 if emit else 2
    weight_specs = [pl.BlockSpec((d, tf), lambda i, k: (0, k)),
                    pl.BlockSpec((d, tf), lambda i, k: (0, k)),
                    pl.BlockSpec((tf, d), lambda i, k: (k, 0))]
    job_specs, job_shapes = _cast_job_specs(cast_jobs, n_i * n_k, lambda i, k: i * n_k + k)
    vmem = (2 * 2 * tm * d * 4
            + tm * d * 2
            + 2 * 3 * d * tf * (w_bytes + (2 if emit else 0))
            + 4 * tm * tf * 4
            + tm * d * 4
            + sum(2 * j.chunk_rows * j.src.shape[1] * 6 for j in cast_jobs))
    n_out = 4 if emit else 1
    return pl.pallas_call(
        _with_cast_jobs(_ffn_kernel, 6, n_out, len(cast_jobs)),
        out_shape=[jax.ShapeDtypeStruct((rows, d), F32)]
        + ([jax.ShapeDtypeStruct(w.shape, BF16) for w in (wg, wu, wd)] if emit else []) + job_shapes,
        grid=(n_i, n_k),
        in_specs=[
            pl.BlockSpec((tm, d), lambda i, k: (i, 0)),
            _resident((1, d), lambda i, k: (0, 0)),
            _resident((1, d), lambda i, k: (0, 0)),
        ] + weight_specs + job_specs,
        out_specs=[pl.BlockSpec((tm, d), lambda i, k: (i, 0))]
        + (weight_specs if emit else []) + job_specs,
        scratch_shapes=[pltpu.VMEM((tm, d), BF16)],
        compiler_params=_params(("arbitrary" if cast_jobs else "parallel", "arbitrary"), vmem),
        name=name,
    )(x, pre_g, post_g, wg, wu, wd, *[j.src for j in cast_jobs])


VT_ROWS = HEAD_W + V7X_BF16_SUBLANES


def _store_vt(vt_ref, vt):
    tm = vt.shape[1]
    vt_ref[0, :, 0:HEAD_W, :] = vt.reshape(HEADS, HEAD_W, tm).astype(BF16)
    extra = lax.broadcasted_iota(jnp.int32, (HEADS, VT_ROWS - HEAD_W, tm), 1)
    vt_ref[0, :, HEAD_W:VT_ROWS, :] = jnp.where(extra == 0, 1.0, 0.0).astype(BF16)


def _da_proj_kernel(h_ref, g_ref, wqt_ref, wk_ref, wvt_ref, qt_ref, k_ref, vt_ref):
    hn = _rms(h_ref[...], g_ref[...]).astype(BF16)
    tm = hn.shape[0]
    qt = _dot_nt(wqt_ref[...], hn) * (DA_HEAD_DIM ** -0.5 * LOG2_E)
    qt_ref[0] = qt.reshape(HEADS, HEAD_W, tm).astype(BF16)
    k = _dot(hn, wk_ref[...])
    for h in range(HEADS):
        k_ref[0, h] = k[:, h * HEAD_W:(h + 1) * HEAD_W].astype(BF16)
    _store_vt(vt_ref, _dot_nt(wvt_ref[...], hn))


_C_CKV = MLA_Q_RANK
_C_KR1 = _C_CKV + MLA_KV_RANK
_C_KR2 = _C_KR1 + HEAD_W
_C_END = _C_KR2 + HEAD_W
_UQ_HEAD_W = 3 * HEAD_W


def _mla_proj_kernel(h_ref, g_ref, cos_ref, sin_ref, cost_ref, sint_ref, wc_ref, gq_ref, gkv_ref,
                     wuqt_ref, wuk_ref, wuvt_ref, qt_ref, k_ref, vt_ref):
    hn = _rms(h_ref[...], g_ref[...]).astype(BF16)
    tm = hn.shape[0]
    scale = (MLA_NOPE + MLA_ROPE) ** -0.5 * LOG2_E

    c_all = _dot(hn, wc_ref[...])

    def proj(lo, hi):
        return c_all[:, lo:hi]

    cq = _rms(proj(0, _C_CKV), gq_ref[...]).astype(BF16)
    qt = _dot_nt(wuqt_ref[...], cq)
    cos_t = cost_ref[...]
    sin_t = sint_ref[...]
    for h in range(HEADS):
        base = h * _UQ_HEAD_W
        nope = qt[base:base + HEAD_W]
        r1 = qt[base + HEAD_W:base + 2 * HEAD_W]
        r2 = qt[base + 2 * HEAD_W:base + 3 * HEAD_W]
        qt_ref[0, h, 0:HEAD_W, :] = (nope * scale).astype(BF16)
        qt_ref[0, h, HEAD_W:MLA_QK_W, :] = ((r1 * cos_t + r2 * sin_t) * scale).astype(BF16)

    ckv = _rms(proj(_C_CKV, _C_KR1), gkv_ref[...]).astype(BF16)
    k_nope = _dot(ckv, wuk_ref[...])
    k_rope = (proj(_C_KR1, _C_KR2) * cos_ref[...] + proj(_C_KR2, _C_END) * sin_ref[...]).astype(BF16)
    for h in range(HEADS):
        k_ref[0, h, :, 0:HEAD_W] = k_nope[:, h * HEAD_W:(h + 1) * HEAD_W].astype(BF16)
        k_ref[0, h, :, HEAD_W:MLA_QK_W] = k_rope
    _store_vt(vt_ref, _dot_nt(wuvt_ref[...], ckv))


def _proj_specs(b, lp, tm, q_w):
    nt = lp // tm
    rows = lambda w: pl.BlockSpec((1, HEADS, tm, w), lambda i: (i // nt, 0, i % nt, 0))
    cols = lambda w: pl.BlockSpec((1, HEADS, w, tm), lambda i: (i // nt, 0, 0, i % nt))
    shapes = [jax.ShapeDtypeStruct((b, HEADS, q_w, lp), BF16),
              jax.ShapeDtypeStruct((b, HEADS, lp, q_w), BF16),
              jax.ShapeDtypeStruct((b, HEADS, VT_ROWS, lp), BF16)]
    return shapes, [cols(q_w), rows(q_w), cols(VT_ROWS)]


def _da_proj(h1, g, wqt, wk, wvt, *, name):
    b, lp, d = h1.shape
    tm = _col_tile(lp, 704)
    nt = lp // tm
    shapes, out_specs = _proj_specs(b, lp, tm, HEAD_W)
    vmem = (2 * tm * d * 4 + tm * d * 2 + 3 * DA_W * d * 2
            + 2 * 3 * tm * DA_W * 2 + 3 * tm * DA_W * 4)
    return pl.pallas_call(
        _da_proj_kernel,
        out_shape=shapes,
        grid=(b * nt,),
        in_specs=[
            pl.BlockSpec((None, tm, d), lambda i: (i // nt, i % nt, 0)),
            _resident((1, d), lambda i: (0, 0)),
            _resident(wqt.shape, lambda i: (0, 0)),
            _resident(wk.shape, lambda i: (0, 0)),
            _resident(wvt.shape, lambda i: (0, 0)),
        ],
        out_specs=out_specs,
        compiler_params=_params(("parallel",), vmem),
        name=name,
    )(h1, g, wqt, wk, wvt)


def _mla_proj(h1, g, cos, sin, cos_t, sin_t, wc, gq, gkv, wuqt, wuk, wuvt, *, name):
    b, lp, d = h1.shape
    tm = _col_tile(lp, 704)
    nt = lp // tm
    shapes, out_specs = _proj_specs(b, lp, tm, MLA_QK_W)
    vmem = (2 * tm * d * 4 + tm * d * 2
            + (d * _C_END + HEADS * _UQ_HEAD_W * MLA_Q_RANK + 2 * DA_W * MLA_KV_RANK) * 2
            + 2 * tm * HEADS * (2 * MLA_QK_W + HEAD_W) * 2
            + 2 * tm * HEADS * _UQ_HEAD_W * 4)
    return pl.pallas_call(
        _mla_proj_kernel,
        out_shape=shapes,
        grid=(b * nt,),
        in_specs=[
            pl.BlockSpec((None, tm, d), lambda i: (i // nt, i % nt, 0)),
            _resident((1, d), lambda i: (0, 0)),
            pl.BlockSpec((tm, HEAD_W), lambda i: (i % nt, 0)),
            pl.BlockSpec((tm, HEAD_W), lambda i: (i % nt, 0)),
            pl.BlockSpec((HEAD_W, tm), lambda i: (0, i % nt)),
            pl.BlockSpec((HEAD_W, tm), lambda i: (0, i % nt)),
            _resident(wc.shape, lambda i: (0, 0)),
            _resident((1, MLA_Q_RANK), lambda i: (0, 0)),
            _resident((1, MLA_KV_RANK), lambda i: (0, 0)),
            _resident(wuqt.shape, lambda i: (0, 0)),
            _resident(wuk.shape, lambda i: (0, 0)),
            _resident(wuvt.shape, lambda i: (0, 0)),
        ],
        out_specs=out_specs,
        compiler_params=_params(("parallel",), vmem),
        name=name,
    )(h1, g, cos, sin, cos_t, sin_t, wc, gq, gkv, wuqt, wuk, wuvt)


def _t5_bucket(rel):
    n = np.maximum(rel, 0)
    max_exact = REL_BUCKETS // 2
    n_f = np.maximum(n, 1).astype(np.float64)
    large = max_exact + (np.log(n_f / max_exact) / math.log(REL_MAX_DIST / max_exact)
                         * (REL_BUCKETS - max_exact)).astype(np.int32)
    large = np.minimum(large, REL_BUCKETS - 1)
    return np.where(n < max_exact, n, large).astype(np.int32)


def _bucket_tiles(t):
    j = np.arange(t)[:, None]
    i = np.arange(t)[None, :]
    diag = np.where(j <= i, _t5_bucket(i - j), -1)
    sub = _t5_bucket(t + i - j)
    assert _t5_bucket(np.array([t + 1]))[0] == REL_BUCKETS - 1
    jm = np.arange(META_BLK)[:, None]
    meta_first = np.where(jm < N_META, _t5_bucket(N_META + i - jm), -1)
    meta_rest = np.where(jm < N_META, REL_BUCKETS - 1, -1) + 0 * i
    return (diag.astype(np.int32), np.stack([np.full_like(sub, -1), sub]).astype(np.int32),
            np.stack([meta_first, meta_rest]).astype(np.int32))


def _bias_kernel(table_ref, bd_ref, bs_ref, bm_ref, od_ref, os_ref, om_ref, *, buckets):
    h = pl.program_id(0)
    far = table_ref[REL_BUCKETS - 1, h]

    def lookup(bucket_ref, out_ref, known):
        rows, cols = known.shape
        for r0 in range(0, rows, V7X_LANES):
            for c0 in range(0, cols, V7X_LANES):
                patch = (slice(r0, r0 + V7X_LANES), slice(c0, c0 + V7X_LANES))
                present = np.unique(known[patch])
                acc = jnp.zeros((V7X_LANES, V7X_LANES), F32)
                if present.size > 1 or 0 <= present[0] < REL_BUCKETS - 1:
                    bucket = bucket_ref[patch]
                    for b in present:
                        if 0 <= b < REL_BUCKETS - 1:
                            acc = jnp.where(bucket == int(b), (table_ref[int(b), h] - far) * LOG2_E, acc)
                    if present[0] < 0:
                        acc = jnp.where(bucket < 0, NEG_INF, acc)
                elif present[0] < 0:
                    acc = jnp.full((V7X_LANES, V7X_LANES), NEG_INF, F32)
                out_ref[patch] = acc

    bd, bs, bm = buckets
    lookup(bd_ref, od_ref.at[0], bd)
    for i in range(2):
        lookup(bs_ref.at[i], os_ref.at[0, i], bs[i])
        lookup(bm_ref.at[i], om_ref.at[0, i], bm[i])


def _bias_tiles(table, t):
    bd, bs, bm = _bucket_tiles(t)
    return pl.pallas_call(
        functools.partial(_bias_kernel, buckets=(bd, bs, bm)),
        out_shape=[jax.ShapeDtypeStruct((HEADS, t, t), F32),
                   jax.ShapeDtypeStruct((HEADS, 2, t, t), F32),
                   jax.ShapeDtypeStruct((HEADS, 2, META_BLK, t), F32)],
        grid=(HEADS,),
        in_specs=[
            pl.BlockSpec(memory_space=pltpu.SMEM),
            pl.BlockSpec((t, t), lambda h: (0, 0)),
            pl.BlockSpec((2, t, t), lambda h: (0, 0, 0)),
            pl.BlockSpec((2, META_BLK, t), lambda h: (0, 0, 0)),
        ],
        out_specs=[pl.BlockSpec((1, t, t), lambda h: (h, 0, 0)),
                   pl.BlockSpec((1, 2, t, t), lambda h: (h, 0, 0, 0)),
                   pl.BlockSpec((1, 2, META_BLK, t), lambda h: (h, 0, 0, 0))],
        compiler_params=_params(("parallel",), 12 * t * t * 4),
        name="rel_bias_tiles",
    )(table, jnp.asarray(bd), jnp.asarray(bs), jnp.asarray(bm))


ATTN_QUERY_CHUNK = 256


ATTN_SCORE_BUFFERS = 4
ATTN_HEADS_PER_STEP = 2


class _Step(NamedTuple):
    q: Callable[[], Any]
    k: Callable[[], Any]
    vt: Callable[[], Any]
    bias: Optional[Callable[[], Any]]
    chunk: int


def _block_steps(n_chunks, q, k, vt, bias=None):
    return [_Step(functools.partial(q, c), functools.partial(k, c), functools.partial(vt, c),
                  functools.partial(bias, c) if bias is not None else None, c)
            for c in range(n_chunks)]


def _issue_scores(s_ref, step, slot):
    k = step.k()
    s_ref[slot, 0:k.shape[0], :] = _dot(k, step.q())


def _prefetch_scores(s_ref, steps):
    for i, step in enumerate(steps[:s_ref.shape[0] - 1]):
        _issue_scores(s_ref, step, i)


def _run_steps(s_ref, steps, following, m_ref, acc_ref):
    n_buf, _, qc = s_ref.shape
    ahead = n_buf - 1
    assert not following or len(steps) % n_buf == 0
    seq = list(steps) + list(following)[:ahead]
    for i, step in enumerate(steps):
        if i + ahead < len(seq):
            _issue_scores(s_ref, seq[i + ahead], (i + ahead) % n_buf)
        vt = step.vt()
        s = s_ref[i % n_buf, 0:vt.shape[1], :]
        if step.bias is not None:
            s = s + step.bias()
        sl = pl.ds(step.chunk * qc, qc)
        m_prev = m_ref[:, sl]
        m_new = jnp.maximum(m_prev, jnp.max(s, axis=0, keepdims=True))
        alpha = jnp.exp2(m_prev - m_new)
        p = jnp.exp2(s - m_new)
        acc_ref[:, sl] = alpha * acc_ref[:, sl] + _dot(vt, p.astype(BF16))
        m_ref[:, sl] = m_new


def _attend_init(m_ref, acc_ref):
    m_ref[...] = jnp.full(m_ref.shape, NEG_INF, F32)
    acc_ref[...] = jnp.zeros(acc_ref.shape, F32)


FAR_UNROLL = 4


def _walk_far_blocks(n_far, body):
    def loop_body(i, carry):
        body(i * FAR_UNROLL, FAR_UNROLL)
        return carry

    lax.fori_loop(0, n_far >> (FAR_UNROLL.bit_length() - 1), loop_body, 0)
    count = FAR_UNROLL // 2
    while count >= 1:
        first = n_far - (n_far & (2 * count - 1))

        @pl.when((n_far & count) != 0)
        def _(first=first, count=count):
            body(first, count)

        count //= 2


def _normalised(acc_ref):
    acc = acc_ref[...]
    return acc[0:HEAD_W] * (1.0 / acc[HEAD_W:HEAD_W + 1])


def _da_kernel(qt_ref, k_ref, vt_ref, km_ref, vtm_ref, bd_ref, bs_ref, bm_ref, lam_ref, subg_ref,
               o_ref, qq_ref, s_ref, m_ref, acc_ref, *, t):
    qi = pl.program_id(2)
    feat = lax.broadcasted_iota(jnp.int32, (HEAD_W, 1), 0)
    for hd in range(ATTN_HEADS_PER_STEP):
        qt = qt_ref[0, hd]
        zero = jnp.zeros_like(qt)
        qq_ref[:, pl.ds(2 * t * hd, t)] = jnp.where(feat < DA_HEAD_DIM, qt, zero)
        qq_ref[:, pl.ds(2 * t * hd + t, t)] = jnp.where(feat >= DA_HEAD_DIM, qt, zero)
    qc = s_ref.shape[2]
    chunks_per_map = t // qc
    chunks_per_head = 2 * chunks_per_map
    n_chunks = ATTN_HEADS_PER_STEP * chunks_per_head

    def head(c):
        return c // chunks_per_head

    def cols(c):
        return pl.ds((c % chunks_per_map) * qc, qc)

    def q_chunk(c):
        return qq_ref[:, pl.ds(c * qc, qc)]

    def block_steps(kb, bias=None, keys=t):
        rows = pl.ds(pl.multiple_of(kb * t, t), keys)
        return _block_steps(n_chunks, q_chunk, lambda c: k_ref[0, head(c), rows, :],
                            lambda c: vt_ref[0, head(c), :, rows], bias)

    state = (m_ref, acc_ref)
    n_far = jnp.maximum(qi - 1, 0)
    near_sel = jnp.minimum(qi, 1)
    prev_kb = jnp.maximum(qi - 1, 0)
    meta_steps = _block_steps(n_chunks, q_chunk, lambda c: km_ref[0, head(c)],
                              lambda c: vtm_ref[0, head(c)],
                              lambda c: bm_ref[head(c), near_sel, :, cols(c)])
    _attend_init(*state)
    _prefetch_scores(s_ref, meta_steps)
    _run_steps(s_ref, meta_steps, block_steps(jnp.where(qi >= 2, 0, qi)), *state)

    def far(first, count):
        steps = [st for u in range(count) for st in block_steps(first + u)]
        last = first + count - 1
        _run_steps(s_ref, steps, block_steps(jnp.where(last == n_far - 1, qi, last + 1)), *state)

    _walk_far_blocks(n_far, far)

    diag = []
    for c in range(n_chunks):
        keys = (c % chunks_per_map + 1) * qc
        diag.append(block_steps(qi, lambda c, keys=keys: bd_ref[head(c), 0:keys, cols(c)], keys)[c])
    prev = block_steps(prev_kb, lambda c: bs_ref[head(c), near_sel, :, cols(c)])
    _run_steps(s_ref, diag + prev, (), *state)

    ot = _normalised(acc_ref)
    lp = lam_ref[...]
    lam = (jnp.exp(jnp.sum(lp[0:1] * lp[1:2], axis=-1, keepdims=True))
           - jnp.exp(jnp.sum(lp[2:3] * lp[3:4], axis=-1, keepdims=True)) + LAMBDA_INIT)
    for hd in range(ATTN_HEADS_PER_STEP):
        o1 = ot[:, 2 * t * hd:2 * t * hd + t]
        o2 = ot[:, 2 * t * hd + t:2 * t * (hd + 1)]
        d = o1 - lam * o2
        ms = jnp.mean(d * d, axis=0, keepdims=True)
        y = d * lax.rsqrt(ms + RMS_EPS) * subg_ref[...] * (1.0 - LAMBDA_INIT)
        o_ref[0, :, hd * HEAD_W:(hd + 1) * HEAD_W] = y.T.astype(BF16)


def _attn_specs(t, s_len, q_w):
    hp = ATTN_HEADS_PER_STEP
    return [
        pl.BlockSpec((1, hp, q_w, t), lambda bi, h, qi: (bi, h, 0, qi)),
        pl.BlockSpec((1, hp, s_len, q_w), lambda bi, h, qi: (bi, h, 0, 0)),
        pl.BlockSpec((1, hp, VT_ROWS, s_len), lambda bi, h, qi: (bi, h, 0, 0)),
        pl.BlockSpec((1, hp, META_BLK, q_w), lambda bi, h, qi: (0, h, 0, 0)),
        pl.BlockSpec((1, hp, VT_ROWS, META_BLK), lambda bi, h, qi: (0, h, 0, 0)),
    ]


def _da_attention(qt, k, vt, k_meta, vt_meta, bias_d, bias_s, bias_m, lam_p, sub_g, *, t):
    b, _, s_len, _ = k.shape
    nq = s_len // t
    hp = ATTN_HEADS_PER_STEP
    qc = min(ATTN_QUERY_CHUNK, t)
    vmem = hp * (2 * (s_len + META_BLK) * (HEAD_W + VT_ROWS) * 2 + 2 * 3 * t * t * 4
                 + 2 * 2 * t * META_BLK * 4
                 + 2 * 2 * t * HEAD_W * 2 + 2 * t * HEAD_W * 2 + (VT_ROWS + 8) * 2 * t * 4
                 ) + (ATTN_SCORE_BUFFERS + 4) * t * qc * 4
    return pl.pallas_call(
        functools.partial(_da_kernel, t=t),
        out_shape=jax.ShapeDtypeStruct((b, s_len, DA_W), BF16),
        grid=(b, HEADS // hp, nq),
        in_specs=_attn_specs(t, s_len, HEAD_W) + [
            pl.BlockSpec((hp, t, t), lambda bi, h, qi: (h, 0, 0)),
            pl.BlockSpec((hp, 2, t, t), lambda bi, h, qi: (h, 0, 0, 0)),
            pl.BlockSpec((hp, 2, META_BLK, t), lambda bi, h, qi: (h, 0, 0, 0)),
            _resident((4, DA_HEAD_DIM), lambda bi, h, qi: (0, 0)),
            _resident((HEAD_W, 1), lambda bi, h, qi: (0, 0)),
        ],
        out_specs=pl.BlockSpec((1, t, hp * HEAD_W), lambda bi, h, qi: (bi, qi, h)),
        scratch_shapes=[pltpu.VMEM((HEAD_W, hp * 2 * t), BF16),
                        pltpu.VMEM((ATTN_SCORE_BUFFERS, t, qc), F32),
                        pltpu.VMEM((1, hp * 2 * t), F32), pltpu.VMEM((VT_ROWS, hp * 2 * t), F32)],
        compiler_params=_params(("parallel", "parallel", "arbitrary"), vmem),
        name="diff_attention",
    )(qt, k, vt, k_meta, vt_meta, bias_d, bias_s, bias_m, lam_p, sub_g)


def _mla_kernel(qt_ref, k_ref, vt_ref, km_ref, vtm_ref, o_ref, s_ref, m_ref, acc_ref, *, tq, tk):
    qi = pl.program_id(2)
    qc = s_ref.shape[2]
    diag_blocks = tq // tk
    chunks_per_head = tq // qc
    n_chunks = ATTN_HEADS_PER_STEP * chunks_per_head

    def head(c):
        return c // chunks_per_head

    def q_chunk(c):
        return qt_ref[0, head(c), :, pl.ds((c % chunks_per_head) * qc, qc)]

    def block_steps(kb, bias=None, keys=tk):
        rows = pl.ds(pl.multiple_of(kb * tk, tk), keys)
        return _block_steps(n_chunks, q_chunk, lambda c: k_ref[0, head(c), rows, :],
                            lambda c: vt_ref[0, head(c), :, rows], bias)

    state = (m_ref, acc_ref)

    def meta_mask(c):
        key = lax.broadcasted_iota(jnp.int32, (META_BLK, qc), 0)
        return jnp.where(key < N_META, 0.0, NEG_INF)

    meta_steps = _block_steps(n_chunks, q_chunk, lambda c: km_ref[0, head(c)],
                              lambda c: vtm_ref[0, head(c)], meta_mask)
    _attend_init(*state)
    _prefetch_scores(s_ref, meta_steps)
    _run_steps(s_ref, meta_steps, block_steps(0), *state)

    def far(first, count):
        steps = [st for u in range(count) for st in block_steps(first + u)]
        _run_steps(s_ref, steps, block_steps(first + count), *state)

    first_diag = qi * diag_blocks
    _walk_far_blocks(first_diag, far)

    diag = []
    for d in range(diag_blocks):
        for c in range(n_chunks):
            cq = (c % chunks_per_head) * qc
            keys = min(tk, cq + qc - d * tk)
            if keys <= 0:
                continue

            def causal(c, d=d, keys=keys, cq=cq):
                key = lax.broadcasted_iota(jnp.int32, (keys, qc), 0) + d * tk
                query = lax.broadcasted_iota(jnp.int32, (keys, qc), 1) + cq
                return jnp.where(key <= query, 0.0, NEG_INF)

            diag.append(block_steps(first_diag + d, causal, keys)[c])
    _run_steps(s_ref, diag, (), *state)

    ot = _normalised(acc_ref)
    for hd in range(ATTN_HEADS_PER_STEP):
        o_ref[0, :, hd * HEAD_W:(hd + 1) * HEAD_W] = ot[:, hd * tq:(hd + 1) * tq].T.astype(BF16)


def _mla_attention(qt, k, vt, k_meta, vt_meta, *, tq, tk):
    b, _, s_len, _ = k.shape
    nq = s_len // tq
    hp = ATTN_HEADS_PER_STEP
    qc = min(ATTN_QUERY_CHUNK, tq)
    vmem = hp * (2 * (s_len + META_BLK) * (MLA_QK_W + VT_ROWS) * 2 + 2 * tq * MLA_QK_W * 2
                 + 2 * tq * HEAD_W * 2 + (VT_ROWS + 8) * tq * 4 + tq * HEAD_W * 4
                 ) + (ATTN_SCORE_BUFFERS + 4) * tk * qc * 4
    return pl.pallas_call(
        functools.partial(_mla_kernel, tq=tq, tk=tk),
        out_shape=jax.ShapeDtypeStruct((b, s_len, DA_W), BF16),
        grid=(b, HEADS // hp, nq),
        in_specs=_attn_specs(tq, s_len, MLA_QK_W),
        out_specs=pl.BlockSpec((1, tq, hp * HEAD_W), lambda bi, h, qi: (bi, qi, h)),
        scratch_shapes=[pltpu.VMEM((ATTN_SCORE_BUFFERS, tk, qc), F32),
                        pltpu.VMEM((1, hp * tq), F32), pltpu.VMEM((VT_ROWS, hp * tq), F32)],
        compiler_params=_params(("parallel", "parallel", "arbitrary"), vmem),
        name="mla_attention",
    )(qt, k, vt, k_meta, vt_meta)


def _merge_kernel(h_ref, pre_ref, post_ref, yda_ref, ymla_ref, wg0_ref, wg1_ref, b0_ref, b1_ref,
                  wbd_ref, wbm_ref, wo_ref, o_ref, hn_ref):
    j = pl.program_id(1)
    last = pl.num_programs(1) - 1

    def partial_out(hn):
        g0 = jax.nn.sigmoid(_dot(hn, wg0_ref[...]) + b0_ref[...])
        g1 = jax.nn.sigmoid(_dot(hn, wg1_ref[...]) + b1_ref[...])
        merged = g0 * _dot(yda_ref[0], wbd_ref[...]) + g1 * _dot(ymla_ref[0], wbm_ref[...])
        return _dot(merged.astype(BF16), wo_ref[...])

    @pl.when(j == 0)
    def _():
        hn = _rms(h_ref[0], pre_ref[...]).astype(BF16)
        hn_ref[...] = hn
        o_ref[0] = partial_out(hn)

    @pl.when(jnp.logical_and(j > 0, j < last))
    def _():
        o_ref[0] += partial_out(hn_ref[...])

    @pl.when(j == last)
    def _():
        m = o_ref[0] + partial_out(hn_ref[...])
        o_ref[0] = h_ref[0] + _rms(m, post_ref[...])


def _merge(h1, pre_g, post_g, y_da, y_mla, w_gate, b_gate, wb_da, wb_mla, w_out, *, cast_weights=()):
    b, s_len, d = h1.shape
    yw = y_da.shape[-1]
    tm = _row_tile(s_len, 512)
    tn = _col_tile(d, 512)
    nt = s_len // tm
    nj = d // tn
    assert nj >= 2, "the kernel treats the first and the last column tile as different steps"
    cast_jobs = _plan_cast_jobs(cast_weights, d, b * nt * nj) if cast_weights else ()
    job_specs, job_shapes = _cast_job_specs(cast_jobs, b * nt * nj, lambda i, j: i * nj + j)
    vmem = (2 * 2 * tm * d * 4 + tm * d * 2 + 2 * 2 * tm * yw * 2
            + 2 * (3 * d * tn + 2 * yw * tn) * 2 + 6 * tm * tn * 4 + tm * d * 4
            + sum(2 * j.chunk_rows * j.src.shape[1] * 6 for j in cast_jobs))
    return pl.pallas_call(
        _with_cast_jobs(_merge_kernel, 12, 1, len(cast_jobs)),
        out_shape=[jax.ShapeDtypeStruct((b, s_len, d), F32)] + job_shapes,
        grid=(b * nt, nj),
        in_specs=[
            pl.BlockSpec((1, tm, d), lambda i, j: (i // nt, i % nt, 0)),
            _resident((1, d), lambda i, j: (0, 0)),
            _resident((1, d), lambda i, j: (0, 0)),
            pl.BlockSpec((1, tm, yw), lambda i, j: (i // nt, i % nt, 0)),
            pl.BlockSpec((1, tm, yw), lambda i, j: (i // nt, i % nt, 0)),
            pl.BlockSpec((d, tn), lambda i, j: (0, j)),
            pl.BlockSpec((d, tn), lambda i, j: (0, nj + j)),
            pl.BlockSpec((1, tn), lambda i, j: (0, j)),
            pl.BlockSpec((1, tn), lambda i, j: (0, nj + j)),
            pl.BlockSpec((yw, tn), lambda i, j: (0, j)),
            pl.BlockSpec((yw, tn), lambda i, j: (0, j)),
            pl.BlockSpec((tn, d), lambda i, j: (j, 0)),
        ] + job_specs,
        out_specs=[pl.BlockSpec((1, tm, d), lambda i, j: (i // nt, i % nt, 0))] + job_specs,
        scratch_shapes=[pltpu.VMEM((tm, d), BF16)],
        compiler_params=_params(("arbitrary" if cast_jobs else "parallel", "arbitrary"), vmem),
        name="gated_merge_out_proj",
    )(h1, pre_g, post_g, y_da, y_mla, w_gate, w_gate, b_gate, b_gate, wb_da, wb_mla, w_out,
      *[j.src for j in cast_jobs])


def _rope_tables(pos):
    half = MLA_ROPE // 2
    pos = pos.astype(F32)
    inv = ROPE_THETA ** (-jnp.arange(half, dtype=F32) * 2.0 / MLA_ROPE)
    ang = pos[:, None] * inv[None, :]
    cos, sin = jnp.cos(ang), jnp.sin(ang)
    pad = jnp.zeros((pos.shape[0], HEAD_W - MLA_ROPE), F32)
    return (jnp.concatenate([cos, cos, pad], axis=-1),
            jnp.concatenate([-sin, sin, pad], axis=-1))


def _swap_halves(w):
    half = w.shape[-1] // 2
    return jnp.concatenate([w[..., half:], w[..., :half]], axis=-1)


def _pad_lanes(w, width):
    return jnp.concatenate([w, jnp.zeros(w.shape[:-1] + (width - w.shape[-1],), w.dtype)], axis=-1)


def kernel(x, meta_tokens, rel_bias_table, ffn1_pre_g, ffn1_post_g, ffn1_w_gate, ffn1_w_up, ffn1_w_down, mix_pre_g, mix_post_g, w_in, b_gate, da_lambda_q1, da_lambda_k1, da_lambda_q2, da_lambda_k2, da_sub_g, mla_q_norm_g, mla_kv_norm_g, mla_w_uq, mla_w_ukv, w_branch_da, w_branch_mla, w_out, ffn2_pre_g, ffn2_post_g, ffn2_w_gate, ffn2_w_up, ffn2_w_down):
    b, s_len, d = x.shape
    assert ffn1_pre_g.shape[0] == 1, "single-layer trunk"
    t = _col_tile(s_len, 512)
    row = lambda g: g.reshape(1, -1).astype(F32)

    meta_in = jnp.concatenate([meta_tokens.astype(x.dtype),
                               jnp.zeros((META_BLK - N_META, d), x.dtype)], axis=0)
    ffn1_g = (row(ffn1_pre_g), row(ffn1_post_g))
    h1_meta, *ffn1_w = _ffn(meta_in, *ffn1_g, ffn1_w_gate[0].astype(F32), ffn1_w_up[0].astype(F32),
                            ffn1_w_down[0].astype(F32), name="ffn1_meta")
    h1_meta = h1_meta.reshape(1, META_BLK, d)
    mixer_w = (w_in[0], mla_w_uq[0], mla_w_ukv[0], w_branch_da[0], w_branch_mla[0], w_out[0])
    h1, *mixer_bf16 = _ffn(x.reshape(b * s_len, d), *ffn1_g, *ffn1_w, name="ffn1",
                           cast_weights=[v.astype(F32) for v in mixer_w])
    h1 = h1.reshape(b, s_len, d)
    w, w_uq_bf16, w_ukv_bf16, wb_da, wb_mla, w_o = [
        v.reshape(src.shape) for v, src in zip(mixer_bf16, mixer_w)]

    o_k, o_v, o_cq = DA_W, 2 * DA_W, 3 * DA_W
    o_kr = o_cq + MLA_Q_RANK + MLA_KV_RANK
    o_gate = o_kr + MLA_ROPE
    da_w = (row(mix_pre_g), w[:, :o_k].T.astype(BF16), w[:, o_k:o_v].astype(BF16),
            w[:, o_v:o_cq].T.astype(BF16))
    qt_da, k_da, vt_da = _da_proj(h1, *da_w, name="da_in_proj")
    _, k_da_meta, vt_da_meta = _da_proj(h1_meta, *da_w, name="da_in_proj_meta")

    kr = w[:, o_kr:o_gate]
    w_c = jnp.concatenate([w[:, o_cq:o_kr], _pad_lanes(kr, HEAD_W),
                           _pad_lanes(_swap_halves(kr), HEAD_W)], axis=-1).astype(BF16)
    uq = w_uq_bf16.reshape(MLA_Q_RANK, HEADS, MLA_NOPE + MLA_ROPE)
    uq_rope = uq[..., MLA_NOPE:]
    w_uqt = jnp.concatenate([uq[..., :MLA_NOPE], _pad_lanes(uq_rope, HEAD_W),
                             _pad_lanes(_swap_halves(uq_rope), HEAD_W)],
                            axis=-1).reshape(MLA_Q_RANK, HEADS * _UQ_HEAD_W).T.astype(BF16)
    ukv = w_ukv_bf16.reshape(MLA_KV_RANK, HEADS, 2 * HEAD_W)
    w_uk = ukv[..., :HEAD_W].reshape(MLA_KV_RANK, DA_W).astype(BF16)
    w_uvt = ukv[..., HEAD_W:].reshape(MLA_KV_RANK, DA_W).T.astype(BF16)
    mla_w = (w_c, row(mla_q_norm_g), row(mla_kv_norm_g), w_uqt, w_uk, w_uvt)

    def mla_proj(h, pos, name):
        cos, sin = _rope_tables(pos)
        return _mla_proj(h, row(mix_pre_g), cos, sin, cos.T, sin.T, *mla_w, name=name)

    qt_mla, k_mla, vt_mla = mla_proj(h1, N_META + jnp.arange(s_len), "mla_in_proj")
    _, k_mla_meta, vt_mla_meta = mla_proj(h1_meta, jnp.arange(META_BLK), "mla_in_proj_meta")

    bias_d, bias_s, bias_m = _bias_tiles(rel_bias_table.astype(F32), t)
    lam_p = jnp.concatenate([da_lambda_q1, da_lambda_k1, da_lambda_q2, da_lambda_k2], axis=0).astype(F32)
    y_da = _da_attention(qt_da, k_da, vt_da, k_da_meta, vt_da_meta, bias_d, bias_s, bias_m, lam_p,
                         da_sub_g.reshape(HEAD_W, 1).astype(F32), t=t)
    y_mla = _mla_attention(qt_mla, k_mla, vt_mla, k_mla_meta, vt_mla_meta,
                           tq=_col_tile(s_len, 2 * t), tk=t)

    ffn2_w = (ffn2_w_gate[0], ffn2_w_up[0], ffn2_w_down[0])
    h2, *ffn2_bf16 = _merge(h1, row(mix_pre_g), row(mix_post_g), y_da, y_mla,
                            w[:, o_gate:], row(b_gate), wb_da, wb_mla, w_o,
                            cast_weights=[v.astype(F32) for v in ffn2_w])
    ffn2_bf16 = [v.reshape(src.shape) for v, src in zip(ffn2_bf16, ffn2_w)]

    out, = _ffn(h2.reshape(b * s_len, d), row(ffn2_pre_g), row(ffn2_post_g), *ffn2_bf16, name="ffn2")
    return out.reshape(b, s_len, d)
```

```python
import functools
import math
from typing import Any, Callable, NamedTuple, Optional

import numpy as np
import jax
import jax.numpy as jnp
from jax import lax
from jax.experimental import pallas as pl
from jax.experimental.pallas import tpu as pltpu

F32 = jnp.float32
BF16 = jnp.bfloat16

N_META = 16
RMS_EPS = 1e-6
NEG_INF = -1e30
HEADS = 8
HEAD_W = 128
DA_HEAD_DIM = 64
DA_W = HEADS * HEAD_W
MLA_Q_RANK = 768
MLA_KV_RANK = 512
MLA_NOPE = 128
MLA_ROPE = 64
MLA_QK_W = 256
ROPE_THETA = 10000.0
REL_BUCKETS = 32
REL_MAX_DIST = 128
LAMBDA_INIT = 0.8 - 0.6 * math.exp(-0.3 * 0)
LOG2_E = math.log2(math.e)

V7X_LANES = 128
V7X_BF16_SUBLANES = 16
V7X_VMEM_BYTES = 64 * 1024 * 1024
VMEM_CAP_BYTES = V7X_VMEM_BYTES - 6 * 1024 * 1024

META_BLK = V7X_LANES


def _row_tile(rows, max_tile, multiple=V7X_BF16_SUBLANES):
    best = None
    for t in range(multiple, max_tile + 1, multiple):
        if rows % t == 0:
            best = t
    assert best is not None, (rows, max_tile)
    return best


def _col_tile(cols, max_tile):
    return _row_tile(cols, max_tile, V7X_LANES)


def _params(semantics, vmem_bytes):
    return pltpu.CompilerParams(
        dimension_semantics=semantics,
        vmem_limit_bytes=int(min(max(vmem_bytes, 16 * 1024 * 1024), VMEM_CAP_BYTES)))


def _resident(block_shape, index_map):
    return pl.BlockSpec(block_shape, index_map, pipeline_mode=pl.Buffered(1))


def _rms(x, g):
    ms = jnp.mean(x * x, axis=-1, keepdims=True)
    return x * lax.rsqrt(ms + RMS_EPS) * g


def _dot(a, b):
    return jnp.dot(a, b, preferred_element_type=F32)


def _dot_nt(a, b):
    return lax.dot_general(a, b, (((1,), (1,)), ((), ())), preferred_element_type=F32)


class _CastJob(NamedTuple):
    src: Any
    chunk_rows: int
    start: int

    @property
    def n_chunks(self):
        return self.src.shape[1] // self.chunk_rows


def _plan_cast_jobs(weights, n_steps):
    chunk_elems = V7X_BF16_SUBLANES * min(w.shape[2] for w in weights)
    while True:
        chunks = [_row_tile(w.shape[1], max(V7X_BF16_SUBLANES, chunk_elems // w.shape[2]))
                  for w in weights]
        counts = [w.shape[1] // c for w, c in zip(weights, chunks)]
        if sum(counts) <= n_steps:
            starts = np.cumsum([0] + counts[:-1])
            return [_CastJob(w, c, int(s)) for w, c, s in zip(weights, chunks, starts)]
        assert chunk_elems < max(w.shape[1] * w.shape[2] for w in weights), "too few grid steps"
        chunk_elems += chunk_elems // 8


def _cast_job_specs(jobs, flat_step):
    in_specs, out_specs, shapes = [], [], []
    for job in jobs:
        _, rows, cols = job.src.shape

        def chunk_index(*ids, job=job):
            return jnp.clip(flat_step(*ids) - job.start, 0, job.n_chunks - 1)

        in_specs.append(pl.BlockSpec((None, job.chunk_rows, cols),
                                     lambda *ids, f=chunk_index: (0, f(*ids), 0)))
        out_specs.append(pl.BlockSpec((job.chunk_rows, cols), lambda *ids, f=chunk_index: (f(*ids), 0)))
        shapes.append(jax.ShapeDtypeStruct((rows, cols), BF16))
    return in_specs, out_specs, shapes


def _with_cast_jobs(body, n_in, n_out, jobs, flat_step):
    n_jobs = len(jobs)

    def kernel(*refs):
        ins, refs = refs[:n_in], refs[n_in:]
        job_in, refs = refs[:n_jobs], refs[n_jobs:]
        outs, refs = refs[:n_out], refs[n_out:]
        job_out, scratch = refs[:n_jobs], refs[n_jobs:]
        if jobs:
            step = flat_step(pl.program_id(0), pl.program_id(1))
        for job, src, dst in zip(jobs, job_in, job_out):
            @pl.when(jnp.logical_and(step >= job.start, step < job.start + job.n_chunks))
            def _(src=src, dst=dst):
                dst[...] = src[...].astype(BF16)
        body(*ins, *outs, *scratch)
    return kernel


def _ffn_kernel(x_ref, pre_ref, post_ref, wg_ref, wu_ref, wd_ref, o_ref, *rest):
    *bf16_out, xn_ref = rest
    k = pl.program_id(1)
    last = pl.num_programs(1) - 1

    def partial_out(xn):
        wg, wu, wd = wg_ref[...], wu_ref[...], wd_ref[...]
        if bf16_out:
            wg, wu, wd = wg.astype(BF16), wu.astype(BF16), wd.astype(BF16)
            for ref, tile in zip(bf16_out, (wg, wu, wd)):
                ref[...] = tile
        g = _dot(xn, wg)
        u = _dot(xn, wu)
        a = (g * jax.nn.sigmoid(g) * u).astype(BF16)
        return _dot(a, wd)

    @pl.when(k == 0)
    def _():
        xn = _rms(x_ref[...], pre_ref[...]).astype(BF16)
        xn_ref[...] = xn
        o_ref[...] = partial_out(xn)

    @pl.when(jnp.logical_and(k > 0, k < last))
    def _():
        o_ref[...] += partial_out(xn_ref[...])

    @pl.when(k == last)
    def _():
        f = o_ref[...] + partial_out(xn_ref[...])
        o_ref[...] = x_ref[...] + 0.5 * _rms(f, post_ref[...])


def _ffn(x, pre_g, post_g, wg, wu, wd, *, name, cast_weights=()):
    rows, d = x.shape
    f = wg.shape[-1]
    tm = _row_tile(rows, 896)
    tf = _col_tile(f, 512)
    n_i, n_k = rows // tm, f // tf
    assert n_k >= 2, "the kernel treats the first and the last d_ff tile as different steps"
    flat_step = lambda i, k: i * n_k + k
    cast_jobs = _plan_cast_jobs(cast_weights, n_i * n_k) if cast_weights else ()
    emit = wg.dtype == F32
    assert not emit or (n_i == 1 and wg.ndim == 3)
    w_bytes = 4 if emit else 2
    bf16_weight_specs = [pl.BlockSpec((d, tf), lambda i, k: (0, k)),
                         pl.BlockSpec((d, tf), lambda i, k: (0, k)),
                         pl.BlockSpec((tf, d), lambda i, k: (k, 0))]
    f32_weight_specs = [pl.BlockSpec((None, d, tf), lambda i, k: (0, 0, k)),
                        pl.BlockSpec((None, d, tf), lambda i, k: (0, 0, k)),
                        pl.BlockSpec((None, tf, d), lambda i, k: (0, k, 0))]
    job_in_specs, job_out_specs, job_shapes = _cast_job_specs(cast_jobs, flat_step)
    vmem = (2 * 2 * tm * d * 4
            + tm * d * 2
            + 2 * 3 * d * tf * (w_bytes + (2 if emit else 0))
            + 4 * tm * tf * 4
            + tm * d * 4
            + sum(2 * j.chunk_rows * j.src.shape[2] * 6 for j in cast_jobs))
    n_out = 4 if emit else 1
    return pl.pallas_call(
        _with_cast_jobs(_ffn_kernel, 6, n_out, cast_jobs, flat_step),
        out_shape=[jax.ShapeDtypeStruct((rows, d), F32)]
        + ([jax.ShapeDtypeStruct(w.shape[1:], BF16) for w in (wg, wu, wd)] if emit else [])
        + job_shapes,
        grid=(n_i, n_k),
        in_specs=[
            pl.BlockSpec((tm, d), lambda i, k: (i, 0)),
            _resident((1, d), lambda i, k: (0, 0)),
            _resident((1, d), lambda i, k: (0, 0)),
        ] + (f32_weight_specs if emit else bf16_weight_specs) + job_in_specs,
        out_specs=[pl.BlockSpec((tm, d), lambda i, k: (i, 0))]
        + (bf16_weight_specs if emit else []) + job_out_specs,
        scratch_shapes=[pltpu.VMEM((tm, d), BF16)],
        compiler_params=_params(("arbitrary" if cast_jobs else "parallel", "arbitrary"), vmem),
        name=name,
    )(x, pre_g, post_g, wg, wu, wd, *[j.src for j in cast_jobs])


VT_ROWS = HEAD_W + V7X_BF16_SUBLANES


def _store_vt(vt_ref, vt):
    tm = vt.shape[1]
    vt_ref[0, :, 0:HEAD_W, :] = vt.reshape(HEADS, HEAD_W, tm).astype(BF16)
    extra = lax.broadcasted_iota(jnp.int32, (HEADS, VT_ROWS - HEAD_W, tm), 1)
    vt_ref[0, :, HEAD_W:VT_ROWS, :] = jnp.where(extra == 0, 1.0, 0.0).astype(BF16)


def _da_proj_kernel(h_ref, g_ref, wqt_ref, wk_ref, wvt_ref, qt_ref, k_ref, vt_ref):
    hn = _rms(h_ref[...], g_ref[...]).astype(BF16)
    tm = hn.shape[0]
    qt = _dot_nt(wqt_ref[...], hn) * (DA_HEAD_DIM ** -0.5 * LOG2_E)
    qt_ref[0] = qt.reshape(HEADS, HEAD_W, tm).astype(BF16)
    k = _dot(hn, wk_ref[...])
    for h in range(HEADS):
        k_ref[0, h] = k[:, h * HEAD_W:(h + 1) * HEAD_W].astype(BF16)
    _store_vt(vt_ref, _dot_nt(wvt_ref[...], hn))


_C_CKV = MLA_Q_RANK
_C_KR1 = _C_CKV + MLA_KV_RANK
_C_KR2 = _C_KR1 + HEAD_W
_C_END = _C_KR2 + HEAD_W
_UQ_HEAD_W = 3 * HEAD_W


def _mla_proj_kernel(h_ref, g_ref, cos_ref, sin_ref, cost_ref, sint_ref, wc_ref, gq_ref, gkv_ref,
                     wuqt_ref, wuk_ref, wuvt_ref, qt_ref, k_ref, vt_ref):
    hn = _rms(h_ref[...], g_ref[...]).astype(BF16)
    tm = hn.shape[0]
    scale = (MLA_NOPE + MLA_ROPE) ** -0.5 * LOG2_E

    c_all = _dot(hn, wc_ref[...])

    def proj(lo, hi):
        return c_all[:, lo:hi]

    cq = _rms(proj(0, _C_CKV), gq_ref[...]).astype(BF16)
    qt = _dot_nt(wuqt_ref[...], cq)
    cos_t = cost_ref[...]
    sin_t = sint_ref[...]
    for h in range(HEADS):
        base = h * _UQ_HEAD_W
        nope = qt[base:base + HEAD_W]
        r1 = qt[base + HEAD_W:base + 2 * HEAD_W]
        r2 = qt[base + 2 * HEAD_W:base + 3 * HEAD_W]
        qt_ref[0, h, 0:HEAD_W, :] = (nope * scale).astype(BF16)
        qt_ref[0, h, HEAD_W:MLA_QK_W, :] = ((r1 * cos_t + r2 * sin_t) * scale).astype(BF16)

    ckv = _rms(proj(_C_CKV, _C_KR1), gkv_ref[...]).astype(BF16)
    k_nope = _dot(ckv, wuk_ref[...])
    k_rope = (proj(_C_KR1, _C_KR2) * cos_ref[...] + proj(_C_KR2, _C_END) * sin_ref[...]).astype(BF16)
    for h in range(HEADS):
        k_ref[0, h, :, 0:HEAD_W] = k_nope[:, h * HEAD_W:(h + 1) * HEAD_W].astype(BF16)
        k_ref[0, h, :, HEAD_W:MLA_QK_W] = k_rope
    _store_vt(vt_ref, _dot_nt(wuvt_ref[...], ckv))


def _proj_specs(b, lp, tm, q_w):
    nt = lp // tm
    rows = lambda w: pl.BlockSpec((1, HEADS, tm, w), lambda i: (i // nt, 0, i % nt, 0))
    cols = lambda w: pl.BlockSpec((1, HEADS, w, tm), lambda i: (i // nt, 0, 0, i % nt))
    shapes = [jax.ShapeDtypeStruct((b, HEADS, q_w, lp), BF16),
              jax.ShapeDtypeStruct((b, HEADS, lp, q_w), BF16),
              jax.ShapeDtypeStruct((b, HEADS, VT_ROWS, lp), BF16)]
    return shapes, [cols(q_w), rows(q_w), cols(VT_ROWS)]


def _da_proj(h1, g, wqt, wk, wvt, *, name):
    b, lp, d = h1.shape
    tm = _col_tile(lp, 704)
    nt = lp // tm
    shapes, out_specs = _proj_specs(b, lp, tm, HEAD_W)
    vmem = (2 * tm * d * 4 + tm * d * 2 + 3 * DA_W * d * 2
            + 2 * 3 * tm * DA_W * 2 + 3 * tm * DA_W * 4)
    return pl.pallas_call(
        _da_proj_kernel,
        out_shape=shapes,
        grid=(b * nt,),
        in_specs=[
            pl.BlockSpec((None, tm, d), lambda i: (i // nt, i % nt, 0)),
            _resident((1, d), lambda i: (0, 0)),
            _resident(wqt.shape, lambda i: (0, 0)),
            _resident(wk.shape, lambda i: (0, 0)),
            _resident(wvt.shape, lambda i: (0, 0)),
        ],
        out_specs=out_specs,
        compiler_params=_params(("parallel",), vmem),
        name=name,
    )(h1, g, wqt, wk, wvt)


def _mla_proj(h1, g, cos, sin, cos_t, sin_t, wc, gq, gkv, wuqt, wuk, wuvt, *, name):
    b, lp, d = h1.shape
    tm = _col_tile(lp, 704)
    nt = lp // tm
    shapes, out_specs = _proj_specs(b, lp, tm, MLA_QK_W)
    vmem = (2 * tm * d * 4 + tm * d * 2
            + (d * _C_END + HEADS * _UQ_HEAD_W * MLA_Q_RANK + 2 * DA_W * MLA_KV_RANK) * 2
            + 2 * tm * HEADS * (2 * MLA_QK_W + HEAD_W) * 2
            + 2 * tm * HEADS * _UQ_HEAD_W * 4)
    return pl.pallas_call(
        _mla_proj_kernel,
        out_shape=shapes,
        grid=(b * nt,),
        in_specs=[
            pl.BlockSpec((None, tm, d), lambda i: (i // nt, i % nt, 0)),
            _resident((1, d), lambda i: (0, 0)),
            pl.BlockSpec((tm, HEAD_W), lambda i: (i % nt, 0)),
            pl.BlockSpec((tm, HEAD_W), lambda i: (i % nt, 0)),
            pl.BlockSpec((HEAD_W, tm), lambda i: (0, i % nt)),
            pl.BlockSpec((HEAD_W, tm), lambda i: (0, i % nt)),
            _resident(wc.shape, lambda i: (0, 0)),
            _resident((1, MLA_Q_RANK), lambda i: (0, 0)),
            _resident((1, MLA_KV_RANK), lambda i: (0, 0)),
            _resident(wuqt.shape, lambda i: (0, 0)),
            _resident(wuk.shape, lambda i: (0, 0)),
            _resident(wuvt.shape, lambda i: (0, 0)),
        ],
        out_specs=out_specs,
        compiler_params=_params(("parallel",), vmem),
        name=name,
    )(h1, g, cos, sin, cos_t, sin_t, wc, gq, gkv, wuqt, wuk, wuvt)


def _t5_bucket(rel):
    n = np.maximum(rel, 0)
    max_exact = REL_BUCKETS // 2
    n_f = np.maximum(n, 1).astype(np.float64)
    large = max_exact + (np.log(n_f / max_exact) / math.log(REL_MAX_DIST / max_exact)
                         * (REL_BUCKETS - max_exact)).astype(np.int32)
    large = np.minimum(large, REL_BUCKETS - 1)
    return np.where(n < max_exact, n, large).astype(np.int32)


def _bucket_tiles(t):
    j = np.arange(t)[:, None]
    i = np.arange(t)[None, :]
    diag = np.where(j <= i, _t5_bucket(i - j), -1)
    sub = _t5_bucket(t + i - j)
    assert _t5_bucket(np.array([t + 1]))[0] == REL_BUCKETS - 1
    jm = np.arange(META_BLK)[:, None]
    meta_first = np.where(jm < N_META, _t5_bucket(N_META + i - jm), -1)
    meta_rest = np.where(jm < N_META, REL_BUCKETS - 1, -1) + 0 * i
    return (diag.astype(np.int32), np.stack([np.full_like(sub, -1), sub]).astype(np.int32),
            np.stack([meta_first, meta_rest]).astype(np.int32))


def _bias_kernel(table_ref, bd_ref, bs_ref, bm_ref, od_ref, os_ref, om_ref, *, buckets):
    h = pl.program_id(0)
    far = table_ref[REL_BUCKETS - 1, h]

    def lookup(bucket_ref, out_ref, known):
        rows, cols = known.shape
        for r0 in range(0, rows, V7X_LANES):
            for c0 in range(0, cols, V7X_LANES):
                patch = (slice(r0, r0 + V7X_LANES), slice(c0, c0 + V7X_LANES))
                present = np.unique(known[patch])
                acc = jnp.zeros((V7X_LANES, V7X_LANES), F32)
                if present.size > 1 or 0 <= present[0] < REL_BUCKETS - 1:
                    bucket = bucket_ref[patch]
                    for b in present:
                        if 0 <= b < REL_BUCKETS - 1:
                            acc = jnp.where(bucket == int(b), (table_ref[int(b), h] - far) * LOG2_E, acc)
                    if present[0] < 0:
                        acc = jnp.where(bucket < 0, NEG_INF, acc)
                elif present[0] < 0:
                    acc = jnp.full((V7X_LANES, V7X_LANES), NEG_INF, F32)
                out_ref[patch] = acc

    bd, bs, bm = buckets
    lookup(bd_ref, od_ref.at[0], bd)
    for i in range(2):
        lookup(bs_ref.at[i], os_ref.at[0, i], bs[i])
        lookup(bm_ref.at[i], om_ref.at[0, i], bm[i])


def _bias_tiles(table, t):
    bd, bs, bm = _bucket_tiles(t)
    return pl.pallas_call(
        functools.partial(_bias_kernel, buckets=(bd, bs, bm)),
        out_shape=[jax.ShapeDtypeStruct((HEADS, t, t), F32),
                   jax.ShapeDtypeStruct((HEADS, 2, t, t), F32),
                   jax.ShapeDtypeStruct((HEADS, 2, META_BLK, t), F32)],
        grid=(HEADS,),
        in_specs=[
            pl.BlockSpec(memory_space=pltpu.SMEM),
            pl.BlockSpec((t, t), lambda h: (0, 0)),
            pl.BlockSpec((2, t, t), lambda h: (0, 0, 0)),
            pl.BlockSpec((2, META_BLK, t), lambda h: (0, 0, 0)),
        ],
        out_specs=[pl.BlockSpec((1, t, t), lambda h: (h, 0, 0)),
                   pl.BlockSpec((1, 2, t, t), lambda h: (h, 0, 0, 0)),
                   pl.BlockSpec((1, 2, META_BLK, t), lambda h: (h, 0, 0, 0))],
        compiler_params=_params(("parallel",), 12 * t * t * 4),
        name="rel_bias_tiles",
    )(table, jnp.asarray(bd), jnp.asarray(bs), jnp.asarray(bm))


ATTN_QUERY_CHUNK = 256


ATTN_SCORE_BUFFERS = 4
ATTN_HEADS_PER_STEP = 2


class _Step(NamedTuple):
    q: Callable[[], Any]
    k: Callable[[], Any]
    vt: Callable[[], Any]
    bias: Optional[Callable[[], Any]]
    chunk: int


def _block_steps(n_chunks, q, k, vt, bias=None):
    return [_Step(functools.partial(q, c), functools.partial(k, c), functools.partial(vt, c),
                  functools.partial(bias, c) if bias is not None else None, c)
            for c in range(n_chunks)]


def _issue_scores(s_ref, step, slot):
    k = step.k()
    s_ref[slot, 0:k.shape[0], :] = _dot(k, step.q())


def _prefetch_scores(s_ref, steps):
    for i, step in enumerate(steps[:s_ref.shape[0] - 1]):
        _issue_scores(s_ref, step, i)


def _run_steps(s_ref, steps, following, m_ref, acc_ref):
    n_buf, _, qc = s_ref.shape
    ahead = n_buf - 1
    assert not following or len(steps) % n_buf == 0
    seq = list(steps) + list(following)[:ahead]
    for i, step in enumerate(steps):
        if i + ahead < len(seq):
            _issue_scores(s_ref, seq[i + ahead], (i + ahead) % n_buf)
        vt = step.vt()
        s = s_ref[i % n_buf, 0:vt.shape[1], :]
        if step.bias is not None:
            s = s + step.bias()
        sl = pl.ds(step.chunk * qc, qc)
        m_prev = m_ref[:, sl]
        m_new = jnp.maximum(m_prev, jnp.max(s, axis=0, keepdims=True))
        alpha = jnp.exp2(m_prev - m_new)
        p = jnp.exp2(s - m_new)
        acc_ref[:, sl] = alpha * acc_ref[:, sl] + _dot(vt, p.astype(BF16))
        m_ref[:, sl] = m_new


def _attend_init(m_ref, acc_ref):
    m_ref[...] = jnp.full(m_ref.shape, NEG_INF, F32)
    acc_ref[...] = jnp.zeros(acc_ref.shape, F32)


FAR_UNROLL = 4


def _walk_far_blocks(n_far, body):
    def loop_body(i, carry):
        body(i * FAR_UNROLL, FAR_UNROLL)
        return carry

    lax.fori_loop(0, n_far >> (FAR_UNROLL.bit_length() - 1), loop_body, 0)
    count = FAR_UNROLL // 2
    while count >= 1:
        first = n_far - (n_far & (2 * count - 1))

        @pl.when((n_far & count) != 0)
        def _(first=first, count=count):
            body(first, count)

        count //= 2


def _normalised(acc_ref):
    acc = acc_ref[...]
    return acc[0:HEAD_W] * (1.0 / acc[HEAD_W:HEAD_W + 1])


def _da_kernel(qt_ref, k_ref, vt_ref, km_ref, vtm_ref, bd_ref, bs_ref, bm_ref, lam_ref, subg_ref,
               o_ref, qq_ref, s_ref, m_ref, acc_ref, *, t):
    qi = pl.program_id(2)
    feat = lax.broadcasted_iota(jnp.int32, (HEAD_W, 1), 0)
    for hd in range(ATTN_HEADS_PER_STEP):
        qt = qt_ref[0, hd]
        zero = jnp.zeros_like(qt)
        qq_ref[:, pl.ds(2 * t * hd, t)] = jnp.where(feat < DA_HEAD_DIM, qt, zero)
        qq_ref[:, pl.ds(2 * t * hd + t, t)] = jnp.where(feat >= DA_HEAD_DIM, qt, zero)
    qc = s_ref.shape[2]
    chunks_per_map = t // qc
    chunks_per_head = 2 * chunks_per_map
    n_chunks = ATTN_HEADS_PER_STEP * chunks_per_head

    def head(c):
        return c // chunks_per_head

    def cols(c):
        return pl.ds((c % chunks_per_map) * qc, qc)

    def q_chunk(c):
        return qq_ref[:, pl.ds(c * qc, qc)]

    def block_steps(kb, bias=None, keys=t):
        rows = pl.ds(pl.multiple_of(kb * t, t), keys)
        return _block_steps(n_chunks, q_chunk, lambda c: k_ref[0, head(c), rows, :],
                            lambda c: vt_ref[0, head(c), :, rows], bias)

    state = (m_ref, acc_ref)
    n_far = jnp.maximum(qi - 1, 0)
    near_sel = jnp.minimum(qi, 1)
    prev_kb = jnp.maximum(qi - 1, 0)
    meta_steps = _block_steps(n_chunks, q_chunk, lambda c: km_ref[0, head(c)],
                              lambda c: vtm_ref[0, head(c)],
                              lambda c: bm_ref[head(c), near_sel, :, cols(c)])
    _attend_init(*state)
    _prefetch_scores(s_ref, meta_steps)
    _run_steps(s_ref, meta_steps, block_steps(jnp.where(qi >= 2, 0, qi)), *state)

    def far(first, count):
        steps = [st for u in range(count) for st in block_steps(first + u)]
        last = first + count - 1
        _run_steps(s_ref, steps, block_steps(jnp.where(last == n_far - 1, qi, last + 1)), *state)

    _walk_far_blocks(n_far, far)

    diag = []
    for c in range(n_chunks):
        keys = (c % chunks_per_map + 1) * qc
        diag.append(block_steps(qi, lambda c, keys=keys: bd_ref[head(c), 0:keys, cols(c)], keys)[c])
    prev = block_steps(prev_kb, lambda c: bs_ref[head(c), near_sel, :, cols(c)])
    _run_steps(s_ref, diag + prev, (), *state)

    ot = _normalised(acc_ref)
    lp = lam_ref[...]
    lam = (jnp.exp(jnp.sum(lp[0:1] * lp[1:2], axis=-1, keepdims=True))
           - jnp.exp(jnp.sum(lp[2:3] * lp[3:4], axis=-1, keepdims=True)) + LAMBDA_INIT)
    for hd in range(ATTN_HEADS_PER_STEP):
        o1 = ot[:, 2 * t * hd:2 * t * hd + t]
        o2 = ot[:, 2 * t * hd + t:2 * t * (hd + 1)]
        d = o1 - lam * o2
        ms = jnp.mean(d * d, axis=0, keepdims=True)
        y = d * lax.rsqrt(ms + RMS_EPS) * subg_ref[...] * (1.0 - LAMBDA_INIT)
        o_ref[0, :, hd * HEAD_W:(hd + 1) * HEAD_W] = y.T.astype(BF16)


def _attn_specs(t, s_len, q_w):
    hp = ATTN_HEADS_PER_STEP
    return [
        pl.BlockSpec((1, hp, q_w, t), lambda bi, h, qi: (bi, h, 0, qi)),
        pl.BlockSpec((1, hp, s_len, q_w), lambda bi, h, qi: (bi, h, 0, 0)),
        pl.BlockSpec((1, hp, VT_ROWS, s_len), lambda bi, h, qi: (bi, h, 0, 0)),
        pl.BlockSpec((1, hp, META_BLK, q_w), lambda bi, h, qi: (0, h, 0, 0)),
        pl.BlockSpec((1, hp, VT_ROWS, META_BLK), lambda bi, h, qi: (0, h, 0, 0)),
    ]


def _da_attention(qt, k, vt, k_meta, vt_meta, bias_d, bias_s, bias_m, lam_p, sub_g, *, t):
    b, _, s_len, _ = k.shape
    nq = s_len // t
    hp = ATTN_HEADS_PER_STEP
    qc = min(ATTN_QUERY_CHUNK, t)
    vmem = hp * (2 * (s_len + META_BLK) * (HEAD_W + VT_ROWS) * 2 + 2 * 3 * t * t * 4
                 + 2 * 2 * t * META_BLK * 4
                 + 2 * 2 * t * HEAD_W * 2 + 2 * t * HEAD_W * 2 + (VT_ROWS + 8) * 2 * t * 4
                 ) + (ATTN_SCORE_BUFFERS + 4) * t * qc * 4
    return pl.pallas_call(
        functools.partial(_da_kernel, t=t),
        out_shape=jax.ShapeDtypeStruct((b, s_len, DA_W), BF16),
        grid=(b, HEADS // hp, nq),
        in_specs=_attn_specs(t, s_len, HEAD_W) + [
            pl.BlockSpec((hp, t, t), lambda bi, h, qi: (h, 0, 0)),
            pl.BlockSpec((hp, 2, t, t), lambda bi, h, qi: (h, 0, 0, 0)),
            pl.BlockSpec((hp, 2, META_BLK, t), lambda bi, h, qi: (h, 0, 0, 0)),
            _resident((4, DA_HEAD_DIM), lambda bi, h, qi: (0, 0)),
            _resident((HEAD_W, 1), lambda bi, h, qi: (0, 0)),
        ],
        out_specs=pl.BlockSpec((1, t, hp * HEAD_W), lambda bi, h, qi: (bi, qi, h)),
        scratch_shapes=[pltpu.VMEM((HEAD_W, hp * 2 * t), BF16),
                        pltpu.VMEM((ATTN_SCORE_BUFFERS, t, qc), F32),
                        pltpu.VMEM((1, hp * 2 * t), F32), pltpu.VMEM((VT_ROWS, hp * 2 * t), F32)],
        compiler_params=_params(("parallel", "parallel", "arbitrary"), vmem),
        name="diff_attention",
    )(qt, k, vt, k_meta, vt_meta, bias_d, bias_s, bias_m, lam_p, sub_g)


def _mla_kernel(qt_ref, k_ref, vt_ref, km_ref, vtm_ref, o_ref, s_ref, m_ref, acc_ref, *, tq, tk):
    qi = pl.program_id(2)
    qc = s_ref.shape[2]
    diag_blocks = tq // tk
    chunks_per_head = tq // qc
    n_chunks = ATTN_HEADS_PER_STEP * chunks_per_head

    def head(c):
        return c // chunks_per_head

    def q_chunk(c):
        return qt_ref[0, head(c), :, pl.ds((c % chunks_per_head) * qc, qc)]

    def block_steps(kb, bias=None, keys=tk):
        rows = pl.ds(pl.multiple_of(kb * tk, tk), keys)
        return _block_steps(n_chunks, q_chunk, lambda c: k_ref[0, head(c), rows, :],
                            lambda c: vt_ref[0, head(c), :, rows], bias)

    state = (m_ref, acc_ref)

    def meta_mask(c):
        key = lax.broadcasted_iota(jnp.int32, (META_BLK, qc), 0)
        return jnp.where(key < N_META, 0.0, NEG_INF)

    meta_steps = _block_steps(n_chunks, q_chunk, lambda c: km_ref[0, head(c)],
                              lambda c: vtm_ref[0, head(c)], meta_mask)
    _attend_init(*state)
    _prefetch_scores(s_ref, meta_steps)
    _run_steps(s_ref, meta_steps, block_steps(0), *state)

    def far(first, count):
        steps = [st for u in range(count) for st in block_steps(first + u)]
        _run_steps(s_ref, steps, block_steps(first + count), *state)

    first_diag = qi * diag_blocks
    _walk_far_blocks(first_diag, far)

    diag = []
    for d in range(diag_blocks):
        for c in range(n_chunks):
            cq = (c % chunks_per_head) * qc
            keys = min(tk, cq + qc - d * tk)
            if keys <= 0:
                continue

            def causal(c, d=d, keys=keys, cq=cq):
                key = lax.broadcasted_iota(jnp.int32, (keys, qc), 0) + d * tk
                query = lax.broadcasted_iota(jnp.int32, (keys, qc), 1) + cq
                return jnp.where(key <= query, 0.0, NEG_INF)

            diag.append(block_steps(first_diag + d, causal, keys)[c])
    _run_steps(s_ref, diag, (), *state)

    ot = _normalised(acc_ref)
    for hd in range(ATTN_HEADS_PER_STEP):
        o_ref[0, :, hd * HEAD_W:(hd + 1) * HEAD_W] = ot[:, hd * tq:(hd + 1) * tq].T.astype(BF16)


def _mla_attention(qt, k, vt, k_meta, vt_meta, *, tq, tk):
    b, _, s_len, _ = k.shape
    nq = s_len // tq
    hp = ATTN_HEADS_PER_STEP
    qc = min(ATTN_QUERY_CHUNK, tq)
    vmem = hp * (2 * (s_len + META_BLK) * (MLA_QK_W + VT_ROWS) * 2 + 2 * tq * MLA_QK_W * 2
                 + 2 * tq * HEAD_W * 2 + (VT_ROWS + 8) * tq * 4 + tq * HEAD_W * 4
                 ) + (ATTN_SCORE_BUFFERS + 4) * tk * qc * 4
    return pl.pallas_call(
        functools.partial(_mla_kernel, tq=tq, tk=tk),
        out_shape=jax.ShapeDtypeStruct((b, s_len, DA_W), BF16),
        grid=(b, HEADS // hp, nq),
        in_specs=_attn_specs(tq, s_len, MLA_QK_W),
        out_specs=pl.BlockSpec((1, tq, hp * HEAD_W), lambda bi, h, qi: (bi, qi, h)),
        scratch_shapes=[pltpu.VMEM((ATTN_SCORE_BUFFERS, tk, qc), F32),
                        pltpu.VMEM((1, hp * tq), F32), pltpu.VMEM((VT_ROWS, hp * tq), F32)],
        compiler_params=_params(("parallel", "parallel", "arbitrary"), vmem),
        name="mla_attention",
    )(qt, k, vt, k_meta, vt_meta)


def _merge_kernel(h_ref, pre_ref, post_ref, yda_ref, ymla_ref, wg0_ref, wg1_ref, b0_ref, b1_ref,
                  wbd_ref, wbm_ref, wo_ref, o_ref, hn_ref):
    j = pl.program_id(1)
    last = pl.num_programs(1) - 1

    def partial_out(hn):
        g0 = jax.nn.sigmoid(_dot(hn, wg0_ref[...]) + b0_ref[...])
        g1 = jax.nn.sigmoid(_dot(hn, wg1_ref[...]) + b1_ref[...])
        merged = g0 * _dot(yda_ref[0], wbd_ref[...]) + g1 * _dot(ymla_ref[0], wbm_ref[...])
        return _dot(merged.astype(BF16), wo_ref[...])

    @pl.when(j == 0)
    def _():
        hn = _rms(h_ref[0], pre_ref[...]).astype(BF16)
        hn_ref[...] = hn
        o_ref[0] = partial_out(hn)

    @pl.when(jnp.logical_and(j > 0, j < last))
    def _():
        o_ref[0] += partial_out(hn_ref[...])

    @pl.when(j == last)
    def _():
        m = o_ref[0] + partial_out(hn_ref[...])
        o_ref[0] = h_ref[0] + _rms(m, post_ref[...])


def _merge(h1, pre_g, post_g, y_da, y_mla, w_gate, b_gate, wb_da, wb_mla, w_out, *, cast_weights=()):
    b, s_len, d = h1.shape
    yw = y_da.shape[-1]
    tm = _row_tile(s_len, 512)
    tn = _col_tile(d, 512)
    nt = s_len // tm
    nj = d // tn
    assert nj >= 2, "the kernel treats the first and the last column tile as different steps"
    flat_step = lambda i, j: i * nj + j
    cast_jobs = _plan_cast_jobs(cast_weights, b * nt * nj) if cast_weights else ()
    job_in_specs, job_out_specs, job_shapes = _cast_job_specs(cast_jobs, flat_step)
    vmem = (2 * 2 * tm * d * 4 + tm * d * 2 + 2 * 2 * tm * yw * 2
            + 2 * (3 * d * tn + 2 * yw * tn) * 2 + 6 * tm * tn * 4 + tm * d * 4
            + sum(2 * j.chunk_rows * j.src.shape[2] * 6 for j in cast_jobs))
    return pl.pallas_call(
        _with_cast_jobs(_merge_kernel, 12, 1, cast_jobs, flat_step),
        out_shape=[jax.ShapeDtypeStruct((b, s_len, d), F32)] + job_shapes,
        grid=(b * nt, nj),
        in_specs=[
            pl.BlockSpec((1, tm, d), lambda i, j: (i // nt, i % nt, 0)),
            _resident((1, d), lambda i, j: (0, 0)),
            _resident((1, d), lambda i, j: (0, 0)),
            pl.BlockSpec((1, tm, yw), lambda i, j: (i // nt, i % nt, 0)),
            pl.BlockSpec((1, tm, yw), lambda i, j: (i // nt, i % nt, 0)),
            pl.BlockSpec((d, tn), lambda i, j: (0, j)),
            pl.BlockSpec((d, tn), lambda i, j: (0, nj + j)),
            pl.BlockSpec((1, tn), lambda i, j: (0, j)),
            pl.BlockSpec((1, tn), lambda i, j: (0, nj + j)),
            pl.BlockSpec((yw, tn), lambda i, j: (0, j)),
            pl.BlockSpec((yw, tn), lambda i, j: (0, j)),
            pl.BlockSpec((tn, d), lambda i, j: (j, 0)),
        ] + job_in_specs,
        out_specs=[pl.BlockSpec((1, tm, d), lambda i, j: (i // nt, i % nt, 0))] + job_out_specs,
        scratch_shapes=[pltpu.VMEM((tm, d), BF16)],
        compiler_params=_params(("arbitrary" if cast_jobs else "parallel", "arbitrary"), vmem),
        name="gated_merge_out_proj",
    )(h1, pre_g, post_g, y_da, y_mla, w_gate, w_gate, b_gate, b_gate, wb_da, wb_mla, w_out,
      *[j.src for j in cast_jobs])


def _rope_tables(pos):
    half = MLA_ROPE // 2
    pos = pos.astype(F32)
    inv = ROPE_THETA ** (-jnp.arange(half, dtype=F32) * 2.0 / MLA_ROPE)
    ang = pos[:, None] * inv[None, :]
    cos, sin = jnp.cos(ang), jnp.sin(ang)
    pad = jnp.zeros((pos.shape[0], HEAD_W - MLA_ROPE), F32)
    return (jnp.concatenate([cos, cos, pad], axis=-1),
            jnp.concatenate([-sin, sin, pad], axis=-1))


def _swap_halves(w):
    half = w.shape[-1] // 2
    return jnp.concatenate([w[..., half:], w[..., :half]], axis=-1)


def _pad_lanes(w, width):
    return jnp.concatenate([w, jnp.zeros(w.shape[:-1] + (width - w.shape[-1],), w.dtype)], axis=-1)


def kernel(x, meta_tokens, rel_bias_table, ffn1_pre_g, ffn1_post_g, ffn1_w_gate, ffn1_w_up, ffn1_w_down, mix_pre_g, mix_post_g, w_in, b_gate, da_lambda_q1, da_lambda_k1, da_lambda_q2, da_lambda_k2, da_sub_g, mla_q_norm_g, mla_kv_norm_g, mla_w_uq, mla_w_ukv, w_branch_da, w_branch_mla, w_out, ffn2_pre_g, ffn2_post_g, ffn2_w_gate, ffn2_w_up, ffn2_w_down):
    b, s_len, d = x.shape
    assert ffn1_pre_g.shape[0] == 1, "single-layer trunk"
    t = _col_tile(s_len, 512)
    row = lambda g: g.reshape(1, -1).astype(F32)

    meta_in = jnp.concatenate([meta_tokens.astype(x.dtype),
                               jnp.zeros((META_BLK - N_META, d), x.dtype)], axis=0)
    ffn1_g = (row(ffn1_pre_g), row(ffn1_post_g))
    f32 = lambda v: v.astype(F32)
    h1_meta, *ffn1_w = _ffn(meta_in, *ffn1_g, f32(ffn1_w_gate), f32(ffn1_w_up), f32(ffn1_w_down),
                            name="ffn1_meta")
    h1_meta = h1_meta.reshape(1, META_BLK, d)
    mixer_w = (w_in, mla_w_uq, mla_w_ukv, w_branch_da, w_branch_mla, w_out)
    h1, w, w_uq_bf16, w_ukv_bf16, wb_da, wb_mla, w_o = _ffn(
        x.reshape(b * s_len, d), *ffn1_g, *ffn1_w, name="ffn1", cast_weights=[f32(v) for v in mixer_w])
    h1 = h1.reshape(b, s_len, d)

    o_k, o_v, o_cq = DA_W, 2 * DA_W, 3 * DA_W
    o_kr = o_cq + MLA_Q_RANK + MLA_KV_RANK
    o_gate = o_kr + MLA_ROPE
    da_w = (row(mix_pre_g), w[:, :o_k].T.astype(BF16), w[:, o_k:o_v].astype(BF16),
            w[:, o_v:o_cq].T.astype(BF16))
    qt_da, k_da, vt_da = _da_proj(h1, *da_w, name="da_in_proj")
    _, k_da_meta, vt_da_meta = _da_proj(h1_meta, *da_w, name="da_in_proj_meta")

    kr = w[:, o_kr:o_gate]
    w_c = jnp.concatenate([w[:, o_cq:o_kr], _pad_lanes(kr, HEAD_W),
                           _pad_lanes(_swap_halves(kr), HEAD_W)], axis=-1).astype(BF16)
    uq = w_uq_bf16.reshape(MLA_Q_RANK, HEADS, MLA_NOPE + MLA_ROPE)
    uq_rope = uq[..., MLA_NOPE:]
    w_uqt = jnp.concatenate([uq[..., :MLA_NOPE], _pad_lanes(uq_rope, HEAD_W),
                             _pad_lanes(_swap_halves(uq_rope), HEAD_W)],
                            axis=-1).reshape(MLA_Q_RANK, HEADS * _UQ_HEAD_W).T.astype(BF16)
    ukv = w_ukv_bf16.reshape(MLA_KV_RANK, HEADS, 2 * HEAD_W)
    w_uk = ukv[..., :HEAD_W].reshape(MLA_KV_RANK, DA_W).astype(BF16)
    w_uvt = ukv[..., HEAD_W:].reshape(MLA_KV_RANK, DA_W).T.astype(BF16)
    mla_w = (w_c, row(mla_q_norm_g), row(mla_kv_norm_g), w_uqt, w_uk, w_uvt)

    def mla_proj(h, pos, name):
        cos, sin = _rope_tables(pos)
        return _mla_proj(h, row(mix_pre_g), cos, sin, cos.T, sin.T, *mla_w, name=name)

    qt_mla, k_mla, vt_mla = mla_proj(h1, N_META + jnp.arange(s_len), "mla_in_proj")
    _, k_mla_meta, vt_mla_meta = mla_proj(h1_meta, jnp.arange(META_BLK), "mla_in_proj_meta")

    bias_d, bias_s, bias_m = _bias_tiles(rel_bias_table.astype(F32), t)
    lam_p = jnp.concatenate([da_lambda_q1, da_lambda_k1, da_lambda_q2, da_lambda_k2], axis=0).astype(F32)
    y_da = _da_attention(qt_da, k_da, vt_da, k_da_meta, vt_da_meta, bias_d, bias_s, bias_m, lam_p,
                         da_sub_g.reshape(HEAD_W, 1).astype(F32), t=t)
    y_mla = _mla_attention(qt_mla, k_mla, vt_mla, k_mla_meta, vt_mla_meta,
                           tq=_col_tile(s_len, 2 * t), tk=t)

    h2, *ffn2_bf16 = _merge(h1, row(mix_pre_g), row(mix_post_g), y_da, y_mla,
                            w[:, o_gate:], row(b_gate), wb_da, wb_mla, w_o,
                            cast_weights=[f32(ffn2_w_gate), f32(ffn2_w_up), f32(ffn2_w_down)])

    out, = _ffn(h2.reshape(b * s_len, d), row(ffn2_pre_g), row(ffn2_post_g), *ffn2_bf16, name="ffn2")
    return out.reshape(b, s_len, d)
```

```python
import functools
import math
from typing import Any, Callable, NamedTuple, Optional

import numpy as np
import jax
import jax.numpy as jnp
from jax import lax
from jax.experimental import pallas as pl
from jax.experimental.pallas import tpu as pltpu

F32 = jnp.float32
BF16 = jnp.bfloat16

N_META = 16
RMS_EPS = 1e-6
NEG_INF = -1e30
HEADS = 8
HEAD_W = 128
DA_HEAD_DIM = 64
DA_W = HEADS * HEAD_W
MLA_Q_RANK = 768
MLA_KV_RANK = 512
MLA_NOPE = 128
MLA_ROPE = 64
MLA_QK_W = 256
ROPE_THETA = 10000.0
REL_BUCKETS = 32
REL_MAX_DIST = 128
LAMBDA_INIT = 0.8 - 0.6 * math.exp(-0.3 * 0)
LOG2_E = math.log2(math.e)

V7X_LANES = 128
V7X_BF16_SUBLANES = 16
V7X_VMEM_BYTES = 64 * 1024 * 1024
VMEM_CAP_BYTES = V7X_VMEM_BYTES - 6 * 1024 * 1024

META_BLK = V7X_LANES

FFN_ROW_TILE_MAX = 896
FFN_FF_TILE_MAX = 512
PROJ_ROW_TILE_MAX = 704
MERGE_ROW_TILE_MAX = 512
MERGE_COL_TILE_MAX = 512
ATTN_KEY_BLOCK_MAX = 512


def _row_tile(rows, max_tile, multiple=V7X_BF16_SUBLANES):
    best = None
    for t in range(multiple, max_tile + 1, multiple):
        if rows % t == 0:
            best = t
    assert best is not None, (rows, max_tile)
    return best


def _col_tile(cols, max_tile):
    return _row_tile(cols, max_tile, V7X_LANES)


def _params(semantics, vmem_bytes):
    return pltpu.CompilerParams(
        dimension_semantics=semantics,
        vmem_limit_bytes=int(min(max(vmem_bytes, 16 * 1024 * 1024), VMEM_CAP_BYTES)))


def _resident(block_shape, index_map):
    return pl.BlockSpec(block_shape, index_map, pipeline_mode=pl.Buffered(1))


def _rms(x, g):
    ms = jnp.mean(x * x, axis=-1, keepdims=True)
    return x * lax.rsqrt(ms + RMS_EPS) * g


def _dot(a, b):
    return jnp.dot(a, b, preferred_element_type=F32)


def _dot_nt(a, b):
    return lax.dot_general(a, b, (((1,), (1,)), ((), ())), preferred_element_type=F32)


def _ffn_kernel(x_ref, pre_ref, post_ref, wg_ref, wu_ref, wd_ref, o_ref, xn_ref):
    k = pl.program_id(1)
    last = pl.num_programs(1) - 1

    def partial_out(xn):
        g = _dot(xn, wg_ref[...])
        u = _dot(xn, wu_ref[...])
        a = (g * jax.nn.sigmoid(g) * u).astype(BF16)
        return _dot(a, wd_ref[...])

    @pl.when(k == 0)
    def _():
        xn = _rms(x_ref[...], pre_ref[...]).astype(BF16)
        xn_ref[...] = xn
        o_ref[...] = partial_out(xn)

    @pl.when(jnp.logical_and(k > 0, k < last))
    def _():
        o_ref[...] += partial_out(xn_ref[...])

    @pl.when(k == last)
    def _():
        f = o_ref[...] + partial_out(xn_ref[...])
        o_ref[...] = x_ref[...] + 0.5 * _rms(f, post_ref[...])


def _ffn(x, pre_g, post_g, wg, wu, wd, *, name):
    rows, d = x.shape
    f = wg.shape[1]
    tm = _row_tile(rows, FFN_ROW_TILE_MAX)
    tf = _col_tile(f, FFN_FF_TILE_MAX)
    assert f // tf >= 2, "the kernel treats the first and the last d_ff tile as different steps"
    vmem = (2 * 2 * tm * d * 4
            + tm * d * 2
            + 2 * 3 * d * tf * 2
            + 4 * tm * tf * 4
            + tm * d * 4)
    return pl.pallas_call(
        _ffn_kernel,
        out_shape=jax.ShapeDtypeStruct((rows, d), F32),
        grid=(rows // tm, f // tf),
        in_specs=[
            pl.BlockSpec((tm, d), lambda i, k: (i, 0)),
            _resident((1, d), lambda i, k: (0, 0)),
            _resident((1, d), lambda i, k: (0, 0)),
            pl.BlockSpec((d, tf), lambda i, k: (0, k)),
            pl.BlockSpec((d, tf), lambda i, k: (0, k)),
            pl.BlockSpec((tf, d), lambda i, k: (k, 0)),
        ],
        out_specs=pl.BlockSpec((tm, d), lambda i, k: (i, 0)),
        scratch_shapes=[pltpu.VMEM((tm, d), BF16)],
        compiler_params=_params(("parallel", "arbitrary"), vmem),
        name=name,
    )(x, pre_g, post_g, wg, wu, wd)


VT_ROWS = HEAD_W + V7X_BF16_SUBLANES


def _store_vt(vt_ref, vt):
    tm = vt.shape[1]
    vt_ref[0, :, 0:HEAD_W, :] = vt.reshape(HEADS, HEAD_W, tm).astype(BF16)
    extra = lax.broadcasted_iota(jnp.int32, (HEADS, VT_ROWS - HEAD_W, tm), 1)
    vt_ref[0, :, HEAD_W:VT_ROWS, :] = jnp.where(extra == 0, 1.0, 0.0).astype(BF16)


def _da_proj_kernel(h_ref, g_ref, wqt_ref, wk_ref, wvt_ref, qt_ref, k_ref, vt_ref):
    hn = _rms(h_ref[...], g_ref[...]).astype(BF16)
    tm = hn.shape[0]
    qt = _dot_nt(wqt_ref[...], hn) * (DA_HEAD_DIM ** -0.5 * LOG2_E)
    qt_ref[0] = qt.reshape(HEADS, HEAD_W, tm).astype(BF16)
    k = _dot(hn, wk_ref[...])
    for h in range(HEADS):
        k_ref[0, h] = k[:, h * HEAD_W:(h + 1) * HEAD_W].astype(BF16)
    _store_vt(vt_ref, _dot_nt(wvt_ref[...], hn))


_C_CKV = MLA_Q_RANK
_C_KR1 = _C_CKV + MLA_KV_RANK
_C_KR2 = _C_KR1 + HEAD_W
_C_END = _C_KR2 + HEAD_W
_UQ_HEAD_W = HEAD_W + 2 * MLA_ROPE


def _mla_proj_kernel(h_ref, g_ref, cos_ref, sin_ref, cost_ref, sint_ref, wc_ref, gq_ref, gkv_ref,
                     wuqt_ref, wuk_ref, wuvt_ref, qt_ref, k_ref, vt_ref):
    hn = _rms(h_ref[...], g_ref[...]).astype(BF16)
    tm = hn.shape[0]
    scale = (MLA_NOPE + MLA_ROPE) ** -0.5 * LOG2_E

    c_all = _dot(hn, wc_ref[...])

    def proj(lo, hi):
        return c_all[:, lo:hi]

    cq = _rms(proj(0, _C_CKV), gq_ref[...]).astype(BF16)
    qt = _dot_nt(wuqt_ref[...], cq)
    cos_t = cost_ref[0:MLA_ROPE, :]
    sin_t = sint_ref[0:MLA_ROPE, :]
    rope_end = HEAD_W + MLA_ROPE
    for h in range(HEADS):
        base = h * _UQ_HEAD_W
        nope = qt[base:base + HEAD_W]
        r1 = qt[base + HEAD_W:base + rope_end]
        r2 = qt[base + rope_end:base + _UQ_HEAD_W]
        qt_ref[0, h, 0:HEAD_W, :] = (nope * scale).astype(BF16)
        qt_ref[0, h, HEAD_W:rope_end, :] = ((r1 * cos_t + r2 * sin_t) * scale).astype(BF16)
        qt_ref[0, h, rope_end:MLA_QK_W, :] = jnp.zeros((MLA_QK_W - rope_end, tm), BF16)

    ckv = _rms(proj(_C_CKV, _C_KR1), gkv_ref[...]).astype(BF16)
    k_nope = _dot(ckv, wuk_ref[...])
    k_rope = (proj(_C_KR1, _C_KR2) * cos_ref[...] + proj(_C_KR2, _C_END) * sin_ref[...]).astype(BF16)
    for h in range(HEADS):
        k_ref[0, h, :, 0:HEAD_W] = k_nope[:, h * HEAD_W:(h + 1) * HEAD_W].astype(BF16)
        k_ref[0, h, :, HEAD_W:MLA_QK_W] = k_rope
    _store_vt(vt_ref, _dot_nt(wuvt_ref[...], ckv))


def _proj_specs(b, lp, tm, q_w):
    nt = lp // tm
    rows = lambda w: pl.BlockSpec((1, HEADS, tm, w), lambda i: (i // nt, 0, i % nt, 0))
    cols = lambda w: pl.BlockSpec((1, HEADS, w, tm), lambda i: (i // nt, 0, 0, i % nt))
    shapes = [jax.ShapeDtypeStruct((b, HEADS, q_w, lp), BF16),
              jax.ShapeDtypeStruct((b, HEADS, lp, q_w), BF16),
              jax.ShapeDtypeStruct((b, HEADS, VT_ROWS, lp), BF16)]
    return shapes, [cols(q_w), rows(q_w), cols(VT_ROWS)]


def _da_proj(h1, g, wqt, wk, wvt, *, name):
    b, lp, d = h1.shape
    tm = _col_tile(lp, PROJ_ROW_TILE_MAX)
    nt = lp // tm
    shapes, out_specs = _proj_specs(b, lp, tm, HEAD_W)
    vmem = (2 * tm * d * 4 + tm * d * 2 + 3 * DA_W * d * 2
            + 2 * 3 * tm * DA_W * 2 + 3 * tm * DA_W * 4)
    return pl.pallas_call(
        _da_proj_kernel,
        out_shape=shapes,
        grid=(b * nt,),
        in_specs=[
            pl.BlockSpec((None, tm, d), lambda i: (i // nt, i % nt, 0)),
            _resident((1, d), lambda i: (0, 0)),
            _resident(wqt.shape, lambda i: (0, 0)),
            _resident(wk.shape, lambda i: (0, 0)),
            _resident(wvt.shape, lambda i: (0, 0)),
        ],
        out_specs=out_specs,
        compiler_params=_params(("parallel",), vmem),
        name=name,
    )(h1, g, wqt, wk, wvt)


def _mla_proj(h1, g, cos, sin, cos_t, sin_t, wc, gq, gkv, wuqt, wuk, wuvt, *, name):
    b, lp, d = h1.shape
    tm = _col_tile(lp, PROJ_ROW_TILE_MAX)
    nt = lp // tm
    shapes, out_specs = _proj_specs(b, lp, tm, MLA_QK_W)
    vmem = (2 * tm * d * 4 + tm * d * 2
            + (d * _C_END + HEADS * _UQ_HEAD_W * MLA_Q_RANK + 2 * DA_W * MLA_KV_RANK) * 2
            + 2 * tm * HEADS * (2 * MLA_QK_W + HEAD_W) * 2
            + 2 * tm * HEADS * _UQ_HEAD_W * 4)
    return pl.pallas_call(
        _mla_proj_kernel,
        out_shape=shapes,
        grid=(b * nt,),
        in_specs=[
            pl.BlockSpec((None, tm, d), lambda i: (i // nt, i % nt, 0)),
            _resident((1, d), lambda i: (0, 0)),
            pl.BlockSpec((tm, HEAD_W), lambda i: (i % nt, 0)),
            pl.BlockSpec((tm, HEAD_W), lambda i: (i % nt, 0)),
            pl.BlockSpec((HEAD_W, tm), lambda i: (0, i % nt)),
            pl.BlockSpec((HEAD_W, tm), lambda i: (0, i % nt)),
            _resident(wc.shape, lambda i: (0, 0)),
            _resident((1, MLA_Q_RANK), lambda i: (0, 0)),
            _resident((1, MLA_KV_RANK), lambda i: (0, 0)),
            _resident(wuqt.shape, lambda i: (0, 0)),
            _resident(wuk.shape, lambda i: (0, 0)),
            _resident(wuvt.shape, lambda i: (0, 0)),
        ],
        out_specs=out_specs,
        compiler_params=_params(("parallel",), vmem),
        name=name,
    )(h1, g, cos, sin, cos_t, sin_t, wc, gq, gkv, wuqt, wuk, wuvt)


def _t5_bucket(rel):
    n = np.maximum(rel, 0)
    max_exact = REL_BUCKETS // 2
    n_f = np.maximum(n, 1).astype(np.float64)
    large = max_exact + (np.log(n_f / max_exact) / math.log(REL_MAX_DIST / max_exact)
                         * (REL_BUCKETS - max_exact)).astype(np.int32)
    large = np.minimum(large, REL_BUCKETS - 1)
    return np.where(n < max_exact, n, large).astype(np.int32)


def _bucket_tiles(t):
    j = np.arange(t)[:, None]
    i = np.arange(t)[None, :]
    diag = np.where(j <= i, _t5_bucket(i - j), -1)
    sub = _t5_bucket(t + i - j)
    assert _t5_bucket(np.array([t + 1]))[0] == REL_BUCKETS - 1
    jm = np.arange(META_BLK)[:, None]
    meta_first = np.where(jm < N_META, _t5_bucket(N_META + i - jm), -1)
    meta_rest = np.where(jm < N_META, REL_BUCKETS - 1, -1) + 0 * i
    return (diag.astype(np.int32), np.stack([np.full_like(sub, -1), sub]).astype(np.int32),
            np.stack([meta_first, meta_rest]).astype(np.int32))


def _bias_kernel(table_ref, bd_ref, bs_ref, bm_ref, od_ref, os_ref, om_ref, *, buckets):
    h = pl.program_id(0)
    far = table_ref[REL_BUCKETS - 1, h]

    def lookup(bucket_ref, out_ref, known):
        rows, cols = known.shape
        for r0 in range(0, rows, V7X_LANES):
            for c0 in range(0, cols, V7X_LANES):
                patch = (slice(r0, r0 + V7X_LANES), slice(c0, c0 + V7X_LANES))
                present = np.unique(known[patch])
                acc = jnp.zeros((V7X_LANES, V7X_LANES), F32)
                if present.size > 1 or 0 <= present[0] < REL_BUCKETS - 1:
                    bucket = bucket_ref[patch]
                    for b in present:
                        if 0 <= b < REL_BUCKETS - 1:
                            acc = jnp.where(bucket == int(b), (table_ref[int(b), h] - far) * LOG2_E, acc)
                    if present[0] < 0:
                        acc = jnp.where(bucket < 0, NEG_INF, acc)
                elif present[0] < 0:
                    acc = jnp.full((V7X_LANES, V7X_LANES), NEG_INF, F32)
                out_ref[patch] = acc

    bd, bs, bm = buckets
    lookup(bd_ref, od_ref.at[0], bd)
    for i in range(2):
        lookup(bs_ref.at[i], os_ref.at[0, i], bs[i])
        lookup(bm_ref.at[i], om_ref.at[0, i], bm[i])


def _bias_tiles(table, t):
    bd, bs, bm = _bucket_tiles(t)
    return pl.pallas_call(
        functools.partial(_bias_kernel, buckets=(bd, bs, bm)),
        out_shape=[jax.ShapeDtypeStruct((HEADS, t, t), F32),
                   jax.ShapeDtypeStruct((HEADS, 2, t, t), F32),
                   jax.ShapeDtypeStruct((HEADS, 2, META_BLK, t), F32)],
        grid=(HEADS,),
        in_specs=[
            pl.BlockSpec(memory_space=pltpu.SMEM),
            pl.BlockSpec((t, t), lambda h: (0, 0)),
            pl.BlockSpec((2, t, t), lambda h: (0, 0, 0)),
            pl.BlockSpec((2, META_BLK, t), lambda h: (0, 0, 0)),
        ],
        out_specs=[pl.BlockSpec((1, t, t), lambda h: (h, 0, 0)),
                   pl.BlockSpec((1, 2, t, t), lambda h: (h, 0, 0, 0)),
                   pl.BlockSpec((1, 2, META_BLK, t), lambda h: (h, 0, 0, 0))],
        compiler_params=_params(("parallel",), 12 * t * t * 4),
        name="rel_bias_tiles",
    )(table, jnp.asarray(bd), jnp.asarray(bs), jnp.asarray(bm))


ATTN_QUERY_CHUNK = 256


ATTN_SCORE_BUFFERS = 4
ATTN_HEADS_PER_STEP = 2


class _Step(NamedTuple):
    q: Callable[[], Any]
    k: Callable[[], Any]
    vt: Callable[[], Any]
    bias: Optional[Callable[[], Any]]
    chunk: int


def _block_steps(n_chunks, q, k, vt, bias=None):
    return [_Step(functools.partial(q, c), functools.partial(k, c), functools.partial(vt, c),
                  functools.partial(bias, c) if bias is not None else None, c)
            for c in range(n_chunks)]


def _issue_scores(s_ref, step, slot):
    k = step.k()
    s_ref[slot, 0:k.shape[0], :] = _dot(k, step.q())


def _prefetch_scores(s_ref, steps):
    for i, step in enumerate(steps[:s_ref.shape[0] - 1]):
        _issue_scores(s_ref, step, i)


def _run_steps(s_ref, steps, following, m_ref, acc_ref):
    n_buf, _, qc = s_ref.shape
    ahead = n_buf - 1
    assert not following or len(steps) % n_buf == 0
    seq = list(steps) + list(following)[:ahead]
    for i, step in enumerate(steps):
        if i + ahead < len(seq):
            _issue_scores(s_ref, seq[i + ahead], (i + ahead) % n_buf)
        vt = step.vt()
        s = s_ref[i % n_buf, 0:vt.shape[1], :]
        if step.bias is not None:
            s = s + step.bias()
        sl = pl.ds(step.chunk * qc, qc)
        m_prev = m_ref[:, sl]
        m_new = jnp.maximum(m_prev, jnp.max(s, axis=0, keepdims=True))
        alpha = jnp.exp2(m_prev - m_new)
        p = jnp.exp2(s - m_new)
        acc_ref[:, sl] = alpha * acc_ref[:, sl] + _dot(vt, p.astype(BF16))
        m_ref[:, sl] = m_new


def _attend_init(m_ref, acc_ref):
    m_ref[...] = jnp.full(m_ref.shape, NEG_INF, F32)
    acc_ref[...] = jnp.zeros(acc_ref.shape, F32)


FAR_UNROLL = 4


def _walk_far_blocks(n_far, body):
    def loop_body(i, carry):
        body(i * FAR_UNROLL, FAR_UNROLL)
        return carry

    lax.fori_loop(0, n_far >> (FAR_UNROLL.bit_length() - 1), loop_body, 0)
    count = FAR_UNROLL // 2
    while count >= 1:
        first = n_far - (n_far & (2 * count - 1))

        @pl.when((n_far & count) != 0)
        def _(first=first, count=count):
            body(first, count)

        count //= 2


def _normalised(acc_ref):
    acc = acc_ref[...]
    return acc[0:HEAD_W] * (1.0 / acc[HEAD_W:HEAD_W + 1])


def _da_kernel(qt_ref, k_ref, vt_ref, km_ref, vtm_ref, bd_ref, bs_ref, bm_ref, lam_ref, subg_ref,
               o_ref, qq_ref, s_ref, m_ref, acc_ref, *, t):
    qi = pl.program_id(2)
    feat = lax.broadcasted_iota(jnp.int32, (HEAD_W, 1), 0)
    for hd in range(ATTN_HEADS_PER_STEP):
        qt = qt_ref[0, hd]
        zero = jnp.zeros_like(qt)
        qq_ref[:, pl.ds(2 * t * hd, t)] = jnp.where(feat < DA_HEAD_DIM, qt, zero)
        qq_ref[:, pl.ds(2 * t * hd + t, t)] = jnp.where(feat >= DA_HEAD_DIM, qt, zero)
    qc = s_ref.shape[2]
    chunks_per_map = t // qc
    chunks_per_head = 2 * chunks_per_map
    n_chunks = ATTN_HEADS_PER_STEP * chunks_per_head

    def head(c):
        return c // chunks_per_head

    def cols(c):
        return pl.ds((c % chunks_per_map) * qc, qc)

    def q_chunk(c):
        return qq_ref[:, pl.ds(c * qc, qc)]

    def block_steps(kb, bias=None, keys=t):
        rows = pl.ds(pl.multiple_of(kb * t, t), keys)
        return _block_steps(n_chunks, q_chunk, lambda c: k_ref[0, head(c), rows, :],
                            lambda c: vt_ref[0, head(c), :, rows], bias)

    state = (m_ref, acc_ref)
    n_far = jnp.maximum(qi - 1, 0)
    near_sel = jnp.minimum(qi, 1)
    prev_kb = jnp.maximum(qi - 1, 0)
    meta_steps = _block_steps(n_chunks, q_chunk, lambda c: km_ref[0, head(c)],
                              lambda c: vtm_ref[0, head(c)],
                              lambda c: bm_ref[head(c), near_sel, :, cols(c)])
    _attend_init(*state)
    _prefetch_scores(s_ref, meta_steps)
    _run_steps(s_ref, meta_steps, block_steps(jnp.where(qi >= 2, 0, qi)), *state)

    def far(first, count):
        steps = [st for u in range(count) for st in block_steps(first + u)]
        last = first + count - 1
        _run_steps(s_ref, steps, block_steps(jnp.where(last == n_far - 1, qi, last + 1)), *state)

    _walk_far_blocks(n_far, far)

    diag = []
    for c in range(n_chunks):
        keys = (c % chunks_per_map + 1) * qc
        diag.append(block_steps(qi, lambda c, keys=keys: bd_ref[head(c), 0:keys, cols(c)], keys)[c])
    prev = block_steps(prev_kb, lambda c: bs_ref[head(c), near_sel, :, cols(c)])
    _run_steps(s_ref, diag + prev, (), *state)

    ot = _normalised(acc_ref)
    lp = lam_ref[...]
    lam = (jnp.exp(jnp.sum(lp[0:1] * lp[1:2], axis=-1, keepdims=True))
           - jnp.exp(jnp.sum(lp[2:3] * lp[3:4], axis=-1, keepdims=True)) + LAMBDA_INIT)
    for hd in range(ATTN_HEADS_PER_STEP):
        o1 = ot[:, 2 * t * hd:2 * t * hd + t]
        o2 = ot[:, 2 * t * hd + t:2 * t * (hd + 1)]
        d = o1 - lam * o2
        ms = jnp.mean(d * d, axis=0, keepdims=True)
        y = d * lax.rsqrt(ms + RMS_EPS) * subg_ref[...] * (1.0 - LAMBDA_INIT)
        o_ref[0, :, hd * HEAD_W:(hd + 1) * HEAD_W] = y.T.astype(BF16)


def _attn_specs(t, s_len, q_w):
    hp = ATTN_HEADS_PER_STEP
    return [
        pl.BlockSpec((1, hp, q_w, t), lambda bi, h, qi: (bi, h, 0, qi)),
        pl.BlockSpec((1, hp, s_len, q_w), lambda bi, h, qi: (bi, h, 0, 0)),
        pl.BlockSpec((1, hp, VT_ROWS, s_len), lambda bi, h, qi: (bi, h, 0, 0)),
        pl.BlockSpec((1, hp, META_BLK, q_w), lambda bi, h, qi: (0, h, 0, 0)),
        pl.BlockSpec((1, hp, VT_ROWS, META_BLK), lambda bi, h, qi: (0, h, 0, 0)),
    ]


def _da_attention(qt, k, vt, k_meta, vt_meta, bias_d, bias_s, bias_m, lam_p, sub_g, *, t):
    b, _, s_len, _ = k.shape
    nq = s_len // t
    hp = ATTN_HEADS_PER_STEP
    qc = min(ATTN_QUERY_CHUNK, t)
    vmem = hp * (2 * (s_len + META_BLK) * (HEAD_W + VT_ROWS) * 2 + 2 * 3 * t * t * 4
                 + 2 * 2 * t * META_BLK * 4
                 + 2 * 2 * t * HEAD_W * 2 + 2 * t * HEAD_W * 2 + (VT_ROWS + 8) * 2 * t * 4
                 ) + (ATTN_SCORE_BUFFERS + 4) * t * qc * 4
    return pl.pallas_call(
        functools.partial(_da_kernel, t=t),
        out_shape=jax.ShapeDtypeStruct((b, s_len, DA_W), BF16),
        grid=(b, HEADS // hp, nq),
        in_specs=_attn_specs(t, s_len, HEAD_W) + [
            pl.BlockSpec((hp, t, t), lambda bi, h, qi: (h, 0, 0)),
            pl.BlockSpec((hp, 2, t, t), lambda bi, h, qi: (h, 0, 0, 0)),
            pl.BlockSpec((hp, 2, META_BLK, t), lambda bi, h, qi: (h, 0, 0, 0)),
            _resident((4, DA_HEAD_DIM), lambda bi, h, qi: (0, 0)),
            _resident((HEAD_W, 1), lambda bi, h, qi: (0, 0)),
        ],
        out_specs=pl.BlockSpec((1, t, hp * HEAD_W), lambda bi, h, qi: (bi, qi, h)),
        scratch_shapes=[pltpu.VMEM((HEAD_W, hp * 2 * t), BF16),
                        pltpu.VMEM((ATTN_SCORE_BUFFERS, t, qc), F32),
                        pltpu.VMEM((1, hp * 2 * t), F32), pltpu.VMEM((VT_ROWS, hp * 2 * t), F32)],
        compiler_params=_params(("parallel", "parallel", "arbitrary"), vmem),
        name="diff_attention",
    )(qt, k, vt, k_meta, vt_meta, bias_d, bias_s, bias_m, lam_p, sub_g)


def _mla_kernel(qt_ref, k_ref, vt_ref, km_ref, vtm_ref, o_ref, s_ref, m_ref, acc_ref, *, tq, tk):
    qi = pl.program_id(2)
    qc = s_ref.shape[2]
    diag_blocks = tq // tk
    chunks_per_head = tq // qc
    n_chunks = ATTN_HEADS_PER_STEP * chunks_per_head

    def head(c):
        return c // chunks_per_head

    def q_chunk(c):
        return qt_ref[0, head(c), :, pl.ds((c % chunks_per_head) * qc, qc)]

    def block_steps(kb, bias=None, keys=tk):
        rows = pl.ds(pl.multiple_of(kb * tk, tk), keys)
        return _block_steps(n_chunks, q_chunk, lambda c: k_ref[0, head(c), rows, :],
                            lambda c: vt_ref[0, head(c), :, rows], bias)

    state = (m_ref, acc_ref)

    def meta_mask(c):
        key = lax.broadcasted_iota(jnp.int32, (META_BLK, qc), 0)
        return jnp.where(key < N_META, 0.0, NEG_INF)

    meta_steps = _block_steps(n_chunks, q_chunk, lambda c: km_ref[0, head(c)],
                              lambda c: vtm_ref[0, head(c)], meta_mask)
    _attend_init(*state)
    _prefetch_scores(s_ref, meta_steps)
    _run_steps(s_ref, meta_steps, block_steps(0), *state)

    def far(first, count):
        steps = [st for u in range(count) for st in block_steps(first + u)]
        _run_steps(s_ref, steps, block_steps(first + count), *state)

    first_diag = qi * diag_blocks
    _walk_far_blocks(first_diag, far)

    diag = []
    for d in range(diag_blocks):
        for c in range(n_chunks):
            cq = (c % chunks_per_head) * qc
            keys = min(tk, cq + qc - d * tk)
            if keys <= 0:
                continue

            def causal(c, d=d, keys=keys, cq=cq):
                key = lax.broadcasted_iota(jnp.int32, (keys, qc), 0) + d * tk
                query = lax.broadcasted_iota(jnp.int32, (keys, qc), 1) + cq
                return jnp.where(key <= query, 0.0, NEG_INF)

            diag.append(block_steps(first_diag + d, causal, keys)[c])
    _run_steps(s_ref, diag, (), *state)

    ot = _normalised(acc_ref)
    for hd in range(ATTN_HEADS_PER_STEP):
        o_ref[0, :, hd * HEAD_W:(hd + 1) * HEAD_W] = ot[:, hd * tq:(hd + 1) * tq].T.astype(BF16)


def _mla_attention(qt, k, vt, k_meta, vt_meta, *, tq, tk):
    b, _, s_len, _ = k.shape
    nq = s_len // tq
    hp = ATTN_HEADS_PER_STEP
    qc = min(ATTN_QUERY_CHUNK, tq)
    vmem = hp * (2 * (s_len + META_BLK) * (MLA_QK_W + VT_ROWS) * 2 + 2 * tq * MLA_QK_W * 2
                 + 2 * tq * HEAD_W * 2 + (VT_ROWS + 8) * tq * 4 + tq * HEAD_W * 4
                 ) + (ATTN_SCORE_BUFFERS + 4) * tk * qc * 4
    return pl.pallas_call(
        functools.partial(_mla_kernel, tq=tq, tk=tk),
        out_shape=jax.ShapeDtypeStruct((b, s_len, DA_W), BF16),
        grid=(b, HEADS // hp, nq),
        in_specs=_attn_specs(tq, s_len, MLA_QK_W),
        out_specs=pl.BlockSpec((1, tq, hp * HEAD_W), lambda bi, h, qi: (bi, qi, h)),
        scratch_shapes=[pltpu.VMEM((ATTN_SCORE_BUFFERS, tk, qc), F32),
                        pltpu.VMEM((1, hp * tq), F32), pltpu.VMEM((VT_ROWS, hp * tq), F32)],
        compiler_params=_params(("parallel", "parallel", "arbitrary"), vmem),
        name="mla_attention",
    )(qt, k, vt, k_meta, vt_meta)


def _merge_kernel(h_ref, pre_ref, post_ref, yda_ref, ymla_ref, wg0_ref, wg1_ref, b0_ref, b1_ref,
                  wbd_ref, wbm_ref, wo_ref, o_ref, hn_ref):
    j = pl.program_id(1)
    last = pl.num_programs(1) - 1

    def partial_out(hn):
        g0 = jax.nn.sigmoid(_dot(hn, wg0_ref[...]) + b0_ref[...])
        g1 = jax.nn.sigmoid(_dot(hn, wg1_ref[...]) + b1_ref[...])
        merged = g0 * _dot(yda_ref[0], wbd_ref[...]) + g1 * _dot(ymla_ref[0], wbm_ref[...])
        return _dot(merged.astype(BF16), wo_ref[...])

    @pl.when(j == 0)
    def _():
        hn = _rms(h_ref[0], pre_ref[...]).astype(BF16)
        hn_ref[...] = hn
        o_ref[0] = partial_out(hn)

    @pl.when(jnp.logical_and(j > 0, j < last))
    def _():
        o_ref[0] += partial_out(hn_ref[...])

    @pl.when(j == last)
    def _():
        m = o_ref[0] + partial_out(hn_ref[...])
        o_ref[0] = h_ref[0] + _rms(m, post_ref[...])


def _merge(h1, pre_g, post_g, y_da, y_mla, w_gate, b_gate, wb_da, wb_mla, w_out):
    b, s_len, d = h1.shape
    yw = y_da.shape[-1]
    tm = _row_tile(s_len, MERGE_ROW_TILE_MAX)
    tn = _col_tile(d, MERGE_COL_TILE_MAX)
    nt = s_len // tm
    nj = d // tn
    assert nj >= 2, "the kernel treats the first and the last column tile as different steps"
    vmem = (2 * 2 * tm * d * 4 + tm * d * 2 + 2 * 2 * tm * yw * 2
            + 2 * (3 * d * tn + 2 * yw * tn) * 2 + 6 * tm * tn * 4 + tm * d * 4)
    return pl.pallas_call(
        _merge_kernel,
        out_shape=jax.ShapeDtypeStruct((b, s_len, d), F32),
        grid=(b * nt, nj),
        in_specs=[
            pl.BlockSpec((1, tm, d), lambda i, j: (i // nt, i % nt, 0)),
            _resident((1, d), lambda i, j: (0, 0)),
            _resident((1, d), lambda i, j: (0, 0)),
            pl.BlockSpec((1, tm, yw), lambda i, j: (i // nt, i % nt, 0)),
            pl.BlockSpec((1, tm, yw), lambda i, j: (i // nt, i % nt, 0)),
            pl.BlockSpec((d, tn), lambda i, j: (0, j)),
            pl.BlockSpec((d, tn), lambda i, j: (0, nj + j)),
            pl.BlockSpec((1, tn), lambda i, j: (0, j)),
            pl.BlockSpec((1, tn), lambda i, j: (0, nj + j)),
            pl.BlockSpec((yw, tn), lambda i, j: (0, j)),
            pl.BlockSpec((yw, tn), lambda i, j: (0, j)),
            pl.BlockSpec((tn, d), lambda i, j: (j, 0)),
        ],
        out_specs=pl.BlockSpec((1, tm, d), lambda i, j: (i // nt, i % nt, 0)),
        scratch_shapes=[pltpu.VMEM((tm, d), BF16)],
        compiler_params=_params(("parallel", "arbitrary"), vmem),
        name="gated_merge_out_proj",
    )(h1, pre_g, post_g, y_da, y_mla, w_gate, w_gate, b_gate, b_gate, wb_da, wb_mla, w_out)


def _rope_tables(pos):
    half = MLA_ROPE // 2
    pos = pos.astype(F32)
    inv = ROPE_THETA ** (-jnp.arange(half, dtype=F32) * 2.0 / MLA_ROPE)
    ang = pos[:, None] * inv[None, :]
    cos, sin = jnp.cos(ang), jnp.sin(ang)
    pad = jnp.zeros((pos.shape[0], HEAD_W - MLA_ROPE), F32)
    return (jnp.concatenate([cos, cos, pad], axis=-1),
            jnp.concatenate([-sin, sin, pad], axis=-1))


def _swap_halves(w):
    half = w.shape[-1] // 2
    return jnp.concatenate([w[..., half:], w[..., :half]], axis=-1)


def _pad_lanes(w, width):
    return jnp.concatenate([w, jnp.zeros(w.shape[:-1] + (width - w.shape[-1],), w.dtype)], axis=-1)


def kernel(x, meta_tokens, rel_bias_table, ffn1_pre_g, ffn1_post_g, ffn1_w_gate, ffn1_w_up, ffn1_w_down, mix_pre_g, mix_post_g, w_in, b_gate, da_lambda_q1, da_lambda_k1, da_lambda_q2, da_lambda_k2, da_sub_g, mla_q_norm_g, mla_kv_norm_g, mla_w_uq, mla_w_ukv, w_branch_da, w_branch_mla, w_out, ffn2_pre_g, ffn2_post_g, ffn2_w_gate, ffn2_w_up, ffn2_w_down):
    b, s_len, d = x.shape
    assert ffn1_pre_g.shape[0] == 1, "single-layer trunk"
    t = _col_tile(s_len, ATTN_KEY_BLOCK_MAX)
    row = lambda g: g.reshape(1, -1).astype(F32)

    meta_in = jnp.concatenate([meta_tokens.astype(x.dtype),
                               jnp.zeros((META_BLK - N_META, d), x.dtype)], axis=0)
    ffn1_w = (row(ffn1_pre_g), row(ffn1_post_g), ffn1_w_gate[0].astype(BF16),
              ffn1_w_up[0].astype(BF16), ffn1_w_down[0].astype(BF16))
    h1 = _ffn(x.reshape(b * s_len, d), *ffn1_w, name="ffn1").reshape(b, s_len, d)
    h1_meta = _ffn(meta_in, *ffn1_w, name="ffn1_meta").reshape(1, META_BLK, d)

    w = w_in[0]
    o_k, o_v, o_cq = DA_W, 2 * DA_W, 3 * DA_W
    o_kr = o_cq + MLA_Q_RANK + MLA_KV_RANK
    o_gate = o_kr + MLA_ROPE
    da_w = (row(mix_pre_g), w[:, :o_k].T.astype(BF16), w[:, o_k:o_v].astype(BF16),
            w[:, o_v:o_cq].T.astype(BF16))
    qt_da, k_da, vt_da = _da_proj(h1, *da_w, name="da_in_proj")
    _, k_da_meta, vt_da_meta = _da_proj(h1_meta, *da_w, name="da_in_proj_meta")

    kr = w[:, o_kr:o_gate]
    w_c = jnp.concatenate([w[:, o_cq:o_kr], _pad_lanes(kr, HEAD_W),
                           _pad_lanes(_swap_halves(kr), HEAD_W)], axis=-1).astype(BF16)
    uq = mla_w_uq[0].reshape(MLA_Q_RANK, HEADS, MLA_NOPE + MLA_ROPE)
    uq_rope = uq[..., MLA_NOPE:]
    w_uqt = jnp.concatenate([uq[..., :MLA_NOPE], uq_rope, _swap_halves(uq_rope)],
                            axis=-1).reshape(MLA_Q_RANK, HEADS * _UQ_HEAD_W).T.astype(BF16)
    ukv = mla_w_ukv[0].reshape(MLA_KV_RANK, HEADS, 2 * HEAD_W)
    w_uk = ukv[..., :HEAD_W].reshape(MLA_KV_RANK, DA_W).astype(BF16)
    w_uvt = ukv[..., HEAD_W:].reshape(MLA_KV_RANK, DA_W).T.astype(BF16)
    mla_w = (w_c, row(mla_q_norm_g), row(mla_kv_norm_g), w_uqt, w_uk, w_uvt)

    def mla_proj(h, pos, name):
        cos, sin = _rope_tables(pos)
        return _mla_proj(h, row(mix_pre_g), cos, sin, cos.T, sin.T, *mla_w, name=name)

    qt_mla, k_mla, vt_mla = mla_proj(h1, N_META + jnp.arange(s_len), "mla_in_proj")
    _, k_mla_meta, vt_mla_meta = mla_proj(h1_meta, jnp.arange(META_BLK), "mla_in_proj_meta")

    bias_d, bias_s, bias_m = _bias_tiles(rel_bias_table.astype(F32), t)
    lam_p = jnp.concatenate([da_lambda_q1, da_lambda_k1, da_lambda_q2, da_lambda_k2], axis=0).astype(F32)
    y_da = _da_attention(qt_da, k_da, vt_da, k_da_meta, vt_da_meta, bias_d, bias_s, bias_m, lam_p,
                         da_sub_g.reshape(HEAD_W, 1).astype(F32), t=t)
    y_mla = _mla_attention(qt_mla, k_mla, vt_mla, k_mla_meta, vt_mla_meta,
                           tq=_col_tile(s_len, 2 * t), tk=t)

    h2 = _merge(h1, row(mix_pre_g), row(mix_post_g), y_da, y_mla,
                w[:, o_gate:].astype(BF16), row(b_gate),
                w_branch_da[0].astype(BF16), w_branch_mla[0].astype(BF16), w_out[0].astype(BF16))

    out = _ffn(h2.reshape(b * s_len, d), row(ffn2_pre_g), row(ffn2_post_g),
               ffn2_w_gate[0].astype(BF16), ffn2_w_up[0].astype(BF16), ffn2_w_down[0].astype(BF16),
               name="ffn2")
    return out.reshape(b, s_len, d)
```

```python
import functools
import math
from typing import Any, Callable, NamedTuple, Optional

import numpy as np
import jax
import jax.numpy as jnp
from jax import lax
from jax.experimental import pallas as pl
from jax.experimental.pallas import tpu as pltpu

F32 = jnp.float32
BF16 = jnp.bfloat16

N_META = 16
RMS_EPS = 1e-6
NEG_INF = -1e30
HEADS = 8
HEAD_W = 128
DA_HEAD_DIM = 64
DA_W = HEADS * HEAD_W
MLA_Q_RANK = 768
MLA_KV_RANK = 512
MLA_NOPE = 128
MLA_ROPE = 64
MLA_QK_W = 256
ROPE_THETA = 10000.0
REL_BUCKETS = 32
REL_MAX_DIST = 128
LAMBDA_INIT = 0.8 - 0.6 * math.exp(-0.3 * 0)
LOG2_E = math.log2(math.e)

V7X_LANES = 128
V7X_BF16_SUBLANES = 16
V7X_VMEM_BYTES = 64 * 1024 * 1024
VMEM_CAP_BYTES = V7X_VMEM_BYTES - 6 * 1024 * 1024

META_BLK = V7X_LANES

FFN_ROW_TILE_MAX = 1024
FFN_FF_TILE_MAX = 512
PROJ_ROW_TILE_MAX = 704
MERGE_ROW_TILE_MAX = 512
MERGE_COL_TILE_MAX = 512
ATTN_KEY_BLOCK_MAX = 512


def _row_tile(rows, max_tile, multiple=V7X_BF16_SUBLANES):
    best = None
    for t in range(multiple, max_tile + 1, multiple):
        if rows % t == 0:
            best = t
    assert best is not None, (rows, max_tile)
    return best


def _col_tile(cols, max_tile):
    return _row_tile(cols, max_tile, V7X_LANES)


def _params(semantics, vmem_bytes):
    return pltpu.CompilerParams(
        dimension_semantics=semantics,
        vmem_limit_bytes=int(min(max(vmem_bytes, 16 * 1024 * 1024), VMEM_CAP_BYTES)))


def _resident(block_shape, index_map):
    return pl.BlockSpec(block_shape, index_map, pipeline_mode=pl.Buffered(1))


def _rms(x, g):
    ms = jnp.mean(x * x, axis=-1, keepdims=True)
    return x * lax.rsqrt(ms + RMS_EPS) * g


def _dot(a, b):
    return jnp.dot(a, b, preferred_element_type=F32)


def _dot_nt(a, b):
    return lax.dot_general(a, b, (((1,), (1,)), ((), ())), preferred_element_type=F32)


def _ffn_kernel(x_ref, pre_ref, post_ref, wg_ref, wu_ref, wd_ref, o_ref, xn_ref):
    k = pl.program_id(1)
    last = pl.num_programs(1) - 1

    def partial_out(xn):
        g = _dot(xn, wg_ref[...])
        u = _dot(xn, wu_ref[...])
        a = (g * jax.nn.sigmoid(g) * u).astype(BF16)
        return _dot(a, wd_ref[...])

    @pl.when(k == 0)
    def _():
        xn = _rms(x_ref[...], pre_ref[...]).astype(BF16)
        xn_ref[...] = xn
        o_ref[...] = partial_out(xn)

    @pl.when(jnp.logical_and(k > 0, k < last))
    def _():
        o_ref[...] += partial_out(xn_ref[...])

    @pl.when(k == last)
    def _():
        f = o_ref[...] + partial_out(xn_ref[...])
        o_ref[...] = x_ref[...] + 0.5 * _rms(f, post_ref[...])


def _ffn(x, pre_g, post_g, wg, wu, wd, *, name):
    rows, d = x.shape
    f = wg.shape[1]
    tm = _row_tile(rows, FFN_ROW_TILE_MAX)
    tf = _col_tile(f, FFN_FF_TILE_MAX)
    assert f // tf >= 2, "the kernel treats the first and the last d_ff tile as different steps"
    vmem = (3 * tm * d * 4
            + tm * d * 2
            + 2 * 3 * d * tf * 2
            + 4 * tm * tf * 4
            + tm * d * 4)
    return pl.pallas_call(
        _ffn_kernel,
        out_shape=jax.ShapeDtypeStruct((rows, d), F32),
        grid=(rows // tm, f // tf),
        in_specs=[
            pl.BlockSpec((tm, d), lambda i, k: (i, 0), pipeline_mode=pl.Buffered(1)),
            _resident((1, d), lambda i, k: (0, 0)),
            _resident((1, d), lambda i, k: (0, 0)),
            pl.BlockSpec((d, tf), lambda i, k: (0, k)),
            pl.BlockSpec((d, tf), lambda i, k: (0, k)),
            pl.BlockSpec((tf, d), lambda i, k: (k, 0)),
        ],
        out_specs=pl.BlockSpec((tm, d), lambda i, k: (i, 0)),
        scratch_shapes=[pltpu.VMEM((tm, d), BF16)],
        compiler_params=_params(("parallel", "arbitrary"), vmem),
        name=name,
    )(x, pre_g, post_g, wg, wu, wd)


VT_ROWS = HEAD_W + V7X_BF16_SUBLANES


def _store_vt(vt_ref, vt):
    tm = vt.shape[1]
    vt_ref[0, :, 0:HEAD_W, :] = vt.reshape(HEADS, HEAD_W, tm).astype(BF16)
    extra = lax.broadcasted_iota(jnp.int32, (HEADS, VT_ROWS - HEAD_W, tm), 1)
    vt_ref[0, :, HEAD_W:VT_ROWS, :] = jnp.where(extra == 0, 1.0, 0.0).astype(BF16)


def _da_proj_kernel(h_ref, g_ref, wqt_ref, wk_ref, wvt_ref, qt_ref, k_ref, vt_ref):
    hn = _rms(h_ref[...], g_ref[...]).astype(BF16)
    tm = hn.shape[0]
    qt = _dot_nt(wqt_ref[...], hn) * (DA_HEAD_DIM ** -0.5 * LOG2_E)
    qt_ref[0] = qt.reshape(HEADS, HEAD_W, tm).astype(BF16)
    k = _dot(hn, wk_ref[...])
    for h in range(HEADS):
        k_ref[0, h] = k[:, h * HEAD_W:(h + 1) * HEAD_W].astype(BF16)
    _store_vt(vt_ref, _dot_nt(wvt_ref[...], hn))


_C_CKV = MLA_Q_RANK
_C_KR1 = _C_CKV + MLA_KV_RANK
_C_KR2 = _C_KR1 + HEAD_W
_C_END = _C_KR2 + HEAD_W
_UQ_HEAD_W = HEAD_W + 2 * MLA_ROPE


def _mla_proj_kernel(h_ref, g_ref, cos_ref, sin_ref, cost_ref, sint_ref, wc_ref, gq_ref, gkv_ref,
                     wuqt_ref, wuk_ref, wuvt_ref, qt_ref, k_ref, vt_ref):
    hn = _rms(h_ref[...], g_ref[...]).astype(BF16)
    tm = hn.shape[0]
    scale = (MLA_NOPE + MLA_ROPE) ** -0.5 * LOG2_E

    c_all = _dot(hn, wc_ref[...])

    def proj(lo, hi):
        return c_all[:, lo:hi]

    cq = _rms(proj(0, _C_CKV), gq_ref[...]).astype(BF16)
    qt = _dot_nt(wuqt_ref[...], cq)
    cos_t = cost_ref[0:MLA_ROPE, :]
    sin_t = sint_ref[0:MLA_ROPE, :]
    rope_end = HEAD_W + MLA_ROPE
    for h in range(HEADS):
        base = h * _UQ_HEAD_W
        nope = qt[base:base + HEAD_W]
        r1 = qt[base + HEAD_W:base + rope_end]
        r2 = qt[base + rope_end:base + _UQ_HEAD_W]
        qt_ref[0, h, 0:HEAD_W, :] = (nope * scale).astype(BF16)
        qt_ref[0, h, HEAD_W:rope_end, :] = ((r1 * cos_t + r2 * sin_t) * scale).astype(BF16)
        qt_ref[0, h, rope_end:MLA_QK_W, :] = jnp.zeros((MLA_QK_W - rope_end, tm), BF16)

    ckv = _rms(proj(_C_CKV, _C_KR1), gkv_ref[...]).astype(BF16)
    k_nope = _dot(ckv, wuk_ref[...])
    k_rope = (proj(_C_KR1, _C_KR2) * cos_ref[...] + proj(_C_KR2, _C_END) * sin_ref[...]).astype(BF16)
    for h in range(HEADS):
        k_ref[0, h, :, 0:HEAD_W] = k_nope[:, h * HEAD_W:(h + 1) * HEAD_W].astype(BF16)
        k_ref[0, h, :, HEAD_W:MLA_QK_W] = k_rope
    _store_vt(vt_ref, _dot_nt(wuvt_ref[...], ckv))


def _proj_specs(b, lp, tm, q_w):
    nt = lp // tm
    rows = lambda w: pl.BlockSpec((1, HEADS, tm, w), lambda i: (i // nt, 0, i % nt, 0))
    cols = lambda w: pl.BlockSpec((1, HEADS, w, tm), lambda i: (i // nt, 0, 0, i % nt))
    shapes = [jax.ShapeDtypeStruct((b, HEADS, q_w, lp), BF16),
              jax.ShapeDtypeStruct((b, HEADS, lp, q_w), BF16),
              jax.ShapeDtypeStruct((b, HEADS, VT_ROWS, lp), BF16)]
    return shapes, [cols(q_w), rows(q_w), cols(VT_ROWS)]


def _da_proj(h1, g, wqt, wk, wvt, *, name):
    b, lp, d = h1.shape
    tm = _col_tile(lp, PROJ_ROW_TILE_MAX)
    nt = lp // tm
    shapes, out_specs = _proj_specs(b, lp, tm, HEAD_W)
    vmem = (2 * tm * d * 4 + tm * d * 2 + 3 * DA_W * d * 2
            + 2 * 3 * tm * DA_W * 2 + 3 * tm * DA_W * 4)
    return pl.pallas_call(
        _da_proj_kernel,
        out_shape=shapes,
        grid=(b * nt,),
        in_specs=[
            pl.BlockSpec((None, tm, d), lambda i: (i // nt, i % nt, 0)),
            _resident((1, d), lambda i: (0, 0)),
            _resident(wqt.shape, lambda i: (0, 0)),
            _resident(wk.shape, lambda i: (0, 0)),
            _resident(wvt.shape, lambda i: (0, 0)),
        ],
        out_specs=out_specs,
        compiler_params=_params(("parallel",), vmem),
        name=name,
    )(h1, g, wqt, wk, wvt)


def _mla_proj(h1, g, cos, sin, cos_t, sin_t, wc, gq, gkv, wuqt, wuk, wuvt, *, name):
    b, lp, d = h1.shape
    tm = _col_tile(lp, PROJ_ROW_TILE_MAX)
    nt = lp // tm
    shapes, out_specs = _proj_specs(b, lp, tm, MLA_QK_W)
    vmem = (2 * tm * d * 4 + tm * d * 2
            + (d * _C_END + HEADS * _UQ_HEAD_W * MLA_Q_RANK + 2 * DA_W * MLA_KV_RANK) * 2
            + 2 * tm * HEADS * (2 * MLA_QK_W + HEAD_W) * 2
            + 2 * tm * HEADS * _UQ_HEAD_W * 4)
    return pl.pallas_call(
        _mla_proj_kernel,
        out_shape=shapes,
        grid=(b * nt,),
        in_specs=[
            pl.BlockSpec((None, tm, d), lambda i: (i // nt, i % nt, 0)),
            _resident((1, d), lambda i: (0, 0)),
            pl.BlockSpec((tm, HEAD_W), lambda i: (i % nt, 0)),
            pl.BlockSpec((tm, HEAD_W), lambda i: (i % nt, 0)),
            pl.BlockSpec((HEAD_W, tm), lambda i: (0, i % nt)),
            pl.BlockSpec((HEAD_W, tm), lambda i: (0, i % nt)),
            _resident(wc.shape, lambda i: (0, 0)),
            _resident((1, MLA_Q_RANK), lambda i: (0, 0)),
            _resident((1, MLA_KV_RANK), lambda i: (0, 0)),
            _resident(wuqt.shape, lambda i: (0, 0)),
            _resident(wuk.shape, lambda i: (0, 0)),
            _resident(wuvt.shape, lambda i: (0, 0)),
        ],
        out_specs=out_specs,
        compiler_params=_params(("parallel",), vmem),
        name=name,
    )(h1, g, cos, sin, cos_t, sin_t, wc, gq, gkv, wuqt, wuk, wuvt)


def _t5_bucket(rel):
    n = np.maximum(rel, 0)
    max_exact = REL_BUCKETS // 2
    n_f = np.maximum(n, 1).astype(np.float64)
    large = max_exact + (np.log(n_f / max_exact) / math.log(REL_MAX_DIST / max_exact)
                         * (REL_BUCKETS - max_exact)).astype(np.int32)
    large = np.minimum(large, REL_BUCKETS - 1)
    return np.where(n < max_exact, n, large).astype(np.int32)


def _bucket_tiles(t):
    j = np.arange(t)[:, None]
    i = np.arange(t)[None, :]
    diag = np.where(j <= i, _t5_bucket(i - j), -1)
    sub = _t5_bucket(t + i - j)
    assert _t5_bucket(np.array([t + 1]))[0] == REL_BUCKETS - 1
    jm = np.arange(META_BLK)[:, None]
    meta_first = np.where(jm < N_META, _t5_bucket(N_META + i - jm), -1)
    meta_rest = np.where(jm < N_META, REL_BUCKETS - 1, -1) + 0 * i
    return (diag.astype(np.int32), np.stack([np.full_like(sub, -1), sub]).astype(np.int32),
            np.stack([meta_first, meta_rest]).astype(np.int32))


def _bias_kernel(table_ref, bd_ref, bs_ref, bm_ref, od_ref, os_ref, om_ref, *, buckets):
    h = pl.program_id(0)
    far = table_ref[REL_BUCKETS - 1, h]

    def lookup(bucket_ref, out_ref, known):
        rows, cols = known.shape
        for r0 in range(0, rows, V7X_LANES):
            for c0 in range(0, cols, V7X_LANES):
                patch = (slice(r0, r0 + V7X_LANES), slice(c0, c0 + V7X_LANES))
                present = np.unique(known[patch])
                acc = jnp.zeros((V7X_LANES, V7X_LANES), F32)
                if present.size > 1 or 0 <= present[0] < REL_BUCKETS - 1:
                    bucket = bucket_ref[patch]
                    for b in present:
                        if 0 <= b < REL_BUCKETS - 1:
                            acc = jnp.where(bucket == int(b), (table_ref[int(b), h] - far) * LOG2_E, acc)
                    if present[0] < 0:
                        acc = jnp.where(bucket < 0, NEG_INF, acc)
                elif present[0] < 0:
                    acc = jnp.full((V7X_LANES, V7X_LANES), NEG_INF, F32)
                out_ref[patch] = acc

    bd, bs, bm = buckets
    lookup(bd_ref, od_ref.at[0], bd)
    for i in range(2):
        lookup(bs_ref.at[i], os_ref.at[0, i], bs[i])
        lookup(bm_ref.at[i], om_ref.at[0, i], bm[i])


def _bias_tiles(table, t):
    bd, bs, bm = _bucket_tiles(t)
    return pl.pallas_call(
        functools.partial(_bias_kernel, buckets=(bd, bs, bm)),
        out_shape=[jax.ShapeDtypeStruct((HEADS, t, t), F32),
                   jax.ShapeDtypeStruct((HEADS, 2, t, t), F32),
                   jax.ShapeDtypeStruct((HEADS, 2, META_BLK, t), F32)],
        grid=(HEADS,),
        in_specs=[
            pl.BlockSpec(memory_space=pltpu.SMEM),
            pl.BlockSpec((t, t), lambda h: (0, 0)),
            pl.BlockSpec((2, t, t), lambda h: (0, 0, 0)),
            pl.BlockSpec((2, META_BLK, t), lambda h: (0, 0, 0)),
        ],
        out_specs=[pl.BlockSpec((1, t, t), lambda h: (h, 0, 0)),
                   pl.BlockSpec((1, 2, t, t), lambda h: (h, 0, 0, 0)),
                   pl.BlockSpec((1, 2, META_BLK, t), lambda h: (h, 0, 0, 0))],
        compiler_params=_params(("parallel",), 12 * t * t * 4),
        name="rel_bias_tiles",
    )(table, jnp.asarray(bd), jnp.asarray(bs), jnp.asarray(bm))


ATTN_QUERY_CHUNK = 256


ATTN_SCORE_BUFFERS = 4
ATTN_HEADS_PER_STEP = 2


class _Step(NamedTuple):
    q: Callable[[], Any]
    k: Callable[[], Any]
    vt: Callable[[], Any]
    bias: Optional[Callable[[], Any]]
    chunk: int


def _block_steps(n_chunks, q, k, vt, bias=None):
    return [_Step(functools.partial(q, c), functools.partial(k, c), functools.partial(vt, c),
                  functools.partial(bias, c) if bias is not None else None, c)
            for c in range(n_chunks)]


def _issue_scores(s_ref, step, slot):
    k = step.k()
    s_ref[slot, 0:k.shape[0], :] = _dot(k, step.q())


def _prefetch_scores(s_ref, steps):
    for i, step in enumerate(steps[:s_ref.shape[0] - 1]):
        _issue_scores(s_ref, step, i)


def _run_steps(s_ref, steps, following, m_ref, acc_ref):
    n_buf, _, qc = s_ref.shape
    ahead = n_buf - 1
    assert not following or len(steps) % n_buf == 0
    seq = list(steps) + list(following)[:ahead]
    for i, step in enumerate(steps):
        if i + ahead < len(seq):
            _issue_scores(s_ref, seq[i + ahead], (i + ahead) % n_buf)
        vt = step.vt()
        s = s_ref[i % n_buf, 0:vt.shape[1], :]
        if step.bias is not None:
            s = s + step.bias()
        sl = pl.ds(step.chunk * qc, qc)
        m_prev = m_ref[:, sl]
        m_new = jnp.maximum(m_prev, jnp.max(s, axis=0, keepdims=True))
        alpha = jnp.exp2(m_prev - m_new)
        p = jnp.exp2(s - m_new)
        acc_ref[:, sl] = alpha * acc_ref[:, sl] + _dot(vt, p.astype(BF16))
        m_ref[:, sl] = m_new


def _attend_init(m_ref, acc_ref):
    m_ref[...] = jnp.full(m_ref.shape, NEG_INF, F32)
    acc_ref[...] = jnp.zeros(acc_ref.shape, F32)


FAR_UNROLL = 4


def _walk_far_blocks(n_far, body):
    def loop_body(i, carry):
        body(i * FAR_UNROLL, FAR_UNROLL)
        return carry

    lax.fori_loop(0, n_far >> (FAR_UNROLL.bit_length() - 1), loop_body, 0)
    count = FAR_UNROLL // 2
    while count >= 1:
        first = n_far - (n_far & (2 * count - 1))

        @pl.when((n_far & count) != 0)
        def _(first=first, count=count):
            body(first, count)

        count //= 2


def _normalised(acc_ref):
    acc = acc_ref[...]
    return acc[0:HEAD_W] * (1.0 / acc[HEAD_W:HEAD_W + 1])


def _da_kernel(qt_ref, k_ref, vt_ref, km_ref, vtm_ref, bd_ref, bs_ref, bm_ref, lam_ref, subg_ref,
               o_ref, qq_ref, s_ref, m_ref, acc_ref, *, t):
    qi = pl.program_id(2)
    feat = lax.broadcasted_iota(jnp.int32, (HEAD_W, 1), 0)
    for hd in range(ATTN_HEADS_PER_STEP):
        qt = qt_ref[0, hd]
        zero = jnp.zeros_like(qt)
        qq_ref[:, pl.ds(2 * t * hd, t)] = jnp.where(feat < DA_HEAD_DIM, qt, zero)
        qq_ref[:, pl.ds(2 * t * hd + t, t)] = jnp.where(feat >= DA_HEAD_DIM, qt, zero)
    qc = s_ref.shape[2]
    chunks_per_map = t // qc
    chunks_per_head = 2 * chunks_per_map
    n_chunks = ATTN_HEADS_PER_STEP * chunks_per_head

    def head(c):
        return c // chunks_per_head

    def cols(c):
        return pl.ds((c % chunks_per_map) * qc, qc)

    def q_chunk(c):
        return qq_ref[:, pl.ds(c * qc, qc)]

    def block_steps(kb, bias=None, keys=t):
        rows = pl.ds(pl.multiple_of(kb * t, t), keys)
        return _block_steps(n_chunks, q_chunk, lambda c: k_ref[0, head(c), rows, :],
                            lambda c: vt_ref[0, head(c), :, rows], bias)

    state = (m_ref, acc_ref)
    n_far = jnp.maximum(qi - 1, 0)
    near_sel = jnp.minimum(qi, 1)
    prev_kb = jnp.maximum(qi - 1, 0)
    meta_steps = _block_steps(n_chunks, q_chunk, lambda c: km_ref[0, head(c)],
                              lambda c: vtm_ref[0, head(c)],
                              lambda c: bm_ref[head(c), near_sel, :, cols(c)])
    _attend_init(*state)
    _prefetch_scores(s_ref, meta_steps)
    _run_steps(s_ref, meta_steps, block_steps(jnp.where(qi >= 2, 0, qi)), *state)

    def far(first, count):
        steps = [st for u in range(count) for st in block_steps(first + u)]
        last = first + count - 1
        _run_steps(s_ref, steps, block_steps(jnp.where(last == n_far - 1, qi, last + 1)), *state)

    _walk_far_blocks(n_far, far)

    diag = []
    for c in range(n_chunks):
        keys = (c % chunks_per_map + 1) * qc
        diag.append(block_steps(qi, lambda c, keys=keys: bd_ref[head(c), 0:keys, cols(c)], keys)[c])
    prev = block_steps(prev_kb, lambda c: bs_ref[head(c), near_sel, :, cols(c)])
    _run_steps(s_ref, diag + prev, (), *state)

    ot = _normalised(acc_ref)
    lp = lam_ref[...]
    lam = (jnp.exp(jnp.sum(lp[0:1] * lp[1:2], axis=-1, keepdims=True))
           - jnp.exp(jnp.sum(lp[2:3] * lp[3:4], axis=-1, keepdims=True)) + LAMBDA_INIT)
    for hd in range(ATTN_HEADS_PER_STEP):
        o1 = ot[:, 2 * t * hd:2 * t * hd + t]
        o2 = ot[:, 2 * t * hd + t:2 * t * (hd + 1)]
        d = o1 - lam * o2
        ms = jnp.mean(d * d, axis=0, keepdims=True)
        y = d * lax.rsqrt(ms + RMS_EPS) * subg_ref[...] * (1.0 - LAMBDA_INIT)
        o_ref[0, :, hd * HEAD_W:(hd + 1) * HEAD_W] = y.T.astype(BF16)


def _attn_specs(t, s_len, q_w):
    hp = ATTN_HEADS_PER_STEP
    return [
        pl.BlockSpec((1, hp, q_w, t), lambda bi, h, qi: (bi, h, 0, qi)),
        pl.BlockSpec((1, hp, s_len, q_w), lambda bi, h, qi: (bi, h, 0, 0)),
        pl.BlockSpec((1, hp, VT_ROWS, s_len), lambda bi, h, qi: (bi, h, 0, 0)),
        pl.BlockSpec((1, hp, META_BLK, q_w), lambda bi, h, qi: (0, h, 0, 0)),
        pl.BlockSpec((1, hp, VT_ROWS, META_BLK), lambda bi, h, qi: (0, h, 0, 0)),
    ]


def _da_attention(qt, k, vt, k_meta, vt_meta, bias_d, bias_s, bias_m, lam_p, sub_g, *, t):
    b, _, s_len, _ = k.shape
    nq = s_len // t
    hp = ATTN_HEADS_PER_STEP
    qc = min(ATTN_QUERY_CHUNK, t)
    vmem = hp * (2 * (s_len + META_BLK) * (HEAD_W + VT_ROWS) * 2 + 2 * 3 * t * t * 4
                 + 2 * 2 * t * META_BLK * 4
                 + 2 * 2 * t * HEAD_W * 2 + 2 * t * HEAD_W * 2 + (VT_ROWS + 8) * 2 * t * 4
                 ) + (ATTN_SCORE_BUFFERS + 4) * t * qc * 4
    return pl.pallas_call(
        functools.partial(_da_kernel, t=t),
        out_shape=jax.ShapeDtypeStruct((b, s_len, DA_W), BF16),
        grid=(b, HEADS // hp, nq),
        in_specs=_attn_specs(t, s_len, HEAD_W) + [
            pl.BlockSpec((hp, t, t), lambda bi, h, qi: (h, 0, 0)),
            pl.BlockSpec((hp, 2, t, t), lambda bi, h, qi: (h, 0, 0, 0)),
            pl.BlockSpec((hp, 2, META_BLK, t), lambda bi, h, qi: (h, 0, 0, 0)),
            _resident((4, DA_HEAD_DIM), lambda bi, h, qi: (0, 0)),
            _resident((HEAD_W, 1), lambda bi, h, qi: (0, 0)),
        ],
        out_specs=pl.BlockSpec((1, t, hp * HEAD_W), lambda bi, h, qi: (bi, qi, h)),
        scratch_shapes=[pltpu.VMEM((HEAD_W, hp * 2 * t), BF16),
                        pltpu.VMEM((ATTN_SCORE_BUFFERS, t, qc), F32),
                        pltpu.VMEM((1, hp * 2 * t), F32), pltpu.VMEM((VT_ROWS, hp * 2 * t), F32)],
        compiler_params=_params(("parallel", "parallel", "arbitrary"), vmem),
        name="diff_attention",
    )(qt, k, vt, k_meta, vt_meta, bias_d, bias_s, bias_m, lam_p, sub_g)


def _mla_kernel(qt_ref, k_ref, vt_ref, km_ref, vtm_ref, o_ref, s_ref, m_ref, acc_ref, *, tq, tk):
    qi = pl.program_id(2)
    qc = s_ref.shape[2]
    diag_blocks = tq // tk
    chunks_per_head = tq // qc
    n_chunks = ATTN_HEADS_PER_STEP * chunks_per_head

    def head(c):
        return c // chunks_per_head

    def q_chunk(c):
        return qt_ref[0, head(c), :, pl.ds((c % chunks_per_head) * qc, qc)]

    def block_steps(kb, bias=None, keys=tk):
        rows = pl.ds(pl.multiple_of(kb * tk, tk), keys)
        return _block_steps(n_chunks, q_chunk, lambda c: k_ref[0, head(c), rows, :],
                            lambda c: vt_ref[0, head(c), :, rows], bias)

    state = (m_ref, acc_ref)

    def meta_mask(c):
        key = lax.broadcasted_iota(jnp.int32, (META_BLK, qc), 0)
        return jnp.where(key < N_META, 0.0, NEG_INF)

    meta_steps = _block_steps(n_chunks, q_chunk, lambda c: km_ref[0, head(c)],
                              lambda c: vtm_ref[0, head(c)], meta_mask)
    _attend_init(*state)
    _prefetch_scores(s_ref, meta_steps)
    _run_steps(s_ref, meta_steps, block_steps(0), *state)

    def far(first, count):
        steps = [st for u in range(count) for st in block_steps(first + u)]
        _run_steps(s_ref, steps, block_steps(first + count), *state)

    first_diag = qi * diag_blocks
    _walk_far_blocks(first_diag, far)

    diag = []
    for d in range(diag_blocks):
        for c in range(n_chunks):
            cq = (c % chunks_per_head) * qc
            keys = min(tk, cq + qc - d * tk)
            if keys <= 0:
                continue

            def causal(c, d=d, keys=keys, cq=cq):
                key = lax.broadcasted_iota(jnp.int32, (keys, qc), 0) + d * tk
                query = lax.broadcasted_iota(jnp.int32, (keys, qc), 1) + cq
                return jnp.where(key <= query, 0.0, NEG_INF)

            diag.append(block_steps(first_diag + d, causal, keys)[c])
    _run_steps(s_ref, diag, (), *state)

    ot = _normalised(acc_ref)
    for hd in range(ATTN_HEADS_PER_STEP):
        o_ref[0, :, hd * HEAD_W:(hd + 1) * HEAD_W] = ot[:, hd * tq:(hd + 1) * tq].T.astype(BF16)


def _mla_attention(qt, k, vt, k_meta, vt_meta, *, tq, tk):
    b, _, s_len, _ = k.shape
    nq = s_len // tq
    hp = ATTN_HEADS_PER_STEP
    qc = min(ATTN_QUERY_CHUNK, tq)
    vmem = hp * (2 * (s_len + META_BLK) * (MLA_QK_W + VT_ROWS) * 2 + 2 * tq * MLA_QK_W * 2
                 + 2 * tq * HEAD_W * 2 + (VT_ROWS + 8) * tq * 4 + tq * HEAD_W * 4
                 ) + (ATTN_SCORE_BUFFERS + 4) * tk * qc * 4
    return pl.pallas_call(
        functools.partial(_mla_kernel, tq=tq, tk=tk),
        out_shape=jax.ShapeDtypeStruct((b, s_len, DA_W), BF16),
        grid=(b, HEADS // hp, nq),
        in_specs=_attn_specs(tq, s_len, MLA_QK_W),
        out_specs=pl.BlockSpec((1, tq, hp * HEAD_W), lambda bi, h, qi: (bi, qi, h)),
        scratch_shapes=[pltpu.VMEM((ATTN_SCORE_BUFFERS, tk, qc), F32),
                        pltpu.VMEM((1, hp * tq), F32), pltpu.VMEM((VT_ROWS, hp * tq), F32)],
        compiler_params=_params(("parallel", "parallel", "arbitrary"), vmem),
        name="mla_attention",
    )(qt, k, vt, k_meta, vt_meta)


def _merge_kernel(h_ref, pre_ref, post_ref, yda_ref, ymla_ref, wg0_ref, wg1_ref, b0_ref, b1_ref,
                  wbd_ref, wbm_ref, wo_ref, o_ref, hn_ref):
    j = pl.program_id(1)
    last = pl.num_programs(1) - 1

    def partial_out(hn):
        g0 = jax.nn.sigmoid(_dot(hn, wg0_ref[...]) + b0_ref[...])
        g1 = jax.nn.sigmoid(_dot(hn, wg1_ref[...]) + b1_ref[...])
        merged = g0 * _dot(yda_ref[0], wbd_ref[...]) + g1 * _dot(ymla_ref[0], wbm_ref[...])
        return _dot(merged.astype(BF16), wo_ref[...])

    @pl.when(j == 0)
    def _():
        hn = _rms(h_ref[0], pre_ref[...]).astype(BF16)
        hn_ref[...] = hn
        o_ref[0] = partial_out(hn)

    @pl.when(jnp.logical_and(j > 0, j < last))
    def _():
        o_ref[0] += partial_out(hn_ref[...])

    @pl.when(j == last)
    def _():
        m = o_ref[0] + partial_out(hn_ref[...])
        o_ref[0] = h_ref[0] + _rms(m, post_ref[...])


def _merge(h1, pre_g, post_g, y_da, y_mla, w_gate, b_gate, wb_da, wb_mla, w_out):
    b, s_len, d = h1.shape
    yw = y_da.shape[-1]
    tm = _row_tile(s_len, MERGE_ROW_TILE_MAX)
    tn = _col_tile(d, MERGE_COL_TILE_MAX)
    nt = s_len // tm
    nj = d // tn
    assert nj >= 2, "the kernel treats the first and the last column tile as different steps"
    vmem = (2 * 2 * tm * d * 4 + tm * d * 2 + 2 * 2 * tm * yw * 2
            + 2 * (3 * d * tn + 2 * yw * tn) * 2 + 6 * tm * tn * 4 + tm * d * 4)
    return pl.pallas_call(
        _merge_kernel,
        out_shape=jax.ShapeDtypeStruct((b, s_len, d), F32),
        grid=(b * nt, nj),
        in_specs=[
            pl.BlockSpec((1, tm, d), lambda i, j: (i // nt, i % nt, 0)),
            _resident((1, d), lambda i, j: (0, 0)),
            _resident((1, d), lambda i, j: (0, 0)),
            pl.BlockSpec((1, tm, yw), lambda i, j: (i // nt, i % nt, 0)),
            pl.BlockSpec((1, tm, yw), lambda i, j: (i // nt, i % nt, 0)),
            pl.BlockSpec((d, tn), lambda i, j: (0, j)),
            pl.BlockSpec((d, tn), lambda i, j: (0, nj + j)),
            pl.BlockSpec((1, tn), lambda i, j: (0, j)),
            pl.BlockSpec((1, tn), lambda i, j: (0, nj + j)),
            pl.BlockSpec((yw, tn), lambda i, j: (0, j)),
            pl.BlockSpec((yw, tn), lambda i, j: (0, j)),
            pl.BlockSpec((tn, d), lambda i, j: (j, 0)),
        ],
        out_specs=pl.BlockSpec((1, tm, d), lambda i, j: (i // nt, i % nt, 0)),
        scratch_shapes=[pltpu.VMEM((tm, d), BF16)],
        compiler_params=_params(("parallel", "arbitrary"), vmem),
        name="gated_merge_out_proj",
    )(h1, pre_g, post_g, y_da, y_mla, w_gate, w_gate, b_gate, b_gate, wb_da, wb_mla, w_out)


def _rope_tables(pos):
    half = MLA_ROPE // 2
    pos = pos.astype(F32)
    inv = ROPE_THETA ** (-jnp.arange(half, dtype=F32) * 2.0 / MLA_ROPE)
    ang = pos[:, None] * inv[None, :]
    cos, sin = jnp.cos(ang), jnp.sin(ang)
    pad = jnp.zeros((pos.shape[0], HEAD_W - MLA_ROPE), F32)
    return (jnp.concatenate([cos, cos, pad], axis=-1),
            jnp.concatenate([-sin, sin, pad], axis=-1))


def _swap_halves(w):
    half = w.shape[-1] // 2
    return jnp.concatenate([w[..., half:], w[..., :half]], axis=-1)


def _pad_lanes(w, width):
    return jnp.concatenate([w, jnp.zeros(w.shape[:-1] + (width - w.shape[-1],), w.dtype)], axis=-1)


def kernel(x, meta_tokens, rel_bias_table, ffn1_pre_g, ffn1_post_g, ffn1_w_gate, ffn1_w_up, ffn1_w_down, mix_pre_g, mix_post_g, w_in, b_gate, da_lambda_q1, da_lambda_k1, da_lambda_q2, da_lambda_k2, da_sub_g, mla_q_norm_g, mla_kv_norm_g, mla_w_uq, mla_w_ukv, w_branch_da, w_branch_mla, w_out, ffn2_pre_g, ffn2_post_g, ffn2_w_gate, ffn2_w_up, ffn2_w_down):
    b, s_len, d = x.shape
    assert ffn1_pre_g.shape[0] == 1, "single-layer trunk"
    t = _col_tile(s_len, ATTN_KEY_BLOCK_MAX)
    row = lambda g: g.reshape(1, -1).astype(F32)

    meta_in = jnp.concatenate([meta_tokens.astype(x.dtype),
                               jnp.zeros((META_BLK - N_META, d), x.dtype)], axis=0)
    ffn1_w = (row(ffn1_pre_g), row(ffn1_post_g), ffn1_w_gate[0].astype(BF16),
              ffn1_w_up[0].astype(BF16), ffn1_w_down[0].astype(BF16))
    h1 = _ffn(x.reshape(b * s_len, d), *ffn1_w, name="ffn1").reshape(b, s_len, d)
    h1_meta = _ffn(meta_in, *ffn1_w, name="ffn1_meta").reshape(1, META_BLK, d)

    w = w_in[0]
    o_k, o_v, o_cq = DA_W, 2 * DA_W, 3 * DA_W
    o_kr = o_cq + MLA_Q_RANK + MLA_KV_RANK
    o_gate = o_kr + MLA_ROPE
    da_w = (row(mix_pre_g), w[:, :o_k].T.astype(BF16), w[:, o_k:o_v].astype(BF16),
            w[:, o_v:o_cq].T.astype(BF16))
    qt_da, k_da, vt_da = _da_proj(h1, *da_w, name="da_in_proj")
    _, k_da_meta, vt_da_meta = _da_proj(h1_meta, *da_w, name="da_in_proj_meta")

    kr = w[:, o_kr:o_gate]
    w_c = jnp.concatenate([w[:, o_cq:o_kr], _pad_lanes(kr, HEAD_W),
                           _pad_lanes(_swap_halves(kr), HEAD_W)], axis=-1).astype(BF16)
    uq = mla_w_uq[0].reshape(MLA_Q_RANK, HEADS, MLA_NOPE + MLA_ROPE)
    uq_rope = uq[..., MLA_NOPE:]
    w_uqt = jnp.concatenate([uq[..., :MLA_NOPE], uq_rope, _swap_halves(uq_rope)],
                            axis=-1).reshape(MLA_Q_RANK, HEADS * _UQ_HEAD_W).T.astype(BF16)
    ukv = mla_w_ukv[0].reshape(MLA_KV_RANK, HEADS, 2 * HEAD_W)
    w_uk = ukv[..., :HEAD_W].reshape(MLA_KV_RANK, DA_W).astype(BF16)
    w_uvt = ukv[..., HEAD_W:].reshape(MLA_KV_RANK, DA_W).T.astype(BF16)
    mla_w = (w_c, row(mla_q_norm_g), row(mla_kv_norm_g), w_uqt, w_uk, w_uvt)

    def mla_proj(h, pos, name):
        cos, sin = _rope_tables(pos)
        return _mla_proj(h, row(mix_pre_g), cos, sin, cos.T, sin.T, *mla_w, name=name)

    qt_mla, k_mla, vt_mla = mla_proj(h1, N_META + jnp.arange(s_len), "mla_in_proj")
    _, k_mla_meta, vt_mla_meta = mla_proj(h1_meta, jnp.arange(META_BLK), "mla_in_proj_meta")

    bias_d, bias_s, bias_m = _bias_tiles(rel_bias_table.astype(F32), t)
    lam_p = jnp.concatenate([da_lambda_q1, da_lambda_k1, da_lambda_q2, da_lambda_k2], axis=0).astype(F32)
    y_da = _da_attention(qt_da, k_da, vt_da, k_da_meta, vt_da_meta, bias_d, bias_s, bias_m, lam_p,
                         da_sub_g.reshape(HEAD_W, 1).astype(F32), t=t)
    y_mla = _mla_attention(qt_mla, k_mla, vt_mla, k_mla_meta, vt_mla_meta,
                           tq=_col_tile(s_len, 2 * t), tk=t)

    h2 = _merge(h1, row(mix_pre_g), row(mix_post_g), y_da, y_mla,
                w[:, o_gate:].astype(BF16), row(b_gate),
                w_branch_da[0].astype(BF16), w_branch_mla[0].astype(BF16), w_out[0].astype(BF16))

    out = _ffn(h2.reshape(b * s_len, d), row(ffn2_pre_g), row(ffn2_post_g),
               ffn2_w_gate[0].astype(BF16), ffn2_w_up[0].astype(BF16), ffn2_w_down[0].astype(BF16),
               name="ffn2")
    return out.reshape(b, s_len, d)
```

```python
import functools
import math
from typing import Any, Callable, NamedTuple, Optional

import numpy as np
import jax
import jax.numpy as jnp
from jax import lax
from jax.experimental import pallas as pl
from jax.experimental.pallas import tpu as pltpu

F32 = jnp.float32
BF16 = jnp.bfloat16

N_META = 16
RMS_EPS = 1e-6
NEG_INF = -1e30
HEADS = 8
HEAD_W = 128
DA_HEAD_DIM = 64
DA_W = HEADS * HEAD_W
MLA_Q_RANK = 768
MLA_KV_RANK = 512
MLA_NOPE = 128
MLA_ROPE = 64
MLA_QK_W = 256
ROPE_THETA = 10000.0
REL_BUCKETS = 32
REL_MAX_DIST = 128
LAMBDA_INIT = 0.8 - 0.6 * math.exp(-0.3 * 0)
LOG2_E = math.log2(math.e)

V7X_LANES = 128
V7X_BF16_SUBLANES = 16
V7X_VMEM_BYTES = 64 * 1024 * 1024
VMEM_CAP_BYTES = V7X_VMEM_BYTES - 6 * 1024 * 1024

META_BLK = V7X_LANES

FFN_ROW_TILE_MAX = 896
FFN_FF_TILE_MAX = 512
PROJ_ROW_TILE_MAX = 704
MERGE_ROW_TILE_MAX = 512
MERGE_COL_TILE_MAX = 512
ATTN_KEY_BLOCK_MAX = 512


def _row_tile(rows, max_tile, multiple=V7X_BF16_SUBLANES):
    best = None
    for t in range(multiple, max_tile + 1, multiple):
        if rows % t == 0:
            best = t
    assert best is not None, (rows, max_tile)
    return best


def _col_tile(cols, max_tile):
    return _row_tile(cols, max_tile, V7X_LANES)


def _params(semantics, vmem_bytes):
    return pltpu.CompilerParams(
        dimension_semantics=semantics,
        vmem_limit_bytes=int(min(max(vmem_bytes, 16 * 1024 * 1024), VMEM_CAP_BYTES)))


def _resident(block_shape, index_map):
    return pl.BlockSpec(block_shape, index_map, pipeline_mode=pl.Buffered(1))


def _rms(x, g):
    ms = jnp.mean(x * x, axis=-1, keepdims=True)
    return x * lax.rsqrt(ms + RMS_EPS) * g


def _dot(a, b):
    return jnp.dot(a, b, preferred_element_type=F32)


def _dot_nt(a, b):
    return lax.dot_general(a, b, (((1,), (1,)), ((), ())), preferred_element_type=F32)


def _ffn_kernel(x_ref, pre_ref, post_ref, wg_ref, wu_ref, wd_ref, o_ref, xn_ref):
    k = pl.program_id(1)
    last = pl.num_programs(1) - 1

    def partial_out(xn):
        g = _dot(xn, wg_ref[...])
        u = _dot(xn, wu_ref[...])
        a = (g * jax.nn.sigmoid(g) * u).astype(BF16)
        return _dot(a, wd_ref[...])

    @pl.when(k == 0)
    def _():
        xn = _rms(x_ref[...], pre_ref[...]).astype(BF16)
        xn_ref[...] = xn
        o_ref[...] = partial_out(xn)

    @pl.when(jnp.logical_and(k > 0, k < last))
    def _():
        o_ref[...] += partial_out(xn_ref[...])

    @pl.when(k == last)
    def _():
        f = o_ref[...] + partial_out(xn_ref[...])
        o_ref[...] = x_ref[...] + 0.5 * _rms(f, post_ref[...])


def _ffn(x, pre_g, post_g, wg, wu, wd, *, name):
    rows, d = x.shape
    f = wg.shape[1]
    tm = _row_tile(rows, FFN_ROW_TILE_MAX)
    tf = _col_tile(f, FFN_FF_TILE_MAX)
    assert f // tf >= 2, "the kernel treats the first and the last d_ff tile as different steps"
    vmem = (2 * 2 * tm * d * 4
            + tm * d * 2
            + 2 * 3 * d * tf * 2
            + 4 * tm * tf * 4
            + tm * d * 4)
    return pl.pallas_call(
        _ffn_kernel,
        out_shape=jax.ShapeDtypeStruct((rows, d), F32),
        grid=(rows // tm, f // tf),
        in_specs=[
            pl.BlockSpec((tm, d), lambda i, k: (i, 0)),
            _resident((1, d), lambda i, k: (0, 0)),
            _resident((1, d), lambda i, k: (0, 0)),
            pl.BlockSpec((d, tf), lambda i, k: (0, k)),
            pl.BlockSpec((d, tf), lambda i, k: (0, k)),
            pl.BlockSpec((tf, d), lambda i, k: (k, 0)),
        ],
        out_specs=pl.BlockSpec((tm, d), lambda i, k: (i, 0)),
        scratch_shapes=[pltpu.VMEM((tm, d), BF16)],
        compiler_params=_params(("parallel", "arbitrary"), vmem),
        name=name,
    )(x, pre_g, post_g, wg, wu, wd)


VT_ROWS = HEAD_W + V7X_BF16_SUBLANES


def _store_vt(vt_ref, vt):
    tm = vt.shape[1]
    vt_ref[0, :, 0:HEAD_W, :] = vt.reshape(HEADS, HEAD_W, tm).astype(BF16)
    extra = lax.broadcasted_iota(jnp.int32, (HEADS, VT_ROWS - HEAD_W, tm), 1)
    vt_ref[0, :, HEAD_W:VT_ROWS, :] = jnp.where(extra == 0, 1.0, 0.0).astype(BF16)


def _da_proj_kernel(h_ref, g_ref, wqt_ref, wk_ref, wvt_ref, qt_ref, k_ref, vt_ref):
    hn = _rms(h_ref[...], g_ref[...]).astype(BF16)
    tm = hn.shape[0]
    qt = _dot_nt(wqt_ref[...], hn) * (DA_HEAD_DIM ** -0.5 * LOG2_E)
    qt_ref[0] = qt.reshape(HEADS, HEAD_W, tm).astype(BF16)
    k = _dot(hn, wk_ref[...])
    for h in range(HEADS):
        k_ref[0, h] = k[:, h * HEAD_W:(h + 1) * HEAD_W].astype(BF16)
    _store_vt(vt_ref, _dot_nt(wvt_ref[...], hn))


_C_CKV = MLA_Q_RANK
_C_KR1 = _C_CKV + MLA_KV_RANK
_C_KR2 = _C_KR1 + HEAD_W
_C_END = _C_KR2 + HEAD_W
_UQ_HEAD_W = HEAD_W + 2 * MLA_ROPE


def _mla_proj_kernel(h_ref, g_ref, cos_ref, sin_ref, cost_ref, sint_ref, wc_ref, gq_ref, gkv_ref,
                     wuqt_ref, wuk_ref, wuvt_ref, qt_ref, k_ref, vt_ref):
    hn = _rms(h_ref[...], g_ref[...]).astype(BF16)
    tm = hn.shape[0]
    scale = (MLA_NOPE + MLA_ROPE) ** -0.5 * LOG2_E

    c_all = _dot(hn, wc_ref[...])

    def proj(lo, hi):
        return c_all[:, lo:hi]

    cq = _rms(proj(0, _C_CKV), gq_ref[...]).astype(BF16)
    qt = _dot_nt(wuqt_ref[...], cq)
    cos_t = cost_ref[0:MLA_ROPE, :]
    sin_t = sint_ref[0:MLA_ROPE, :]
    rope_end = HEAD_W + MLA_ROPE
    for h in range(HEADS):
        base = h * _UQ_HEAD_W
        nope = qt[base:base + HEAD_W]
        r1 = qt[base + HEAD_W:base + rope_end]
        r2 = qt[base + rope_end:base + _UQ_HEAD_W]
        qt_ref[0, h, 0:HEAD_W, :] = (nope * scale).astype(BF16)
        qt_ref[0, h, HEAD_W:rope_end, :] = ((r1 * cos_t + r2 * sin_t) * scale).astype(BF16)
        qt_ref[0, h, rope_end:MLA_QK_W, :] = jnp.zeros((MLA_QK_W - rope_end, tm), BF16)

    ckv = _rms(proj(_C_CKV, _C_KR1), gkv_ref[...]).astype(BF16)
    k_nope = _dot(ckv, wuk_ref[...])
    k_rope = (proj(_C_KR1, _C_KR2) * cos_ref[...] + proj(_C_KR2, _C_END) * sin_ref[...]).astype(BF16)
    for h in range(HEADS):
        k_ref[0, h, :, 0:HEAD_W] = k_nope[:, h * HEAD_W:(h + 1) * HEAD_W].astype(BF16)
        k_ref[0, h, :, HEAD_W:MLA_QK_W] = k_rope
    _store_vt(vt_ref, _dot_nt(wuvt_ref[...], ckv))


def _proj_specs(b, lp, tm, q_w):
    nt = lp // tm
    rows = lambda w: pl.BlockSpec((1, HEADS, tm, w), lambda i: (i // nt, 0, i % nt, 0))
    cols = lambda w: pl.BlockSpec((1, HEADS, w, tm), lambda i: (i // nt, 0, 0, i % nt))
    shapes = [jax.ShapeDtypeStruct((b, HEADS, q_w, lp), BF16),
              jax.ShapeDtypeStruct((b, HEADS, lp, q_w), BF16),
              jax.ShapeDtypeStruct((b, HEADS, VT_ROWS, lp), BF16)]
    return shapes, [cols(q_w), rows(q_w), cols(VT_ROWS)]


def _da_proj(h1, g, wqt, wk, wvt, *, name):
    b, lp, d = h1.shape
    tm = _col_tile(lp, PROJ_ROW_TILE_MAX)
    nt = lp // tm
    shapes, out_specs = _proj_specs(b, lp, tm, HEAD_W)
    vmem = (2 * tm * d * 4 + tm * d * 2 + 3 * DA_W * d * 2
            + 2 * 3 * tm * DA_W * 2 + 3 * tm * DA_W * 4)
    return pl.pallas_call(
        _da_proj_kernel,
        out_shape=shapes,
        grid=(b * nt,),
        in_specs=[
            pl.BlockSpec((None, tm, d), lambda i: (i // nt, i % nt, 0)),
            _resident((1, d), lambda i: (0, 0)),
            _resident(wqt.shape, lambda i: (0, 0)),
            _resident(wk.shape, lambda i: (0, 0)),
            _resident(wvt.shape, lambda i: (0, 0)),
        ],
        out_specs=out_specs,
        compiler_params=_params(("parallel",), vmem),
        name=name,
    )(h1, g, wqt, wk, wvt)


def _mla_proj(h1, g, cos, sin, cos_t, sin_t, wc, gq, gkv, wuqt, wuk, wuvt, *, name):
    b, lp, d = h1.shape
    tm = _col_tile(lp, PROJ_ROW_TILE_MAX)
    nt = lp // tm
    shapes, out_specs = _proj_specs(b, lp, tm, MLA_QK_W)
    vmem = (2 * tm * d * 4 + tm * d * 2
            + (d * _C_END + HEADS * _UQ_HEAD_W * MLA_Q_RANK + 2 * DA_W * MLA_KV_RANK) * 2
            + 2 * tm * HEADS * (2 * MLA_QK_W + HEAD_W) * 2
            + 2 * tm * HEADS * _UQ_HEAD_W * 4)
    return pl.pallas_call(
        _mla_proj_kernel,
        out_shape=shapes,
        grid=(b * nt,),
        in_specs=[
            pl.BlockSpec((None, tm, d), lambda i: (i // nt, i % nt, 0)),
            _resident((1, d), lambda i: (0, 0)),
            pl.BlockSpec((tm, HEAD_W), lambda i: (i % nt, 0)),
            pl.BlockSpec((tm, HEAD_W), lambda i: (i % nt, 0)),
            pl.BlockSpec((HEAD_W, tm), lambda i: (0, i % nt)),
            pl.BlockSpec((HEAD_W, tm), lambda i: (0, i % nt)),
            _resident(wc.shape, lambda i: (0, 0)),
            _resident((1, MLA_Q_RANK), lambda i: (0, 0)),
            _resident((1, MLA_KV_RANK), lambda i: (0, 0)),
            _resident(wuqt.shape, lambda i: (0, 0)),
            _resident(wuk.shape, lambda i: (0, 0)),
            _resident(wuvt.shape, lambda i: (0, 0)),
        ],
        out_specs=out_specs,
        compiler_params=_params(("parallel",), vmem),
        name=name,
    )(h1, g, cos, sin, cos_t, sin_t, wc, gq, gkv, wuqt, wuk, wuvt)


def _t5_bucket(rel):
    n = np.maximum(rel, 0)
    max_exact = REL_BUCKETS // 2
    n_f = np.maximum(n, 1).astype(np.float64)
    large = max_exact + (np.log(n_f / max_exact) / math.log(REL_MAX_DIST / max_exact)
                         * (REL_BUCKETS - max_exact)).astype(np.int32)
    large = np.minimum(large, REL_BUCKETS - 1)
    return np.where(n < max_exact, n, large).astype(np.int32)


def _bucket_tiles(t):
    j = np.arange(t)[:, None]
    i = np.arange(t)[None, :]
    diag = np.where(j <= i, _t5_bucket(i - j), -1)
    sub = _t5_bucket(t + i - j)
    assert _t5_bucket(np.array([t + 1]))[0] == REL_BUCKETS - 1
    jm = np.arange(META_BLK)[:, None]
    meta_first = np.where(jm < N_META, _t5_bucket(N_META + i - jm), -1)
    meta_rest = np.where(jm < N_META, REL_BUCKETS - 1, -1) + 0 * i
    return (diag.astype(np.int32), np.stack([np.full_like(sub, -1), sub]).astype(np.int32),
            np.stack([meta_first, meta_rest]).astype(np.int32))


def _bias_kernel(table_ref, bd_ref, bs_ref, bm_ref, od_ref, os_ref, om_ref, *, buckets):
    h = pl.program_id(0)
    far = table_ref[REL_BUCKETS - 1, h]

    def lookup(bucket_ref, out_ref, known):
        rows, cols = known.shape
        for r0 in range(0, rows, V7X_LANES):
            for c0 in range(0, cols, V7X_LANES):
                patch = (slice(r0, r0 + V7X_LANES), slice(c0, c0 + V7X_LANES))
                present = np.unique(known[patch])
                acc = jnp.zeros((V7X_LANES, V7X_LANES), F32)
                if present.size > 1 or 0 <= present[0] < REL_BUCKETS - 1:
                    bucket = bucket_ref[patch]
                    for b in present:
                        if 0 <= b < REL_BUCKETS - 1:
                            acc = jnp.where(bucket == int(b), (table_ref[int(b), h] - far) * LOG2_E, acc)
                    if present[0] < 0:
                        acc = jnp.where(bucket < 0, NEG_INF, acc)
                elif present[0] < 0:
                    acc = jnp.full((V7X_LANES, V7X_LANES), NEG_INF, F32)
                out_ref[patch] = acc

    bd, bs, bm = buckets
    lookup(bd_ref, od_ref.at[0], bd)
    for i in range(2):
        lookup(bs_ref.at[i], os_ref.at[0, i], bs[i])
        lookup(bm_ref.at[i], om_ref.at[0, i], bm[i])


def _bias_tiles(table, t):
    bd, bs, bm = _bucket_tiles(t)
    return pl.pallas_call(
        functools.partial(_bias_kernel, buckets=(bd, bs, bm)),
        out_shape=[jax.ShapeDtypeStruct((HEADS, t, t), F32),
                   jax.ShapeDtypeStruct((HEADS, 2, t, t), F32),
                   jax.ShapeDtypeStruct((HEADS, 2, META_BLK, t), F32)],
        grid=(HEADS,),
        in_specs=[
            pl.BlockSpec(memory_space=pltpu.SMEM),
            pl.BlockSpec((t, t), lambda h: (0, 0)),
            pl.BlockSpec((2, t, t), lambda h: (0, 0, 0)),
            pl.BlockSpec((2, META_BLK, t), lambda h: (0, 0, 0)),
        ],
        out_specs=[pl.BlockSpec((1, t, t), lambda h: (h, 0, 0)),
                   pl.BlockSpec((1, 2, t, t), lambda h: (h, 0, 0, 0)),
                   pl.BlockSpec((1, 2, META_BLK, t), lambda h: (h, 0, 0, 0))],
        compiler_params=_params(("parallel",), 12 * t * t * 4),
        name="rel_bias_tiles",
    )(table, jnp.asarray(bd), jnp.asarray(bs), jnp.asarray(bm))


ATTN_QUERY_CHUNK = 256


ATTN_SCORE_BUFFERS = 4
ATTN_HEADS_PER_STEP = 2


class _Step(NamedTuple):
    q: Callable[[], Any]
    k: Callable[[], Any]
    vt: Callable[[], Any]
    bias: Optional[Callable[[], Any]]
    chunk: int


def _block_steps(n_chunks, q, k, vt, bias=None):
    return [_Step(functools.partial(q, c), functools.partial(k, c), functools.partial(vt, c),
                  functools.partial(bias, c) if bias is not None else None, c)
            for c in range(n_chunks)]


def _issue_scores(s_ref, step, slot):
    k = step.k()
    s_ref[slot, 0:k.shape[0], :] = _dot(k, step.q())


def _prefetch_scores(s_ref, steps):
    for i, step in enumerate(steps[:s_ref.shape[0] - 1]):
        _issue_scores(s_ref, step, i)


def _run_steps(s_ref, steps, following, m_ref, acc_ref):
    n_buf, _, qc = s_ref.shape
    ahead = n_buf - 1
    assert not following or len(steps) % n_buf == 0
    seq = list(steps) + list(following)[:ahead]
    for i, step in enumerate(steps):
        if i + ahead < len(seq):
            _issue_scores(s_ref, seq[i + ahead], (i + ahead) % n_buf)
        vt = step.vt()
        s = s_ref[i % n_buf, 0:vt.shape[1], :]
        if step.bias is not None:
            s = s + step.bias()
        sl = pl.ds(step.chunk * qc, qc)
        m_prev = m_ref[:, sl]
        m_new = jnp.maximum(m_prev, jnp.max(s, axis=0, keepdims=True))
        alpha = jnp.exp2(m_prev - m_new)
        p = jnp.exp2(s - m_new)
        acc_ref[:, sl] = alpha * acc_ref[:, sl] + _dot(vt, p.astype(BF16))
        m_ref[:, sl] = m_new


def _attend_init(m_ref, acc_ref):
    m_ref[...] = jnp.full(m_ref.shape, NEG_INF, F32)
    acc_ref[...] = jnp.zeros(acc_ref.shape, F32)


FAR_UNROLL = 8


def _walk_far_blocks(n_far, body):
    def loop_body(i, carry):
        body(i * FAR_UNROLL, FAR_UNROLL)
        return carry

    lax.fori_loop(0, n_far >> (FAR_UNROLL.bit_length() - 1), loop_body, 0)
    count = FAR_UNROLL // 2
    while count >= 1:
        first = n_far - (n_far & (2 * count - 1))

        @pl.when((n_far & count) != 0)
        def _(first=first, count=count):
            body(first, count)

        count //= 2


def _normalised(acc_ref):
    acc = acc_ref[...]
    return acc[0:HEAD_W] * (1.0 / acc[HEAD_W:HEAD_W + 1])


def _da_kernel(qt_ref, k_ref, vt_ref, km_ref, vtm_ref, bd_ref, bs_ref, bm_ref, lam_ref, subg_ref,
               o_ref, qq_ref, s_ref, m_ref, acc_ref, *, t):
    qi = pl.program_id(2)
    feat = lax.broadcasted_iota(jnp.int32, (HEAD_W, 1), 0)
    for hd in range(ATTN_HEADS_PER_STEP):
        qt = qt_ref[0, hd]
        zero = jnp.zeros_like(qt)
        qq_ref[:, pl.ds(2 * t * hd, t)] = jnp.where(feat < DA_HEAD_DIM, qt, zero)
        qq_ref[:, pl.ds(2 * t * hd + t, t)] = jnp.where(feat >= DA_HEAD_DIM, qt, zero)
    qc = s_ref.shape[2]
    chunks_per_map = t // qc
    chunks_per_head = 2 * chunks_per_map
    n_chunks = ATTN_HEADS_PER_STEP * chunks_per_head

    def head(c):
        return c // chunks_per_head

    def cols(c):
        return pl.ds((c % chunks_per_map) * qc, qc)

    def q_chunk(c):
        return qq_ref[:, pl.ds(c * qc, qc)]

    def block_steps(kb, bias=None, keys=t):
        rows = pl.ds(pl.multiple_of(kb * t, t), keys)
        return _block_steps(n_chunks, q_chunk, lambda c: k_ref[0, head(c), rows, :],
                            lambda c: vt_ref[0, head(c), :, rows], bias)

    state = (m_ref, acc_ref)
    n_far = jnp.maximum(qi - 1, 0)
    near_sel = jnp.minimum(qi, 1)
    prev_kb = jnp.maximum(qi - 1, 0)
    meta_steps = _block_steps(n_chunks, q_chunk, lambda c: km_ref[0, head(c)],
                              lambda c: vtm_ref[0, head(c)],
                              lambda c: bm_ref[head(c), near_sel, :, cols(c)])
    _attend_init(*state)
    _prefetch_scores(s_ref, meta_steps)
    _run_steps(s_ref, meta_steps, block_steps(jnp.where(qi >= 2, 0, qi)), *state)

    def far(first, count):
        steps = [st for u in range(count) for st in block_steps(first + u)]
        last = first + count - 1
        _run_steps(s_ref, steps, block_steps(jnp.where(last == n_far - 1, qi, last + 1)), *state)

    _walk_far_blocks(n_far, far)

    diag = []
    for c in range(n_chunks):
        keys = (c % chunks_per_map + 1) * qc
        diag.append(block_steps(qi, lambda c, keys=keys: bd_ref[head(c), 0:keys, cols(c)], keys)[c])
    prev = block_steps(prev_kb, lambda c: bs_ref[head(c), near_sel, :, cols(c)])
    _run_steps(s_ref, diag + prev, (), *state)

    ot = _normalised(acc_ref)
    lp = lam_ref[...]
    lam = (jnp.exp(jnp.sum(lp[0:1] * lp[1:2], axis=-1, keepdims=True))
           - jnp.exp(jnp.sum(lp[2:3] * lp[3:4], axis=-1, keepdims=True)) + LAMBDA_INIT)
    for hd in range(ATTN_HEADS_PER_STEP):
        o1 = ot[:, 2 * t * hd:2 * t * hd + t]
        o2 = ot[:, 2 * t * hd + t:2 * t * (hd + 1)]
        d = o1 - lam * o2
        ms = jnp.mean(d * d, axis=0, keepdims=True)
        y = d * lax.rsqrt(ms + RMS_EPS) * subg_ref[...] * (1.0 - LAMBDA_INIT)
        o_ref[0, :, hd * HEAD_W:(hd + 1) * HEAD_W] = y.T.astype(BF16)


def _attn_specs(t, s_len, q_w):
    hp = ATTN_HEADS_PER_STEP
    return [
        pl.BlockSpec((1, hp, q_w, t), lambda bi, h, qi: (bi, h, 0, qi)),
        pl.BlockSpec((1, hp, s_len, q_w), lambda bi, h, qi: (bi, h, 0, 0)),
        pl.BlockSpec((1, hp, VT_ROWS, s_len), lambda bi, h, qi: (bi, h, 0, 0)),
        pl.BlockSpec((1, hp, META_BLK, q_w), lambda bi, h, qi: (0, h, 0, 0)),
        pl.BlockSpec((1, hp, VT_ROWS, META_BLK), lambda bi, h, qi: (0, h, 0, 0)),
    ]


def _da_attention(qt, k, vt, k_meta, vt_meta, bias_d, bias_s, bias_m, lam_p, sub_g, *, t):
    b, _, s_len, _ = k.shape
    nq = s_len // t
    hp = ATTN_HEADS_PER_STEP
    qc = min(ATTN_QUERY_CHUNK, t)
    vmem = hp * (2 * (s_len + META_BLK) * (HEAD_W + VT_ROWS) * 2 + 2 * 3 * t * t * 4
                 + 2 * 2 * t * META_BLK * 4
                 + 2 * 2 * t * HEAD_W * 2 + 2 * t * HEAD_W * 2 + (VT_ROWS + 8) * 2 * t * 4
                 ) + (ATTN_SCORE_BUFFERS + 4) * t * qc * 4
    return pl.pallas_call(
        functools.partial(_da_kernel, t=t),
        out_shape=jax.ShapeDtypeStruct((b, s_len, DA_W), BF16),
        grid=(b, HEADS // hp, nq),
        in_specs=_attn_specs(t, s_len, HEAD_W) + [
            pl.BlockSpec((hp, t, t), lambda bi, h, qi: (h, 0, 0)),
            pl.BlockSpec((hp, 2, t, t), lambda bi, h, qi: (h, 0, 0, 0)),
            pl.BlockSpec((hp, 2, META_BLK, t), lambda bi, h, qi: (h, 0, 0, 0)),
            _resident((4, DA_HEAD_DIM), lambda bi, h, qi: (0, 0)),
            _resident((HEAD_W, 1), lambda bi, h, qi: (0, 0)),
        ],
        out_specs=pl.BlockSpec((1, t, hp * HEAD_W), lambda bi, h, qi: (bi, qi, h)),
        scratch_shapes=[pltpu.VMEM((HEAD_W, hp * 2 * t), BF16),
                        pltpu.VMEM((ATTN_SCORE_BUFFERS, t, qc), F32),
                        pltpu.VMEM((1, hp * 2 * t), F32), pltpu.VMEM((VT_ROWS, hp * 2 * t), F32)],
        compiler_params=_params(("parallel", "parallel", "arbitrary"), vmem),
        name="diff_attention",
    )(qt, k, vt, k_meta, vt_meta, bias_d, bias_s, bias_m, lam_p, sub_g)


def _mla_kernel(qt_ref, k_ref, vt_ref, km_ref, vtm_ref, o_ref, s_ref, m_ref, acc_ref, *, tq, tk):
    qi = pl.program_id(2)
    qc = s_ref.shape[2]
    diag_blocks = tq // tk
    chunks_per_head = tq // qc
    n_chunks = ATTN_HEADS_PER_STEP * chunks_per_head

    def head(c):
        return c // chunks_per_head

    def q_chunk(c):
        return qt_ref[0, head(c), :, pl.ds((c % chunks_per_head) * qc, qc)]

    def block_steps(kb, bias=None, keys=tk):
        rows = pl.ds(pl.multiple_of(kb * tk, tk), keys)
        return _block_steps(n_chunks, q_chunk, lambda c: k_ref[0, head(c), rows, :],
                            lambda c: vt_ref[0, head(c), :, rows], bias)

    state = (m_ref, acc_ref)

    def meta_mask(c):
        key = lax.broadcasted_iota(jnp.int32, (META_BLK, qc), 0)
        return jnp.where(key < N_META, 0.0, NEG_INF)

    meta_steps = _block_steps(n_chunks, q_chunk, lambda c: km_ref[0, head(c)],
                              lambda c: vtm_ref[0, head(c)], meta_mask)
    _attend_init(*state)
    _prefetch_scores(s_ref, meta_steps)
    _run_steps(s_ref, meta_steps, block_steps(0), *state)

    def far(first, count):
        steps = [st for u in range(count) for st in block_steps(first + u)]
        _run_steps(s_ref, steps, block_steps(first + count), *state)

    first_diag = qi * diag_blocks
    _walk_far_blocks(first_diag, far)

    diag = []
    for d in range(diag_blocks):
        for c in range(n_chunks):
            cq = (c % chunks_per_head) * qc
            keys = min(tk, cq + qc - d * tk)
            if keys <= 0:
                continue

            def causal(c, d=d, keys=keys, cq=cq):
                key = lax.broadcasted_iota(jnp.int32, (keys, qc), 0) + d * tk
                query = lax.broadcasted_iota(jnp.int32, (keys, qc), 1) + cq
                return jnp.where(key <= query, 0.0, NEG_INF)

            diag.append(block_steps(first_diag + d, causal, keys)[c])
    _run_steps(s_ref, diag, (), *state)

    ot = _normalised(acc_ref)
    for hd in range(ATTN_HEADS_PER_STEP):
        o_ref[0, :, hd * HEAD_W:(hd + 1) * HEAD_W] = ot[:, hd * tq:(hd + 1) * tq].T.astype(BF16)


def _mla_attention(qt, k, vt, k_meta, vt_meta, *, tq, tk):
    b, _, s_len, _ = k.shape
    nq = s_len // tq
    hp = ATTN_HEADS_PER_STEP
    qc = min(ATTN_QUERY_CHUNK, tq)
    vmem = hp * (2 * (s_len + META_BLK) * (MLA_QK_W + VT_ROWS) * 2 + 2 * tq * MLA_QK_W * 2
                 + 2 * tq * HEAD_W * 2 + (VT_ROWS + 8) * tq * 4 + tq * HEAD_W * 4
                 ) + (ATTN_SCORE_BUFFERS + 4) * tk * qc * 4
    return pl.pallas_call(
        functools.partial(_mla_kernel, tq=tq, tk=tk),
        out_shape=jax.ShapeDtypeStruct((b, s_len, DA_W), BF16),
        grid=(b, HEADS // hp, nq),
        in_specs=_attn_specs(tq, s_len, MLA_QK_W),
        out_specs=pl.BlockSpec((1, tq, hp * HEAD_W), lambda bi, h, qi: (bi, qi, h)),
        scratch_shapes=[pltpu.VMEM((ATTN_SCORE_BUFFERS, tk, qc), F32),
                        pltpu.VMEM((1, hp * tq), F32), pltpu.VMEM((VT_ROWS, hp * tq), F32)],
        compiler_params=_params(("parallel", "parallel", "arbitrary"), vmem),
        name="mla_attention",
    )(qt, k, vt, k_meta, vt_meta)


def _merge_kernel(h_ref, pre_ref, post_ref, yda_ref, ymla_ref, wg0_ref, wg1_ref, b0_ref, b1_ref,
                  wbd_ref, wbm_ref, wo_ref, o_ref, hn_ref):
    j = pl.program_id(1)
    last = pl.num_programs(1) - 1

    def partial_out(hn):
        g0 = jax.nn.sigmoid(_dot(hn, wg0_ref[...]) + b0_ref[...])
        g1 = jax.nn.sigmoid(_dot(hn, wg1_ref[...]) + b1_ref[...])
        merged = g0 * _dot(yda_ref[0], wbd_ref[...]) + g1 * _dot(ymla_ref[0], wbm_ref[...])
        return _dot(merged.astype(BF16), wo_ref[...])

    @pl.when(j == 0)
    def _():
        hn = _rms(h_ref[0], pre_ref[...]).astype(BF16)
        hn_ref[...] = hn
        o_ref[0] = partial_out(hn)

    @pl.when(jnp.logical_and(j > 0, j < last))
    def _():
        o_ref[0] += partial_out(hn_ref[...])

    @pl.when(j == last)
    def _():
        m = o_ref[0] + partial_out(hn_ref[...])
        o_ref[0] = h_ref[0] + _rms(m, post_ref[...])


def _merge(h1, pre_g, post_g, y_da, y_mla, w_gate, b_gate, wb_da, wb_mla, w_out):
    b, s_len, d = h1.shape
    yw = y_da.shape[-1]
    tm = _row_tile(s_len, MERGE_ROW_TILE_MAX)
    tn = _col_tile(d, MERGE_COL_TILE_MAX)
    nt = s_len // tm
    nj = d // tn
    assert nj >= 2, "the kernel treats the first and the last column tile as different steps"
    vmem = (2 * 2 * tm * d * 4 + tm * d * 2 + 2 * 2 * tm * yw * 2
            + 2 * (3 * d * tn + 2 * yw * tn) * 2 + 6 * tm * tn * 4 + tm * d * 4)
    return pl.pallas_call(
        _merge_kernel,
        out_shape=jax.ShapeDtypeStruct((b, s_len, d), F32),
        grid=(b * nt, nj),
        in_specs=[
            pl.BlockSpec((1, tm, d), lambda i, j: (i // nt, i % nt, 0)),
            _resident((1, d), lambda i, j: (0, 0)),
            _resident((1, d), lambda i, j: (0, 0)),
            pl.BlockSpec((1, tm, yw), lambda i, j: (i // nt, i % nt, 0)),
            pl.BlockSpec((1, tm, yw), lambda i, j: (i // nt, i % nt, 0)),
            pl.BlockSpec((d, tn), lambda i, j: (0, j)),
            pl.BlockSpec((d, tn), lambda i, j: (0, nj + j)),
            pl.BlockSpec((1, tn), lambda i, j: (0, j)),
            pl.BlockSpec((1, tn), lambda i, j: (0, nj + j)),
            pl.BlockSpec((yw, tn), lambda i, j: (0, j)),
            pl.BlockSpec((yw, tn), lambda i, j: (0, j)),
            pl.BlockSpec((tn, d), lambda i, j: (j, 0)),
        ],
        out_specs=pl.BlockSpec((1, tm, d), lambda i, j: (i // nt, i % nt, 0)),
        scratch_shapes=[pltpu.VMEM((tm, d), BF16)],
        compiler_params=_params(("parallel", "arbitrary"), vmem),
        name="gated_merge_out_proj",
    )(h1, pre_g, post_g, y_da, y_mla, w_gate, w_gate, b_gate, b_gate, wb_da, wb_mla, w_out)


def _rope_tables(pos):
    half = MLA_ROPE // 2
    pos = pos.astype(F32)
    inv = ROPE_THETA ** (-jnp.arange(half, dtype=F32) * 2.0 / MLA_ROPE)
    ang = pos[:, None] * inv[None, :]
    cos, sin = jnp.cos(ang), jnp.sin(ang)
    pad = jnp.zeros((pos.shape[0], HEAD_W - MLA_ROPE), F32)
    return (jnp.concatenate([cos, cos, pad], axis=-1),
            jnp.concatenate([-sin, sin, pad], axis=-1))


def _swap_halves(w):
    half = w.shape[-1] // 2
    return jnp.concatenate([w[..., half:], w[..., :half]], axis=-1)


def _pad_lanes(w, width):
    return jnp.concatenate([w, jnp.zeros(w.shape[:-1] + (width - w.shape[-1],), w.dtype)], axis=-1)


def kernel(x, meta_tokens, rel_bias_table, ffn1_pre_g, ffn1_post_g, ffn1_w_gate, ffn1_w_up, ffn1_w_down, mix_pre_g, mix_post_g, w_in, b_gate, da_lambda_q1, da_lambda_k1, da_lambda_q2, da_lambda_k2, da_sub_g, mla_q_norm_g, mla_kv_norm_g, mla_w_uq, mla_w_ukv, w_branch_da, w_branch_mla, w_out, ffn2_pre_g, ffn2_post_g, ffn2_w_gate, ffn2_w_up, ffn2_w_down):
    b, s_len, d = x.shape
    assert ffn1_pre_g.shape[0] == 1, "single-layer trunk"
    t = _col_tile(s_len, ATTN_KEY_BLOCK_MAX)
    row = lambda g: g.reshape(1, -1).astype(F32)

    meta_in = jnp.concatenate([meta_tokens.astype(x.dtype),
                               jnp.zeros((META_BLK - N_META, d), x.dtype)], axis=0)
    ffn1_w = (row(ffn1_pre_g), row(ffn1_post_g), ffn1_w_gate[0].astype(BF16),
              ffn1_w_up[0].astype(BF16), ffn1_w_down[0].astype(BF16))
    h1 = _ffn(x.reshape(b * s_len, d), *ffn1_w, name="ffn1").reshape(b, s_len, d)
    h1_meta = _ffn(meta_in, *ffn1_w, name="ffn1_meta").reshape(1, META_BLK, d)

    w = w_in[0]
    o_k, o_v, o_cq = DA_W, 2 * DA_W, 3 * DA_W
    o_kr = o_cq + MLA_Q_RANK + MLA_KV_RANK
    o_gate = o_kr + MLA_ROPE
    da_w = (row(mix_pre_g), w[:, :o_k].T.astype(BF16), w[:, o_k:o_v].astype(BF16),
            w[:, o_v:o_cq].T.astype(BF16))
    qt_da, k_da, vt_da = _da_proj(h1, *da_w, name="da_in_proj")
    _, k_da_meta, vt_da_meta = _da_proj(h1_meta, *da_w, name="da_in_proj_meta")

    kr = w[:, o_kr:o_gate]
    w_c = jnp.concatenate([w[:, o_cq:o_kr], _pad_lanes(kr, HEAD_W),
                           _pad_lanes(_swap_halves(kr), HEAD_W)], axis=-1).astype(BF16)
    uq = mla_w_uq[0].reshape(MLA_Q_RANK, HEADS, MLA_NOPE + MLA_ROPE)
    uq_rope = uq[..., MLA_NOPE:]
    w_uqt = jnp.concatenate([uq[..., :MLA_NOPE], uq_rope, _swap_halves(uq_rope)],
                            axis=-1).reshape(MLA_Q_RANK, HEADS * _UQ_HEAD_W).T.astype(BF16)
    ukv = mla_w_ukv[0].reshape(MLA_KV_RANK, HEADS, 2 * HEAD_W)
    w_uk = ukv[..., :HEAD_W].reshape(MLA_KV_RANK, DA_W).astype(BF16)
    w_uvt = ukv[..., HEAD_W:].reshape(MLA_KV_RANK, DA_W).T.astype(BF16)
    mla_w = (w_c, row(mla_q_norm_g), row(mla_kv_norm_g), w_uqt, w_uk, w_uvt)

    def mla_proj(h, pos, name):
        cos, sin = _rope_tables(pos)
        return _mla_proj(h, row(mix_pre_g), cos, sin, cos.T, sin.T, *mla_w, name=name)

    qt_mla, k_mla, vt_mla = mla_proj(h1, N_META + jnp.arange(s_len), "mla_in_proj")
    _, k_mla_meta, vt_mla_meta = mla_proj(h1_meta, jnp.arange(META_BLK), "mla_in_proj_meta")

    bias_d, bias_s, bias_m = _bias_tiles(rel_bias_table.astype(F32), t)
    lam_p = jnp.concatenate([da_lambda_q1, da_lambda_k1, da_lambda_q2, da_lambda_k2], axis=0).astype(F32)
    y_da = _da_attention(qt_da, k_da, vt_da, k_da_meta, vt_da_meta, bias_d, bias_s, bias_m, lam_p,
                         da_sub_g.reshape(HEAD_W, 1).astype(F32), t=t)
    y_mla = _mla_attention(qt_mla, k_mla, vt_mla, k_mla_meta, vt_mla_meta,
                           tq=_col_tile(s_len, 2 * t), tk=t)

    h2 = _merge(h1, row(mix_pre_g), row(mix_post_g), y_da, y_mla,
                w[:, o_gate:].astype(BF16), row(b_gate),
                w_branch_da[0].astype(BF16), w_branch_mla[0].astype(BF16), w_out[0].astype(BF16))

    out = _ffn(h2.reshape(b * s_len, d), row(ffn2_pre_g), row(ffn2_post_g),
               ffn2_w_gate[0].astype(BF16), ffn2_w_up[0].astype(BF16), ffn2_w_down[0].astype(BF16),
               name="ffn2")
    return out.reshape(b, s_len, d)
```

```python
import functools
import math
from typing import Any, Callable, NamedTuple, Optional

import numpy as np
import jax
import jax.numpy as jnp
from jax import lax
from jax.experimental import pallas as pl
from jax.experimental.pallas import tpu as pltpu

F32 = jnp.float32
BF16 = jnp.bfloat16

N_META = 16
RMS_EPS = 1e-6
NEG_INF = -1e30
HEADS = 8
HEAD_W = 128
DA_HEAD_DIM = 64
DA_W = HEADS * HEAD_W
MLA_Q_RANK = 768
MLA_KV_RANK = 512
MLA_NOPE = 128
MLA_ROPE = 64
MLA_QK_W = 256
ROPE_THETA = 10000.0
REL_BUCKETS = 32
REL_MAX_DIST = 128
LAMBDA_INIT = 0.8 - 0.6 * math.exp(-0.3 * 0)
LOG2_E = math.log2(math.e)

V7X_LANES = 128
V7X_BF16_SUBLANES = 16
V7X_VMEM_BYTES = 64 * 1024 * 1024
VMEM_CAP_BYTES = V7X_VMEM_BYTES - 6 * 1024 * 1024

META_BLK = V7X_LANES

FFN_ROW_TILE_MAX = 896
FFN_FF_TILE_MAX = 512
PROJ_ROW_TILE_MAX = 704
MERGE_ROW_TILE_MAX = 512
MERGE_COL_TILE_MAX = 512
ATTN_KEY_BLOCK_MAX = 512


def _row_tile(rows, max_tile, multiple=V7X_BF16_SUBLANES):
    best = None
    for t in range(multiple, max_tile + 1, multiple):
        if rows % t == 0:
            best = t
    assert best is not None, (rows, max_tile)
    return best


def _col_tile(cols, max_tile):
    return _row_tile(cols, max_tile, V7X_LANES)


def _params(semantics, vmem_bytes):
    return pltpu.CompilerParams(
        dimension_semantics=semantics,
        vmem_limit_bytes=int(min(max(vmem_bytes, 16 * 1024 * 1024), VMEM_CAP_BYTES)))


def _resident(block_shape, index_map):
    return pl.BlockSpec(block_shape, index_map, pipeline_mode=pl.Buffered(1))


def _rms(x, g):
    ms = jnp.mean(x * x, axis=-1, keepdims=True)
    return x * lax.rsqrt(ms + RMS_EPS) * g


def _dot(a, b):
    return jnp.dot(a, b, preferred_element_type=F32)


def _dot_nt(a, b):
    return lax.dot_general(a, b, (((1,), (1,)), ((), ())), preferred_element_type=F32)


def _ffn_tile(f):
    return _col_tile(f, FFN_FF_TILE_MAX)


def _gate_up_tiles(w_gate, w_up):
    d, f = w_gate.shape
    tf = _ffn_tile(f)
    return jnp.concatenate([w_gate.reshape(d, f // tf, tf), w_up.reshape(d, f // tf, tf)],
                           axis=2).reshape(d, 2 * f)


def _ffn_kernel(x_ref, pre_ref, post_ref, wgu_ref, wd_ref, o_ref, xn_ref):
    k = pl.program_id(1)
    last = pl.num_programs(1) - 1
    tf = wd_ref.shape[0]

    def partial_out(xn):
        gu = _dot(xn, wgu_ref[...])
        g, u = gu[:, :tf], gu[:, tf:]
        a = (g * jax.nn.sigmoid(g) * u).astype(BF16)
        return _dot(a, wd_ref[...])

    @pl.when(k == 0)
    def _():
        xn = _rms(x_ref[...], pre_ref[...]).astype(BF16)
        xn_ref[...] = xn
        o_ref[...] = partial_out(xn)

    @pl.when(jnp.logical_and(k > 0, k < last))
    def _():
        o_ref[...] += partial_out(xn_ref[...])

    @pl.when(k == last)
    def _():
        f = o_ref[...] + partial_out(xn_ref[...])
        o_ref[...] = x_ref[...] + 0.5 * _rms(f, post_ref[...])


def _ffn(x, pre_g, post_g, wgu, wd, *, name):
    rows, d = x.shape
    f = wd.shape[0]
    tm = _row_tile(rows, FFN_ROW_TILE_MAX)
    tf = _ffn_tile(f)
    assert f // tf >= 2, "the kernel treats the first and the last d_ff tile as different steps"
    vmem = (2 * 2 * tm * d * 4
            + tm * d * 2
            + 2 * 3 * d * tf * 2
            + 4 * tm * tf * 4
            + tm * d * 4)
    return pl.pallas_call(
        _ffn_kernel,
        out_shape=jax.ShapeDtypeStruct((rows, d), F32),
        grid=(rows // tm, f // tf),
        in_specs=[
            pl.BlockSpec((tm, d), lambda i, k: (i, 0)),
            _resident((1, d), lambda i, k: (0, 0)),
            _resident((1, d), lambda i, k: (0, 0)),
            pl.BlockSpec((d, 2 * tf), lambda i, k: (0, k)),
            pl.BlockSpec((tf, d), lambda i, k: (k, 0)),
        ],
        out_specs=pl.BlockSpec((tm, d), lambda i, k: (i, 0)),
        scratch_shapes=[pltpu.VMEM((tm, d), BF16)],
        compiler_params=_params(("parallel", "arbitrary"), vmem),
        name=name,
    )(x, pre_g, post_g, wgu, wd)


VT_ROWS = HEAD_W + V7X_BF16_SUBLANES


def _store_vt(vt_ref, vt):
    tm = vt.shape[1]
    vt_ref[0, :, 0:HEAD_W, :] = vt.reshape(HEADS, HEAD_W, tm).astype(BF16)
    extra = lax.broadcasted_iota(jnp.int32, (HEADS, VT_ROWS - HEAD_W, tm), 1)
    vt_ref[0, :, HEAD_W:VT_ROWS, :] = jnp.where(extra == 0, 1.0, 0.0).astype(BF16)


def _da_proj_kernel(h_ref, g_ref, wqt_ref, wk_ref, wvt_ref, qt_ref, k_ref, vt_ref):
    hn = _rms(h_ref[...], g_ref[...]).astype(BF16)
    tm = hn.shape[0]
    qt = _dot_nt(wqt_ref[...], hn) * (DA_HEAD_DIM ** -0.5 * LOG2_E)
    qt_ref[0] = qt.reshape(HEADS, HEAD_W, tm).astype(BF16)
    k = _dot(hn, wk_ref[...])
    for h in range(HEADS):
        k_ref[0, h] = k[:, h * HEAD_W:(h + 1) * HEAD_W].astype(BF16)
    _store_vt(vt_ref, _dot_nt(wvt_ref[...], hn))


_C_CKV = MLA_Q_RANK
_C_KR1 = _C_CKV + MLA_KV_RANK
_C_KR2 = _C_KR1 + HEAD_W
_C_END = _C_KR2 + HEAD_W
_UQ_HEAD_W = HEAD_W + 2 * MLA_ROPE


def _mla_proj_kernel(h_ref, g_ref, cos_ref, sin_ref, cost_ref, sint_ref, wc_ref, gq_ref, gkv_ref,
                     wuqt_ref, wuk_ref, wuvt_ref, qt_ref, k_ref, vt_ref):
    hn = _rms(h_ref[...], g_ref[...]).astype(BF16)
    tm = hn.shape[0]
    scale = (MLA_NOPE + MLA_ROPE) ** -0.5 * LOG2_E

    c_all = _dot(hn, wc_ref[...])

    def proj(lo, hi):
        return c_all[:, lo:hi]

    cq = _rms(proj(0, _C_CKV), gq_ref[...]).astype(BF16)
    qt = _dot_nt(wuqt_ref[...], cq)
    cos_t = cost_ref[0:MLA_ROPE, :]
    sin_t = sint_ref[0:MLA_ROPE, :]
    rope_end = HEAD_W + MLA_ROPE
    for h in range(HEADS):
        base = h * _UQ_HEAD_W
        nope = qt[base:base + HEAD_W]
        r1 = qt[base + HEAD_W:base + rope_end]
        r2 = qt[base + rope_end:base + _UQ_HEAD_W]
        qt_ref[0, h, 0:HEAD_W, :] = (nope * scale).astype(BF16)
        qt_ref[0, h, HEAD_W:rope_end, :] = ((r1 * cos_t + r2 * sin_t) * scale).astype(BF16)
        qt_ref[0, h, rope_end:MLA_QK_W, :] = jnp.zeros((MLA_QK_W - rope_end, tm), BF16)

    ckv = _rms(proj(_C_CKV, _C_KR1), gkv_ref[...]).astype(BF16)
    k_nope = _dot(ckv, wuk_ref[...])
    k_rope = (proj(_C_KR1, _C_KR2) * cos_ref[...] + proj(_C_KR2, _C_END) * sin_ref[...]).astype(BF16)
    for h in range(HEADS):
        k_ref[0, h, :, 0:HEAD_W] = k_nope[:, h * HEAD_W:(h + 1) * HEAD_W].astype(BF16)
        k_ref[0, h, :, HEAD_W:MLA_QK_W] = k_rope
    _store_vt(vt_ref, _dot_nt(wuvt_ref[...], ckv))


def _proj_specs(b, lp, tm, q_w):
    nt = lp // tm
    rows = lambda w: pl.BlockSpec((1, HEADS, tm, w), lambda i: (i // nt, 0, i % nt, 0))
    cols = lambda w: pl.BlockSpec((1, HEADS, w, tm), lambda i: (i // nt, 0, 0, i % nt))
    shapes = [jax.ShapeDtypeStruct((b, HEADS, q_w, lp), BF16),
              jax.ShapeDtypeStruct((b, HEADS, lp, q_w), BF16),
              jax.ShapeDtypeStruct((b, HEADS, VT_ROWS, lp), BF16)]
    return shapes, [cols(q_w), rows(q_w), cols(VT_ROWS)]


def _da_proj(h1, g, wqt, wk, wvt, *, name):
    b, lp, d = h1.shape
    tm = _col_tile(lp, PROJ_ROW_TILE_MAX)
    nt = lp // tm
    shapes, out_specs = _proj_specs(b, lp, tm, HEAD_W)
    vmem = (2 * tm * d * 4 + tm * d * 2 + 3 * DA_W * d * 2
            + 2 * 3 * tm * DA_W * 2 + 3 * tm * DA_W * 4)
    return pl.pallas_call(
        _da_proj_kernel,
        out_shape=shapes,
        grid=(b * nt,),
        in_specs=[
            pl.BlockSpec((None, tm, d), lambda i: (i // nt, i % nt, 0)),
            _resident((1, d), lambda i: (0, 0)),
            _resident(wqt.shape, lambda i: (0, 0)),
            _resident(wk.shape, lambda i: (0, 0)),
            _resident(wvt.shape, lambda i: (0, 0)),
        ],
        out_specs=out_specs,
        compiler_params=_params(("parallel",), vmem),
        name=name,
    )(h1, g, wqt, wk, wvt)


def _mla_proj(h1, g, cos, sin, cos_t, sin_t, wc, gq, gkv, wuqt, wuk, wuvt, *, name):
    b, lp, d = h1.shape
    tm = _col_tile(lp, PROJ_ROW_TILE_MAX)
    nt = lp // tm
    shapes, out_specs = _proj_specs(b, lp, tm, MLA_QK_W)
    vmem = (2 * tm * d * 4 + tm * d * 2
            + (d * _C_END + HEADS * _UQ_HEAD_W * MLA_Q_RANK + 2 * DA_W * MLA_KV_RANK) * 2
            + 2 * tm * HEADS * (2 * MLA_QK_W + HEAD_W) * 2
            + 2 * tm * HEADS * _UQ_HEAD_W * 4)
    return pl.pallas_call(
        _mla_proj_kernel,
        out_shape=shapes,
        grid=(b * nt,),
        in_specs=[
            pl.BlockSpec((None, tm, d), lambda i: (i // nt, i % nt, 0)),
            _resident((1, d), lambda i: (0, 0)),
            pl.BlockSpec((tm, HEAD_W), lambda i: (i % nt, 0)),
            pl.BlockSpec((tm, HEAD_W), lambda i: (i % nt, 0)),
            pl.BlockSpec((HEAD_W, tm), lambda i: (0, i % nt)),
            pl.BlockSpec((HEAD_W, tm), lambda i: (0, i % nt)),
            _resident(wc.shape, lambda i: (0, 0)),
            _resident((1, MLA_Q_RANK), lambda i: (0, 0)),
            _resident((1, MLA_KV_RANK), lambda i: (0, 0)),
            _resident(wuqt.shape, lambda i: (0, 0)),
            _resident(wuk.shape, lambda i: (0, 0)),
            _resident(wuvt.shape, lambda i: (0, 0)),
        ],
        out_specs=out_specs,
        compiler_params=_params(("parallel",), vmem),
        name=name,
    )(h1, g, cos, sin, cos_t, sin_t, wc, gq, gkv, wuqt, wuk, wuvt)


def _t5_bucket(rel):
    n = np.maximum(rel, 0)
    max_exact = REL_BUCKETS // 2
    n_f = np.maximum(n, 1).astype(np.float64)
    large = max_exact + (np.log(n_f / max_exact) / math.log(REL_MAX_DIST / max_exact)
                         * (REL_BUCKETS - max_exact)).astype(np.int32)
    large = np.minimum(large, REL_BUCKETS - 1)
    return np.where(n < max_exact, n, large).astype(np.int32)


def _bucket_tiles(t):
    j = np.arange(t)[:, None]
    i = np.arange(t)[None, :]
    diag = np.where(j <= i, _t5_bucket(i - j), -1)
    sub = _t5_bucket(t + i - j)
    assert _t5_bucket(np.array([t + 1]))[0] == REL_BUCKETS - 1
    jm = np.arange(META_BLK)[:, None]
    meta_first = np.where(jm < N_META, _t5_bucket(N_META + i - jm), -1)
    meta_rest = np.where(jm < N_META, REL_BUCKETS - 1, -1) + 0 * i
    return (diag.astype(np.int32), np.stack([np.full_like(sub, -1), sub]).astype(np.int32),
            np.stack([meta_first, meta_rest]).astype(np.int32))


def _bias_kernel(table_ref, bd_ref, bs_ref, bm_ref, od_ref, os_ref, om_ref, *, buckets):
    h = pl.program_id(0)
    far = table_ref[REL_BUCKETS - 1, h]

    def lookup(bucket_ref, out_ref, known):
        rows, cols = known.shape
        for r0 in range(0, rows, V7X_LANES):
            for c0 in range(0, cols, V7X_LANES):
                patch = (slice(r0, r0 + V7X_LANES), slice(c0, c0 + V7X_LANES))
                present = np.unique(known[patch])
                acc = jnp.zeros((V7X_LANES, V7X_LANES), F32)
                if present.size > 1 or 0 <= present[0] < REL_BUCKETS - 1:
                    bucket = bucket_ref[patch]
                    for b in present:
                        if 0 <= b < REL_BUCKETS - 1:
                            acc = jnp.where(bucket == int(b), (table_ref[int(b), h] - far) * LOG2_E, acc)
                    if present[0] < 0:
                        acc = jnp.where(bucket < 0, NEG_INF, acc)
                elif present[0] < 0:
                    acc = jnp.full((V7X_LANES, V7X_LANES), NEG_INF, F32)
                out_ref[patch] = acc

    bd, bs, bm = buckets
    lookup(bd_ref, od_ref.at[0], bd)
    for i in range(2):
        lookup(bs_ref.at[i], os_ref.at[0, i], bs[i])
        lookup(bm_ref.at[i], om_ref.at[0, i], bm[i])


def _bias_tiles(table, t):
    bd, bs, bm = _bucket_tiles(t)
    return pl.pallas_call(
        functools.partial(_bias_kernel, buckets=(bd, bs, bm)),
        out_shape=[jax.ShapeDtypeStruct((HEADS, t, t), F32),
                   jax.ShapeDtypeStruct((HEADS, 2, t, t), F32),
                   jax.ShapeDtypeStruct((HEADS, 2, META_BLK, t), F32)],
        grid=(HEADS,),
        in_specs=[
            pl.BlockSpec(memory_space=pltpu.SMEM),
            pl.BlockSpec((t, t), lambda h: (0, 0)),
            pl.BlockSpec((2, t, t), lambda h: (0, 0, 0)),
            pl.BlockSpec((2, META_BLK, t), lambda h: (0, 0, 0)),
        ],
        out_specs=[pl.BlockSpec((1, t, t), lambda h: (h, 0, 0)),
                   pl.BlockSpec((1, 2, t, t), lambda h: (h, 0, 0, 0)),
                   pl.BlockSpec((1, 2, META_BLK, t), lambda h: (h, 0, 0, 0))],
        compiler_params=_params(("parallel",), 12 * t * t * 4),
        name="rel_bias_tiles",
    )(table, jnp.asarray(bd), jnp.asarray(bs), jnp.asarray(bm))


ATTN_QUERY_CHUNK = 256


ATTN_SCORE_BUFFERS = 4
ATTN_HEADS_PER_STEP = 2


class _Step(NamedTuple):
    q: Callable[[], Any]
    k: Callable[[], Any]
    vt: Callable[[], Any]
    bias: Optional[Callable[[], Any]]
    chunk: int


def _block_steps(n_chunks, q, k, vt, bias=None):
    return [_Step(functools.partial(q, c), functools.partial(k, c), functools.partial(vt, c),
                  functools.partial(bias, c) if bias is not None else None, c)
            for c in range(n_chunks)]


def _issue_scores(s_ref, step, slot):
    k = step.k()
    s_ref[slot, 0:k.shape[0], :] = _dot(k, step.q())


def _prefetch_scores(s_ref, steps):
    for i, step in enumerate(steps[:s_ref.shape[0] - 1]):
        _issue_scores(s_ref, step, i)


def _run_steps(s_ref, steps, following, m_ref, acc_ref):
    n_buf, _, qc = s_ref.shape
    ahead = n_buf - 1
    assert not following or len(steps) % n_buf == 0
    seq = list(steps) + list(following)[:ahead]
    for i, step in enumerate(steps):
        if i + ahead < len(seq):
            _issue_scores(s_ref, seq[i + ahead], (i + ahead) % n_buf)
        vt = step.vt()
        s = s_ref[i % n_buf, 0:vt.shape[1], :]
        if step.bias is not None:
            s = s + step.bias()
        sl = pl.ds(step.chunk * qc, qc)
        m_prev = m_ref[:, sl]
        m_new = jnp.maximum(m_prev, jnp.max(s, axis=0, keepdims=True))
        alpha = jnp.exp2(m_prev - m_new)
        p = jnp.exp2(s - m_new)
        acc_ref[:, sl] = alpha * acc_ref[:, sl] + _dot(vt, p.astype(BF16))
        m_ref[:, sl] = m_new


def _attend_init(m_ref, acc_ref):
    m_ref[...] = jnp.full(m_ref.shape, NEG_INF, F32)
    acc_ref[...] = jnp.zeros(acc_ref.shape, F32)


FAR_UNROLL = 8


def _walk_far_blocks(n_far, body):
    def loop_body(i, carry):
        body(i * FAR_UNROLL, FAR_UNROLL)
        return carry

    lax.fori_loop(0, n_far >> (FAR_UNROLL.bit_length() - 1), loop_body, 0)
    count = FAR_UNROLL // 2
    while count >= 1:
        first = n_far - (n_far & (2 * count - 1))

        @pl.when((n_far & count) != 0)
        def _(first=first, count=count):
            body(first, count)

        count //= 2


def _normalised(acc_ref):
    acc = acc_ref[...]
    return acc[0:HEAD_W] * (1.0 / acc[HEAD_W:HEAD_W + 1])


def _da_kernel(qt_ref, k_ref, vt_ref, km_ref, vtm_ref, bd_ref, bs_ref, bm_ref, lam_ref, subg_ref,
               o_ref, qq_ref, s_ref, m_ref, acc_ref, *, t):
    qi = pl.program_id(2)
    feat = lax.broadcasted_iota(jnp.int32, (HEAD_W, 1), 0)
    for hd in range(ATTN_HEADS_PER_STEP):
        qt = qt_ref[0, hd]
        zero = jnp.zeros_like(qt)
        qq_ref[:, pl.ds(2 * t * hd, t)] = jnp.where(feat < DA_HEAD_DIM, qt, zero)
        qq_ref[:, pl.ds(2 * t * hd + t, t)] = jnp.where(feat >= DA_HEAD_DIM, qt, zero)
    qc = s_ref.shape[2]
    chunks_per_map = t // qc
    chunks_per_head = 2 * chunks_per_map
    n_chunks = ATTN_HEADS_PER_STEP * chunks_per_head

    def head(c):
        return c // chunks_per_head

    def cols(c):
        return pl.ds((c % chunks_per_map) * qc, qc)

    def q_chunk(c):
        return qq_ref[:, pl.ds(c * qc, qc)]

    def block_steps(kb, bias=None, keys=t):
        rows = pl.ds(pl.multiple_of(kb * t, t), keys)
        return _block_steps(n_chunks, q_chunk, lambda c: k_ref[0, head(c), rows, :],
                            lambda c: vt_ref[0, head(c), :, rows], bias)

    state = (m_ref, acc_ref)
    n_far = jnp.maximum(qi - 1, 0)
    near_sel = jnp.minimum(qi, 1)
    prev_kb = jnp.maximum(qi - 1, 0)
    meta_steps = _block_steps(n_chunks, q_chunk, lambda c: km_ref[0, head(c)],
                              lambda c: vtm_ref[0, head(c)],
                              lambda c: bm_ref[head(c), near_sel, :, cols(c)])
    _attend_init(*state)
    _prefetch_scores(s_ref, meta_steps)
    _run_steps(s_ref, meta_steps, block_steps(jnp.where(qi >= 2, 0, qi)), *state)

    def far(first, count):
        steps = [st for u in range(count) for st in block_steps(first + u)]
        last = first + count - 1
        _run_steps(s_ref, steps, block_steps(jnp.where(last == n_far - 1, qi, last + 1)), *state)

    _walk_far_blocks(n_far, far)

    diag = []
    for c in range(n_chunks):
        keys = (c % chunks_per_map + 1) * qc
        diag.append(block_steps(qi, lambda c, keys=keys: bd_ref[head(c), 0:keys, cols(c)], keys)[c])
    prev = block_steps(prev_kb, lambda c: bs_ref[head(c), near_sel, :, cols(c)])
    _run_steps(s_ref, diag + prev, (), *state)

    ot = _normalised(acc_ref)
    lp = lam_ref[...]
    lam = (jnp.exp(jnp.sum(lp[0:1] * lp[1:2], axis=-1, keepdims=True))
           - jnp.exp(jnp.sum(lp[2:3] * lp[3:4], axis=-1, keepdims=True)) + LAMBDA_INIT)
    for hd in range(ATTN_HEADS_PER_STEP):
        o1 = ot[:, 2 * t * hd:2 * t * hd + t]
        o2 = ot[:, 2 * t * hd + t:2 * t * (hd + 1)]
        d = o1 - lam * o2
        ms = jnp.mean(d * d, axis=0, keepdims=True)
        y = d * lax.rsqrt(ms + RMS_EPS) * subg_ref[...] * (1.0 - LAMBDA_INIT)
        o_ref[0, :, hd * HEAD_W:(hd + 1) * HEAD_W] = y.T.astype(BF16)


def _attn_specs(t, s_len, q_w):
    hp = ATTN_HEADS_PER_STEP
    return [
        pl.BlockSpec((1, hp, q_w, t), lambda bi, h, qi: (bi, h, 0, qi)),
        pl.BlockSpec((1, hp, s_len, q_w), lambda bi, h, qi: (bi, h, 0, 0)),
        pl.BlockSpec((1, hp, VT_ROWS, s_len), lambda bi, h, qi: (bi, h, 0, 0)),
        pl.BlockSpec((1, hp, META_BLK, q_w), lambda bi, h, qi: (0, h, 0, 0)),
        pl.BlockSpec((1, hp, VT_ROWS, META_BLK), lambda bi, h, qi: (0, h, 0, 0)),
    ]


def _da_attention(qt, k, vt, k_meta, vt_meta, bias_d, bias_s, bias_m, lam_p, sub_g, *, t):
    b, _, s_len, _ = k.shape
    nq = s_len // t
    hp = ATTN_HEADS_PER_STEP
    qc = min(ATTN_QUERY_CHUNK, t)
    vmem = hp * (2 * (s_len + META_BLK) * (HEAD_W + VT_ROWS) * 2 + 2 * 3 * t * t * 4
                 + 2 * 2 * t * META_BLK * 4
                 + 2 * 2 * t * HEAD_W * 2 + 2 * t * HEAD_W * 2 + (VT_ROWS + 8) * 2 * t * 4
                 ) + (ATTN_SCORE_BUFFERS + 4) * t * qc * 4
    return pl.pallas_call(
        functools.partial(_da_kernel, t=t),
        out_shape=jax.ShapeDtypeStruct((b, s_len, DA_W), BF16),
        grid=(b, HEADS // hp, nq),
        in_specs=_attn_specs(t, s_len, HEAD_W) + [
            pl.BlockSpec((hp, t, t), lambda bi, h, qi: (h, 0, 0)),
            pl.BlockSpec((hp, 2, t, t), lambda bi, h, qi: (h, 0, 0, 0)),
            pl.BlockSpec((hp, 2, META_BLK, t), lambda bi, h, qi: (h, 0, 0, 0)),
            _resident((4, DA_HEAD_DIM), lambda bi, h, qi: (0, 0)),
            _resident((HEAD_W, 1), lambda bi, h, qi: (0, 0)),
        ],
        out_specs=pl.BlockSpec((1, t, hp * HEAD_W), lambda bi, h, qi: (bi, qi, h)),
        scratch_shapes=[pltpu.VMEM((HEAD_W, hp * 2 * t), BF16),
                        pltpu.VMEM((ATTN_SCORE_BUFFERS, t, qc), F32),
                        pltpu.VMEM((1, hp * 2 * t), F32), pltpu.VMEM((VT_ROWS, hp * 2 * t), F32)],
        compiler_params=_params(("parallel", "parallel", "arbitrary"), vmem),
        name="diff_attention",
    )(qt, k, vt, k_meta, vt_meta, bias_d, bias_s, bias_m, lam_p, sub_g)


def _mla_kernel(qt_ref, k_ref, vt_ref, km_ref, vtm_ref, o_ref, s_ref, m_ref, acc_ref, *, tq, tk):
    qi = pl.program_id(2)
    qc = s_ref.shape[2]
    diag_blocks = tq // tk
    chunks_per_head = tq // qc
    n_chunks = ATTN_HEADS_PER_STEP * chunks_per_head

    def head(c):
        return c // chunks_per_head

    def q_chunk(c):
        return qt_ref[0, head(c), :, pl.ds((c % chunks_per_head) * qc, qc)]

    def block_steps(kb, bias=None, keys=tk):
        rows = pl.ds(pl.multiple_of(kb * tk, tk), keys)
        return _block_steps(n_chunks, q_chunk, lambda c: k_ref[0, head(c), rows, :],
                            lambda c: vt_ref[0, head(c), :, rows], bias)

    state = (m_ref, acc_ref)

    def meta_mask(c):
        key = lax.broadcasted_iota(jnp.int32, (META_BLK, qc), 0)
        return jnp.where(key < N_META, 0.0, NEG_INF)

    meta_steps = _block_steps(n_chunks, q_chunk, lambda c: km_ref[0, head(c)],
                              lambda c: vtm_ref[0, head(c)], meta_mask)
    _attend_init(*state)
    _prefetch_scores(s_ref, meta_steps)
    _run_steps(s_ref, meta_steps, block_steps(0), *state)

    def far(first, count):
        steps = [st for u in range(count) for st in block_steps(first + u)]
        _run_steps(s_ref, steps, block_steps(first + count), *state)

    first_diag = qi * diag_blocks
    _walk_far_blocks(first_diag, far)

    diag = []
    for d in range(diag_blocks):
        for c in range(n_chunks):
            cq = (c % chunks_per_head) * qc
            keys = min(tk, cq + qc - d * tk)
            if keys <= 0:
                continue

            def causal(c, d=d, keys=keys, cq=cq):
                key = lax.broadcasted_iota(jnp.int32, (keys, qc), 0) + d * tk
                query = lax.broadcasted_iota(jnp.int32, (keys, qc), 1) + cq
                return jnp.where(key <= query, 0.0, NEG_INF)

            diag.append(block_steps(first_diag + d, causal, keys)[c])
    _run_steps(s_ref, diag, (), *state)

    ot = _normalised(acc_ref)
    for hd in range(ATTN_HEADS_PER_STEP):
        o_ref[0, :, hd * HEAD_W:(hd + 1) * HEAD_W] = ot[:, hd * tq:(hd + 1) * tq].T.astype(BF16)


def _mla_attention(qt, k, vt, k_meta, vt_meta, *, tq, tk):
    b, _, s_len, _ = k.shape
    nq = s_len // tq
    hp = ATTN_HEADS_PER_STEP
    qc = min(ATTN_QUERY_CHUNK, tq)
    vmem = hp * (2 * (s_len + META_BLK) * (MLA_QK_W + VT_ROWS) * 2 + 2 * tq * MLA_QK_W * 2
                 + 2 * tq * HEAD_W * 2 + (VT_ROWS + 8) * tq * 4 + tq * HEAD_W * 4
                 ) + (ATTN_SCORE_BUFFERS + 4) * tk * qc * 4
    return pl.pallas_call(
        functools.partial(_mla_kernel, tq=tq, tk=tk),
        out_shape=jax.ShapeDtypeStruct((b, s_len, DA_W), BF16),
        grid=(b, HEADS // hp, nq),
        in_specs=_attn_specs(tq, s_len, MLA_QK_W),
        out_specs=pl.BlockSpec((1, tq, hp * HEAD_W), lambda bi, h, qi: (bi, qi, h)),
        scratch_shapes=[pltpu.VMEM((ATTN_SCORE_BUFFERS, tk, qc), F32),
                        pltpu.VMEM((1, hp * tq), F32), pltpu.VMEM((VT_ROWS, hp * tq), F32)],
        compiler_params=_params(("parallel", "parallel", "arbitrary"), vmem),
        name="mla_attention",
    )(qt, k, vt, k_meta, vt_meta)


def _merge_kernel(h_ref, pre_ref, post_ref, yda_ref, ymla_ref, wg0_ref, wg1_ref, b0_ref, b1_ref,
                  wbd_ref, wbm_ref, wo_ref, o_ref, hn_ref):
    j = pl.program_id(1)
    last = pl.num_programs(1) - 1

    def partial_out(hn):
        g0 = jax.nn.sigmoid(_dot(hn, wg0_ref[...]) + b0_ref[...])
        g1 = jax.nn.sigmoid(_dot(hn, wg1_ref[...]) + b1_ref[...])
        merged = g0 * _dot(yda_ref[0], wbd_ref[...]) + g1 * _dot(ymla_ref[0], wbm_ref[...])
        return _dot(merged.astype(BF16), wo_ref[...])

    @pl.when(j == 0)
    def _():
        hn = _rms(h_ref[0], pre_ref[...]).astype(BF16)
        hn_ref[...] = hn
        o_ref[0] = partial_out(hn)

    @pl.when(jnp.logical_and(j > 0, j < last))
    def _():
        o_ref[0] += partial_out(hn_ref[...])

    @pl.when(j == last)
    def _():
        m = o_ref[0] + partial_out(hn_ref[...])
        o_ref[0] = h_ref[0] + _rms(m, post_ref[...])


def _merge(h1, pre_g, post_g, y_da, y_mla, w_gate, b_gate, wb_da, wb_mla, w_out):
    b, s_len, d = h1.shape
    yw = y_da.shape[-1]
    tm = _row_tile(s_len, MERGE_ROW_TILE_MAX)
    tn = _col_tile(d, MERGE_COL_TILE_MAX)
    nt = s_len // tm
    nj = d // tn
    assert nj >= 2, "the kernel treats the first and the last column tile as different steps"
    vmem = (2 * 2 * tm * d * 4 + tm * d * 2 + 2 * 2 * tm * yw * 2
            + 2 * (3 * d * tn + 2 * yw * tn) * 2 + 6 * tm * tn * 4 + tm * d * 4)
    return pl.pallas_call(
        _merge_kernel,
        out_shape=jax.ShapeDtypeStruct((b, s_len, d), F32),
        grid=(b * nt, nj),
        in_specs=[
            pl.BlockSpec((1, tm, d), lambda i, j: (i // nt, i % nt, 0)),
            _resident((1, d), lambda i, j: (0, 0)),
            _resident((1, d), lambda i, j: (0, 0)),
            pl.BlockSpec((1, tm, yw), lambda i, j: (i // nt, i % nt, 0)),
            pl.BlockSpec((1, tm, yw), lambda i, j: (i // nt, i % nt, 0)),
            pl.BlockSpec((d, tn), lambda i, j: (0, j)),
            pl.BlockSpec((d, tn), lambda i, j: (0, nj + j)),
            pl.BlockSpec((1, tn), lambda i, j: (0, j)),
            pl.BlockSpec((1, tn), lambda i, j: (0, nj + j)),
            pl.BlockSpec((yw, tn), lambda i, j: (0, j)),
            pl.BlockSpec((yw, tn), lambda i, j: (0, j)),
            pl.BlockSpec((tn, d), lambda i, j: (j, 0)),
        ],
        out_specs=pl.BlockSpec((1, tm, d), lambda i, j: (i // nt, i % nt, 0)),
        scratch_shapes=[pltpu.VMEM((tm, d), BF16)],
        compiler_params=_params(("parallel", "arbitrary"), vmem),
        name="gated_merge_out_proj",
    )(h1, pre_g, post_g, y_da, y_mla, w_gate, w_gate, b_gate, b_gate, wb_da, wb_mla, w_out)


def _rope_tables(pos):
    half = MLA_ROPE // 2
    pos = pos.astype(F32)
    inv = ROPE_THETA ** (-jnp.arange(half, dtype=F32) * 2.0 / MLA_ROPE)
    ang = pos[:, None] * inv[None, :]
    cos, sin = jnp.cos(ang), jnp.sin(ang)
    pad = jnp.zeros((pos.shape[0], HEAD_W - MLA_ROPE), F32)
    return (jnp.concatenate([cos, cos, pad], axis=-1),
            jnp.concatenate([-sin, sin, pad], axis=-1))


def _swap_halves(w):
    half = w.shape[-1] // 2
    return jnp.concatenate([w[..., half:], w[..., :half]], axis=-1)


def _pad_lanes(w, width):
    return jnp.concatenate([w, jnp.zeros(w.shape[:-1] + (width - w.shape[-1],), w.dtype)], axis=-1)


def kernel(x, meta_tokens, rel_bias_table, ffn1_pre_g, ffn1_post_g, ffn1_w_gate, ffn1_w_up, ffn1_w_down, mix_pre_g, mix_post_g, w_in, b_gate, da_lambda_q1, da_lambda_k1, da_lambda_q2, da_lambda_k2, da_sub_g, mla_q_norm_g, mla_kv_norm_g, mla_w_uq, mla_w_ukv, w_branch_da, w_branch_mla, w_out, ffn2_pre_g, ffn2_post_g, ffn2_w_gate, ffn2_w_up, ffn2_w_down):
    b, s_len, d = x.shape
    assert ffn1_pre_g.shape[0] == 1, "single-layer trunk"
    t = _col_tile(s_len, ATTN_KEY_BLOCK_MAX)
    row = lambda g: g.reshape(1, -1).astype(F32)

    meta_in = jnp.concatenate([meta_tokens.astype(x.dtype),
                               jnp.zeros((META_BLK - N_META, d), x.dtype)], axis=0)
    ffn1_w = (row(ffn1_pre_g), row(ffn1_post_g),
              _gate_up_tiles(ffn1_w_gate[0], ffn1_w_up[0]).astype(BF16), ffn1_w_down[0].astype(BF16))
    h1 = _ffn(x.reshape(b * s_len, d), *ffn1_w, name="ffn1").reshape(b, s_len, d)
    h1_meta = _ffn(meta_in, *ffn1_w, name="ffn1_meta").reshape(1, META_BLK, d)

    w = w_in[0]
    o_k, o_v, o_cq = DA_W, 2 * DA_W, 3 * DA_W
    o_kr = o_cq + MLA_Q_RANK + MLA_KV_RANK
    o_gate = o_kr + MLA_ROPE
    da_w = (row(mix_pre_g), w[:, :o_k].T.astype(BF16), w[:, o_k:o_v].astype(BF16),
            w[:, o_v:o_cq].T.astype(BF16))
    qt_da, k_da, vt_da = _da_proj(h1, *da_w, name="da_in_proj")
    _, k_da_meta, vt_da_meta = _da_proj(h1_meta, *da_w, name="da_in_proj_meta")

    kr = w[:, o_kr:o_gate]
    w_c = jnp.concatenate([w[:, o_cq:o_kr], _pad_lanes(kr, HEAD_W),
                           _pad_lanes(_swap_halves(kr), HEAD_W)], axis=-1).astype(BF16)
    uq = mla_w_uq[0].reshape(MLA_Q_RANK, HEADS, MLA_NOPE + MLA_ROPE)
    uq_rope = uq[..., MLA_NOPE:]
    w_uqt = jnp.concatenate([uq[..., :MLA_NOPE], uq_rope, _swap_halves(uq_rope)],
                            axis=-1).reshape(MLA_Q_RANK, HEADS * _UQ_HEAD_W).T.astype(BF16)
    ukv = mla_w_ukv[0].reshape(MLA_KV_RANK, HEADS, 2 * HEAD_W)
    w_uk = ukv[..., :HEAD_W].reshape(MLA_KV_RANK, DA_W).astype(BF16)
    w_uvt = ukv[..., HEAD_W:].reshape(MLA_KV_RANK, DA_W).T.astype(BF16)
    mla_w = (w_c, row(mla_q_norm_g), row(mla_kv_norm_g), w_uqt, w_uk, w_uvt)

    def mla_proj(h, pos, name):
        cos, sin = _rope_tables(pos)
        return _mla_proj(h, row(mix_pre_g), cos, sin, cos.T, sin.T, *mla_w, name=name)

    qt_mla, k_mla, vt_mla = mla_proj(h1, N_META + jnp.arange(s_len), "mla_in_proj")
    _, k_mla_meta, vt_mla_meta = mla_proj(h1_meta, jnp.arange(META_BLK), "mla_in_proj_meta")

    bias_d, bias_s, bias_m = _bias_tiles(rel_bias_table.astype(F32), t)
    lam_p = jnp.concatenate([da_lambda_q1, da_lambda_k1, da_lambda_q2, da_lambda_k2], axis=0).astype(F32)
    y_da = _da_attention(qt_da, k_da, vt_da, k_da_meta, vt_da_meta, bias_d, bias_s, bias_m, lam_p,
                         da_sub_g.reshape(HEAD_W, 1).astype(F32), t=t)
    y_mla = _mla_attention(qt_mla, k_mla, vt_mla, k_mla_meta, vt_mla_meta,
                           tq=_col_tile(s_len, 2 * t), tk=t)

    h2 = _merge(h1, row(mix_pre_g), row(mix_post_g), y_da, y_mla,
                w[:, o_gate:].astype(BF16), row(b_gate),
                w_branch_da[0].astype(BF16), w_branch_mla[0].astype(BF16), w_out[0].astype(BF16))

    out = _ffn(h2.reshape(b * s_len, d), row(ffn2_pre_g), row(ffn2_post_g),
               _gate_up_tiles(ffn2_w_gate[0], ffn2_w_up[0]).astype(BF16), ffn2_w_down[0].astype(BF16),
               name="ffn2")
    return out.reshape(b, s_len, d)
```

```python
import functools
import math
from typing import Any, Callable, NamedTuple, Optional

import numpy as np
import jax
import jax.numpy as jnp
from jax import lax
from jax.experimental import pallas as pl
from jax.experimental.pallas import tpu as pltpu

F32 = jnp.float32
BF16 = jnp.bfloat16

N_META = 16
RMS_EPS = 1e-6
NEG_INF = -1e30
HEADS = 8
HEAD_W = 128
DA_HEAD_DIM = 64
DA_W = HEADS * HEAD_W
MLA_Q_RANK = 768
MLA_KV_RANK = 512
MLA_NOPE = 128
MLA_ROPE = 64
MLA_QK_W = 256
ROPE_THETA = 10000.0
REL_BUCKETS = 32
REL_MAX_DIST = 128
LAMBDA_INIT = 0.8 - 0.6 * math.exp(-0.3 * 0)
LOG2_E = math.log2(math.e)

V7X_LANES = 128
V7X_BF16_SUBLANES = 16
V7X_VMEM_BYTES = 64 * 1024 * 1024
VMEM_CAP_BYTES = V7X_VMEM_BYTES - 6 * 1024 * 1024

META_BLK = V7X_LANES

FFN_ROW_TILE_MAX = 896
FFN_FF_TILE_MAX = 512
PROJ_ROW_TILE_MAX = 704
MERGE_ROW_TILE_MAX = 512
MERGE_COL_TILE_MAX = 512
ATTN_KEY_BLOCK_MAX = 512


def _row_tile(rows, max_tile, multiple=V7X_BF16_SUBLANES):
    best = None
    for t in range(multiple, max_tile + 1, multiple):
        if rows % t == 0:
            best = t
    assert best is not None, (rows, max_tile)
    return best


def _col_tile(cols, max_tile):
    return _row_tile(cols, max_tile, V7X_LANES)


def _params(semantics, vmem_bytes):
    return pltpu.CompilerParams(
        dimension_semantics=semantics,
        vmem_limit_bytes=int(min(max(vmem_bytes, 16 * 1024 * 1024), VMEM_CAP_BYTES)))


def _resident(block_shape, index_map):
    return pl.BlockSpec(block_shape, index_map, pipeline_mode=pl.Buffered(1))


def _rms(x, g):
    ms = jnp.mean(x * x, axis=-1, keepdims=True)
    return x * lax.rsqrt(ms + RMS_EPS) * g


def _dot(a, b):
    return jnp.dot(a, b, preferred_element_type=F32)


def _dot_nt(a, b):
    return lax.dot_general(a, b, (((1,), (1,)), ((), ())), preferred_element_type=F32)


def _ffn_kernel(x_ref, pre_ref, post_ref, wg_ref, wu_ref, wd_ref, o_ref, xn_ref):
    k = pl.program_id(1)
    last = pl.num_programs(1) - 1

    def partial_out(xn):
        g = _dot(xn, wg_ref[...])
        u = _dot(xn, wu_ref[...])
        a = (g * jax.nn.sigmoid(g) * u).astype(BF16)
        return _dot(a, wd_ref[...])

    @pl.when(k == 0)
    def _():
        xn = _rms(x_ref[...], pre_ref[...]).astype(BF16)
        xn_ref[...] = xn
        o_ref[...] = partial_out(xn)

    @pl.when(jnp.logical_and(k > 0, k < last))
    def _():
        o_ref[...] += partial_out(xn_ref[...])

    @pl.when(k == last)
    def _():
        f = o_ref[...] + partial_out(xn_ref[...])
        o_ref[...] = x_ref[...] + 0.5 * _rms(f, post_ref[...])


def _ffn(x, pre_g, post_g, wg, wu, wd, *, name):
    rows, d = x.shape
    f = wg.shape[1]
    tm = _row_tile(rows, FFN_ROW_TILE_MAX)
    tf = _col_tile(f, FFN_FF_TILE_MAX)
    assert f // tf >= 2, "the kernel treats the first and the last d_ff tile as different steps"
    vmem = (2 * 2 * tm * d * 4
            + tm * d * 2
            + 2 * 3 * d * tf * 2
            + 4 * tm * tf * 4
            + tm * d * 4)
    return pl.pallas_call(
        _ffn_kernel,
        out_shape=jax.ShapeDtypeStruct((rows, d), F32),
        grid=(rows // tm, f // tf),
        in_specs=[
            pl.BlockSpec((tm, d), lambda i, k: (i, 0)),
            _resident((1, d), lambda i, k: (0, 0)),
            _resident((1, d), lambda i, k: (0, 0)),
            pl.BlockSpec((d, tf), lambda i, k: (0, k)),
            pl.BlockSpec((d, tf), lambda i, k: (0, k)),
            pl.BlockSpec((tf, d), lambda i, k: (k, 0)),
        ],
        out_specs=pl.BlockSpec((tm, d), lambda i, k: (i, 0)),
        scratch_shapes=[pltpu.VMEM((tm, d), BF16)],
        compiler_params=_params(("parallel", "arbitrary"), vmem),
        name=name,
    )(x, pre_g, post_g, wg, wu, wd)


VT_ROWS = HEAD_W + V7X_BF16_SUBLANES


def _store_vt(vt_ref, vt):
    tm = vt.shape[1]
    vt_ref[0, :, 0:HEAD_W, :] = vt.reshape(HEADS, HEAD_W, tm).astype(BF16)
    extra = lax.broadcasted_iota(jnp.int32, (HEADS, VT_ROWS - HEAD_W, tm), 1)
    vt_ref[0, :, HEAD_W:VT_ROWS, :] = jnp.where(extra == 0, 1.0, 0.0).astype(BF16)


def _da_proj_kernel(h_ref, g_ref, wqt_ref, wk_ref, wvt_ref, qt_ref, k_ref, vt_ref):
    hn = _rms(h_ref[...], g_ref[...]).astype(BF16)
    tm = hn.shape[0]
    qt = (_dot_nt(wqt_ref[...], hn) * (DA_HEAD_DIM ** -0.5 * LOG2_E)).reshape(HEADS, HEAD_W, tm)
    feat = lax.broadcasted_iota(jnp.int32, (1, HEAD_W, 1), 1)
    qt_ref[0, :, 0] = jnp.where(feat < DA_HEAD_DIM, qt, 0.0).astype(BF16)
    qt_ref[0, :, 1] = jnp.where(feat >= DA_HEAD_DIM, qt, 0.0).astype(BF16)
    k = _dot(hn, wk_ref[...])
    for h in range(HEADS):
        k_ref[0, h] = k[:, h * HEAD_W:(h + 1) * HEAD_W].astype(BF16)
    _store_vt(vt_ref, _dot_nt(wvt_ref[...], hn))


_C_CKV = MLA_Q_RANK
_C_KR1 = _C_CKV + MLA_KV_RANK
_C_KR2 = _C_KR1 + HEAD_W
_C_END = _C_KR2 + HEAD_W
_UQ_HEAD_W = HEAD_W + 2 * MLA_ROPE


def _mla_proj_kernel(h_ref, g_ref, cos_ref, sin_ref, cost_ref, sint_ref, wc_ref, gq_ref, gkv_ref,
                     wuqt_ref, wuk_ref, wuvt_ref, qt_ref, k_ref, vt_ref):
    hn = _rms(h_ref[...], g_ref[...]).astype(BF16)
    tm = hn.shape[0]
    scale = (MLA_NOPE + MLA_ROPE) ** -0.5 * LOG2_E

    c_all = _dot(hn, wc_ref[...])

    def proj(lo, hi):
        return c_all[:, lo:hi]

    cq = _rms(proj(0, _C_CKV), gq_ref[...]).astype(BF16)
    qt = _dot_nt(wuqt_ref[...], cq)
    cos_t = cost_ref[0:MLA_ROPE, :]
    sin_t = sint_ref[0:MLA_ROPE, :]
    rope_end = HEAD_W + MLA_ROPE
    for h in range(HEADS):
        base = h * _UQ_HEAD_W
        nope = qt[base:base + HEAD_W]
        r1 = qt[base + HEAD_W:base + rope_end]
        r2 = qt[base + rope_end:base + _UQ_HEAD_W]
        qt_ref[0, h, 0:HEAD_W, :] = (nope * scale).astype(BF16)
        qt_ref[0, h, HEAD_W:rope_end, :] = ((r1 * cos_t + r2 * sin_t) * scale).astype(BF16)
        qt_ref[0, h, rope_end:MLA_QK_W, :] = jnp.zeros((MLA_QK_W - rope_end, tm), BF16)

    ckv = _rms(proj(_C_CKV, _C_KR1), gkv_ref[...]).astype(BF16)
    k_nope = _dot(ckv, wuk_ref[...])
    k_rope = (proj(_C_KR1, _C_KR2) * cos_ref[...] + proj(_C_KR2, _C_END) * sin_ref[...]).astype(BF16)
    for h in range(HEADS):
        k_ref[0, h, :, 0:HEAD_W] = k_nope[:, h * HEAD_W:(h + 1) * HEAD_W].astype(BF16)
        k_ref[0, h, :, HEAD_W:MLA_QK_W] = k_rope
    _store_vt(vt_ref, _dot_nt(wuvt_ref[...], ckv))


def _proj_specs(b, lp, tm, q_w):
    nt = lp // tm
    rows = lambda w: pl.BlockSpec((1, HEADS, tm, w), lambda i: (i // nt, 0, i % nt, 0))
    cols = lambda w: pl.BlockSpec((1, HEADS, w, tm), lambda i: (i // nt, 0, 0, i % nt))
    shapes = [jax.ShapeDtypeStruct((b, HEADS, q_w, lp), BF16),
              jax.ShapeDtypeStruct((b, HEADS, lp, q_w), BF16),
              jax.ShapeDtypeStruct((b, HEADS, VT_ROWS, lp), BF16)]
    return shapes, [cols(q_w), rows(q_w), cols(VT_ROWS)]


def _da_proj(h1, g, wqt, wk, wvt, *, name):
    b, lp, d = h1.shape
    tm = _col_tile(lp, PROJ_ROW_TILE_MAX)
    nt = lp // tm
    shapes, out_specs = _proj_specs(b, lp, tm, HEAD_W)
    shapes[0] = jax.ShapeDtypeStruct((b, HEADS, 2, HEAD_W, lp), BF16)
    out_specs[0] = pl.BlockSpec((1, HEADS, 2, HEAD_W, tm), lambda i: (i // nt, 0, 0, 0, i % nt))
    vmem = (2 * tm * d * 4 + tm * d * 2 + 3 * DA_W * d * 2
            + 2 * 4 * tm * DA_W * 2 + 3 * tm * DA_W * 4)
    return pl.pallas_call(
        _da_proj_kernel,
        out_shape=shapes,
        grid=(b * nt,),
        in_specs=[
            pl.BlockSpec((None, tm, d), lambda i: (i // nt, i % nt, 0)),
            _resident((1, d), lambda i: (0, 0)),
            _resident(wqt.shape, lambda i: (0, 0)),
            _resident(wk.shape, lambda i: (0, 0)),
            _resident(wvt.shape, lambda i: (0, 0)),
        ],
        out_specs=out_specs,
        compiler_params=_params(("parallel",), vmem),
        name=name,
    )(h1, g, wqt, wk, wvt)


def _mla_proj(h1, g, cos, sin, cos_t, sin_t, wc, gq, gkv, wuqt, wuk, wuvt, *, name):
    b, lp, d = h1.shape
    tm = _col_tile(lp, PROJ_ROW_TILE_MAX)
    nt = lp // tm
    shapes, out_specs = _proj_specs(b, lp, tm, MLA_QK_W)
    vmem = (2 * tm * d * 4 + tm * d * 2
            + (d * _C_END + HEADS * _UQ_HEAD_W * MLA_Q_RANK + 2 * DA_W * MLA_KV_RANK) * 2
            + 2 * tm * HEADS * (2 * MLA_QK_W + HEAD_W) * 2
            + 2 * tm * HEADS * _UQ_HEAD_W * 4)
    return pl.pallas_call(
        _mla_proj_kernel,
        out_shape=shapes,
        grid=(b * nt,),
        in_specs=[
            pl.BlockSpec((None, tm, d), lambda i: (i // nt, i % nt, 0)),
            _resident((1, d), lambda i: (0, 0)),
            pl.BlockSpec((tm, HEAD_W), lambda i: (i % nt, 0)),
            pl.BlockSpec((tm, HEAD_W), lambda i: (i % nt, 0)),
            pl.BlockSpec((HEAD_W, tm), lambda i: (0, i % nt)),
            pl.BlockSpec((HEAD_W, tm), lambda i: (0, i % nt)),
            _resident(wc.shape, lambda i: (0, 0)),
            _resident((1, MLA_Q_RANK), lambda i: (0, 0)),
            _resident((1, MLA_KV_RANK), lambda i: (0, 0)),
            _resident(wuqt.shape, lambda i: (0, 0)),
            _resident(wuk.shape, lambda i: (0, 0)),
            _resident(wuvt.shape, lambda i: (0, 0)),
        ],
        out_specs=out_specs,
        compiler_params=_params(("parallel",), vmem),
        name=name,
    )(h1, g, cos, sin, cos_t, sin_t, wc, gq, gkv, wuqt, wuk, wuvt)


def _t5_bucket(rel):
    n = np.maximum(rel, 0)
    max_exact = REL_BUCKETS // 2
    n_f = np.maximum(n, 1).astype(np.float64)
    large = max_exact + (np.log(n_f / max_exact) / math.log(REL_MAX_DIST / max_exact)
                         * (REL_BUCKETS - max_exact)).astype(np.int32)
    large = np.minimum(large, REL_BUCKETS - 1)
    return np.where(n < max_exact, n, large).astype(np.int32)


def _bucket_tiles(t):
    j = np.arange(t)[:, None]
    i = np.arange(t)[None, :]
    diag = np.where(j <= i, _t5_bucket(i - j), -1)
    sub = _t5_bucket(t + i - j)
    assert _t5_bucket(np.array([t + 1]))[0] == REL_BUCKETS - 1
    jm = np.arange(META_BLK)[:, None]
    meta_first = np.where(jm < N_META, _t5_bucket(N_META + i - jm), -1)
    meta_rest = np.where(jm < N_META, REL_BUCKETS - 1, -1) + 0 * i
    return (diag.astype(np.int32), np.stack([np.full_like(sub, -1), sub]).astype(np.int32),
            np.stack([meta_first, meta_rest]).astype(np.int32))


def _bias_kernel(table_ref, bd_ref, bs_ref, bm_ref, od_ref, os_ref, om_ref, *, buckets):
    h = pl.program_id(0)
    far = table_ref[REL_BUCKETS - 1, h]

    def lookup(bucket_ref, out_ref, known):
        rows, cols = known.shape
        for r0 in range(0, rows, V7X_LANES):
            for c0 in range(0, cols, V7X_LANES):
                patch = (slice(r0, r0 + V7X_LANES), slice(c0, c0 + V7X_LANES))
                present = np.unique(known[patch])
                acc = jnp.zeros((V7X_LANES, V7X_LANES), F32)
                if present.size > 1 or 0 <= present[0] < REL_BUCKETS - 1:
                    bucket = bucket_ref[patch]
                    for b in present:
                        if 0 <= b < REL_BUCKETS - 1:
                            acc = jnp.where(bucket == int(b), (table_ref[int(b), h] - far) * LOG2_E, acc)
                    if present[0] < 0:
                        acc = jnp.where(bucket < 0, NEG_INF, acc)
                elif present[0] < 0:
                    acc = jnp.full((V7X_LANES, V7X_LANES), NEG_INF, F32)
                out_ref[patch] = acc

    bd, bs, bm = buckets
    lookup(bd_ref, od_ref.at[0], bd)
    for i in range(2):
        lookup(bs_ref.at[i], os_ref.at[0, i], bs[i])
        lookup(bm_ref.at[i], om_ref.at[0, i], bm[i])


def _bias_tiles(table, t):
    bd, bs, bm = _bucket_tiles(t)
    return pl.pallas_call(
        functools.partial(_bias_kernel, buckets=(bd, bs, bm)),
        out_shape=[jax.ShapeDtypeStruct((HEADS, t, t), F32),
                   jax.ShapeDtypeStruct((HEADS, 2, t, t), F32),
                   jax.ShapeDtypeStruct((HEADS, 2, META_BLK, t), F32)],
        grid=(HEADS,),
        in_specs=[
            pl.BlockSpec(memory_space=pltpu.SMEM),
            pl.BlockSpec((t, t), lambda h: (0, 0)),
            pl.BlockSpec((2, t, t), lambda h: (0, 0, 0)),
            pl.BlockSpec((2, META_BLK, t), lambda h: (0, 0, 0)),
        ],
        out_specs=[pl.BlockSpec((1, t, t), lambda h: (h, 0, 0)),
                   pl.BlockSpec((1, 2, t, t), lambda h: (h, 0, 0, 0)),
                   pl.BlockSpec((1, 2, META_BLK, t), lambda h: (h, 0, 0, 0))],
        compiler_params=_params(("parallel",), 12 * t * t * 4),
        name="rel_bias_tiles",
    )(table, jnp.asarray(bd), jnp.asarray(bs), jnp.asarray(bm))


ATTN_QUERY_CHUNK = 256


ATTN_SCORE_BUFFERS = 4
ATTN_HEADS_PER_STEP = 2


class _Step(NamedTuple):
    q: Callable[[], Any]
    k: Callable[[], Any]
    vt: Callable[[], Any]
    bias: Optional[Callable[[], Any]]
    chunk: int


def _block_steps(n_chunks, q, k, vt, bias=None):
    return [_Step(functools.partial(q, c), functools.partial(k, c), functools.partial(vt, c),
                  functools.partial(bias, c) if bias is not None else None, c)
            for c in range(n_chunks)]


def _issue_scores(s_ref, step, slot):
    k = step.k()
    s_ref[slot, 0:k.shape[0], :] = _dot(k, step.q())


def _prefetch_scores(s_ref, steps):
    for i, step in enumerate(steps[:s_ref.shape[0] - 1]):
        _issue_scores(s_ref, step, i)


def _run_steps(s_ref, steps, following, m_ref, acc_ref):
    n_buf, _, qc = s_ref.shape
    ahead = n_buf - 1
    assert not following or len(steps) % n_buf == 0
    seq = list(steps) + list(following)[:ahead]
    for i, step in enumerate(steps):
        if i + ahead < len(seq):
            _issue_scores(s_ref, seq[i + ahead], (i + ahead) % n_buf)
        vt = step.vt()
        s = s_ref[i % n_buf, 0:vt.shape[1], :]
        if step.bias is not None:
            s = s + step.bias()
        sl = pl.ds(step.chunk * qc, qc)
        m_prev = m_ref[:, sl]
        m_new = jnp.maximum(m_prev, jnp.max(s, axis=0, keepdims=True))
        alpha = jnp.exp2(m_prev - m_new)
        p = jnp.exp2(s - m_new)
        acc_ref[:, sl] = alpha * acc_ref[:, sl] + _dot(vt, p.astype(BF16))
        m_ref[:, sl] = m_new


def _attend_init(m_ref, acc_ref):
    m_ref[...] = jnp.full(m_ref.shape, NEG_INF, F32)
    acc_ref[...] = jnp.zeros(acc_ref.shape, F32)


FAR_UNROLL = 8


def _walk_far_blocks(n_far, body):
    def loop_body(i, carry):
        body(i * FAR_UNROLL, FAR_UNROLL)
        return carry

    lax.fori_loop(0, n_far >> (FAR_UNROLL.bit_length() - 1), loop_body, 0)
    count = FAR_UNROLL // 2
    while count >= 1:
        first = n_far - (n_far & (2 * count - 1))

        @pl.when((n_far & count) != 0)
        def _(first=first, count=count):
            body(first, count)

        count //= 2


def _normalised(acc_ref):
    acc = acc_ref[...]
    return acc[0:HEAD_W] * (1.0 / acc[HEAD_W:HEAD_W + 1])


def _da_kernel(qt_ref, k_ref, vt_ref, km_ref, vtm_ref, bd_ref, bs_ref, bm_ref, lam_ref, subg_ref,
               o_ref, s_ref, m_ref, acc_ref, *, t):
    qi = pl.program_id(2)
    qc = s_ref.shape[2]
    chunks_per_map = t // qc
    chunks_per_head = 2 * chunks_per_map
    n_chunks = ATTN_HEADS_PER_STEP * chunks_per_head

    def head(c):
        return c // chunks_per_head

    def cols(c):
        return pl.ds((c % chunks_per_map) * qc, qc)

    def q_chunk(c):
        in_head = c % chunks_per_head
        return qt_ref[0, head(c), in_head // chunks_per_map, :, cols(c)]

    def block_steps(kb, bias=None, keys=t):
        rows = pl.ds(pl.multiple_of(kb * t, t), keys)
        return _block_steps(n_chunks, q_chunk, lambda c: k_ref[0, head(c), rows, :],
                            lambda c: vt_ref[0, head(c), :, rows], bias)

    state = (m_ref, acc_ref)
    n_far = jnp.maximum(qi - 1, 0)
    near_sel = jnp.minimum(qi, 1)
    prev_kb = jnp.maximum(qi - 1, 0)
    meta_steps = _block_steps(n_chunks, q_chunk, lambda c: km_ref[0, head(c)],
                              lambda c: vtm_ref[0, head(c)],
                              lambda c: bm_ref[head(c), near_sel, :, cols(c)])
    _attend_init(*state)
    _prefetch_scores(s_ref, meta_steps)
    _run_steps(s_ref, meta_steps, block_steps(jnp.where(qi >= 2, 0, qi)), *state)

    def far(first, count):
        steps = [st for u in range(count) for st in block_steps(first + u)]
        last = first + count - 1
        _run_steps(s_ref, steps, block_steps(jnp.where(last == n_far - 1, qi, last + 1)), *state)

    _walk_far_blocks(n_far, far)

    diag = []
    for c in range(n_chunks):
        keys = (c % chunks_per_map + 1) * qc
        diag.append(block_steps(qi, lambda c, keys=keys: bd_ref[head(c), 0:keys, cols(c)], keys)[c])
    prev = block_steps(prev_kb, lambda c: bs_ref[head(c), near_sel, :, cols(c)])
    _run_steps(s_ref, diag + prev, (), *state)

    ot = _normalised(acc_ref)
    lp = lam_ref[...]
    lam = (jnp.exp(jnp.sum(lp[0:1] * lp[1:2], axis=-1, keepdims=True))
           - jnp.exp(jnp.sum(lp[2:3] * lp[3:4], axis=-1, keepdims=True)) + LAMBDA_INIT)
    for hd in range(ATTN_HEADS_PER_STEP):
        o1 = ot[:, 2 * t * hd:2 * t * hd + t]
        o2 = ot[:, 2 * t * hd + t:2 * t * (hd + 1)]
        d = o1 - lam * o2
        ms = jnp.mean(d * d, axis=0, keepdims=True)
        y = d * lax.rsqrt(ms + RMS_EPS) * subg_ref[...] * (1.0 - LAMBDA_INIT)
        o_ref[0, :, hd * HEAD_W:(hd + 1) * HEAD_W] = y.T.astype(BF16)


def _attn_specs(t, s_len, q_w):
    hp = ATTN_HEADS_PER_STEP
    return [
        pl.BlockSpec((1, hp, q_w, t), lambda bi, h, qi: (bi, h, 0, qi)),
        pl.BlockSpec((1, hp, s_len, q_w), lambda bi, h, qi: (bi, h, 0, 0)),
        pl.BlockSpec((1, hp, VT_ROWS, s_len), lambda bi, h, qi: (bi, h, 0, 0)),
        pl.BlockSpec((1, hp, META_BLK, q_w), lambda bi, h, qi: (0, h, 0, 0)),
        pl.BlockSpec((1, hp, VT_ROWS, META_BLK), lambda bi, h, qi: (0, h, 0, 0)),
    ]


def _da_attention(qt, k, vt, k_meta, vt_meta, bias_d, bias_s, bias_m, lam_p, sub_g, *, t):
    b, _, s_len, _ = k.shape
    nq = s_len // t
    hp = ATTN_HEADS_PER_STEP
    qc = min(ATTN_QUERY_CHUNK, t)
    vmem = hp * (2 * (s_len + META_BLK) * (HEAD_W + VT_ROWS) * 2 + 2 * 3 * t * t * 4
                 + 2 * 2 * t * META_BLK * 4
                 + 2 * 2 * t * HEAD_W * 2 + 2 * t * HEAD_W * 2 + (VT_ROWS + 8) * 2 * t * 4
                 ) + (ATTN_SCORE_BUFFERS + 4) * t * qc * 4
    return pl.pallas_call(
        functools.partial(_da_kernel, t=t),
        out_shape=jax.ShapeDtypeStruct((b, s_len, DA_W), BF16),
        grid=(b, HEADS // hp, nq),
        in_specs=[pl.BlockSpec((1, hp, 2, HEAD_W, t), lambda bi, h, qi: (bi, h, 0, 0, qi))]
        + _attn_specs(t, s_len, HEAD_W)[1:] + [
            pl.BlockSpec((hp, t, t), lambda bi, h, qi: (h, 0, 0)),
            pl.BlockSpec((hp, 2, t, t), lambda bi, h, qi: (h, 0, 0, 0)),
            pl.BlockSpec((hp, 2, META_BLK, t), lambda bi, h, qi: (h, 0, 0, 0)),
            _resident((4, DA_HEAD_DIM), lambda bi, h, qi: (0, 0)),
            _resident((HEAD_W, 1), lambda bi, h, qi: (0, 0)),
        ],
        out_specs=pl.BlockSpec((1, t, hp * HEAD_W), lambda bi, h, qi: (bi, qi, h)),
        scratch_shapes=[pltpu.VMEM((ATTN_SCORE_BUFFERS, t, qc), F32),
                        pltpu.VMEM((1, hp * 2 * t), F32), pltpu.VMEM((VT_ROWS, hp * 2 * t), F32)],
        compiler_params=_params(("parallel", "parallel", "arbitrary"), vmem),
        name="diff_attention",
    )(qt, k, vt, k_meta, vt_meta, bias_d, bias_s, bias_m, lam_p, sub_g)


def _mla_kernel(qt_ref, k_ref, vt_ref, km_ref, vtm_ref, o_ref, s_ref, m_ref, acc_ref, *, tq, tk):
    qi = pl.program_id(2)
    qc = s_ref.shape[2]
    diag_blocks = tq // tk
    chunks_per_head = tq // qc
    n_chunks = ATTN_HEADS_PER_STEP * chunks_per_head

    def head(c):
        return c // chunks_per_head

    def q_chunk(c):
        return qt_ref[0, head(c), :, pl.ds((c % chunks_per_head) * qc, qc)]

    def block_steps(kb, bias=None, keys=tk):
        rows = pl.ds(pl.multiple_of(kb * tk, tk), keys)
        return _block_steps(n_chunks, q_chunk, lambda c: k_ref[0, head(c), rows, :],
                            lambda c: vt_ref[0, head(c), :, rows], bias)

    state = (m_ref, acc_ref)

    def meta_mask(c):
        key = lax.broadcasted_iota(jnp.int32, (META_BLK, qc), 0)
        return jnp.where(key < N_META, 0.0, NEG_INF)

    meta_steps = _block_steps(n_chunks, q_chunk, lambda c: km_ref[0, head(c)],
                              lambda c: vtm_ref[0, head(c)], meta_mask)
    _attend_init(*state)
    _prefetch_scores(s_ref, meta_steps)
    _run_steps(s_ref, meta_steps, block_steps(0), *state)

    def far(first, count):
        steps = [st for u in range(count) for st in block_steps(first + u)]
        _run_steps(s_ref, steps, block_steps(first + count), *state)

    first_diag = qi * diag_blocks
    _walk_far_blocks(first_diag, far)

    diag = []
    for d in range(diag_blocks):
        for c in range(n_chunks):
            cq = (c % chunks_per_head) * qc
            keys = min(tk, cq + qc - d * tk)
            if keys <= 0:
                continue

            def causal(c, d=d, keys=keys, cq=cq):
                key = lax.broadcasted_iota(jnp.int32, (keys, qc), 0) + d * tk
                query = lax.broadcasted_iota(jnp.int32, (keys, qc), 1) + cq
                return jnp.where(key <= query, 0.0, NEG_INF)

            diag.append(block_steps(first_diag + d, causal, keys)[c])
    _run_steps(s_ref, diag, (), *state)

    ot = _normalised(acc_ref)
    for hd in range(ATTN_HEADS_PER_STEP):
        o_ref[0, :, hd * HEAD_W:(hd + 1) * HEAD_W] = ot[:, hd * tq:(hd + 1) * tq].T.astype(BF16)


def _mla_attention(qt, k, vt, k_meta, vt_meta, *, tq, tk):
    b, _, s_len, _ = k.shape
    nq = s_len // tq
    hp = ATTN_HEADS_PER_STEP
    qc = min(ATTN_QUERY_CHUNK, tq)
    vmem = hp * (2 * (s_len + META_BLK) * (MLA_QK_W + VT_ROWS) * 2 + 2 * tq * MLA_QK_W * 2
                 + 2 * tq * HEAD_W * 2 + (VT_ROWS + 8) * tq * 4 + tq * HEAD_W * 4
                 ) + (ATTN_SCORE_BUFFERS + 4) * tk * qc * 4
    return pl.pallas_call(
        functools.partial(_mla_kernel, tq=tq, tk=tk),
        out_shape=jax.ShapeDtypeStruct((b, s_len, DA_W), BF16),
        grid=(b, HEADS // hp, nq),
        in_specs=_attn_specs(tq, s_len, MLA_QK_W),
        out_specs=pl.BlockSpec((1, tq, hp * HEAD_W), lambda bi, h, qi: (bi, qi, h)),
        scratch_shapes=[pltpu.VMEM((ATTN_SCORE_BUFFERS, tk, qc), F32),
                        pltpu.VMEM((1, hp * tq), F32), pltpu.VMEM((VT_ROWS, hp * tq), F32)],
        compiler_params=_params(("parallel", "parallel", "arbitrary"), vmem),
        name="mla_attention",
    )(qt, k, vt, k_meta, vt_meta)


def _merge_kernel(h_ref, pre_ref, post_ref, yda_ref, ymla_ref, wg0_ref, wg1_ref, b0_ref, b1_ref,
                  wbd_ref, wbm_ref, wo_ref, o_ref, hn_ref):
    j = pl.program_id(1)
    last = pl.num_programs(1) - 1

    def partial_out(hn):
        g0 = jax.nn.sigmoid(_dot(hn, wg0_ref[...]) + b0_ref[...])
        g1 = jax.nn.sigmoid(_dot(hn, wg1_ref[...]) + b1_ref[...])
        merged = g0 * _dot(yda_ref[0], wbd_ref[...]) + g1 * _dot(ymla_ref[0], wbm_ref[...])
        return _dot(merged.astype(BF16), wo_ref[...])

    @pl.when(j == 0)
    def _():
        hn = _rms(h_ref[0], pre_ref[...]).astype(BF16)
        hn_ref[...] = hn
        o_ref[0] = partial_out(hn)

    @pl.when(jnp.logical_and(j > 0, j < last))
    def _():
        o_ref[0] += partial_out(hn_ref[...])

    @pl.when(j == last)
    def _():
        m = o_ref[0] + partial_out(hn_ref[...])
        o_ref[0] = h_ref[0] + _rms(m, post_ref[...])


def _merge(h1, pre_g, post_g, y_da, y_mla, w_gate, b_gate, wb_da, wb_mla, w_out):
    b, s_len, d = h1.shape
    yw = y_da.shape[-1]
    tm = _row_tile(s_len, MERGE_ROW_TILE_MAX)
    tn = _col_tile(d, MERGE_COL_TILE_MAX)
    nt = s_len // tm
    nj = d // tn
    assert nj >= 2, "the kernel treats the first and the last column tile as different steps"
    vmem = (2 * 2 * tm * d * 4 + tm * d * 2 + 2 * 2 * tm * yw * 2
            + 2 * (3 * d * tn + 2 * yw * tn) * 2 + 6 * tm * tn * 4 + tm * d * 4)
    return pl.pallas_call(
        _merge_kernel,
        out_shape=jax.ShapeDtypeStruct((b, s_len, d), F32),
        grid=(b * nt, nj),
        in_specs=[
            pl.BlockSpec((1, tm, d), lambda i, j: (i // nt, i % nt, 0)),
            _resident((1, d), lambda i, j: (0, 0)),
            _resident((1, d), lambda i, j: (0, 0)),
            pl.BlockSpec((1, tm, yw), lambda i, j: (i // nt, i % nt, 0)),
            pl.BlockSpec((1, tm, yw), lambda i, j: (i // nt, i % nt, 0)),
            pl.BlockSpec((d, tn), lambda i, j: (0, j)),
            pl.BlockSpec((d, tn), lambda i, j: (0, nj + j)),
            pl.BlockSpec((1, tn), lambda i, j: (0, j)),
            pl.BlockSpec((1, tn), lambda i, j: (0, nj + j)),
            pl.BlockSpec((yw, tn), lambda i, j: (0, j)),
            pl.BlockSpec((yw, tn), lambda i, j: (0, j)),
            pl.BlockSpec((tn, d), lambda i, j: (j, 0)),
        ],
        out_specs=pl.BlockSpec((1, tm, d), lambda i, j: (i // nt, i % nt, 0)),
        scratch_shapes=[pltpu.VMEM((tm, d), BF16)],
        compiler_params=_params(("parallel", "arbitrary"), vmem),
        name="gated_merge_out_proj",
    )(h1, pre_g, post_g, y_da, y_mla, w_gate, w_gate, b_gate, b_gate, wb_da, wb_mla, w_out)


def _rope_tables(pos):
    half = MLA_ROPE // 2
    pos = pos.astype(F32)
    inv = ROPE_THETA ** (-jnp.arange(half, dtype=F32) * 2.0 / MLA_ROPE)
    ang = pos[:, None] * inv[None, :]
    cos, sin = jnp.cos(ang), jnp.sin(ang)
    pad = jnp.zeros((pos.shape[0], HEAD_W - MLA_ROPE), F32)
    return (jnp.concatenate([cos, cos, pad], axis=-1),
            jnp.concatenate([-sin, sin, pad], axis=-1))


def _swap_halves(w):
    half = w.shape[-1] // 2
    return jnp.concatenate([w[..., half:], w[..., :half]], axis=-1)


def _pad_lanes(w, width):
    return jnp.concatenate([w, jnp.zeros(w.shape[:-1] + (width - w.shape[-1],), w.dtype)], axis=-1)


def kernel(x, meta_tokens, rel_bias_table, ffn1_pre_g, ffn1_post_g, ffn1_w_gate, ffn1_w_up, ffn1_w_down, mix_pre_g, mix_post_g, w_in, b_gate, da_lambda_q1, da_lambda_k1, da_lambda_q2, da_lambda_k2, da_sub_g, mla_q_norm_g, mla_kv_norm_g, mla_w_uq, mla_w_ukv, w_branch_da, w_branch_mla, w_out, ffn2_pre_g, ffn2_post_g, ffn2_w_gate, ffn2_w_up, ffn2_w_down):
    b, s_len, d = x.shape
    assert ffn1_pre_g.shape[0] == 1, "single-layer trunk"
    t = _col_tile(s_len, ATTN_KEY_BLOCK_MAX)
    row = lambda g: g.reshape(1, -1).astype(F32)

    meta_in = jnp.concatenate([meta_tokens.astype(x.dtype),
                               jnp.zeros((META_BLK - N_META, d), x.dtype)], axis=0)
    ffn1_w = (row(ffn1_pre_g), row(ffn1_post_g), ffn1_w_gate[0].astype(BF16),
              ffn1_w_up[0].astype(BF16), ffn1_w_down[0].astype(BF16))
    h1 = _ffn(x.reshape(b * s_len, d), *ffn1_w, name="ffn1").reshape(b, s_len, d)
    h1_meta = _ffn(meta_in, *ffn1_w, name="ffn1_meta").reshape(1, META_BLK, d)

    w = w_in[0]
    o_k, o_v, o_cq = DA_W, 2 * DA_W, 3 * DA_W
    o_kr = o_cq + MLA_Q_RANK + MLA_KV_RANK
    o_gate = o_kr + MLA_ROPE
    da_w = (row(mix_pre_g), w[:, :o_k].T.astype(BF16), w[:, o_k:o_v].astype(BF16),
            w[:, o_v:o_cq].T.astype(BF16))
    qt_da, k_da, vt_da = _da_proj(h1, *da_w, name="da_in_proj")
    _, k_da_meta, vt_da_meta = _da_proj(h1_meta, *da_w, name="da_in_proj_meta")

    kr = w[:, o_kr:o_gate]
    w_c = jnp.concatenate([w[:, o_cq:o_kr], _pad_lanes(kr, HEAD_W),
                           _pad_lanes(_swap_halves(kr), HEAD_W)], axis=-1).astype(BF16)
    uq = mla_w_uq[0].reshape(MLA_Q_RANK, HEADS, MLA_NOPE + MLA_ROPE)
    uq_rope = uq[..., MLA_NOPE:]
    w_uqt = jnp.concatenate([uq[..., :MLA_NOPE], uq_rope, _swap_halves(uq_rope)],
                            axis=-1).reshape(MLA_Q_RANK, HEADS * _UQ_HEAD_W).T.astype(BF16)
    ukv = mla_w_ukv[0].reshape(MLA_KV_RANK, HEADS, 2 * HEAD_W)
    w_uk = ukv[..., :HEAD_W].reshape(MLA_KV_RANK, DA_W).astype(BF16)
    w_uvt = ukv[..., HEAD_W:].reshape(MLA_KV_RANK, DA_W).T.astype(BF16)
    mla_w = (w_c, row(mla_q_norm_g), row(mla_kv_norm_g), w_uqt, w_uk, w_uvt)

    def mla_proj(h, pos, name):
        cos, sin = _rope_tables(pos)
        return _mla_proj(h, row(mix_pre_g), cos, sin, cos.T, sin.T, *mla_w, name=name)

    qt_mla, k_mla, vt_mla = mla_proj(h1, N_META + jnp.arange(s_len), "mla_in_proj")
    _, k_mla_meta, vt_mla_meta = mla_proj(h1_meta, jnp.arange(META_BLK), "mla_in_proj_meta")

    bias_d, bias_s, bias_m = _bias_tiles(rel_bias_table.astype(F32), t)
    lam_p = jnp.concatenate([da_lambda_q1, da_lambda_k1, da_lambda_q2, da_lambda_k2], axis=0).astype(F32)
    y_da = _da_attention(qt_da, k_da, vt_da, k_da_meta, vt_da_meta, bias_d, bias_s, bias_m, lam_p,
                         da_sub_g.reshape(HEAD_W, 1).astype(F32), t=t)
    y_mla = _mla_attention(qt_mla, k_mla, vt_mla, k_mla_meta, vt_mla_meta,
                           tq=_col_tile(s_len, 2 * t), tk=t)

    h2 = _merge(h1, row(mix_pre_g), row(mix_post_g), y_da, y_mla,
                w[:, o_gate:].astype(BF16), row(b_gate),
                w_branch_da[0].astype(BF16), w_branch_mla[0].astype(BF16), w_out[0].astype(BF16))

    out = _ffn(h2.reshape(b * s_len, d), row(ffn2_pre_g), row(ffn2_post_g),
               ffn2_w_gate[0].astype(BF16), ffn2_w_up[0].astype(BF16), ffn2_w_down[0].astype(BF16),
               name="ffn2")
    return out.reshape(b, s_len, d)
```

```python
import functools
import math
from typing import Any, Callable, NamedTuple, Optional

import numpy as np
import jax
import jax.numpy as jnp
from jax import lax
from jax.experimental import pallas as pl
from jax.experimental.pallas import tpu as pltpu

F32 = jnp.float32
BF16 = jnp.bfloat16

N_META = 16
RMS_EPS = 1e-6
NEG_INF = -1e30
HEADS = 8
HEAD_W = 128
DA_HEAD_DIM = 64
DA_W = HEADS * HEAD_W
MLA_Q_RANK = 768
MLA_KV_RANK = 512
MLA_NOPE = 128
MLA_ROPE = 64
MLA_QK_W = 256
ROPE_THETA = 10000.0
REL_BUCKETS = 32
REL_MAX_DIST = 128
LAMBDA_INIT = 0.8 - 0.6 * math.exp(-0.3 * 0)
LOG2_E = math.log2(math.e)

V7X_LANES = 128
V7X_BF16_SUBLANES = 16
V7X_VMEM_BYTES = 64 * 1024 * 1024
VMEM_CAP_BYTES = V7X_VMEM_BYTES - 6 * 1024 * 1024

META_BLK = V7X_LANES

FFN_ROW_TILE_MAX = 896
FFN_FF_TILE_MAX = 512
PROJ_ROW_TILE_MAX = 704
MERGE_ROW_TILE_MAX = 512
MERGE_COL_TILE_MAX = 512
ATTN_KEY_BLOCK_MAX = 512


def _row_tile(rows, max_tile, multiple=V7X_BF16_SUBLANES):
    best = None
    for t in range(multiple, max_tile + 1, multiple):
        if rows % t == 0:
            best = t
    assert best is not None, (rows, max_tile)
    return best


def _col_tile(cols, max_tile):
    return _row_tile(cols, max_tile, V7X_LANES)


def _params(semantics, vmem_bytes):
    return pltpu.CompilerParams(
        dimension_semantics=semantics,
        vmem_limit_bytes=int(min(max(vmem_bytes, 16 * 1024 * 1024), VMEM_CAP_BYTES)))


def _resident(block_shape, index_map):
    return pl.BlockSpec(block_shape, index_map, pipeline_mode=pl.Buffered(1))


def _rms(x, g):
    ms = jnp.mean(x * x, axis=-1, keepdims=True)
    return x * lax.rsqrt(ms + RMS_EPS) * g


def _dot(a, b):
    return jnp.dot(a, b, preferred_element_type=F32)


def _dot_nt(a, b):
    return lax.dot_general(a, b, (((1,), (1,)), ((), ())), preferred_element_type=F32)


def _ffn_kernel(x_ref, pre_ref, post_ref, wg_ref, wu_ref, wd_ref, o_ref, xn_ref):
    k = pl.program_id(1)
    last = pl.num_programs(1) - 1

    def partial_out(xn):
        g = _dot(xn, wg_ref[...])
        u = _dot(xn, wu_ref[...])
        a = (g * jax.nn.sigmoid(g) * u).astype(BF16)
        return _dot(a, wd_ref[...])

    @pl.when(k == 0)
    def _():
        xn = _rms(x_ref[...], pre_ref[...]).astype(BF16)
        xn_ref[...] = xn
        o_ref[...] = partial_out(xn)

    @pl.when(jnp.logical_and(k > 0, k < last))
    def _():
        o_ref[...] += partial_out(xn_ref[...])

    @pl.when(k == last)
    def _():
        f = o_ref[...] + partial_out(xn_ref[...])
        o_ref[...] = x_ref[...] + 0.5 * _rms(f, post_ref[...])


def _ffn(x, pre_g, post_g, wg, wu, wd, *, name):
    rows, d = x.shape
    f = wg.shape[1]
    tm = _row_tile(rows, FFN_ROW_TILE_MAX)
    tf = _col_tile(f, FFN_FF_TILE_MAX)
    assert f // tf >= 2, "the kernel treats the first and the last d_ff tile as different steps"
    vmem = (2 * 2 * tm * d * 4
            + tm * d * 2
            + 2 * 3 * d * tf * 2
            + 4 * tm * tf * 4
            + tm * d * 4)
    return pl.pallas_call(
        _ffn_kernel,
        out_shape=jax.ShapeDtypeStruct((rows, d), F32),
        grid=(rows // tm, f // tf),
        in_specs=[
            pl.BlockSpec((tm, d), lambda i, k: (i, 0)),
            _resident((1, d), lambda i, k: (0, 0)),
            _resident((1, d), lambda i, k: (0, 0)),
            pl.BlockSpec((d, tf), lambda i, k: (0, k)),
            pl.BlockSpec((d, tf), lambda i, k: (0, k)),
            pl.BlockSpec((tf, d), lambda i, k: (k, 0)),
        ],
        out_specs=pl.BlockSpec((tm, d), lambda i, k: (i, 0)),
        scratch_shapes=[pltpu.VMEM((tm, d), BF16)],
        compiler_params=_params(("parallel", "arbitrary"), vmem),
        name=name,
    )(x, pre_g, post_g, wg, wu, wd)


VT_ROWS = HEAD_W + V7X_BF16_SUBLANES


def _store_vt(vt_ref, vt):
    tm = vt.shape[1]
    vt_ref[0, :, 0:HEAD_W, :] = vt.reshape(HEADS, HEAD_W, tm).astype(BF16)
    extra = lax.broadcasted_iota(jnp.int32, (HEADS, VT_ROWS - HEAD_W, tm), 1)
    vt_ref[0, :, HEAD_W:VT_ROWS, :] = jnp.where(extra == 0, 1.0, 0.0).astype(BF16)


def _da_proj_kernel(h_ref, g_ref, wqt_ref, wk_ref, wvt_ref, qt_ref, k_ref, vt_ref):
    hn = _rms(h_ref[...], g_ref[...]).astype(BF16)
    tm = hn.shape[0]
    qt = _dot_nt(wqt_ref[...], hn) * (DA_HEAD_DIM ** -0.5 * LOG2_E)
    qt_ref[0] = qt.reshape(HEADS, HEAD_W, tm).astype(BF16)
    k = _dot(hn, wk_ref[...])
    for h in range(HEADS):
        k_ref[0, h] = k[:, h * HEAD_W:(h + 1) * HEAD_W].astype(BF16)
    _store_vt(vt_ref, _dot_nt(wvt_ref[...], hn))


_C_CKV = MLA_Q_RANK
_C_KR1 = _C_CKV + MLA_KV_RANK
_C_KR2 = _C_KR1 + HEAD_W
_C_END = _C_KR2 + HEAD_W
_UQ_HEAD_W = HEAD_W + 2 * MLA_ROPE


def _mla_proj_kernel(h_ref, g_ref, cos_ref, sin_ref, cost_ref, sint_ref, wc_ref, gq_ref, gkv_ref,
                     wuqt_ref, wuk_ref, wuvt_ref, qt_ref, k_ref, vt_ref):
    hn = _rms(h_ref[...], g_ref[...]).astype(BF16)
    tm = hn.shape[0]
    scale = (MLA_NOPE + MLA_ROPE) ** -0.5 * LOG2_E

    c_all = _dot(hn, wc_ref[...])

    def proj(lo, hi):
        return c_all[:, lo:hi]

    cq = _rms(proj(0, _C_CKV), gq_ref[...]).astype(BF16)
    qt = _dot_nt(wuqt_ref[...], cq)
    cos_t = cost_ref[0:MLA_ROPE, :]
    sin_t = sint_ref[0:MLA_ROPE, :]
    rope_end = HEAD_W + MLA_ROPE
    for h in range(HEADS):
        base = h * _UQ_HEAD_W
        nope = qt[base:base + HEAD_W]
        r1 = qt[base + HEAD_W:base + rope_end]
        r2 = qt[base + rope_end:base + _UQ_HEAD_W]
        qt_ref[0, h, 0:HEAD_W, :] = (nope * scale).astype(BF16)
        qt_ref[0, h, HEAD_W:rope_end, :] = ((r1 * cos_t + r2 * sin_t) * scale).astype(BF16)
        qt_ref[0, h, rope_end:MLA_QK_W, :] = jnp.zeros((MLA_QK_W - rope_end, tm), BF16)

    ckv = _rms(proj(_C_CKV, _C_KR1), gkv_ref[...]).astype(BF16)
    k_nope = _dot(ckv, wuk_ref[...])
    k_rope = (proj(_C_KR1, _C_KR2) * cos_ref[...] + proj(_C_KR2, _C_END) * sin_ref[...]).astype(BF16)
    for h in range(HEADS):
        k_ref[0, h, :, 0:HEAD_W] = k_nope[:, h * HEAD_W:(h + 1) * HEAD_W].astype(BF16)
        k_ref[0, h, :, HEAD_W:MLA_QK_W] = k_rope
    _store_vt(vt_ref, _dot_nt(wuvt_ref[...], ckv))


def _proj_specs(b, lp, tm, q_w):
    nt = lp // tm
    rows = lambda w: pl.BlockSpec((1, HEADS, tm, w), lambda i: (i // nt, 0, i % nt, 0))
    cols = lambda w: pl.BlockSpec((1, HEADS, w, tm), lambda i: (i // nt, 0, 0, i % nt))
    shapes = [jax.ShapeDtypeStruct((b, HEADS, q_w, lp), BF16),
              jax.ShapeDtypeStruct((b, HEADS, lp, q_w), BF16),
              jax.ShapeDtypeStruct((b, HEADS, VT_ROWS, lp), BF16)]
    return shapes, [cols(q_w), rows(q_w), cols(VT_ROWS)]


def _da_proj(h1, g, wqt, wk, wvt, *, name):
    b, lp, d = h1.shape
    tm = _col_tile(lp, PROJ_ROW_TILE_MAX)
    nt = lp // tm
    shapes, out_specs = _proj_specs(b, lp, tm, HEAD_W)
    vmem = (2 * tm * d * 4 + tm * d * 2 + 3 * DA_W * d * 2
            + 2 * 3 * tm * DA_W * 2 + 3 * tm * DA_W * 4)
    return pl.pallas_call(
        _da_proj_kernel,
        out_shape=shapes,
        grid=(b * nt,),
        in_specs=[
            pl.BlockSpec((None, tm, d), lambda i: (i // nt, i % nt, 0)),
            _resident((1, d), lambda i: (0, 0)),
            _resident(wqt.shape, lambda i: (0, 0)),
            _resident(wk.shape, lambda i: (0, 0)),
            _resident(wvt.shape, lambda i: (0, 0)),
        ],
        out_specs=out_specs,
        compiler_params=_params(("parallel",), vmem),
        name=name,
    )(h1, g, wqt, wk, wvt)


def _mla_proj(h1, g, cos, sin, cos_t, sin_t, wc, gq, gkv, wuqt, wuk, wuvt, *, name):
    b, lp, d = h1.shape
    tm = _col_tile(lp, PROJ_ROW_TILE_MAX)
    nt = lp // tm
    shapes, out_specs = _proj_specs(b, lp, tm, MLA_QK_W)
    vmem = (2 * tm * d * 4 + tm * d * 2
            + (d * _C_END + HEADS * _UQ_HEAD_W * MLA_Q_RANK + 2 * DA_W * MLA_KV_RANK) * 2
            + 2 * tm * HEADS * (2 * MLA_QK_W + HEAD_W) * 2
            + 2 * tm * HEADS * _UQ_HEAD_W * 4)
    return pl.pallas_call(
        _mla_proj_kernel,
        out_shape=shapes,
        grid=(b * nt,),
        in_specs=[
            pl.BlockSpec((None, tm, d), lambda i: (i // nt, i % nt, 0)),
            _resident((1, d), lambda i: (0, 0)),
            pl.BlockSpec((tm, HEAD_W), lambda i: (i % nt, 0)),
            pl.BlockSpec((tm, HEAD_W), lambda i: (i % nt, 0)),
            pl.BlockSpec((HEAD_W, tm), lambda i: (0, i % nt)),
            pl.BlockSpec((HEAD_W, tm), lambda i: (0, i % nt)),
            _resident(wc.shape, lambda i: (0, 0)),
            _resident((1, MLA_Q_RANK), lambda i: (0, 0)),
            _resident((1, MLA_KV_RANK), lambda i: (0, 0)),
            _resident(wuqt.shape, lambda i: (0, 0)),
            _resident(wuk.shape, lambda i: (0, 0)),
            _resident(wuvt.shape, lambda i: (0, 0)),
        ],
        out_specs=out_specs,
        compiler_params=_params(("parallel",), vmem),
        name=name,
    )(h1, g, cos, sin, cos_t, sin_t, wc, gq, gkv, wuqt, wuk, wuvt)


def _t5_bucket(rel):
    n = np.maximum(rel, 0)
    max_exact = REL_BUCKETS // 2
    n_f = np.maximum(n, 1).astype(np.float64)
    large = max_exact + (np.log(n_f / max_exact) / math.log(REL_MAX_DIST / max_exact)
                         * (REL_BUCKETS - max_exact)).astype(np.int32)
    large = np.minimum(large, REL_BUCKETS - 1)
    return np.where(n < max_exact, n, large).astype(np.int32)


def _bucket_tiles(t):
    j = np.arange(t)[:, None]
    i = np.arange(t)[None, :]
    diag = np.where(j <= i, _t5_bucket(i - j), -1)
    sub = _t5_bucket(t + i - j)
    assert _t5_bucket(np.array([t + 1]))[0] == REL_BUCKETS - 1
    jm = np.arange(META_BLK)[:, None]
    meta_first = np.where(jm < N_META, _t5_bucket(N_META + i - jm), -1)
    meta_rest = np.where(jm < N_META, REL_BUCKETS - 1, -1) + 0 * i
    return (diag.astype(np.int32), np.stack([np.full_like(sub, -1), sub]).astype(np.int32),
            np.stack([meta_first, meta_rest]).astype(np.int32))


def _bias_kernel(table_ref, bd_ref, bs_ref, bm_ref, od_ref, os_ref, om_ref, *, buckets):
    h = pl.program_id(0)
    far = table_ref[REL_BUCKETS - 1, h]

    def lookup(bucket_ref, out_ref, known):
        rows, cols = known.shape
        for r0 in range(0, rows, V7X_LANES):
            for c0 in range(0, cols, V7X_LANES):
                patch = (slice(r0, r0 + V7X_LANES), slice(c0, c0 + V7X_LANES))
                present = np.unique(known[patch])
                acc = jnp.zeros((V7X_LANES, V7X_LANES), F32)
                if present.size > 1 or 0 <= present[0] < REL_BUCKETS - 1:
                    bucket = bucket_ref[patch]
                    for b in present:
                        if 0 <= b < REL_BUCKETS - 1:
                            acc = jnp.where(bucket == int(b), (table_ref[int(b), h] - far) * LOG2_E, acc)
                    if present[0] < 0:
                        acc = jnp.where(bucket < 0, NEG_INF, acc)
                elif present[0] < 0:
                    acc = jnp.full((V7X_LANES, V7X_LANES), NEG_INF, F32)
                out_ref[patch] = acc

    bd, bs, bm = buckets
    lookup(bd_ref, od_ref.at[0], bd)
    for i in range(2):
        lookup(bs_ref.at[i], os_ref.at[0, i], bs[i])
        lookup(bm_ref.at[i], om_ref.at[0, i], bm[i])


def _bias_tiles(table, t):
    bd, bs, bm = _bucket_tiles(t)
    return pl.pallas_call(
        functools.partial(_bias_kernel, buckets=(bd, bs, bm)),
        out_shape=[jax.ShapeDtypeStruct((HEADS, t, t), F32),
                   jax.ShapeDtypeStruct((HEADS, 2, t, t), F32),
                   jax.ShapeDtypeStruct((HEADS, 2, META_BLK, t), F32)],
        grid=(HEADS,),
        in_specs=[
            pl.BlockSpec(memory_space=pltpu.SMEM),
            pl.BlockSpec((t, t), lambda h: (0, 0)),
            pl.BlockSpec((2, t, t), lambda h: (0, 0, 0)),
            pl.BlockSpec((2, META_BLK, t), lambda h: (0, 0, 0)),
        ],
        out_specs=[pl.BlockSpec((1, t, t), lambda h: (h, 0, 0)),
                   pl.BlockSpec((1, 2, t, t), lambda h: (h, 0, 0, 0)),
                   pl.BlockSpec((1, 2, META_BLK, t), lambda h: (h, 0, 0, 0))],
        compiler_params=_params(("parallel",), 12 * t * t * 4),
        name="rel_bias_tiles",
    )(table, jnp.asarray(bd), jnp.asarray(bs), jnp.asarray(bm))


ATTN_QUERY_CHUNK = 256


ATTN_SCORE_BUFFERS = 4
ATTN_HEADS_PER_STEP = 2


class _Step(NamedTuple):
    q: Callable[[], Any]
    k: Callable[[], Any]
    vt: Callable[[], Any]
    bias: Optional[Callable[[], Any]]
    chunk: int


def _block_steps(n_chunks, q, k, vt, bias=None):
    return [_Step(functools.partial(q, c), functools.partial(k, c), functools.partial(vt, c),
                  functools.partial(bias, c) if bias is not None else None, c)
            for c in range(n_chunks)]


def _issue_scores(s_ref, step, slot):
    k = step.k()
    s_ref[slot, 0:k.shape[0], :] = _dot(k, step.q())


def _prefetch_scores(s_ref, steps):
    for i, step in enumerate(steps[:s_ref.shape[0] - 1]):
        _issue_scores(s_ref, step, i)


def _run_steps(s_ref, steps, following, m_ref, acc_ref):
    n_buf, _, qc = s_ref.shape
    ahead = n_buf - 1
    assert not following or len(steps) % n_buf == 0
    seq = list(steps) + list(following)[:ahead]
    for i, step in enumerate(steps):
        if i + ahead < len(seq):
            _issue_scores(s_ref, seq[i + ahead], (i + ahead) % n_buf)
        vt = step.vt()
        s = s_ref[i % n_buf, 0:vt.shape[1], :]
        if step.bias is not None:
            s = s + step.bias()
        sl = pl.ds(step.chunk * qc, qc)
        m_prev = m_ref[:, sl]
        m_new = jnp.maximum(m_prev, jnp.max(s, axis=0, keepdims=True))
        alpha = jnp.exp2(m_prev - m_new)
        p = jnp.exp2(s - m_new)
        acc_ref[:, sl] = alpha * acc_ref[:, sl] + _dot(vt, p.astype(BF16))
        m_ref[:, sl] = m_new


def _attend_init(m_ref, acc_ref):
    m_ref[...] = jnp.full(m_ref.shape, NEG_INF, F32)
    acc_ref[...] = jnp.zeros(acc_ref.shape, F32)


FAR_UNROLL = 8


def _walk_far_blocks(n_far, body):
    def loop_body(i, carry):
        body(i * FAR_UNROLL, FAR_UNROLL)
        return carry

    lax.fori_loop(0, n_far >> (FAR_UNROLL.bit_length() - 1), loop_body, 0)
    count = FAR_UNROLL // 2
    while count >= 1:
        first = n_far - (n_far & (2 * count - 1))

        @pl.when((n_far & count) != 0)
        def _(first=first, count=count):
            body(first, count)

        count //= 2


def _normalised(acc_ref):
    acc = acc_ref[...]
    return acc[0:HEAD_W] * (1.0 / acc[HEAD_W:HEAD_W + 1])


def _da_kernel(qt_ref, k_ref, vt_ref, km_ref, vtm_ref, bd_ref, bs_ref, bm_ref, lam_ref, subg_ref,
               o_ref, qq_ref, s_ref, m_ref, acc_ref, *, t):
    qi = pl.program_id(2)
    feat = lax.broadcasted_iota(jnp.int32, (HEAD_W, 1), 0)
    for hd in range(ATTN_HEADS_PER_STEP):
        qt = qt_ref[0, hd]
        zero = jnp.zeros_like(qt)
        qq_ref[:, pl.ds(2 * t * hd, t)] = jnp.where(feat < DA_HEAD_DIM, qt, zero)
        qq_ref[:, pl.ds(2 * t * hd + t, t)] = jnp.where(feat >= DA_HEAD_DIM, qt, zero)
    qc = s_ref.shape[2]
    chunks_per_map = t // qc
    chunks_per_head = 2 * chunks_per_map
    n_chunks = ATTN_HEADS_PER_STEP * chunks_per_head

    def head(c):
        return c // chunks_per_head

    def cols(c):
        return pl.ds((c % chunks_per_map) * qc, qc)

    def q_chunk(c):
        return qq_ref[:, pl.ds(c * qc, qc)]

    def block_steps(kb, bias=None, keys=t):
        rows = pl.ds(pl.multiple_of(kb * t, t), keys)
        return _block_steps(n_chunks, q_chunk, lambda c: k_ref[0, head(c), rows, :],
                            lambda c: vt_ref[0, head(c), :, rows], bias)

    state = (m_ref, acc_ref)
    n_far = jnp.maximum(qi - 1, 0)
    near_sel = jnp.minimum(qi, 1)
    prev_kb = jnp.maximum(qi - 1, 0)
    meta_steps = _block_steps(n_chunks, q_chunk, lambda c: km_ref[0, head(c)],
                              lambda c: vtm_ref[0, head(c)],
                              lambda c: bm_ref[head(c), near_sel, :, cols(c)])
    _attend_init(*state)
    _prefetch_scores(s_ref, meta_steps)
    _run_steps(s_ref, meta_steps, block_steps(jnp.where(qi >= 2, 0, qi)), *state)

    def far(first, count):
        steps = [st for u in range(count) for st in block_steps(first + u)]
        last = first + count - 1
        _run_steps(s_ref, steps, block_steps(jnp.where(last == n_far - 1, qi, last + 1)), *state)

    _walk_far_blocks(n_far, far)

    diag = []
    for c in range(n_chunks):
        keys = (c % chunks_per_map + 1) * qc
        diag.append(block_steps(qi, lambda c, keys=keys: bd_ref[head(c), 0:keys, cols(c)], keys)[c])
    prev = block_steps(prev_kb, lambda c: bs_ref[head(c), near_sel, :, cols(c)])
    _run_steps(s_ref, diag + prev, (), *state)

    ot = _normalised(acc_ref)
    lp = lam_ref[...]
    lam = (jnp.exp(jnp.sum(lp[0:1] * lp[1:2], axis=-1, keepdims=True))
           - jnp.exp(jnp.sum(lp[2:3] * lp[3:4], axis=-1, keepdims=True)) + LAMBDA_INIT)
    for hd in range(ATTN_HEADS_PER_STEP):
        o1 = ot[:, 2 * t * hd:2 * t * hd + t]
        o2 = ot[:, 2 * t * hd + t:2 * t * (hd + 1)]
        d = o1 - lam * o2
        ms = jnp.mean(d * d, axis=0, keepdims=True)
        y = d * lax.rsqrt(ms + RMS_EPS) * subg_ref[...] * (1.0 - LAMBDA_INIT)
        o_ref[0, :, hd * HEAD_W:(hd + 1) * HEAD_W] = y.T.astype(BF16)


def _attn_specs(t, s_len, q_w):
    hp = ATTN_HEADS_PER_STEP
    return [
        pl.BlockSpec((1, hp, q_w, t), lambda bi, h, qi: (bi, h, 0, qi)),
        pl.BlockSpec((1, hp, s_len, q_w), lambda bi, h, qi: (bi, h, 0, 0)),
        pl.BlockSpec((1, hp, VT_ROWS, s_len), lambda bi, h, qi: (bi, h, 0, 0)),
        pl.BlockSpec((1, hp, META_BLK, q_w), lambda bi, h, qi: (0, h, 0, 0)),
        pl.BlockSpec((1, hp, VT_ROWS, META_BLK), lambda bi, h, qi: (0, h, 0, 0)),
    ]


def _da_attention(qt, k, vt, k_meta, vt_meta, bias_d, bias_s, bias_m, lam_p, sub_g, *, t):
    b, _, s_len, _ = k.shape
    nq = s_len // t
    hp = ATTN_HEADS_PER_STEP
    qc = min(ATTN_QUERY_CHUNK, t)
    vmem = hp * (2 * (s_len + META_BLK) * (HEAD_W + VT_ROWS) * 2 + 2 * 3 * t * t * 4
                 + 2 * 2 * t * META_BLK * 4
                 + 2 * 2 * t * HEAD_W * 2 + 2 * t * HEAD_W * 2 + (VT_ROWS + 8) * 2 * t * 4
                 ) + (ATTN_SCORE_BUFFERS + 4) * t * qc * 4
    return pl.pallas_call(
        functools.partial(_da_kernel, t=t),
        out_shape=jax.ShapeDtypeStruct((b, s_len, DA_W), BF16),
        grid=(b, HEADS // hp, nq),
        in_specs=_attn_specs(t, s_len, HEAD_W) + [
            pl.BlockSpec((hp, t, t), lambda bi, h, qi: (h, 0, 0)),
            pl.BlockSpec((hp, 2, t, t), lambda bi, h, qi: (h, 0, 0, 0)),
            pl.BlockSpec((hp, 2, META_BLK, t), lambda bi, h, qi: (h, 0, 0, 0)),
            _resident((4, DA_HEAD_DIM), lambda bi, h, qi: (0, 0)),
            _resident((HEAD_W, 1), lambda bi, h, qi: (0, 0)),
        ],
        out_specs=pl.BlockSpec((1, t, hp * HEAD_W), lambda bi, h, qi: (bi, qi, h)),
        scratch_shapes=[pltpu.VMEM((HEAD_W, hp * 2 * t), BF16),
                        pltpu.VMEM((ATTN_SCORE_BUFFERS, t, qc), F32),
                        pltpu.VMEM((1, hp * 2 * t), F32), pltpu.VMEM((VT_ROWS, hp * 2 * t), F32)],
        compiler_params=_params(("parallel", "parallel", "arbitrary"), vmem),
        name="diff_attention",
    )(qt, k, vt, k_meta, vt_meta, bias_d, bias_s, bias_m, lam_p, sub_g)


def _mla_kernel(qt_ref, k_ref, vt_ref, km_ref, vtm_ref, o_ref, s_ref, m_ref, acc_ref, *, tq, tk):
    qi = pl.program_id(2)
    qc = s_ref.shape[2]
    diag_blocks = tq // tk
    chunks_per_head = tq // qc
    n_chunks = ATTN_HEADS_PER_STEP * chunks_per_head

    def head(c):
        return c // chunks_per_head

    def q_chunk(c):
        return qt_ref[0, head(c), :, pl.ds((c % chunks_per_head) * qc, qc)]

    def block_steps(kb, bias=None, keys=tk):
        rows = pl.ds(pl.multiple_of(kb * tk, tk), keys)
        return _block_steps(n_chunks, q_chunk, lambda c: k_ref[0, head(c), rows, :],
                            lambda c: vt_ref[0, head(c), :, rows], bias)

    state = (m_ref, acc_ref)

    def meta_mask(c):
        key = lax.broadcasted_iota(jnp.int32, (META_BLK, qc), 0)
        return jnp.where(key < N_META, 0.0, NEG_INF)

    meta_steps = _block_steps(n_chunks, q_chunk, lambda c: km_ref[0, head(c)],
                              lambda c: vtm_ref[0, head(c)], meta_mask)
    _attend_init(*state)
    _prefetch_scores(s_ref, meta_steps)
    _run_steps(s_ref, meta_steps, block_steps(0), *state)

    def far(first, count):
        steps = [st for u in range(count) for st in block_steps(first + u)]
        _run_steps(s_ref, steps, block_steps(first + count), *state)

    first_diag = qi * diag_blocks
    _walk_far_blocks(first_diag, far)

    diag = []
    for d in range(diag_blocks):
        for c in range(n_chunks):
            cq = (c % chunks_per_head) * qc
            keys = min(tk, cq + qc - d * tk)
            if keys <= 0:
                continue

            def causal(c, d=d, keys=keys, cq=cq):
                key = lax.broadcasted_iota(jnp.int32, (keys, qc), 0) + d * tk
                query = lax.broadcasted_iota(jnp.int32, (keys, qc), 1) + cq
                return jnp.where(key <= query, 0.0, NEG_INF)

            diag.append(block_steps(first_diag + d, causal, keys)[c])
    _run_steps(s_ref, diag, (), *state)

    ot = _normalised(acc_ref)
    for hd in range(ATTN_HEADS_PER_STEP):
        o_ref[0, :, hd * HEAD_W:(hd + 1) * HEAD_W] = ot[:, hd * tq:(hd + 1) * tq].T.astype(BF16)


def _mla_attention(qt, k, vt, k_meta, vt_meta, *, tq, tk):
    b, _, s_len, _ = k.shape
    nq = s_len // tq
    hp = ATTN_HEADS_PER_STEP
    qc = min(ATTN_QUERY_CHUNK, tq)
    vmem = hp * (2 * (s_len + META_BLK) * (MLA_QK_W + VT_ROWS) * 2 + 2 * tq * MLA_QK_W * 2
                 + 2 * tq * HEAD_W * 2 + (VT_ROWS + 8) * tq * 4 + tq * HEAD_W * 4
                 ) + (ATTN_SCORE_BUFFERS + 4) * tk * qc * 4
    return pl.pallas_call(
        functools.partial(_mla_kernel, tq=tq, tk=tk),
        out_shape=jax.ShapeDtypeStruct((b, s_len, DA_W), BF16),
        grid=(b, HEADS // hp, nq),
        in_specs=_attn_specs(tq, s_len, MLA_QK_W),
        out_specs=pl.BlockSpec((1, tq, hp * HEAD_W), lambda bi, h, qi: (bi, qi, h)),
        scratch_shapes=[pltpu.VMEM((ATTN_SCORE_BUFFERS, tk, qc), F32),
                        pltpu.VMEM((1, hp * tq), F32), pltpu.VMEM((VT_ROWS, hp * tq), F32)],
        compiler_params=_params(("parallel", "parallel", "arbitrary"), vmem),
        name="mla_attention",
    )(qt, k, vt, k_meta, vt_meta)


def _merge_kernel(h_ref, pre_ref, post_ref, yda_ref, ymla_ref, wg0_ref, wg1_ref, b0_ref, b1_ref,
                  wbd_ref, wbm_ref, wo_ref, o_ref, hn_ref):
    j = pl.program_id(1)
    last = pl.num_programs(1) - 1

    def partial_out(hn):
        g0 = jax.nn.sigmoid(_dot(hn, wg0_ref[...]) + b0_ref[...])
        g1 = jax.nn.sigmoid(_dot(hn, wg1_ref[...]) + b1_ref[...])
        merged = g0 * _dot(yda_ref[0], wbd_ref[...]) + g1 * _dot(ymla_ref[0], wbm_ref[...])
        return _dot(merged.astype(BF16), wo_ref[...])

    @pl.when(j == 0)
    def _():
        hn = _rms(h_ref[0], pre_ref[...]).astype(BF16)
        hn_ref[...] = hn
        o_ref[0] = partial_out(hn)

    @pl.when(jnp.logical_and(j > 0, j < last))
    def _():
        o_ref[0] += partial_out(hn_ref[...])

    @pl.when(j == last)
    def _():
        m = o_ref[0] + partial_out(hn_ref[...])
        o_ref[0] = h_ref[0] + _rms(m, post_ref[...])


def _merge(h1, pre_g, post_g, y_da, y_mla, w_gate, b_gate, wb_da, wb_mla, w_out):
    b, s_len, d = h1.shape
    yw = y_da.shape[-1]
    tm = _row_tile(s_len, MERGE_ROW_TILE_MAX)
    tn = _col_tile(d, MERGE_COL_TILE_MAX)
    nt = s_len // tm
    nj = d // tn
    assert nj >= 2, "the kernel treats the first and the last column tile as different steps"
    vmem = (2 * 2 * tm * d * 4 + tm * d * 2 + 2 * 2 * tm * yw * 2
            + 2 * (3 * d * tn + 2 * yw * tn) * 2 + 6 * tm * tn * 4 + tm * d * 4)
    return pl.pallas_call(
        _merge_kernel,
        out_shape=jax.ShapeDtypeStruct((b, s_len, d), F32),
        grid=(b * nt, nj),
        in_specs=[
            pl.BlockSpec((1, tm, d), lambda i, j: (i // nt, i % nt, 0)),
            _resident((1, d), lambda i, j: (0, 0)),
            _resident((1, d), lambda i, j: (0, 0)),
            pl.BlockSpec((1, tm, yw), lambda i, j: (i // nt, i % nt, 0)),
            pl.BlockSpec((1, tm, yw), lambda i, j: (i // nt, i % nt, 0)),
            pl.BlockSpec((d, tn), lambda i, j: (0, j)),
            pl.BlockSpec((d, tn), lambda i, j: (0, nj + j)),
            pl.BlockSpec((1, tn), lambda i, j: (0, j)),
            pl.BlockSpec((1, tn), lambda i, j: (0, nj + j)),
            pl.BlockSpec((yw, tn), lambda i, j: (0, j)),
            pl.BlockSpec((yw, tn), lambda i, j: (0, j)),
            pl.BlockSpec((tn, d), lambda i, j: (j, 0)),
        ],
        out_specs=pl.BlockSpec((1, tm, d), lambda i, j: (i // nt, i % nt, 0)),
        scratch_shapes=[pltpu.VMEM((tm, d), BF16)],
        compiler_params=_params(("parallel", "arbitrary"), vmem),
        name="gated_merge_out_proj",
    )(h1, pre_g, post_g, y_da, y_mla, w_gate, w_gate, b_gate, b_gate, wb_da, wb_mla, w_out)


def _rope_tables(pos):
    half = MLA_ROPE // 2
    pos = np.asarray(pos, np.float32)
    inv = (np.float32(ROPE_THETA) ** (-np.arange(half, dtype=np.float32) * np.float32(2.0 / MLA_ROPE))
           ).astype(np.float32)
    ang = pos[:, None] * inv[None, :]
    cos, sin = np.cos(ang), np.sin(ang)
    pad = np.zeros((pos.shape[0], HEAD_W - MLA_ROPE), np.float32)
    return np.concatenate([cos, cos, pad], axis=-1), np.concatenate([-sin, sin, pad], axis=-1)


def _swap_halves(w):
    half = w.shape[-1] // 2
    return jnp.concatenate([w[..., half:], w[..., :half]], axis=-1)


def _pad_lanes(w, width):
    return jnp.concatenate([w, jnp.zeros(w.shape[:-1] + (width - w.shape[-1],), w.dtype)], axis=-1)


def kernel(x, meta_tokens, rel_bias_table, ffn1_pre_g, ffn1_post_g, ffn1_w_gate, ffn1_w_up, ffn1_w_down, mix_pre_g, mix_post_g, w_in, b_gate, da_lambda_q1, da_lambda_k1, da_lambda_q2, da_lambda_k2, da_sub_g, mla_q_norm_g, mla_kv_norm_g, mla_w_uq, mla_w_ukv, w_branch_da, w_branch_mla, w_out, ffn2_pre_g, ffn2_post_g, ffn2_w_gate, ffn2_w_up, ffn2_w_down):
    b, s_len, d = x.shape
    assert ffn1_pre_g.shape[0] == 1, "single-layer trunk"
    t = _col_tile(s_len, ATTN_KEY_BLOCK_MAX)
    row = lambda g: g.reshape(1, -1).astype(F32)

    meta_in = jnp.concatenate([meta_tokens.astype(x.dtype),
                               jnp.zeros((META_BLK - N_META, d), x.dtype)], axis=0)
    ffn1_w = (row(ffn1_pre_g), row(ffn1_post_g), ffn1_w_gate[0].astype(BF16),
              ffn1_w_up[0].astype(BF16), ffn1_w_down[0].astype(BF16))
    h1 = _ffn(x.reshape(b * s_len, d), *ffn1_w, name="ffn1").reshape(b, s_len, d)
    h1_meta = _ffn(meta_in, *ffn1_w, name="ffn1_meta").reshape(1, META_BLK, d)

    w = w_in[0]
    o_k, o_v, o_cq = DA_W, 2 * DA_W, 3 * DA_W
    o_kr = o_cq + MLA_Q_RANK + MLA_KV_RANK
    o_gate = o_kr + MLA_ROPE
    da_w = (row(mix_pre_g), w[:, :o_k].T.astype(BF16), w[:, o_k:o_v].astype(BF16),
            w[:, o_v:o_cq].T.astype(BF16))
    qt_da, k_da, vt_da = _da_proj(h1, *da_w, name="da_in_proj")
    _, k_da_meta, vt_da_meta = _da_proj(h1_meta, *da_w, name="da_in_proj_meta")

    kr = w[:, o_kr:o_gate]
    w_c = jnp.concatenate([w[:, o_cq:o_kr], _pad_lanes(kr, HEAD_W),
                           _pad_lanes(_swap_halves(kr), HEAD_W)], axis=-1).astype(BF16)
    uq = mla_w_uq[0].reshape(MLA_Q_RANK, HEADS, MLA_NOPE + MLA_ROPE)
    uq_rope = uq[..., MLA_NOPE:]
    w_uqt = jnp.concatenate([uq[..., :MLA_NOPE], uq_rope, _swap_halves(uq_rope)],
                            axis=-1).reshape(MLA_Q_RANK, HEADS * _UQ_HEAD_W).T.astype(BF16)
    ukv = mla_w_ukv[0].reshape(MLA_KV_RANK, HEADS, 2 * HEAD_W)
    w_uk = ukv[..., :HEAD_W].reshape(MLA_KV_RANK, DA_W).astype(BF16)
    w_uvt = ukv[..., HEAD_W:].reshape(MLA_KV_RANK, DA_W).T.astype(BF16)
    mla_w = (w_c, row(mla_q_norm_g), row(mla_kv_norm_g), w_uqt, w_uk, w_uvt)

    def mla_proj(h, pos, name):
        tables = [jnp.asarray(np.ascontiguousarray(v)) for c in _rope_tables(pos) for v in (c,)]
        tables += [jnp.asarray(np.ascontiguousarray(c.T)) for c in _rope_tables(pos)]
        return _mla_proj(h, row(mix_pre_g), *tables, *mla_w, name=name)

    qt_mla, k_mla, vt_mla = mla_proj(h1, N_META + np.arange(s_len), "mla_in_proj")
    _, k_mla_meta, vt_mla_meta = mla_proj(h1_meta, np.arange(META_BLK), "mla_in_proj_meta")

    bias_d, bias_s, bias_m = _bias_tiles(rel_bias_table.astype(F32), t)
    lam_p = jnp.concatenate([da_lambda_q1, da_lambda_k1, da_lambda_q2, da_lambda_k2], axis=0).astype(F32)
    y_da = _da_attention(qt_da, k_da, vt_da, k_da_meta, vt_da_meta, bias_d, bias_s, bias_m, lam_p,
                         da_sub_g.reshape(HEAD_W, 1).astype(F32), t=t)
    y_mla = _mla_attention(qt_mla, k_mla, vt_mla, k_mla_meta, vt_mla_meta,
                           tq=_col_tile(s_len, 2 * t), tk=t)

    h2 = _merge(h1, row(mix_pre_g), row(mix_post_g), y_da, y_mla,
                w[:, o_gate:].astype(BF16), row(b_gate),
                w_branch_da[0].astype(BF16), w_branch_mla[0].astype(BF16), w_out[0].astype(BF16))

    out = _ffn(h2.reshape(b * s_len, d), row(ffn2_pre_g), row(ffn2_post_g),
               ffn2_w_gate[0].astype(BF16), ffn2_w_up[0].astype(BF16), ffn2_w_down[0].astype(BF16),
               name="ffn2")
    return out.reshape(b, s_len, d)
```
